```python
import math
import jax, jax.numpy as jnp
from jax import lax
import numpy as np

D_MODEL = 1024
BATCH = 32
SEQ = 2048
DEPTH = 2

GRID_W = 64
CTX_LEN = 256

GM_DIM = 256
GM_GROUPS = 4
GM_CHUNK = 128
HY_DIM = 256
HY_EMB = 33
HY_BANDS = (HY_EMB - 1) // 2
HY_FFN = 64
HY_DECAY_FAST = 0.3
HY_DECAY_SLOW = 1.5
HY_DECAY_TARGET = 1e-2
HY_DECAY_SHIFT = 0.05
DA_HEADS = 4
DA_HEAD_DIM = 64
DA_V_DIM = 2 * DA_HEAD_DIM
DA_Q_BLOCK = 128
DA_QK_W = DA_HEADS * 2 * DA_HEAD_DIM
DA_V_W = DA_HEADS * DA_V_DIM
ROPE_BASE = 10000.0
N_BRANCH = 3
OFF_GM = 0
OFF_HY = OFF_GM + 2 * GM_DIM
OFF_Q = OFF_HY + 3 * HY_DIM
OFF_K = OFF_Q + DA_QK_W
OFF_GATE = OFF_K + DA_QK_W + DA_V_W
N_IN = OFF_GATE + N_BRANCH * D_MODEL
MOE_GROUPS = 4
MOE_EXPERTS_PER_GROUP = 8
MOE_N_EXPERTS = MOE_GROUPS * MOE_EXPERTS_PER_GROUP
MOE_TOP_K = 2
MOE_HIDDEN = 512
MOE_BLOCK = 256
DEEPNORM_ALPHA = (2.0 * DEPTH) ** 0.25
DEEPNORM_BETA = (8.0 * DEPTH) ** -0.25
LN_EPS = 1e-5

kernel_name = 'hybrid_gmlp_hyena_diffattn_hmoe_block'


def layer_norm(x, g, b):
    xf = x.astype(jnp.float32)
    mu = jnp.mean(xf, axis=-1, keepdims=True)
    var = jnp.mean(jnp.square(xf - mu), axis=-1, keepdims=True)
    y = (xf - mu) * lax.rsqrt(var + LN_EPS) * g.astype(jnp.float32) + b.astype(jnp.float32)
    return y.astype(x.dtype)


def modulate(x, shift, scale):
    return x * (1 + scale) + shift


def chunk_gmlp(z, ln_g, ln_b, ws, bs):
    u, v = jnp.split(jax.nn.gelu(z), 2, axis=-1)
    v = layer_norm(v, ln_g, ln_b)
    B, L, _ = v.shape
    v = v.reshape(B, L // GM_CHUNK, GM_CHUNK, GM_GROUPS, GM_DIM // GM_GROUPS)
    v = jnp.einsum('gpq,bnqgc->bnpgc', ws, v) + jnp.transpose(bs)[:, :, None]
    return u * v.reshape(B, L, GM_DIM)


def short_conv(z, w, b):
    L = z.shape[1]
    zp = jnp.pad(z, ((0, 0), (1, 1), (0, 0)))
    return zp[:, :L] * w[0] + zp[:, 1:L + 1] * w[1] + zp[:, 2:] * w[2] + b


def hyena_filters(L, w1, b1, w2, b2, w3, b3):
    f32 = jnp.float32
    t = jnp.linspace(0.0, 1.0, L, dtype=f32)[:, None]
    w = 2.0 * math.pi * jnp.arange(L, dtype=f32)[:, None] / L
    f = jnp.linspace(1e-4, HY_BANDS - 1, HY_BANDS, dtype=f32)[None, :]
    emb = jnp.concatenate([t, jnp.cos(f * w), -jnp.sin(f * w)], axis=-1)
    h = jnp.sin(emb @ w1.astype(f32) + b1.astype(f32))
    h = jnp.sin(h @ w2.astype(f32) + b2.astype(f32))
    h = h @ w3.astype(f32) + b3.astype(f32)
    max_decay = math.log(HY_DECAY_TARGET) / HY_DECAY_FAST
    min_decay = math.log(HY_DECAY_TARGET) / HY_DECAY_SLOW
    deltas = jnp.abs(jnp.linspace(min_decay, max_decay, HY_DIM, dtype=f32))
    window = jnp.exp(-t * deltas[None, :]) + HY_DECAY_SHIFT
    h = h.reshape(L, 2, HY_DIM) * window[:, None, :]
    return h[:, 0], h[:, 1]


def bidirectional_long_conv(v, h_fwd, h_bwd):
    L = v.shape[1]
    k = jnp.concatenate([h_fwd, jnp.zeros((1, HY_DIM), jnp.float32), jnp.flip(h_bwd[1:], axis=0)], axis=0)
    kf = jnp.fft.rfft(k, axis=0)
    vf = jnp.fft.rfft(v.astype(jnp.float32), n=2 * L, axis=1)
    return jnp.fft.irfft(vf * kf[None], n=2 * L, axis=1)[:, :L]


def hyena_mixer(z, conv_w, conv_b, h_fwd, h_bwd, skip):
    zc = short_conv(z, conv_w, conv_b)
    x0, x1, v = jnp.split(zc, 3, axis=-1)
    v = v * x1
    y = bidirectional_long_conv(v, h_fwd, h_bwd) + v.astype(jnp.float32) * skip.astype(jnp.float32)
    return y.astype(z.dtype) * x0


def axial_rope(rows):
    n_freq = DA_HEAD_DIM // 4
    row = jnp.broadcast_to(jnp.arange(rows)[:, None], (rows, GRID_W)).reshape(-1).astype(jnp.float32)
    col = jnp.broadcast_to(jnp.arange(GRID_W)[None, :], (rows, GRID_W)).reshape(-1).astype(jnp.float32)
    inv = ROPE_BASE ** (-jnp.arange(n_freq, dtype=jnp.float32) / n_freq)
    ang = jnp.stack([row[:, None] * inv, col[:, None] * inv], axis=1)
    return jnp.cos(ang), jnp.sin(ang)


def apply_rope(x, cos, sin):
    n_freq = DA_HEAD_DIM // 4
    xr = x.astype(jnp.float32).reshape(x.shape[:-1] + (2, 2, n_freq))
    x1, x2 = xr[..., 0, :], xr[..., 1, :]
    c = cos[:, None, None]
    s = sin[:, None, None]
    out = jnp.stack([x1 * c - x2 * s, x2 * c + x1 * s], axis=-2)
    return out.reshape(x.shape).astype(x.dtype)


def split_q(zq):
    B, L, _ = zq.shape
    return zq.reshape(B, L, DA_HEADS, 2, DA_HEAD_DIM)


def split_kv(zkv):
    B, L, _ = zkv.shape
    k = zkv[..., :DA_QK_W].reshape(B, L, DA_HEADS, 2, DA_HEAD_DIM)
    v = zkv[..., DA_QK_W:].reshape(B, L, DA_HEADS, DA_V_DIM)
    return k, v


def diff_softmax_attend(q, k, v, lam):
    s = jnp.einsum('bqhmd,bkhmd->bhmqk', q, k, preferred_element_type=jnp.float32) * (DA_HEAD_DIM ** -0.5)
    p = jax.nn.softmax(s, axis=-1)
    a = p[:, :, 0] - lam * p[:, :, 1]
    return jnp.einsum('bhqk,bkhe->bqhe', a.astype(v.dtype), v)


def latent_diff_attention(q, k, v, lam):
    B, L = q.shape[:2]
    nb = L // DA_Q_BLOCK
    qb = jnp.moveaxis(q.reshape((B, nb, DA_Q_BLOCK) + q.shape[2:]), 1, 0)
    ob = lax.map(lambda qq: diff_softmax_attend(qq, k, v, lam), qb)
    return jnp.moveaxis(ob, 0, 1).reshape(B, L, DA_HEADS, DA_V_DIM)


def head_rms(o, g, lam_init):
    B, L = o.shape[:2]
    of = o.astype(jnp.float32)
    of = of * lax.rsqrt(jnp.mean(jnp.square(of), axis=-1, keepdims=True) + LN_EPS) * g.astype(jnp.float32)
    return (of * (1.0 - lam_init)).reshape(B, L, DA_V_W).astype(o.dtype)


def merge_branches(z, y_c, lp):
    L = z.shape[1]
    y_a = chunk_gmlp(z[..., OFF_GM:OFF_HY], lp['gm_ln_g'], lp['gm_ln_b'], lp['gm_ws'], lp['gm_bs'])
    h_fwd, h_bwd = hyena_filters(L, lp['hy_f_w1'], lp['hy_f_b1'], lp['hy_f_w2'], lp['hy_f_b2'],
                                 lp['hy_f_w3'], lp['hy_f_b3'])
    y_b = hyena_mixer(z[..., OFF_HY:OFF_Q], lp['hy_conv_w'], lp['hy_conv_b'], h_fwd, h_bwd, lp['hy_skip'])
    g_a, g_b, g_c = jnp.split(jax.nn.sigmoid(z[..., OFF_GATE:]), N_BRANCH, axis=-1)
    merged = g_a * (y_a @ lp['p_a']) + g_b * (y_b @ lp['p_b']) + g_c * (y_c @ lp['p_c'])
    return merged @ lp['w_out']


def hierarchical_moe(xt, wg, bg, we, be, w_gate, w_up, w_down):
    T, D = xt.shape
    A = T * MOE_TOP_K
    g_logits = (xt @ wg + bg).astype(jnp.float32)
    g_idx = jnp.argmax(g_logits, axis=-1)
    g_prob = jnp.take_along_axis(jax.nn.softmax(g_logits, axis=-1), g_idx[:, None], axis=-1)
    e_logits = (xt @ we + be).astype(jnp.float32).reshape(T, MOE_GROUPS, MOE_EXPERTS_PER_GROUP)
    e_logits = jnp.take_along_axis(e_logits, g_idx[:, None, None], axis=1)[:, 0]
    top_v, top_i = lax.top_k(e_logits, MOE_TOP_K)
    weights = g_prob * jax.nn.softmax(top_v, axis=-1)
    eid = (g_idx[:, None] * MOE_EXPERTS_PER_GROUP + top_i).reshape(-1).astype(jnp.int32)
    w_flat = weights.reshape(-1)
    order = jnp.argsort(eid)
    e_s = eid[order]
    tok_s = (order // MOE_TOP_K).astype(jnp.int32)
    w_s = w_flat[order]
    counts = jnp.bincount(eid, length=MOE_N_EXPERTS)
    starts = jnp.cumsum(counts) - counts
    pcounts = (counts + MOE_BLOCK - 1) // MOE_BLOCK * MOE_BLOCK
    pends = jnp.cumsum(pcounts)
    pstarts = pends - pcounts
    dest = pstarts[e_s] + jnp.arange(A) - starts[e_s]
    n_blk = -(-A // MOE_BLOCK) + MOE_N_EXPERTS
    P = n_blk * MOE_BLOCK
    src = jnp.zeros((P,), jnp.int32).at[dest].set(tok_s)
    wbuf = jnp.zeros((P,), jnp.float32).at[dest].set(w_s)
    valid = jnp.zeros((P,), bool).at[dest].set(True)
    xbuf = jnp.where(valid[:, None], xt[src], 0).reshape(n_blk, MOE_BLOCK, D)
    blk_e = jnp.minimum(jnp.searchsorted(pends, jnp.arange(n_blk) * MOE_BLOCK, side='right'),
                        MOE_N_EXPERTS - 1)

    def expert_block(args):
        xb, e = args
        return (jax.nn.silu(xb @ w_gate[e]) * (xb @ w_up[e])) @ w_down[e]

    ybuf = lax.map(expert_block, (xbuf, blk_e)).reshape(P, D)
    return jax.ops.segment_sum(ybuf * wbuf[:, None].astype(ybuf.dtype), src, num_segments=T)


def setup_inputs(seed: int = 0) -> dict:
    key = jax.random.key(seed)
    ks = iter(jax.random.split(key, 64))
    D = D_MODEL
    L_ = DEPTH

    def nrm(shape, scale):
        return jax.random.normal(next(ks), shape, jnp.float32) * scale

    beta = DEEPNORM_BETA
    return {
        'x': nrm((BATCH, SEQ, D), 1.0),
        'c': nrm((BATCH, D), 1.0),
        'ctx': nrm((BATCH, CTX_LEN, D), 1.0),
        'c_ctx': nrm((D,), 1.0),
        'ada_w': nrm((L_, D, 6 * D), 0.5 * D ** -0.5),
        'ada_b': nrm((L_, 6 * D), 0.02),
        'w_in': nrm((L_, D, N_IN), D ** -0.5),
        'gm_ln_g': 1.0 + nrm((L_, GM_DIM), 0.05),
        'gm_ln_b': nrm((L_, GM_DIM), 0.02),
        'gm_ws': nrm((L_, GM_GROUPS, GM_CHUNK, GM_CHUNK), GM_CHUNK ** -0.5),
        'gm_bs': 1.0 + nrm((L_, GM_GROUPS, GM_CHUNK), 0.1),
        'hy_conv_w': nrm((L_, 3, 3 * HY_DIM), 3 ** -0.5),
        'hy_conv_b': nrm((L_, 3 * HY_DIM), 0.02),
        'hy_f_w1': nrm((L_, HY_EMB, HY_FFN), HY_EMB ** -0.5),
        'hy_f_b1': nrm((L_, HY_FFN), 0.1),
        'hy_f_w2': nrm((L_, HY_FFN, HY_FFN), HY_FFN ** -0.5),
        'hy_f_b2': nrm((L_, HY_FFN), 0.1),
        'hy_f_w3': nrm((L_, HY_FFN, 2 * HY_DIM), HY_FFN ** -0.5),
        'hy_f_b3': nrm((L_, 2 * HY_DIM), 0.02),
        'hy_skip': nrm((L_, HY_DIM), 1.0),
        'da_lq1': nrm((L_, DA_HEAD_DIM), 0.1),
        'da_lk1': nrm((L_, DA_HEAD_DIM), 0.1),
        'da_lq2': nrm((L_, DA_HEAD_DIM), 0.1),
        'da_lk2': nrm((L_, DA_HEAD_DIM), 0.1),
        'da_norm_g': 1.0 + nrm((L_, DA_V_DIM), 0.05),
        'p_a': nrm((L_, GM_DIM, D), beta * GM_DIM ** -0.5),
        'p_b': nrm((L_, HY_DIM, D), beta * HY_DIM ** -0.5),
        'p_c': nrm((L_, DA_V_W, D), beta * DA_V_W ** -0.5),
        'w_out': nrm((L_, D, D), beta * D ** -0.5),
        'ln1_g': 1.0 + nrm((L_, D), 0.05),
        'ln1_b': nrm((L_, D), 0.02),
        'moe_wg': nrm((L_, D, MOE_GROUPS), D ** -0.5),
        'moe_bg': nrm((L_, MOE_GROUPS), 0.01),
        'moe_we': nrm((L_, D, MOE_N_EXPERTS), D ** -0.5),
        'moe_be': nrm((L_, MOE_N_EXPERTS), 0.01),
        'ex_w_gate': nrm((L_, MOE_N_EXPERTS, D, MOE_HIDDEN), D ** -0.5),
        'ex_w_up': nrm((L_, MOE_N_EXPERTS, D, MOE_HIDDEN), D ** -0.5),
        'ex_w_down': nrm((L_, MOE_N_EXPERTS, MOE_HIDDEN, D), beta * MOE_HIDDEN ** -0.5),
        'ln2_g': 1.0 + nrm((L_, D), 0.05),
        'ln2_b': nrm((L_, D), 0.02),
    }


def reference(x, c, ctx, c_ctx, ada_w, ada_b, w_in, gm_ln_g, gm_ln_b, gm_ws, gm_bs,
              hy_conv_w, hy_conv_b, hy_f_w1, hy_f_b1, hy_f_w2, hy_f_b2, hy_f_w3, hy_f_b3, hy_skip,
              da_lq1, da_lk1, da_lq2, da_lk2, da_norm_g, p_a, p_b, p_c, w_out, ln1_g, ln1_b,
              moe_wg, moe_bg, moe_we, moe_be, ex_w_gate, ex_w_up, ex_w_down, ln2_g, ln2_b):
    B, L, D = x.shape
    rows = L // GRID_W
    cos, sin = axial_rope(rows)
    xc = ctx
    for l in range(DEPTH):
        last = l == DEPTH - 1
        lp = {
            'gm_ln_g': gm_ln_g[l], 'gm_ln_b': gm_ln_b[l], 'gm_ws': gm_ws[l], 'gm_bs': gm_bs[l],
            'hy_conv_w': hy_conv_w[l], 'hy_conv_b': hy_conv_b[l],
            'hy_f_w1': hy_f_w1[l], 'hy_f_b1': hy_f_b1[l], 'hy_f_w2': hy_f_w2[l], 'hy_f_b2': hy_f_b2[l],
            'hy_f_w3': hy_f_w3[l], 'hy_f_b3': hy_f_b3[l], 'hy_skip': hy_skip[l],
            'p_a': p_a[l], 'p_b': p_b[l], 'p_c': p_c[l], 'w_out': w_out[l],
        }
        lam_init = 0.8 - 0.6 * math.exp(-0.3 * l)
        lam = (jnp.exp(jnp.sum(da_lq1[l].astype(jnp.float32) * da_lk1[l].astype(jnp.float32)))
               - jnp.exp(jnp.sum(da_lq2[l].astype(jnp.float32) * da_lk2[l].astype(jnp.float32)))
               + lam_init)
        mod = jax.nn.silu(c) @ ada_w[l] + ada_b[l]
        mod_c = jax.nn.silu(c_ctx) @ ada_w[l] + ada_b[l]
        sh1, sc1, g1, sh2, sc2, g2 = jnp.split(mod[:, None, :], 6, axis=-1)
        csh1, csc1, cg1, csh2, csc2, cg2 = jnp.split(mod_c, 6, axis=-1)

        h = modulate(x, sh1, sc1)
        hc = modulate(xc, csh1, csc1)
        z = h @ w_in[l]
        if last:
            k_c, v_c = split_kv(hc @ w_in[l][:, OFF_K:OFF_GATE])
        else:
            zc = hc @ w_in[l]
            k_c, v_c = split_kv(zc[..., OFF_K:OFF_GATE])
        q_l = apply_rope(split_q(z[..., OFF_Q:OFF_K]), cos, sin)
        k_l, v_l = split_kv(z[..., OFF_K:OFF_GATE])
        k_l = apply_rope(k_l, cos, sin)
        k_all = jnp.concatenate([k_l, k_c], axis=1)
        v_all = jnp.concatenate([v_l, v_c], axis=1)
        y_c = head_rms(latent_diff_attention(q_l, k_all, v_all, lam), da_norm_g[l], lam_init)
        out = merge_branches(z, y_c, lp)
        x = layer_norm(DEEPNORM_ALPHA * x + g1 * out, ln1_g[l], ln1_b[l])
        if not last:
            q_c = split_q(zc[..., OFF_Q:OFF_K])
            yc_c = head_rms(diff_softmax_attend(q_c, k_c, v_c, lam), da_norm_g[l], lam_init)
            out_c = merge_branches(zc, yc_c, lp)
            xc = layer_norm(DEEPNORM_ALPHA * xc + cg1 * out_c, ln1_g[l], ln1_b[l])

        h2 = modulate(x, sh2, sc2).reshape(B * L, D)
        if last:
            y = hierarchical_moe(h2, moe_wg[l], moe_bg[l], moe_we[l], moe_be[l],
                                 ex_w_gate[l], ex_w_up[l], ex_w_down[l]).reshape(B, L, D)
        else:
            Lc = xc.shape[1]
            h2c = modulate(xc, csh2, csc2).reshape(B * Lc, D)
            y_all = hierarchical_moe(jnp.concatenate([h2, h2c], axis=0), moe_wg[l], moe_bg[l],
                                     moe_we[l], moe_be[l], ex_w_gate[l], ex_w_up[l], ex_w_down[l])
            y = y_all[:B * L].reshape(B, L, D)
            y_c2 = y_all[B * L:].reshape(B, Lc, D)
            xc = layer_norm(DEEPNORM_ALPHA * xc + cg2 * y_c2, ln2_g[l], ln2_b[l])
        x = layer_norm(DEEPNORM_ALPHA * x + g2 * y, ln2_g[l], ln2_b[l])
    return x
```

```python
import functools
import math

import numpy as np
import jax
import jax.numpy as jnp
from jax import lax
from jax.experimental import pallas as pl
from jax.experimental.pallas import tpu as pltpu

F32 = jnp.float32
BF16 = jnp.bfloat16
HIGHEST = lax.Precision.HIGHEST

GRID_W = 64
GM_DIM = 256
GM_GROUPS = 4
GM_CHUNK = 128
HY_DIM = 256
HY_EMB = 33
HY_BANDS = (HY_EMB - 1) // 2
HY_DECAY_FAST = 0.3
HY_DECAY_SLOW = 1.5
HY_DECAY_TARGET = 1e-2
HY_DECAY_SHIFT = 0.05
DA_HEADS = 4
DA_HEAD_DIM = 64
DA_V_DIM = 2 * DA_HEAD_DIM
DA_QK_W = DA_HEADS * 2 * DA_HEAD_DIM
DA_V_W = DA_HEADS * DA_V_DIM
ROPE_BASE = 10000.0
N_BRANCH = 3
OFF_GM = 0
OFF_HY = OFF_GM + 2 * GM_DIM
OFF_Q = OFF_HY + 3 * HY_DIM
OFF_K = OFF_Q + DA_QK_W
OFF_V = OFF_K + DA_QK_W
OFF_GATE = OFF_V + DA_V_W
MOE_GROUPS = 4
MOE_EXPERTS_PER_GROUP = 8
MOE_N_EXPERTS = MOE_GROUPS * MOE_EXPERTS_PER_GROUP
MOE_TOP_K = 2
LN_EPS = 1e-5
LANES = 128
VMEM_LIMIT = 56 * 1024 * 1024


def _cparams(*sem):
    return pltpu.CompilerParams(dimension_semantics=sem, vmem_limit_bytes=VMEM_LIMIT)


def _sigmoid(x):
    return 1.0 / (1.0 + jnp.exp(-x))


def _layer_norm(x, g, b):
    mu = jnp.mean(x, axis=-1, keepdims=True)
    xc = x - mu
    var = jnp.mean(xc * xc, axis=-1, keepdims=True)
    return xc * lax.rsqrt(var + LN_EPS) * g + b


def _gelu_tanh(x):
    return 0.5 * x * (1.0 + jnp.tanh(math.sqrt(2.0 / math.pi) * (x + 0.044715 * (x * x * x))))


def _const_spec(shape):
    nd = len(shape)
    return pl.BlockSpec(shape, lambda *_: (0,) * nd)


def _mod_kernel(c_ref, w_ref, b_ref, o_ref):
    c = c_ref[...]
    s = c * _sigmoid(c)
    o_ref[...] = jnp.dot(s, w_ref[...], precision=HIGHEST, preferred_element_type=F32) + b_ref[...]


def _modulation(c_all, ada_w, ada_b):
    depth, d, n = ada_w.shape
    mp = c_all.shape[0]
    tn = 512
    return pl.pallas_call(
        _mod_kernel,
        grid=(depth, n // tn),
        in_specs=[pl.BlockSpec((mp, d), lambda l, j: (0, 0)),
                  pl.BlockSpec((None, d, tn), lambda l, j: (l, 0, j)),
                  pl.BlockSpec((None, 1, tn), lambda l, j: (l, 0, j))],
        out_specs=pl.BlockSpec((None, mp, tn), lambda l, j: (l, 0, j)),
        out_shape=jax.ShapeDtypeStruct((depth, mp, n), F32),
        compiler_params=_cparams("arbitrary", "arbitrary"),
        name="adaln_mod",
    )(c_all, ada_w, ada_b.reshape(depth, 1, n))


def _rope_tables(rows):
    n_freq = DA_HEAD_DIM // 4
    row = jnp.broadcast_to(jnp.arange(rows)[:, None], (rows, GRID_W)).reshape(-1).astype(F32)
    col = jnp.broadcast_to(jnp.arange(GRID_W)[None, :], (rows, GRID_W)).reshape(-1).astype(F32)
    inv = ROPE_BASE ** (-jnp.arange(n_freq, dtype=F32) / n_freq)
    ang_r = row[:, None] * inv
    ang_c = col[:, None] * inv
    c64 = jnp.concatenate([jnp.cos(ang_r), jnp.cos(ang_r), jnp.cos(ang_c), jnp.cos(ang_c)], axis=-1)
    s64 = jnp.concatenate([-jnp.sin(ang_r), jnp.sin(ang_r), -jnp.sin(ang_c), jnp.sin(ang_c)], axis=-1)
    return jnp.tile(c64, (1, LANES // DA_HEAD_DIM)), jnp.tile(s64, (1, LANES // DA_HEAD_DIM))


def _rope_block(xb, cos, sin):
    lane = lax.broadcasted_iota(jnp.int32, xb.shape, 1)
    n_freq = DA_HEAD_DIM // 4
    first_half = (lane % (2 * n_freq)) < n_freq
    partner = jnp.where(first_half, pltpu.roll(xb, LANES - n_freq, 1), pltpu.roll(xb, n_freq, 1))
    return xb * cos + partner * sin


def _inproj_kernel(*refs, segs, use_rope, n_chunk):
    if use_rope:
        x_ref, sh_ref, sc_ref, w_ref, cos_ref, sin_ref = refs[:6]
        out_refs = refs[6:]
    else:
        x_ref, sh_ref, sc_ref, w_ref = refs[:4]
        out_refs = refs[4:]
    h = (x_ref[...] * (1.0 + sc_ref[...]) + sh_ref[...]).astype(BF16)
    for (a, b, kind), o_ref in zip(segs, out_refs):
        for c0 in range(a, b, n_chunk):
            c1 = min(c0 + n_chunk, b)
            acc = jnp.dot(h, w_ref[:, c0:c1], preferred_element_type=F32)
            if kind == "q":
                acc = acc * (DA_HEAD_DIM ** -0.5)
            if use_rope and kind in ("q", "k"):
                cos = cos_ref[...]
                sin = sin_ref[...]
                for j in range((c1 - c0) // LANES):
                    blk = _rope_block(acc[:, j * LANES:(j + 1) * LANES], cos, sin)
                    o_ref[:, c0 - a + j * LANES:c0 - a + (j + 1) * LANES] = blk.astype(o_ref.dtype)
            else:
                o_ref[:, c0 - a:c1 - a] = acc.astype(o_ref.dtype)


def _inproj(x2d, mod3, mod_row0, rows_per_mod, w, segs, rope_tabs, seq_len, tm):
    t, d = x2d.shape
    n = w.shape[1]
    use_rope = rope_tabs is not None
    tiles_per_mod = rows_per_mod // tm
    tiles_per_seq = seq_len // tm

    def mod_map(piece):
        return lambda i: (mod_row0 + i // tiles_per_mod, 0, piece)

    in_specs = [pl.BlockSpec((tm, d), lambda i: (i, 0)),
                pl.BlockSpec((None, 1, d), mod_map(0)),
                pl.BlockSpec((None, 1, d), mod_map(1)),
                pl.BlockSpec((d, n), lambda i: (0, 0), pipeline_mode=pl.Buffered(1))]
    args = [x2d, mod3, mod3, w]
    if use_rope:
        in_specs += [pl.BlockSpec((tm, LANES), lambda i: (i % tiles_per_seq, 0))] * 2
        args += list(rope_tabs)
    out_specs = [pl.BlockSpec((tm, b - a), lambda i: (i, 0)) for a, b, _ in segs]
    out_shape = [jax.ShapeDtypeStruct((t, b - a), BF16) for a, b, _ in segs]
    return pl.pallas_call(
        functools.partial(_inproj_kernel, segs=segs, use_rope=use_rope, n_chunk=512),
        grid=(t // tm,),
        in_specs=in_specs, out_specs=out_specs, out_shape=out_shape,
        compiler_params=_cparams("arbitrary"),
        name="inproj",
    )(*args)


def _attn_kernel(*refs, n_src, lam_init):
    lq1, lk1, lq2, lk2, g_ref, q_ref = refs[:6]
    kv_refs = refs[6:6 + 2 * n_src]
    o_ref = refs[6 + 2 * n_src]
    lam = (jnp.exp(jnp.sum(lq1[...] * lk1[...], axis=-1, keepdims=True))
           - jnp.exp(jnp.sum(lq2[...] * lk2[...], axis=-1, keepdims=True)) + lam_init)
    tq = q_ref.shape[0]
    lane = lax.broadcasted_iota(jnp.int32, (tq, LANES), 1)
    dn = (((1,), (1,)), ((), ()))
    for h in range(DA_HEADS):
        cols = slice(h * LANES, (h + 1) * LANES)
        qh = q_ref[:, cols]
        zero = jnp.zeros_like(qh)
        probs = []
        for m in range(2):
            qm = jnp.where(lane < DA_HEAD_DIM if m == 0 else lane >= DA_HEAD_DIM, qh, zero)
            s = [lax.dot_general(qm, kv_refs[2 * j][:, cols], dn, preferred_element_type=F32)
                 for j in range(n_src)]
            mx = functools.reduce(jnp.maximum, [jnp.max(sj, axis=-1, keepdims=True) for sj in s])
            p = [jnp.exp(sj - mx) for sj in s]
            den = functools.reduce(jnp.add, [jnp.sum(pj, axis=-1, keepdims=True) for pj in p])
            probs.append((p, 1.0 / den))
        o = None
        for j in range(n_src):
            a = probs[0][0][j] * probs[0][1] - (lam * probs[1][1]) * probs[1][0][j]
            oj = jnp.dot(a.astype(BF16), kv_refs[2 * j + 1][:, cols], preferred_element_type=F32)
            o = oj if o is None else o + oj
        ms = jnp.mean(o * o, axis=-1, keepdims=True)
        o = o * lax.rsqrt(ms + LN_EPS) * g_ref[...] * (1.0 - lam_init)
        o_ref[:, cols] = o.astype(o_ref.dtype)


def _attention(q, kvs, lparams, norm_g, lam_init, nb, lq, tq):
    t = q.shape[0]
    qt = lq // tq
    in_specs = [_const_spec((1, DA_HEAD_DIM))] * 4 + [_const_spec((1, DA_V_DIM))]
    in_specs.append(pl.BlockSpec((tq, DA_QK_W), lambda b, i: (b * qt + i, 0)))
    args = list(lparams) + [norm_g, q]
    for k, v, lk in kvs:
        in_specs += [pl.BlockSpec((lk, DA_QK_W), lambda b, i: (b, 0)),
                     pl.BlockSpec((lk, DA_V_W), lambda b, i: (b, 0))]
        args += [k, v]
    return pl.pallas_call(
        functools.partial(_attn_kernel, n_src=len(kvs), lam_init=lam_init),
        grid=(nb, qt),
        in_specs=in_specs,
        out_specs=pl.BlockSpec((tq, DA_V_W), lambda b, i: (b * qt + i, 0)),
        out_shape=jax.ShapeDtypeStruct((t, DA_V_W), BF16),
        compiler_params=_cparams("arbitrary", "arbitrary"),
        name="diff_attn",
    )(*args)


def _dft_tables(L):
    k = jnp.arange(L, dtype=jnp.int32)
    m = (k[:, None] * k[None, :]) % (2 * L)
    ang = m.astype(F32) * (math.pi / L)
    return jnp.cos(ang), jnp.sin(ang)


def _filter_consts(L):
    t = jnp.linspace(0.0, 1.0, L, dtype=F32)[:, None]
    w = 2.0 * math.pi * jnp.arange(L, dtype=F32)[:, None] / L
    f = jnp.linspace(1e-4, HY_BANDS - 1, HY_BANDS, dtype=F32)[None, :]
    emb = jnp.concatenate([t, jnp.cos(f * w), -jnp.sin(f * w)], axis=-1)
    max_decay = math.log(HY_DECAY_TARGET) / HY_DECAY_FAST
    min_decay = math.log(HY_DECAY_TARGET) / HY_DECAY_SLOW
    deltas = jnp.abs(jnp.linspace(min_decay, max_decay, HY_DIM, dtype=F32))
    window = jnp.exp(-t * deltas[None, :]) + HY_DECAY_SHIFT
    return emb, window


def _filter_kernel(emb_ref, win_ref, w1, b1, w2, b2, w3, b3, hs_ref, hd_ref, nyq_ref):
    h = jnp.sin(jnp.dot(emb_ref[...], w1[...], precision=HIGHEST, preferred_element_type=F32) + b1[...])
    h = jnp.sin(jnp.dot(h, w2[...], precision=HIGHEST, preferred_element_type=F32) + b2[...])
    h = jnp.dot(h, w3[...], precision=HIGHEST, preferred_element_type=F32) + b3[...]
    win = win_ref[...]
    hf = h[:, :HY_DIM] * win
    hb = h[:, HY_DIM:] * win
    row = lax.broadcasted_iota(jnp.int32, hf.shape, 0)
    hb = jnp.where(row == 0, 0.0, hb)
    alt = jnp.where(row % 2 == 0, 1.0, -1.0)
    hs_ref[...] = hf + hb
    hd_ref[...] = hf - hb
    nyq_ref[...] = jnp.sum((hf + hb) * alt, axis=0, keepdims=True)


def _spectrum_kernel(c_ref, s_ref, hs_ref, hd_ref, kre_ref, kim_ref, *, n_fft):
    i = pl.program_id(0)
    tk = c_ref.shape[0]
    kidx = i * tk + lax.broadcasted_iota(jnp.int32, (tk, 1), 0)
    scale = jnp.where(kidx == 0, 1.0 / n_fft, 2.0 / n_fft)
    kre = jnp.dot(c_ref[...], hs_ref[...], precision=HIGHEST, preferred_element_type=F32)
    kim = -jnp.dot(s_ref[...], hd_ref[...], precision=HIGHEST, preferred_element_type=F32)
    kre_ref[...] = kre * scale
    kim_ref[...] = kim * scale


def _hyena_filter_spectrum(L, cmat, smat, w1, b1, w2, b2, w3, b3):
    emb, window = _filter_consts(L)
    full = lambda a: _const_spec(a.shape)
    ins = [emb, window, w1, b1.reshape(1, -1), w2, b2.reshape(1, -1), w3, b3.reshape(1, -1)]
    hs, hd, nyq = pl.pallas_call(
        _filter_kernel,
        grid=(1,),
        in_specs=[full(a) for a in ins],
        out_specs=[_const_spec((L, HY_DIM)), _const_spec((L, HY_DIM)), _const_spec((1, HY_DIM))],
        out_shape=[jax.ShapeDtypeStruct((L, HY_DIM), F32), jax.ShapeDtypeStruct((L, HY_DIM), F32),
                   jax.ShapeDtypeStruct((1, HY_DIM), F32)],
        compiler_params=_cparams("arbitrary"),
        name="hyena_filter",
    )(*ins)
    tk = min(256, L)
    kre, kim = pl.pallas_call(
        functools.partial(_spectrum_kernel, n_fft=2 * L),
        grid=(L // tk,),
        in_specs=[pl.BlockSpec((tk, L), lambda i: (i, 0)), pl.BlockSpec((tk, L), lambda i: (i, 0)),
                  _const_spec((L, HY_DIM)), _const_spec((L, HY_DIM))],
        out_specs=[pl.BlockSpec((tk, HY_DIM), lambda i: (i, 0))] * 2,
        out_shape=[jax.ShapeDtypeStruct((L, HY_DIM), F32)] * 2,
        compiler_params=_cparams("arbitrary"),
        name="hyena_spectrum",
    )(cmat, smat, hs, hd)
    return kre, kim, nyq * (1.0 / (2 * L))


def _hyena_kernel(z_ref, cw_ref, cb_ref, c_ref, s_ref, kre_ref, kim_ref, nyq_ref, skip_ref, o_ref):
    L = z_ref.shape[0]
    row = lax.broadcasted_iota(jnp.int32, (L, HY_DIM), 0)

    def conv(j):
        cols = slice(j * HY_DIM, (j + 1) * HY_DIM)
        z = z_ref[:, cols].astype(F32)
        zprev = jnp.where(row == 0, 0.0, pltpu.roll(z, 1, 0))
        znext = jnp.where(row == L - 1, 0.0, pltpu.roll(z, L - 1, 0))
        return zprev * cw_ref[0:1, cols] + z * cw_ref[1:2, cols] + znext * cw_ref[2:3, cols] + cb_ref[:, cols]

    u = conv(2) * conv(1)
    ub = u.astype(BF16)
    a = jnp.dot(c_ref[...], ub, preferred_element_type=F32)
    b = jnp.dot(s_ref[...], ub, preferred_element_type=F32)
    kre = kre_ref[...]
    kim = kim_ref[...]
    p = (a * kre + b * kim).astype(BF16)
    q = (b * kre - a * kim).astype(BF16)
    y = jnp.dot(c_ref[...], p, preferred_element_type=F32) + jnp.dot(s_ref[...], q, preferred_element_type=F32)
    alt = jnp.where(row % 2 == 0, 1.0, -1.0)
    vnyq = jnp.sum(u * alt, axis=0, keepdims=True)
    y = y + alt * (vnyq * nyq_ref[...])
    y = y + u * skip_ref[...]
    o_ref[...] = (y * conv(0)).astype(o_ref.dtype)


def _hyena(zhy, nb, L, conv_w, conv_b, cmat_bf, smat_bf, kre, kim, nyq, skip):
    t = zhy.shape[0]
    return pl.pallas_call(
        _hyena_kernel,
        grid=(nb,),
        in_specs=[pl.BlockSpec((L, 3 * HY_DIM), lambda b: (b, 0)),
                  _const_spec((3, 3 * HY_DIM)), _const_spec((1, 3 * HY_DIM)),
                  pl.BlockSpec((L, L), lambda b: (0, 0), pipeline_mode=pl.Buffered(1)),
                  pl.BlockSpec((L, L), lambda b: (0, 0), pipeline_mode=pl.Buffered(1)),
                  _const_spec((L, HY_DIM)), _const_spec((L, HY_DIM)),
                  _const_spec((1, HY_DIM)), _const_spec((1, HY_DIM))],
        out_specs=pl.BlockSpec((L, HY_DIM), lambda b: (b, 0)),
        out_shape=jax.ShapeDtypeStruct((t, HY_DIM), BF16),
        compiler_params=_cparams("arbitrary"),
        name="hyena_conv",
    )(zhy, conv_w, conv_b.reshape(1, -1), cmat_bf, smat_bf, kre, kim, nyq, skip.reshape(1, -1))


def _merge_kernel(zgm_ref, yb_ref, yc_ref, gate_ref, x_ref, g1_ref, sh2_ref, sc2_ref,
                  lng_ref, lnb_ref, ws_ref, bs_ref, pa_ref, pb_ref, pc_ref, wo_ref,
                  l1g_ref, l1b_ref, wr_ref, br_ref, x1_ref, h2_ref, lg_ref, *, alpha):
    tm = x_ref.shape[0]
    d = x_ref.shape[1]
    gm = _gelu_tanh(zgm_ref[...].astype(F32))
    u = gm[:, :GM_DIM]
    v = _layer_norm(gm[:, GM_DIM:], lng_ref[...], lnb_ref[...]).astype(BF16)
    lane_group = lax.broadcasted_iota(jnp.int32, (GM_CHUNK, GM_DIM), 1) // (GM_DIM // GM_GROUPS)
    ya = []
    for cidx in range(tm // GM_CHUNK):
        rows = slice(cidx * GM_CHUNK, (cidx + 1) * GM_CHUNK)
        r = jnp.dot(ws_ref[...], v[rows], preferred_element_type=F32)
        vv = bs_ref[...]
        for g in range(GM_GROUPS):
            vv = vv + jnp.where(lane_group == g, r[g * GM_CHUNK:(g + 1) * GM_CHUNK], 0.0)
        ya.append(u[rows] * vv)
    ya = jnp.concatenate(ya, axis=0) if len(ya) > 1 else ya[0]
    ma = jnp.dot(ya.astype(BF16), pa_ref[...], preferred_element_type=F32)
    mb = jnp.dot(yb_ref[...], pb_ref[...], preferred_element_type=F32)
    mc = jnp.dot(yc_ref[...], pc_ref[...], preferred_element_type=F32)
    merged = (_sigmoid(gate_ref[:, 0:d].astype(F32)) * ma
              + _sigmoid(gate_ref[:, d:2 * d].astype(F32)) * mb
              + _sigmoid(gate_ref[:, 2 * d:3 * d].astype(F32)) * mc)
    out = jnp.dot(merged.astype(BF16), wo_ref[...], preferred_element_type=F32)
    x1 = _layer_norm(alpha * x_ref[...] + g1_ref[...] * out, l1g_ref[...], l1b_ref[...])
    x1_ref[...] = x1
    h2 = x1 * (1.0 + sc2_ref[...]) + sh2_ref[...]
    h2_ref[...] = h2
    lg_ref[...] = jnp.dot(h2, wr_ref[...], precision=HIGHEST, preferred_element_type=F32) + br_ref[...]


def _merge(zgm, yb, yc, gate, x2d, mod3, mod_row0, rows_per_mod, lp, alpha, tm):
    t, d = x2d.shape
    tiles_per_mod = rows_per_mod // tm

    def mod_map(piece):
        return lambda i: (mod_row0 + i // tiles_per_mod, 0, piece)

    row = lambda w: pl.BlockSpec((tm, w), lambda i: (i, 0))
    consts = [lp["gm_ln_g"], lp["gm_ln_b"], lp["gm_ws"], lp["gm_bs"], lp["p_a"], lp["p_b"], lp["p_c"],
              lp["w_out"], lp["ln1_g"], lp["ln1_b"], lp["w_router"], lp["b_router"]]
    in_specs = [row(2 * GM_DIM), row(HY_DIM), row(DA_V_W), row(N_BRANCH * d), row(d),
                pl.BlockSpec((None, 1, d), mod_map(2)), pl.BlockSpec((None, 1, d), mod_map(3)),
                pl.BlockSpec((None, 1, d), mod_map(4))] + [_const_spec(a.shape) for a in consts]
    return pl.pallas_call(
        functools.partial(_merge_kernel, alpha=alpha),
        grid=(t // tm,),
        in_specs=in_specs,
        out_specs=[row(d), row(d), row(LANES)],
        out_shape=[jax.ShapeDtypeStruct((t, d), F32), jax.ShapeDtypeStruct((t, d), F32),
                   jax.ShapeDtypeStruct((t, LANES), F32)],
        compiler_params=_cparams("arbitrary"),
        name="merge_ln1",
    )(zgm, yb, yc, gate, x2d, mod3, mod3, mod3, *consts)


SUBLANES = 8


def _row_copy(src, src_row, dst, dst_row, sem):
    return pltpu.make_async_copy(src.at[pl.ds(src_row, 1)], dst.at[pl.ds(dst_row, 1)], sem)


def _wait_rows(src, dst, sem, n):
    n_al = pl.multiple_of((n // SUBLANES) * SUBLANES, SUBLANES)

    @pl.when(n_al > 0)
    def _():
        pltpu.make_async_copy(src.at[pl.ds(0, n_al)], dst.at[pl.ds(0, n_al)], sem).wait()

    def one(r, carry):
        _row_copy(src, 0, dst, 0, sem).wait()
        return carry

    lax.fori_loop(n_al, n, one, 0)


def _expert_kernel(te_ref, nv_ref, src_ref, dst_ref, w_ref, h2_hbm, wg_ref, wu_ref, wd_ref,
                   y_hbm, xbuf, ybuf, gsem, ssem):
    i = pl.program_id(0)
    nv = nv_ref[i]
    tm = xbuf.shape[0]

    @pl.when(nv > 0)
    def _():
        @pl.when(nv < tm)
        def _():
            xbuf[...] = jnp.zeros_like(xbuf)

        def gather(r, carry):
            _row_copy(h2_hbm, src_ref[0, r], xbuf, r, gsem).start()
            return carry

        lax.fori_loop(0, nv, gather, 0)
        _wait_rows(h2_hbm, xbuf, gsem, nv)
        xb = xbuf[...].astype(BF16)
        g = jnp.dot(xb, wg_ref[...], preferred_element_type=F32)
        u = jnp.dot(xb, wu_ref[...], preferred_element_type=F32)
        hmid = (g * _sigmoid(g) * u).astype(BF16)
        ybuf[...] = jnp.dot(hmid, wd_ref[...], preferred_element_type=F32) * w_ref[...]

        def scatter(r, carry):
            _row_copy(ybuf, r, y_hbm, dst_ref[0, r], ssem).start()
            return carry

        lax.fori_loop(0, nv, scatter, 0)
        _wait_rows(ybuf, y_hbm, ssem, nv)


def _experts(h2, plan, w_gate, w_up, w_down, tm):
    t, d = h2.shape
    tile_e, tile_nv, src, dst, wbuf = plan
    n_tiles = tile_e.shape[0]
    hid = w_gate.shape[-1]
    smem_idx = pl.BlockSpec((None, 1, tm), lambda i, te, nv: (i, 0, 0), memory_space=pltpu.SMEM)
    grid_spec = pltpu.PrefetchScalarGridSpec(
        num_scalar_prefetch=2,
        grid=(n_tiles,),
        in_specs=[smem_idx, smem_idx,
                  pl.BlockSpec((tm, 1), lambda i, te, nv: (i, 0)),
                  pl.BlockSpec(memory_space=pl.ANY),
                  pl.BlockSpec((None, d, hid), lambda i, te, nv: (te[i], 0, 0)),
                  pl.BlockSpec((None, d, hid), lambda i, te, nv: (te[i], 0, 0)),
                  pl.BlockSpec((None, hid, d), lambda i, te, nv: (te[i], 0, 0))],
        out_specs=pl.BlockSpec(memory_space=pl.ANY),
        scratch_shapes=[pltpu.VMEM((tm, d), F32), pltpu.VMEM((tm, d), F32),
                        pltpu.SemaphoreType.DMA(()), pltpu.SemaphoreType.DMA(())],
    )
    return pl.pallas_call(
        _expert_kernel,
        grid_spec=grid_spec,
        out_shape=jax.ShapeDtypeStruct((MOE_TOP_K * t, d), F32),
        compiler_params=_cparams("arbitrary"),
        name="moe_experts",
    )(tile_e, tile_nv, src.reshape(n_tiles, 1, tm), dst.reshape(n_tiles, 1, tm),
      wbuf.reshape(n_tiles * tm, 1), h2, w_gate, w_up, w_down)


def _route_plan(logits, tm):
    t = logits.shape[0]
    a = t * MOE_TOP_K
    g_logits = logits[:, :MOE_GROUPS]
    g_idx = jnp.argmax(g_logits, axis=-1)
    g_prob = jnp.take_along_axis(jax.nn.softmax(g_logits, axis=-1), g_idx[:, None], axis=-1)
    e_logits = logits[:, MOE_GROUPS:MOE_GROUPS + MOE_N_EXPERTS].reshape(t, MOE_GROUPS, MOE_EXPERTS_PER_GROUP)
    e_logits = jnp.take_along_axis(e_logits, g_idx[:, None, None], axis=1)[:, 0]
    top_v, top_i = lax.top_k(e_logits, MOE_TOP_K)
    weights = g_prob * jax.nn.softmax(top_v, axis=-1)
    eid = (g_idx[:, None] * MOE_EXPERTS_PER_GROUP + top_i).reshape(-1).astype(jnp.int32)
    w_flat = weights.reshape(-1)

    order = jnp.argsort(eid)
    e_s = eid[order]
    counts = jnp.bincount(eid, length=MOE_N_EXPERTS)
    starts = jnp.cumsum(counts) - counts
    pcounts = (counts + tm - 1) // tm * tm
    pends = jnp.cumsum(pcounts)
    pstarts = pends - pcounts
    dest = pstarts[e_s] + jnp.arange(a) - starts[e_s]
    n_tiles = -(-a // tm) + MOE_N_EXPERTS
    p = n_tiles * tm
    tok = (order // MOE_TOP_K).astype(jnp.int32)
    slot = (order % MOE_TOP_K).astype(jnp.int32)
    src = jnp.zeros((p,), jnp.int32).at[dest].set(tok)
    dst = jnp.zeros((p,), jnp.int32).at[dest].set(slot * t + tok)
    wbuf = jnp.zeros((p,), F32).at[dest].set(w_flat[order])
    tile_start = jnp.arange(n_tiles) * tm
    tile_e = jnp.minimum(jnp.searchsorted(pends, tile_start, side="right"), MOE_N_EXPERTS - 1)
    tile_nv = jnp.clip(counts[tile_e] - (tile_start - pstarts[tile_e]), 0, tm)
    return tile_e.astype(jnp.int32), tile_nv.astype(jnp.int32), src, dst, wbuf


def _combine_kernel(x_ref, y0_ref, y1_ref, g2_ref, lg_ref, lb_ref, o_ref, *, alpha):
    y = y0_ref[...] + y1_ref[...]
    o_ref[...] = _layer_norm(alpha * x_ref[...] + g2_ref[...] * y, lg_ref[...], lb_ref[...])


def _combine(x1, y2, row0, t_all, mod3, mod_row0, rows_per_mod, ln_g, ln_b, alpha, tm):
    t, d = x1.shape
    tiles_per_mod = rows_per_mod // tm
    t0 = row0 // tm
    t1 = (t_all + row0) // tm
    return pl.pallas_call(
        functools.partial(_combine_kernel, alpha=alpha),
        grid=(t // tm,),
        in_specs=[pl.BlockSpec((tm, d), lambda i: (i, 0)),
                  pl.BlockSpec((tm, d), lambda i: (t0 + i, 0)),
                  pl.BlockSpec((tm, d), lambda i: (t1 + i, 0)),
                  pl.BlockSpec((None, 1, d), lambda i: (mod_row0 + i // tiles_per_mod, 0, 5)),
                  _const_spec((1, d)), _const_spec((1, d))],
        out_specs=pl.BlockSpec((tm, d), lambda i: (i, 0)),
        out_shape=jax.ShapeDtypeStruct((t, d), F32),
        compiler_params=_cparams("arbitrary"),
        name="combine_ln2",
    )(x1, y2, y2, mod3, ln_g.reshape(1, d), ln_b.reshape(1, d))


def _pick_tile(n, pref):
    tm = min(pref, n)
    while n % tm:
        tm //= 2
    return tm


def kernel(x, c, ctx, c_ctx, ada_w, ada_b, w_in, gm_ln_g, gm_ln_b, gm_ws, gm_bs, hy_conv_w, hy_conv_b,
           hy_f_w1, hy_f_b1, hy_f_w2, hy_f_b2, hy_f_w3, hy_f_b3, hy_skip, da_lq1, da_lk1, da_lq2, da_lk2,
           da_norm_g, p_a, p_b, p_c, w_out, ln1_g, ln1_b, moe_wg, moe_bg, moe_we, moe_be,
           ex_w_gate, ex_w_up, ex_w_down, ln2_g, ln2_b):
    B, L, D = x.shape
    Lc = ctx.shape[1]
    depth = ada_w.shape[0]
    alpha = (2.0 * depth) ** 0.25
    T, Tc = B * L, B * Lc
    moe_tm = 512 if T >= 8192 else 64

    mp = -(-(B + 1) // 8) * 8
    c_all = jnp.zeros((mp, D), F32).at[:B].set(c).at[B].set(c_ctx)
    mod = _modulation(c_all, ada_w, ada_b)

    rope_tabs = _rope_tables(L // GRID_W)
    cm, sm = _dft_tables(L)
    cm_bf, sm_bf = cm.astype(BF16), sm.astype(BF16)
    cmc, smc = _dft_tables(Lc)
    cmc_bf, smc_bf = cmc.astype(BF16), smc.astype(BF16)

    seg_all = ((OFF_GM, OFF_HY, "gm"), (OFF_HY, OFF_Q, "hy"), (OFF_Q, OFF_K, "q"), (OFF_K, OFF_V, "k"),
               (OFF_V, OFF_GATE, "v"), (OFF_GATE, OFF_GATE + N_BRANCH * D, "gate"))
    seg_kv = ((0, DA_QK_W, "k"), (DA_QK_W, DA_QK_W + DA_V_W, "v"))

    tm_l = _pick_tile(L, 256)
    tm_c = _pick_tile(Lc, 256)
    tq_l = _pick_tile(L, 256)
    tq_c = _pick_tile(Lc, 256)

    xs = x.reshape(T, D)
    xc = ctx.reshape(Tc, D)
    for l in range(depth):
        last = l == depth - 1
        lam_init = 0.8 - 0.6 * math.exp(-0.3 * l)
        mod3 = mod[l].reshape(mp, 1, 6 * D)
        w_l = w_in[l].astype(BF16)
        lparams = [a[l].reshape(1, DA_HEAD_DIM) for a in (da_lq1, da_lk1, da_lq2, da_lk2)]
        norm_g = da_norm_g[l].reshape(1, DA_V_DIM)
        lp = {
            "gm_ln_g": gm_ln_g[l].reshape(1, GM_DIM), "gm_ln_b": gm_ln_b[l].reshape(1, GM_DIM),
            "gm_ws": gm_ws[l].reshape(GM_GROUPS * GM_CHUNK, GM_CHUNK).astype(BF16),
            "gm_bs": jnp.repeat(jnp.transpose(gm_bs[l]), GM_DIM // GM_GROUPS, axis=1),
            "p_a": p_a[l].astype(BF16), "p_b": p_b[l].astype(BF16), "p_c": p_c[l].astype(BF16),
            "w_out": w_out[l].astype(BF16),
            "ln1_g": ln1_g[l].reshape(1, D), "ln1_b": ln1_b[l].reshape(1, D),
            "w_router": jnp.zeros((D, LANES), F32).at[:, :MOE_GROUPS].set(moe_wg[l])
                           .at[:, MOE_GROUPS:MOE_GROUPS + MOE_N_EXPERTS].set(moe_we[l]),
            "b_router": jnp.zeros((1, LANES), F32).at[0, :MOE_GROUPS].set(moe_bg[l])
                           .at[0, MOE_GROUPS:MOE_GROUPS + MOE_N_EXPERTS].set(moe_be[l]),
        }
        fw = (hy_f_w1[l], hy_f_b1[l], hy_f_w2[l], hy_f_b2[l], hy_f_w3[l], hy_f_b3[l])

        zgm, zhy, q, k, v, gate = _inproj(xs, mod3, 0, L, w_l, seg_all, rope_tabs, L, tm_l)
        if last:
            k_c, v_c = _inproj(xc, mod3, B, Tc, w_l[:, OFF_K:OFF_GATE], seg_kv, None, Lc, tm_c)
        else:
            zgm_c, zhy_c, q_c, k_c, v_c, gate_c = _inproj(xc, mod3, B, Tc, w_l, seg_all, None, Lc, tm_c)
        y_c = _attention(q, [(k, v, L), (k_c, v_c, Lc)], lparams, norm_g, lam_init, B, L, tq_l)
        kre, kim, nyq = _hyena_filter_spectrum(L, cm, sm, *fw)
        y_b = _hyena(zhy, B, L, hy_conv_w[l], hy_conv_b[l], cm_bf, sm_bf, kre, kim, nyq, hy_skip[l])
        x1, h2, logits = _merge(zgm, y_b, y_c, gate, xs, mod3, 0, L, lp, alpha, tm_l)

        if not last:
            yc_c = _attention(q_c, [(k_c, v_c, Lc)], lparams, norm_g, lam_init, B, Lc, tq_c)
            kre_c, kim_c, nyq_c = _hyena_filter_spectrum(Lc, cmc, smc, *fw)
            yb_c = _hyena(zhy_c, B, Lc, hy_conv_w[l], hy_conv_b[l], cmc_bf, smc_bf, kre_c, kim_c, nyq_c,
                          hy_skip[l])
            x1c, h2c, logits_c = _merge(zgm_c, yb_c, yc_c, gate_c, xc, mod3, B, Tc, lp, alpha, tm_c)
            h2 = jnp.concatenate([h2, h2c], axis=0)
            logits = jnp.concatenate([logits, logits_c], axis=0)

        t_all = h2.shape[0]
        plan = _route_plan(logits, moe_tm)
        y2 = _experts(h2, plan, ex_w_gate[l].astype(BF16), ex_w_up[l].astype(BF16),
                      ex_w_down[l].astype(BF16), moe_tm)
        xs = _combine(x1, y2, 0, t_all, mod3, 0, L, ln2_g[l], ln2_b[l], alpha, tm_l)
        if not last:
            xc = _combine(x1c, y2, T, t_all, mod3, B, Tc, ln2_g[l], ln2_b[l], alpha, tm_c)
    return xs.reshape(B, L, D)
```

```python
import functools
import math

import numpy as np
import jax
import jax.numpy as jnp
from jax import lax
from jax.experimental import pallas as pl
from jax.experimental.pallas import tpu as pltpu

F32 = jnp.float32
BF16 = jnp.bfloat16
HIGHEST = lax.Precision.HIGHEST

GRID_W = 64
GM_DIM = 256
GM_GROUPS = 4
GM_CHUNK = 128
HY_DIM = 256
HY_EMB = 33
HY_BANDS = (HY_EMB - 1) // 2
HY_DECAY_FAST = 0.3
HY_DECAY_SLOW = 1.5
HY_DECAY_TARGET = 1e-2
HY_DECAY_SHIFT = 0.05
DA_HEADS = 4
DA_HEAD_DIM = 64
DA_V_DIM = 2 * DA_HEAD_DIM
DA_QK_W = DA_HEADS * 2 * DA_HEAD_DIM
DA_V_W = DA_HEADS * DA_V_DIM
ROPE_BASE = 10000.0
N_BRANCH = 3
OFF_GM = 0
OFF_HY = OFF_GM + 2 * GM_DIM
OFF_Q = OFF_HY + 3 * HY_DIM
OFF_K = OFF_Q + DA_QK_W
OFF_V = OFF_K + DA_QK_W
OFF_GATE = OFF_V + DA_V_W
MOE_GROUPS = 4
MOE_EXPERTS_PER_GROUP = 8
MOE_N_EXPERTS = MOE_GROUPS * MOE_EXPERTS_PER_GROUP
MOE_TOP_K = 2
LN_EPS = 1e-5
LANES = 128
VMEM_LIMIT = 56 * 1024 * 1024


def _cparams(*sem):
    return pltpu.CompilerParams(dimension_semantics=sem, vmem_limit_bytes=VMEM_LIMIT)


def _sigmoid(x):
    return 1.0 / (1.0 + jnp.exp(-x))


def _layer_norm(x, g, b):
    mu = jnp.mean(x, axis=-1, keepdims=True)
    xc = x - mu
    var = jnp.mean(xc * xc, axis=-1, keepdims=True)
    return xc * lax.rsqrt(var + LN_EPS) * g + b


def _gelu_tanh(x):
    return 0.5 * x * (1.0 + jnp.tanh(math.sqrt(2.0 / math.pi) * (x + 0.044715 * (x * x * x))))


def _const_spec(shape):
    nd = len(shape)
    return pl.BlockSpec(shape, lambda *_: (0,) * nd)


def _mod_kernel(c_ref, w_ref, b_ref, o_ref):
    c = c_ref[...]
    s = c * _sigmoid(c)
    o_ref[...] = jnp.dot(s, w_ref[...], precision=HIGHEST, preferred_element_type=F32) + b_ref[...]


def _modulation(c_all, ada_w, ada_b):
    depth, d, n = ada_w.shape
    mp = c_all.shape[0]
    tn = 512
    return pl.pallas_call(
        _mod_kernel,
        grid=(depth, n // tn),
        in_specs=[pl.BlockSpec((mp, d), lambda l, j: (0, 0)),
                  pl.BlockSpec((None, d, tn), lambda l, j: (l, 0, j)),
                  pl.BlockSpec((None, 1, tn), lambda l, j: (l, 0, j))],
        out_specs=pl.BlockSpec((None, mp, tn), lambda l, j: (l, 0, j)),
        out_shape=jax.ShapeDtypeStruct((depth, mp, n), F32),
        compiler_params=_cparams("arbitrary", "arbitrary"),
        name="adaln_mod",
    )(c_all, ada_w, ada_b.reshape(depth, 1, n))


def _rope_tables(rows):
    n_freq = DA_HEAD_DIM // 4
    row = jnp.broadcast_to(jnp.arange(rows)[:, None], (rows, GRID_W)).reshape(-1).astype(F32)
    col = jnp.broadcast_to(jnp.arange(GRID_W)[None, :], (rows, GRID_W)).reshape(-1).astype(F32)
    inv = ROPE_BASE ** (-jnp.arange(n_freq, dtype=F32) / n_freq)
    ang_r = row[:, None] * inv
    ang_c = col[:, None] * inv
    c64 = jnp.concatenate([jnp.cos(ang_r), jnp.cos(ang_r), jnp.cos(ang_c), jnp.cos(ang_c)], axis=-1)
    s64 = jnp.concatenate([-jnp.sin(ang_r), jnp.sin(ang_r), -jnp.sin(ang_c), jnp.sin(ang_c)], axis=-1)
    return jnp.tile(c64, (1, LANES // DA_HEAD_DIM)), jnp.tile(s64, (1, LANES // DA_HEAD_DIM))


def _rope_block(xb, cos, sin):
    lane = lax.broadcasted_iota(jnp.int32, xb.shape, 1)
    n_freq = DA_HEAD_DIM // 4
    first_half = (lane % (2 * n_freq)) < n_freq
    partner = jnp.where(first_half, pltpu.roll(xb, LANES - n_freq, 1), pltpu.roll(xb, n_freq, 1))
    return xb * cos + partner * sin


def _inproj_kernel(*refs, segs, use_rope, n_chunk):
    if use_rope:
        x_ref, sh_ref, sc_ref, w_ref, cos_ref, sin_ref = refs[:6]
        out_refs = refs[6:]
    else:
        x_ref, sh_ref, sc_ref, w_ref = refs[:4]
        out_refs = refs[4:]
    h = (x_ref[...] * (1.0 + sc_ref[...]) + sh_ref[...]).astype(BF16)
    for (a, b, kind), o_ref in zip(segs, out_refs):
        for c0 in range(a, b, n_chunk):
            c1 = min(c0 + n_chunk, b)
            acc = jnp.dot(h, w_ref[:, c0:c1], preferred_element_type=F32)
            if kind == "q":
                acc = acc * (DA_HEAD_DIM ** -0.5)
            if use_rope and kind in ("q", "k"):
                cos = cos_ref[...]
                sin = sin_ref[...]
                for j in range((c1 - c0) // LANES):
                    blk = _rope_block(acc[:, j * LANES:(j + 1) * LANES], cos, sin)
                    o_ref[:, c0 - a + j * LANES:c0 - a + (j + 1) * LANES] = blk.astype(o_ref.dtype)
            else:
                o_ref[:, c0 - a:c1 - a] = acc.astype(o_ref.dtype)


def _inproj(x2d, mod3, mod_row0, rows_per_mod, w, segs, rope_tabs, seq_len, tm):
    t, d = x2d.shape
    n = w.shape[1]
    use_rope = rope_tabs is not None
    tiles_per_mod = rows_per_mod // tm
    tiles_per_seq = seq_len // tm

    def mod_map(piece):
        return lambda i: (mod_row0 + i // tiles_per_mod, 0, piece)

    in_specs = [pl.BlockSpec((tm, d), lambda i: (i, 0)),
                pl.BlockSpec((None, 1, d), mod_map(0)),
                pl.BlockSpec((None, 1, d), mod_map(1)),
                pl.BlockSpec((d, n), lambda i: (0, 0), pipeline_mode=pl.Buffered(1))]
    args = [x2d, mod3, mod3, w]
    if use_rope:
        in_specs += [pl.BlockSpec((tm, LANES), lambda i: (i % tiles_per_seq, 0))] * 2
        args += list(rope_tabs)
    out_specs = [pl.BlockSpec((tm, b - a), lambda i: (i, 0)) for a, b, _ in segs]
    out_shape = [jax.ShapeDtypeStruct((t, b - a), BF16) for a, b, _ in segs]
    return pl.pallas_call(
        functools.partial(_inproj_kernel, segs=segs, use_rope=use_rope, n_chunk=512),
        grid=(t // tm,),
        in_specs=in_specs, out_specs=out_specs, out_shape=out_shape,
        compiler_params=_cparams("arbitrary"),
        name="inproj",
    )(*args)


def _attn_kernel(*refs, n_src, lam_init):
    lq1, lk1, lq2, lk2, g_ref, q_ref = refs[:6]
    kv_refs = refs[6:6 + 2 * n_src]
    o_ref = refs[6 + 2 * n_src]
    lam = (jnp.exp(jnp.sum(lq1[...] * lk1[...], axis=-1, keepdims=True))
           - jnp.exp(jnp.sum(lq2[...] * lk2[...], axis=-1, keepdims=True)) + lam_init)
    tq = q_ref.shape[0]
    lane = lax.broadcasted_iota(jnp.int32, (tq, LANES), 1)
    dn = (((1,), (1,)), ((), ()))
    for h in range(DA_HEADS):
        cols = slice(h * LANES, (h + 1) * LANES)
        qh = q_ref[:, cols]
        zero = jnp.zeros_like(qh)
        probs = []
        for m in range(2):
            qm = jnp.where(lane < DA_HEAD_DIM if m == 0 else lane >= DA_HEAD_DIM, qh, zero)
            s = [lax.dot_general(qm, kv_refs[2 * j][:, cols], dn, preferred_element_type=F32)
                 for j in range(n_src)]
            mx = functools.reduce(jnp.maximum, [jnp.max(sj, axis=-1, keepdims=True) for sj in s])
            p = [jnp.exp(sj - mx) for sj in s]
            den = functools.reduce(jnp.add, [jnp.sum(pj, axis=-1, keepdims=True) for pj in p])
            probs.append((p, 1.0 / den))
        o = None
        for j in range(n_src):
            a = probs[0][0][j] * probs[0][1] - (lam * probs[1][1]) * probs[1][0][j]
            oj = jnp.dot(a.astype(BF16), kv_refs[2 * j + 1][:, cols], preferred_element_type=F32)
            o = oj if o is None else o + oj
        ms = jnp.mean(o * o, axis=-1, keepdims=True)
        o = o * lax.rsqrt(ms + LN_EPS) * g_ref[...] * (1.0 - lam_init)
        o_ref[:, cols] = o.astype(o_ref.dtype)


def _attention(q, kvs, lparams, norm_g, lam_init, nb, lq, tq):
    t = q.shape[0]
    qt = lq // tq
    in_specs = [_const_spec((1, DA_HEAD_DIM))] * 4 + [_const_spec((1, DA_V_DIM))]
    in_specs.append(pl.BlockSpec((tq, DA_QK_W), lambda b, i: (b * qt + i, 0)))
    args = list(lparams) + [norm_g, q]
    for k, v, lk in kvs:
        in_specs += [pl.BlockSpec((lk, DA_QK_W), lambda b, i: (b, 0)),
                     pl.BlockSpec((lk, DA_V_W), lambda b, i: (b, 0))]
        args += [k, v]
    return pl.pallas_call(
        functools.partial(_attn_kernel, n_src=len(kvs), lam_init=lam_init),
        grid=(nb, qt),
        in_specs=in_specs,
        out_specs=pl.BlockSpec((tq, DA_V_W), lambda b, i: (b * qt + i, 0)),
        out_shape=jax.ShapeDtypeStruct((t, DA_V_W), BF16),
        compiler_params=_cparams("arbitrary", "arbitrary"),
        name="diff_attn",
    )(*args)


def _dft_tables(L):
    k = jnp.arange(L, dtype=jnp.int32)
    m = (k[:, None] * k[None, :]) % (2 * L)
    ang = m.astype(F32) * (math.pi / L)
    return jnp.cos(ang), jnp.sin(ang)


def _filter_consts(L):
    t = jnp.linspace(0.0, 1.0, L, dtype=F32)[:, None]
    w = 2.0 * math.pi * jnp.arange(L, dtype=F32)[:, None] / L
    f = jnp.linspace(1e-4, HY_BANDS - 1, HY_BANDS, dtype=F32)[None, :]
    emb = jnp.concatenate([t, jnp.cos(f * w), -jnp.sin(f * w)], axis=-1)
    max_decay = math.log(HY_DECAY_TARGET) / HY_DECAY_FAST
    min_decay = math.log(HY_DECAY_TARGET) / HY_DECAY_SLOW
    deltas = jnp.abs(jnp.linspace(min_decay, max_decay, HY_DIM, dtype=F32))
    window = jnp.exp(-t * deltas[None, :]) + HY_DECAY_SHIFT
    return emb, window


def _filter_kernel(emb_ref, win_ref, w1, b1, w2, b2, w3, b3, hs_ref, hd_ref, nyq_ref):
    h = jnp.sin(jnp.dot(emb_ref[...], w1[...], precision=HIGHEST, preferred_element_type=F32) + b1[...])
    h = jnp.sin(jnp.dot(h, w2[...], precision=HIGHEST, preferred_element_type=F32) + b2[...])
    h = jnp.dot(h, w3[...], precision=HIGHEST, preferred_element_type=F32) + b3[...]
    win = win_ref[...]
    hf = h[:, :HY_DIM] * win
    hb = h[:, HY_DIM:] * win
    row = lax.broadcasted_iota(jnp.int32, hf.shape, 0)
    hb = jnp.where(row == 0, 0.0, hb)
    alt = jnp.where(row % 2 == 0, 1.0, -1.0)
    hs_ref[...] = hf + hb
    hd_ref[...] = hf - hb
    nyq_ref[...] = jnp.sum((hf + hb) * alt, axis=0, keepdims=True)


def _spectrum_kernel(c_ref, s_ref, hs_ref, hd_ref, kre_ref, kim_ref, *, n_fft):
    i = pl.program_id(0)
    tk = c_ref.shape[0]
    kidx = i * tk + lax.broadcasted_iota(jnp.int32, (tk, 1), 0)
    scale = jnp.where(kidx == 0, 1.0 / n_fft, 2.0 / n_fft)
    kre = jnp.dot(c_ref[...], hs_ref[...], precision=HIGHEST, preferred_element_type=F32)
    kim = -jnp.dot(s_ref[...], hd_ref[...], precision=HIGHEST, preferred_element_type=F32)
    kre_ref[...] = kre * scale
    kim_ref[...] = kim * scale


def _hyena_filter_spectrum(L, cmat, smat, w1, b1, w2, b2, w3, b3):
    emb, window = _filter_consts(L)
    full = lambda a: _const_spec(a.shape)
    ins = [emb, window, w1, b1.reshape(1, -1), w2, b2.reshape(1, -1), w3, b3.reshape(1, -1)]
    hs, hd, nyq = pl.pallas_call(
        _filter_kernel,
        grid=(1,),
        in_specs=[full(a) for a in ins],
        out_specs=[_const_spec((L, HY_DIM)), _const_spec((L, HY_DIM)), _const_spec((1, HY_DIM))],
        out_shape=[jax.ShapeDtypeStruct((L, HY_DIM), F32), jax.ShapeDtypeStruct((L, HY_DIM), F32),
                   jax.ShapeDtypeStruct((1, HY_DIM), F32)],
        compiler_params=_cparams("arbitrary"),
        name="hyena_filter",
    )(*ins)
    tk = min(256, L)
    kre, kim = pl.pallas_call(
        functools.partial(_spectrum_kernel, n_fft=2 * L),
        grid=(L // tk,),
        in_specs=[pl.BlockSpec((tk, L), lambda i: (i, 0)), pl.BlockSpec((tk, L), lambda i: (i, 0)),
                  _const_spec((L, HY_DIM)), _const_spec((L, HY_DIM))],
        out_specs=[pl.BlockSpec((tk, HY_DIM), lambda i: (i, 0))] * 2,
        out_shape=[jax.ShapeDtypeStruct((L, HY_DIM), F32)] * 2,
        compiler_params=_cparams("arbitrary"),
        name="hyena_spectrum",
    )(cmat, smat, hs, hd)
    return kre, kim, nyq * (1.0 / (2 * L))


def _hyena_kernel(z_ref, cw_ref, cb_ref, c_ref, s_ref, kre_ref, kim_ref, nyq_ref, skip_ref, o_ref):
    L = z_ref.shape[0]
    row = lax.broadcasted_iota(jnp.int32, (L, HY_DIM), 0)

    def conv(j):
        cols = slice(j * HY_DIM, (j + 1) * HY_DIM)
        z = z_ref[:, cols].astype(F32)
        zprev = jnp.where(row == 0, 0.0, pltpu.roll(z, 1, 0))
        znext = jnp.where(row == L - 1, 0.0, pltpu.roll(z, L - 1, 0))
        return zprev * cw_ref[0:1, cols] + z * cw_ref[1:2, cols] + znext * cw_ref[2:3, cols] + cb_ref[:, cols]

    u = conv(2) * conv(1)
    ub = u.astype(BF16)
    a = jnp.dot(c_ref[...], ub, preferred_element_type=F32)
    b = jnp.dot(s_ref[...], ub, preferred_element_type=F32)
    kre = kre_ref[...]
    kim = kim_ref[...]
    p = (a * kre + b * kim).astype(BF16)
    q = (b * kre - a * kim).astype(BF16)
    y = jnp.dot(c_ref[...], p, preferred_element_type=F32) + jnp.dot(s_ref[...], q, preferred_element_type=F32)
    alt = jnp.where(row % 2 == 0, 1.0, -1.0)
    vnyq = jnp.sum(u * alt, axis=0, keepdims=True)
    y = y + alt * (vnyq * nyq_ref[...])
    y = y + u * skip_ref[...]
    o_ref[...] = (y * conv(0)).astype(o_ref.dtype)


def _hyena(zhy, nb, L, conv_w, conv_b, cmat_bf, smat_bf, kre, kim, nyq, skip):
    t = zhy.shape[0]
    return pl.pallas_call(
        _hyena_kernel,
        grid=(nb,),
        in_specs=[pl.BlockSpec((L, 3 * HY_DIM), lambda b: (b, 0)),
                  _const_spec((3, 3 * HY_DIM)), _const_spec((1, 3 * HY_DIM)),
                  pl.BlockSpec((L, L), lambda b: (0, 0), pipeline_mode=pl.Buffered(1)),
                  pl.BlockSpec((L, L), lambda b: (0, 0), pipeline_mode=pl.Buffered(1)),
                  _const_spec((L, HY_DIM)), _const_spec((L, HY_DIM)),
                  _const_spec((1, HY_DIM)), _const_spec((1, HY_DIM))],
        out_specs=pl.BlockSpec((L, HY_DIM), lambda b: (b, 0)),
        out_shape=jax.ShapeDtypeStruct((t, HY_DIM), BF16),
        compiler_params=_cparams("arbitrary"),
        name="hyena_conv",
    )(zhy, conv_w, conv_b.reshape(1, -1), cmat_bf, smat_bf, kre, kim, nyq, skip.reshape(1, -1))


def _merge_kernel(*refs, alpha, n_alias):
    (zgm_ref, yb_ref, yc_ref, gate_ref, x_ref, g1_ref, sh2_ref, sc2_ref, lng_ref, lnb_ref, ws_ref, bs_ref,
     pa_ref, pb_ref, pc_ref, wo_ref, l1g_ref, l1b_ref, wr_ref, br_ref) = refs[:20]
    x1_ref, h2_ref, rt_ref = refs[20 + n_alias:]
    tm = x_ref.shape[0]
    d = x_ref.shape[1]
    gm = _gelu_tanh(zgm_ref[...].astype(F32))
    u = gm[:, :GM_DIM]
    v = _layer_norm(gm[:, GM_DIM:], lng_ref[...], lnb_ref[...]).astype(BF16)
    lane_group = lax.broadcasted_iota(jnp.int32, (GM_CHUNK, GM_DIM), 1) // (GM_DIM // GM_GROUPS)
    ya = []
    for cidx in range(tm // GM_CHUNK):
        rows = slice(cidx * GM_CHUNK, (cidx + 1) * GM_CHUNK)
        r = jnp.dot(ws_ref[...], v[rows], preferred_element_type=F32)
        vv = bs_ref[...]
        for g in range(GM_GROUPS):
            vv = vv + jnp.where(lane_group == g, r[g * GM_CHUNK:(g + 1) * GM_CHUNK], 0.0)
        ya.append(u[rows] * vv)
    ya = jnp.concatenate(ya, axis=0) if len(ya) > 1 else ya[0]
    ma = jnp.dot(ya.astype(BF16), pa_ref[...], preferred_element_type=F32)
    mb = jnp.dot(yb_ref[...], pb_ref[...], preferred_element_type=F32)
    mc = jnp.dot(yc_ref[...], pc_ref[...], preferred_element_type=F32)
    merged = (_sigmoid(gate_ref[:, 0:d].astype(F32)) * ma
              + _sigmoid(gate_ref[:, d:2 * d].astype(F32)) * mb
              + _sigmoid(gate_ref[:, 2 * d:3 * d].astype(F32)) * mc)
    out = jnp.dot(merged.astype(BF16), wo_ref[...], preferred_element_type=F32)
    x1 = _layer_norm(alpha * x_ref[...] + g1_ref[...] * out, l1g_ref[...], l1b_ref[...])
    x1_ref[...] = x1
    h2 = x1 * (1.0 + sc2_ref[...]) + sh2_ref[...]
    h2_ref[...] = h2
    lg = jnp.dot(h2, wr_ref[...], precision=HIGHEST, preferred_element_type=F32) + br_ref[...]
    rt_ref[...] = _route(lg)


ROUTE_E0, ROUTE_E1, ROUTE_W0, ROUTE_W1 = 0, 1, 2, 3


def _route(lg):
    neg = jnp.float32(-3.0e38)
    lane_i = lax.broadcasted_iota(jnp.int32, lg.shape, 1)
    lane = lane_i.astype(F32)
    big = jnp.float32(LANES)
    is_g = lane_i < MOE_GROUPS
    gl = jnp.where(is_g, lg, neg)
    gmax = jnp.max(gl, axis=-1, keepdims=True)
    g_idx = jnp.min(jnp.where(gl == gmax, lane, big), axis=-1, keepdims=True)
    g_prob = 1.0 / jnp.sum(jnp.where(is_g, jnp.exp(gl - gmax), 0.0), axis=-1, keepdims=True)
    e_lo = MOE_GROUPS + MOE_EXPERTS_PER_GROUP * g_idx
    el = jnp.where(lane >= e_lo, jnp.where(lane < e_lo + MOE_EXPERTS_PER_GROUP, lg, neg), neg)
    v1 = jnp.max(el, axis=-1, keepdims=True)
    i1 = jnp.min(jnp.where(el == v1, lane, big), axis=-1, keepdims=True)
    el2 = jnp.where(lane == i1, neg, el)
    v2 = jnp.max(el2, axis=-1, keepdims=True)
    i2 = jnp.min(jnp.where(el2 == v2, lane, big), axis=-1, keepdims=True)
    e21 = jnp.exp(v2 - v1)
    w1 = g_prob / (1.0 + e21)
    w2 = w1 * e21
    rec = jnp.where(lane_i == ROUTE_E0, i1 - MOE_GROUPS, 0.0)
    rec = jnp.where(lane_i == ROUTE_E1, i2 - MOE_GROUPS, rec)
    rec = jnp.where(lane_i == ROUTE_W0, w1, rec)
    return jnp.where(lane_i == ROUTE_W1, w2, rec)


def _merge(zgm, yb, yc, gate, x2d, mod3, mod_row0, rows_per_mod, lp, alpha, tm, t_all, row0, prev):
    t, d = x2d.shape
    tiles_per_mod = rows_per_mod // tm
    off = row0 // tm

    def mod_map(piece):
        return lambda i: (mod_row0 + i // tiles_per_mod, 0, piece)

    row = lambda w: pl.BlockSpec((tm, w), lambda i: (i, 0))
    row_off = lambda w: pl.BlockSpec((tm, w), lambda i: (off + i, 0))
    consts = [lp["gm_ln_g"], lp["gm_ln_b"], lp["gm_ws"], lp["gm_bs"], lp["p_a"], lp["p_b"], lp["p_c"],
              lp["w_out"], lp["ln1_g"], lp["ln1_b"], lp["w_router"], lp["b_router"]]
    in_specs = [row(2 * GM_DIM), row(HY_DIM), row(DA_V_W), row(N_BRANCH * d), row(d),
                pl.BlockSpec((None, 1, d), mod_map(2)), pl.BlockSpec((None, 1, d), mod_map(3)),
                pl.BlockSpec((None, 1, d), mod_map(4))] + [_const_spec(a.shape) for a in consts]
    args = [zgm, yb, yc, gate, x2d, mod3, mod3, mod3, *consts]
    aliases = {}
    if prev is not None:
        aliases = {len(args): 1, len(args) + 1: 2}
        in_specs += [pl.BlockSpec(memory_space=pl.ANY)] * 2
        args += list(prev)
    return pl.pallas_call(
        functools.partial(_merge_kernel, alpha=alpha, n_alias=len(aliases)),
        grid=(t // tm,),
        in_specs=in_specs,
        out_specs=[row(d), row_off(d), row_off(LANES)],
        out_shape=[jax.ShapeDtypeStruct((t, d), F32), jax.ShapeDtypeStruct((t_all, d), F32),
                   jax.ShapeDtypeStruct((t_all, LANES), F32)],
        input_output_aliases=aliases,
        compiler_params=_cparams("arbitrary"),
        name="merge_ln1",
    )(*args)


SUBLANES = 8


def _row_copy(src, src_row, dst, dst_row, sem):
    return pltpu.make_async_copy(src.at[pl.ds(src_row, 1)], dst.at[pl.ds(dst_row, 1)], sem)


FLAG_FIRST, FLAG_LAST = 1, 2
DMA_UNROLL = 8


def _expert_kernel(vt_ref, ve_ref, vlo_ref, vhi_ref, vflag_ref, src_ref, dst_ref, h2_hbm,
                   wg_ref, wu_ref, wd_ref, y_hbm, xbuf, acc, gsem, ssem):
    v = pl.program_id(0)
    lo, hi, flag = vlo_ref[v], vhi_ref[v], vflag_ref[v]
    tm = xbuf.shape[0]
    first = (flag & FLAG_FIRST) != 0

    @pl.when(first)
    def _():
        def gather(r, carry):
            _row_copy(h2_hbm, src_ref[0, r], xbuf, r, gsem).start()
            return carry

        lax.fori_loop(0, tm, gather, 0, unroll=DMA_UNROLL)
        pltpu.make_async_copy(h2_hbm.at[pl.ds(0, tm)], xbuf, gsem).wait()

    @pl.when(hi > lo)
    def _():
        xb = xbuf[...].astype(BF16)
        g = jnp.dot(xb, wg_ref[...], preferred_element_type=F32)
        u = jnp.dot(xb, wu_ref[...], preferred_element_type=F32)
        hmid = (g * _sigmoid(g) * u).astype(BF16)
        y = jnp.dot(hmid, wd_ref[...], preferred_element_type=F32)
        row = lax.broadcasted_iota(jnp.int32, (tm, 1), 0)
        y = jnp.where((row >= lo) & (row < hi), y, 0.0)

        @pl.when(first)
        def _():
            acc[...] = y

        @pl.when(jnp.logical_not(first))
        def _():
            acc[...] += y

    @pl.when((flag & FLAG_LAST) != 0)
    def _():
        def scatter(r, carry):
            _row_copy(acc, r, y_hbm, dst_ref[0, r], ssem).start()
            return carry

        lax.fori_loop(0, tm, scatter, 0, unroll=DMA_UNROLL)
        pltpu.make_async_copy(acc, y_hbm.at[pl.ds(0, tm)], ssem).wait()


def _experts(h2, plan, w_gate, w_up, w_down, tm):
    t, d = h2.shape
    vt, ve, vlo, vhi, vflag, src, dst = plan
    n_vis = vt.shape[0]
    n_tiles = src.shape[0] // tm
    hid = w_gate.shape[-1]
    smem_idx = pl.BlockSpec((None, 1, tm), lambda v, vt, ve, *_: (vt[v], 0, 0), memory_space=pltpu.SMEM)
    w_spec = lambda shape: pl.BlockSpec((None,) + shape, lambda v, vt, ve, *_: (ve[v], 0, 0))
    grid_spec = pltpu.PrefetchScalarGridSpec(
        num_scalar_prefetch=5,
        grid=(n_vis,),
        in_specs=[smem_idx, smem_idx, pl.BlockSpec(memory_space=pl.ANY),
                  w_spec((d, hid)), w_spec((d, hid)), w_spec((hid, d))],
        out_specs=pl.BlockSpec(memory_space=pl.ANY),
        scratch_shapes=[pltpu.VMEM((tm, d), F32), pltpu.VMEM((tm, d), F32),
                        pltpu.SemaphoreType.DMA(()), pltpu.SemaphoreType.DMA(())],
    )
    return pl.pallas_call(
        _expert_kernel,
        grid_spec=grid_spec,
        out_shape=jax.ShapeDtypeStruct((MOE_TOP_K * t, d), F32),
        compiler_params=_cparams("arbitrary"),
        name="moe_experts",
    )(vt, ve, vlo, vhi, vflag, src.reshape(n_tiles, 1, tm), dst.reshape(n_tiles, 1, tm),
      h2, w_gate, w_up, w_down)


def _visit_plan(route, tm):
    t = route.shape[0]
    a = t * MOE_TOP_K
    i32 = jnp.int32
    eid = route[:, ROUTE_E0:ROUTE_E1 + 1].astype(i32).reshape(a)
    _, order = lax.sort((eid, lax.iota(i32, a)), num_keys=1)
    tok = order // MOE_TOP_K
    src = tok
    dst = (order % MOE_TOP_K) * t + tok
    counts = jnp.sum((eid[:, None] == jnp.arange(MOE_N_EXPERTS, dtype=i32)[None, :]).astype(i32), axis=0)
    ends = jnp.cumsum(counts)
    starts = ends - counts
    n_tiles = a // tm
    first_t = starts // tm
    last_t = jnp.maximum(ends - 1, 0) // tm
    nvis = jnp.where(counts > 0, last_t - first_t + 1, 0)
    cv_end = jnp.cumsum(nvis)
    cv_start = cv_end - nvis
    n_vis = n_tiles + MOE_N_EXPERTS
    v = jnp.arange(n_vis, dtype=i32)
    active = v < cv_end[-1]
    e = jnp.minimum(jnp.sum((cv_end[None, :] <= v[:, None]).astype(i32), axis=1), MOE_N_EXPERTS - 1)
    e_last = jnp.max(jnp.where(counts > 0, jnp.arange(MOE_N_EXPERTS, dtype=i32), 0))
    e = jnp.where(active, e, e_last)
    tile = jnp.where(active, first_t[e] + v - cv_start[e], n_tiles - 1)
    lo = jnp.where(active, jnp.clip(starts[e] - tile * tm, 0, tm), 0)
    hi = jnp.where(active, jnp.clip(ends[e] - tile * tm, 0, tm), 0)
    prev_t = jnp.concatenate([jnp.full((1,), -1, i32), tile[:-1]])
    next_t = jnp.concatenate([tile[1:], jnp.full((1,), -1, i32)])
    is_last = (next_t != tile) | (v == cv_end[-1] - 1)
    flag = jnp.where(active, (prev_t != tile) * FLAG_FIRST + is_last * FLAG_LAST, 0)
    cast = lambda z: z.astype(i32)
    return cast(tile), cast(e), cast(lo), cast(hi), cast(flag), cast(src), cast(dst)


def _combine_kernel(x_ref, y0_ref, y1_ref, rt_ref, g2_ref, lg_ref, lb_ref, o_ref, *, alpha):
    rt = rt_ref[...]
    y = rt[:, ROUTE_W0:ROUTE_W0 + 1] * y0_ref[...] + rt[:, ROUTE_W1:ROUTE_W1 + 1] * y1_ref[...]
    o_ref[...] = _layer_norm(alpha * x_ref[...] + g2_ref[...] * y, lg_ref[...], lb_ref[...])


def _combine(x1, y2, route, row0, t_all, mod3, mod_row0, rows_per_mod, ln_g, ln_b, alpha, tm):
    t, d = x1.shape
    tiles_per_mod = rows_per_mod // tm
    t0 = row0 // tm
    t1 = (t_all + row0) // tm
    return pl.pallas_call(
        functools.partial(_combine_kernel, alpha=alpha),
        grid=(t // tm,),
        in_specs=[pl.BlockSpec((tm, d), lambda i: (i, 0)),
                  pl.BlockSpec((tm, d), lambda i: (t0 + i, 0)),
                  pl.BlockSpec((tm, d), lambda i: (t1 + i, 0)),
                  pl.BlockSpec((tm, LANES), lambda i: (t0 + i, 0)),
                  pl.BlockSpec((None, 1, d), lambda i: (mod_row0 + i // tiles_per_mod, 0, 5)),
                  _const_spec((1, d)), _const_spec((1, d))],
        out_specs=pl.BlockSpec((tm, d), lambda i: (i, 0)),
        out_shape=jax.ShapeDtypeStruct((t, d), F32),
        compiler_params=_cparams("arbitrary"),
        name="combine_ln2",
    )(x1, y2, y2, route, mod3, ln_g.reshape(1, d), ln_b.reshape(1, d))


def _pick_tile(n, pref):
    tm = min(pref, n)
    while n % tm:
        tm //= 2
    return tm


def kernel(x, c, ctx, c_ctx, ada_w, ada_b, w_in, gm_ln_g, gm_ln_b, gm_ws, gm_bs, hy_conv_w, hy_conv_b,
           hy_f_w1, hy_f_b1, hy_f_w2, hy_f_b2, hy_f_w3, hy_f_b3, hy_skip, da_lq1, da_lk1, da_lq2, da_lk2,
           da_norm_g, p_a, p_b, p_c, w_out, ln1_g, ln1_b, moe_wg, moe_bg, moe_we, moe_be,
           ex_w_gate, ex_w_up, ex_w_down, ln2_g, ln2_b):
    B, L, D = x.shape
    Lc = ctx.shape[1]
    depth = ada_w.shape[0]
    alpha = (2.0 * depth) ** 0.25
    T, Tc = B * L, B * Lc
    moe_tm = 512 if T >= 8192 else 64

    mp = -(-(B + 1) // 8) * 8
    c_all = jnp.zeros((mp, D), F32).at[:B].set(c).at[B].set(c_ctx)
    mod = _modulation(c_all, ada_w, ada_b)

    rope_tabs = _rope_tables(L // GRID_W)
    cm, sm = _dft_tables(L)
    cm_bf, sm_bf = cm.astype(BF16), sm.astype(BF16)
    cmc, smc = _dft_tables(Lc)
    cmc_bf, smc_bf = cmc.astype(BF16), smc.astype(BF16)

    seg_all = ((OFF_GM, OFF_HY, "gm"), (OFF_HY, OFF_Q, "hy"), (OFF_Q, OFF_K, "q"), (OFF_K, OFF_V, "k"),
               (OFF_V, OFF_GATE, "v"), (OFF_GATE, OFF_GATE + N_BRANCH * D, "gate"))
    seg_kv = ((0, DA_QK_W, "k"), (DA_QK_W, DA_QK_W + DA_V_W, "v"))

    tm_l = _pick_tile(L, 256)
    tm_c = _pick_tile(Lc, 256)
    tq_l = _pick_tile(L, 256)
    tq_c = _pick_tile(Lc, 256)

    xs = x.reshape(T, D)
    xc = ctx.reshape(Tc, D)
    for l in range(depth):
        last = l == depth - 1
        lam_init = 0.8 - 0.6 * math.exp(-0.3 * l)
        mod3 = mod[l].reshape(mp, 1, 6 * D)
        w_l = w_in[l].astype(BF16)
        lparams = [a[l].reshape(1, DA_HEAD_DIM) for a in (da_lq1, da_lk1, da_lq2, da_lk2)]
        norm_g = da_norm_g[l].reshape(1, DA_V_DIM)
        lp = {
            "gm_ln_g": gm_ln_g[l].reshape(1, GM_DIM), "gm_ln_b": gm_ln_b[l].reshape(1, GM_DIM),
            "gm_ws": gm_ws[l].reshape(GM_GROUPS * GM_CHUNK, GM_CHUNK).astype(BF16),
            "gm_bs": jnp.repeat(jnp.transpose(gm_bs[l]), GM_DIM // GM_GROUPS, axis=1),
            "p_a": p_a[l].astype(BF16), "p_b": p_b[l].astype(BF16), "p_c": p_c[l].astype(BF16),
            "w_out": w_out[l].astype(BF16),
            "ln1_g": ln1_g[l].reshape(1, D), "ln1_b": ln1_b[l].reshape(1, D),
            "w_router": jnp.zeros((D, LANES), F32).at[:, :MOE_GROUPS].set(moe_wg[l])
                           .at[:, MOE_GROUPS:MOE_GROUPS + MOE_N_EXPERTS].set(moe_we[l]),
            "b_router": jnp.zeros((1, LANES), F32).at[0, :MOE_GROUPS].set(moe_bg[l])
                           .at[0, MOE_GROUPS:MOE_GROUPS + MOE_N_EXPERTS].set(moe_be[l]),
        }
        fw = (hy_f_w1[l], hy_f_b1[l], hy_f_w2[l], hy_f_b2[l], hy_f_w3[l], hy_f_b3[l])

        zgm, zhy, q, k, v, gate = _inproj(xs, mod3, 0, L, w_l, seg_all, rope_tabs, L, tm_l)
        if last:
            k_c, v_c = _inproj(xc, mod3, B, Tc, w_l[:, OFF_K:OFF_GATE], seg_kv, None, Lc, tm_c)
        else:
            zgm_c, zhy_c, q_c, k_c, v_c, gate_c = _inproj(xc, mod3, B, Tc, w_l, seg_all, None, Lc, tm_c)
        y_c = _attention(q, [(k, v, L), (k_c, v_c, Lc)], lparams, norm_g, lam_init, B, L, tq_l)
        kre, kim, nyq = _hyena_filter_spectrum(L, cm, sm, *fw)
        y_b = _hyena(zhy, B, L, hy_conv_w[l], hy_conv_b[l], cm_bf, sm_bf, kre, kim, nyq, hy_skip[l])
        t_all = T if last else T + Tc
        x1, h2, route = _merge(zgm, y_b, y_c, gate, xs, mod3, 0, L, lp, alpha, tm_l, T, 0, None)

        if not last:
            yc_c = _attention(q_c, [(k_c, v_c, Lc)], lparams, norm_g, lam_init, B, Lc, tq_c)
            kre_c, kim_c, nyq_c = _hyena_filter_spectrum(Lc, cmc, smc, *fw)
            yb_c = _hyena(zhy_c, B, Lc, hy_conv_w[l], hy_conv_b[l], cmc_bf, smc_bf, kre_c, kim_c, nyq_c,
                          hy_skip[l])
            x1c, h2c, route_c = _merge(zgm_c, yb_c, yc_c, gate_c, xc, mod3, B, Tc, lp, alpha, tm_c, Tc, 0, None)
            h2 = jnp.concatenate([h2, h2c], axis=0)
            route = jnp.concatenate([route, route_c], axis=0)

        plan = _visit_plan(route, moe_tm)
        y2 = _experts(h2, plan, ex_w_gate[l].astype(BF16), ex_w_up[l].astype(BF16),
                      ex_w_down[l].astype(BF16), moe_tm)
        xs = _combine(x1, y2, route, 0, t_all, mod3, 0, L, ln2_g[l], ln2_b[l], alpha, tm_l)
        if not last:
            xc = _combine(x1c, y2, route, T, t_all, mod3, B, Tc, ln2_g[l], ln2_b[l], alpha, tm_c)
    return xs.reshape(B, L, D)
```

```python
import functools
import math

import numpy as np
import jax
import jax.numpy as jnp
from jax import lax
from jax.experimental import pallas as pl
from jax.experimental.pallas import tpu as pltpu

F32 = jnp.float32
BF16 = jnp.bfloat16
HIGHEST = lax.Precision.HIGHEST

GRID_W = 64
GM_DIM = 256
GM_GROUPS = 4
GM_CHUNK = 128
HY_DIM = 256
HY_EMB = 33
HY_BANDS = (HY_EMB - 1) // 2
HY_DECAY_FAST = 0.3
HY_DECAY_SLOW = 1.5
HY_DECAY_TARGET = 1e-2
HY_DECAY_SHIFT = 0.05
DA_HEADS = 4
DA_HEAD_DIM = 64
DA_V_DIM = 2 * DA_HEAD_DIM
DA_QK_W = DA_HEADS * 2 * DA_HEAD_DIM
DA_V_W = DA_HEADS * DA_V_DIM
ROPE_BASE = 10000.0
N_BRANCH = 3
OFF_GM = 0
OFF_HY = OFF_GM + 2 * GM_DIM
OFF_Q = OFF_HY + 3 * HY_DIM
OFF_K = OFF_Q + DA_QK_W
OFF_V = OFF_K + DA_QK_W
OFF_GATE = OFF_V + DA_V_W
MOE_GROUPS = 4
MOE_EXPERTS_PER_GROUP = 8
MOE_N_EXPERTS = MOE_GROUPS * MOE_EXPERTS_PER_GROUP
MOE_TOP_K = 2
LN_EPS = 1e-5
LANES = 128
VMEM_LIMIT = 56 * 1024 * 1024


def _cparams(*sem):
    return pltpu.CompilerParams(dimension_semantics=sem, vmem_limit_bytes=VMEM_LIMIT)


def _sigmoid(x):
    return 1.0 / (1.0 + jnp.exp(-x))


def _layer_norm(x, g, b):
    mu = jnp.mean(x, axis=-1, keepdims=True)
    xc = x - mu
    var = jnp.mean(xc * xc, axis=-1, keepdims=True)
    return xc * lax.rsqrt(var + LN_EPS) * g + b


def _gelu_tanh(x):
    return 0.5 * x * (1.0 + jnp.tanh(math.sqrt(2.0 / math.pi) * (x + 0.044715 * (x * x * x))))


def _const_spec(shape):
    nd = len(shape)
    return pl.BlockSpec(shape, lambda *_: (0,) * nd)


def _mod_kernel(c_ref, w_ref, b_ref, o_ref):
    c = c_ref[...]
    s = c * _sigmoid(c)
    o_ref[...] = jnp.dot(s, w_ref[...], precision=HIGHEST, preferred_element_type=F32) + b_ref[...]


def _modulation(c_all, ada_w, ada_b):
    depth, d, n = ada_w.shape
    mp = c_all.shape[0]
    tn = 512
    return pl.pallas_call(
        _mod_kernel,
        grid=(depth, n // tn),
        in_specs=[pl.BlockSpec((mp, d), lambda l, j: (0, 0)),
                  pl.BlockSpec((None, d, tn), lambda l, j: (l, 0, j)),
                  pl.BlockSpec((None, 1, tn), lambda l, j: (l, 0, j))],
        out_specs=pl.BlockSpec((None, mp, tn), lambda l, j: (l, 0, j)),
        out_shape=jax.ShapeDtypeStruct((depth, mp, n), F32),
        compiler_params=_cparams("arbitrary", "arbitrary"),
        name="adaln_mod",
    )(c_all, ada_w, ada_b.reshape(depth, 1, n))


def _rope_tables(rows):
    n_freq = DA_HEAD_DIM // 4
    row = jnp.broadcast_to(jnp.arange(rows)[:, None], (rows, GRID_W)).reshape(-1).astype(F32)
    col = jnp.broadcast_to(jnp.arange(GRID_W)[None, :], (rows, GRID_W)).reshape(-1).astype(F32)
    inv = ROPE_BASE ** (-jnp.arange(n_freq, dtype=F32) / n_freq)
    ang_r = row[:, None] * inv
    ang_c = col[:, None] * inv
    c64 = jnp.concatenate([jnp.cos(ang_r), jnp.cos(ang_r), jnp.cos(ang_c), jnp.cos(ang_c)], axis=-1)
    s64 = jnp.concatenate([-jnp.sin(ang_r), jnp.sin(ang_r), -jnp.sin(ang_c), jnp.sin(ang_c)], axis=-1)
    return jnp.tile(c64, (1, LANES // DA_HEAD_DIM)), jnp.tile(s64, (1, LANES // DA_HEAD_DIM))


def _rope_block(xb, cos, sin):
    lane = lax.broadcasted_iota(jnp.int32, xb.shape, 1)
    n_freq = DA_HEAD_DIM // 4
    first_half = (lane % (2 * n_freq)) < n_freq
    partner = jnp.where(first_half, pltpu.roll(xb, LANES - n_freq, 1), pltpu.roll(xb, n_freq, 1))
    return xb * cos + partner * sin


def _inproj_kernel(*refs, segs, use_rope, n_chunk):
    if use_rope:
        x_ref, sh_ref, sc_ref, w_ref, cos_ref, sin_ref = refs[:6]
        out_refs = refs[6:]
    else:
        x_ref, sh_ref, sc_ref, w_ref = refs[:4]
        out_refs = refs[4:]
    h = (x_ref[...] * (1.0 + sc_ref[...]) + sh_ref[...]).astype(BF16)
    for (a, b, kind), o_ref in zip(segs, out_refs):
        for c0 in range(a, b, n_chunk):
            c1 = min(c0 + n_chunk, b)
            acc = jnp.dot(h, w_ref[:, c0:c1], preferred_element_type=F32)
            if kind == "q":
                acc = acc * (DA_HEAD_DIM ** -0.5 * math.log2(math.e))
            if use_rope and kind in ("q", "k"):
                cos = cos_ref[...]
                sin = sin_ref[...]
                for j in range((c1 - c0) // LANES):
                    blk = _rope_block(acc[:, j * LANES:(j + 1) * LANES], cos, sin)
                    o_ref[:, c0 - a + j * LANES:c0 - a + (j + 1) * LANES] = blk.astype(o_ref.dtype)
            else:
                o_ref[:, c0 - a:c1 - a] = acc.astype(o_ref.dtype)


def _inproj(x2d, mod3, mod_row0, rows_per_mod, w, segs, rope_tabs, seq_len, tm):
    t, d = x2d.shape
    n = w.shape[1]
    use_rope = rope_tabs is not None
    tiles_per_mod = rows_per_mod // tm
    tiles_per_seq = seq_len // tm

    def mod_map(piece):
        return lambda i: (mod_row0 + i // tiles_per_mod, 0, piece)

    in_specs = [pl.BlockSpec((tm, d), lambda i: (i, 0)),
                pl.BlockSpec((None, 1, d), mod_map(0)),
                pl.BlockSpec((None, 1, d), mod_map(1)),
                pl.BlockSpec((d, n), lambda i: (0, 0), pipeline_mode=pl.Buffered(1))]
    args = [x2d, mod3, mod3, w]
    if use_rope:
        in_specs += [pl.BlockSpec((tm, LANES), lambda i: (i % tiles_per_seq, 0))] * 2
        args += list(rope_tabs)
    out_specs = [pl.BlockSpec((tm, b - a), lambda i: (i, 0)) for a, b, _ in segs]
    out_shape = [jax.ShapeDtypeStruct((t, b - a), BF16) for a, b, _ in segs]
    return pl.pallas_call(
        functools.partial(_inproj_kernel, segs=segs, use_rope=use_rope, n_chunk=512),
        grid=(t // tm,),
        in_specs=in_specs, out_specs=out_specs, out_shape=out_shape,
        compiler_params=_cparams("arbitrary"),
        name="inproj",
    )(*args)


def _attn_kernel(*refs, n_src, lam_init):
    lq1, lk1, lq2, lk2, g_ref, q_ref = refs[:6]
    kv_refs = refs[6:6 + 2 * n_src]
    o_ref = refs[6 + 2 * n_src]
    lam = (jnp.exp(jnp.sum(lq1[...] * lk1[...], axis=-1, keepdims=True))
           - jnp.exp(jnp.sum(lq2[...] * lk2[...], axis=-1, keepdims=True)) + lam_init)
    tq = q_ref.shape[0]
    lane = lax.broadcasted_iota(jnp.int32, (tq, LANES), 1)
    dn = (((1,), (1,)), ((), ()))
    for h in range(DA_HEADS):
        cols = slice(h * LANES, (h + 1) * LANES)
        qh = q_ref[:, cols]
        zero = jnp.zeros_like(qh)
        om = []
        for m in range(2):
            qm = jnp.where(lane < DA_HEAD_DIM if m == 0 else lane >= DA_HEAD_DIM, qh, zero)
            s = [lax.dot_general(qm, kv_refs[2 * j][:, cols], dn, preferred_element_type=F32)
                 for j in range(n_src)]
            mx = functools.reduce(jnp.maximum, [jnp.max(sj, axis=-1, keepdims=True) for sj in s])
            p = [jnp.exp2(sj - mx) for sj in s]
            den = functools.reduce(jnp.add, [jnp.sum(pj, axis=-1, keepdims=True) for pj in p])
            pv = [jnp.dot(p[j].astype(BF16), kv_refs[2 * j + 1][:, cols], preferred_element_type=F32)
                  for j in range(n_src)]
            om.append(functools.reduce(jnp.add, pv) * (1.0 / den))
        o = om[0] - lam * om[1]
        ms = jnp.mean(o * o, axis=-1, keepdims=True)
        o = o * lax.rsqrt(ms + LN_EPS) * g_ref[...] * (1.0 - lam_init)
        o_ref[:, cols] = o.astype(o_ref.dtype)


def _attention(q, kvs, lparams, norm_g, lam_init, nb, lq, tq):
    t = q.shape[0]
    qt = lq // tq
    in_specs = [_const_spec((1, DA_HEAD_DIM))] * 4 + [_const_spec((1, DA_V_DIM))]
    in_specs.append(pl.BlockSpec((tq, DA_QK_W), lambda b, i: (b * qt + i, 0)))
    args = list(lparams) + [norm_g, q]
    for k, v, lk in kvs:
        in_specs += [pl.BlockSpec((lk, DA_QK_W), lambda b, i: (b, 0)),
                     pl.BlockSpec((lk, DA_V_W), lambda b, i: (b, 0))]
        args += [k, v]
    return pl.pallas_call(
        functools.partial(_attn_kernel, n_src=len(kvs), lam_init=lam_init),
        grid=(nb, qt),
        in_specs=in_specs,
        out_specs=pl.BlockSpec((tq, DA_V_W), lambda b, i: (b * qt + i, 0)),
        out_shape=jax.ShapeDtypeStruct((t, DA_V_W), BF16),
        compiler_params=_cparams("arbitrary", "arbitrary"),
        name="diff_attn",
    )(*args)


def _dft_tables(L):
    k = jnp.arange(L, dtype=jnp.int32)
    m = (k[:, None] * k[None, :]) % (2 * L)
    ang = m.astype(F32) * (math.pi / L)
    return jnp.cos(ang), jnp.sin(ang)


def _filter_consts(L):
    t = jnp.linspace(0.0, 1.0, L, dtype=F32)[:, None]
    w = 2.0 * math.pi * jnp.arange(L, dtype=F32)[:, None] / L
    f = jnp.linspace(1e-4, HY_BANDS - 1, HY_BANDS, dtype=F32)[None, :]
    emb = jnp.concatenate([t, jnp.cos(f * w), -jnp.sin(f * w)], axis=-1)
    max_decay = math.log(HY_DECAY_TARGET) / HY_DECAY_FAST
    min_decay = math.log(HY_DECAY_TARGET) / HY_DECAY_SLOW
    deltas = jnp.abs(jnp.linspace(min_decay, max_decay, HY_DIM, dtype=F32))
    window = jnp.exp(-t * deltas[None, :]) + HY_DECAY_SHIFT
    return emb, window


def _filter_kernel(emb_ref, win_ref, w1, b1, w2, b2, w3, b3, hs_ref, hd_ref, nyq_ref):
    h = jnp.sin(jnp.dot(emb_ref[...], w1[...], precision=HIGHEST, preferred_element_type=F32) + b1[...])
    h = jnp.sin(jnp.dot(h, w2[...], precision=HIGHEST, preferred_element_type=F32) + b2[...])
    h = jnp.dot(h, w3[...], precision=HIGHEST, preferred_element_type=F32) + b3[...]
    win = win_ref[...]
    hf = h[:, :HY_DIM] * win
    hb = h[:, HY_DIM:] * win
    row = lax.broadcasted_iota(jnp.int32, hf.shape, 0)
    hb = jnp.where(row == 0, 0.0, hb)
    alt = jnp.where(row % 2 == 0, 1.0, -1.0)
    hs_ref[...] = hf + hb
    hd_ref[...] = hf - hb
    nyq_ref[...] = jnp.sum((hf + hb) * alt, axis=0, keepdims=True)


def _spectrum_kernel(c_ref, s_ref, hs_ref, hd_ref, kre_ref, kim_ref, *, n_fft):
    i = pl.program_id(0)
    tk = c_ref.shape[0]
    kidx = i * tk + lax.broadcasted_iota(jnp.int32, (tk, 1), 0)
    scale = jnp.where(kidx == 0, 1.0 / n_fft, 2.0 / n_fft)
    kre = jnp.dot(c_ref[...], hs_ref[...], precision=HIGHEST, preferred_element_type=F32)
    kim = -jnp.dot(s_ref[...], hd_ref[...], precision=HIGHEST, preferred_element_type=F32)
    kre_ref[...] = kre * scale
    kim_ref[...] = kim * scale


def _hyena_filter_spectrum(L, cmat, smat, w1, b1, w2, b2, w3, b3):
    emb, window = _filter_consts(L)
    full = lambda a: _const_spec(a.shape)
    ins = [emb, window, w1, b1.reshape(1, -1), w2, b2.reshape(1, -1), w3, b3.reshape(1, -1)]
    hs, hd, nyq = pl.pallas_call(
        _filter_kernel,
        grid=(1,),
        in_specs=[full(a) for a in ins],
        out_specs=[_const_spec((L, HY_DIM)), _const_spec((L, HY_DIM)), _const_spec((1, HY_DIM))],
        out_shape=[jax.ShapeDtypeStruct((L, HY_DIM), F32), jax.ShapeDtypeStruct((L, HY_DIM), F32),
                   jax.ShapeDtypeStruct((1, HY_DIM), F32)],
        compiler_params=_cparams("arbitrary"),
        name="hyena_filter",
    )(*ins)
    tk = min(256, L)
    kre, kim = pl.pallas_call(
        functools.partial(_spectrum_kernel, n_fft=2 * L),
        grid=(L // tk,),
        in_specs=[pl.BlockSpec((tk, L), lambda i: (i, 0)), pl.BlockSpec((tk, L), lambda i: (i, 0)),
                  _const_spec((L, HY_DIM)), _const_spec((L, HY_DIM))],
        out_specs=[pl.BlockSpec((tk, HY_DIM), lambda i: (i, 0))] * 2,
        out_shape=[jax.ShapeDtypeStruct((L, HY_DIM), F32)] * 2,
        compiler_params=_cparams("arbitrary"),
        name="hyena_spectrum",
    )(cmat, smat, hs, hd)
    return kre, kim, nyq * (1.0 / (2 * L))


def _hyena_kernel(z_ref, cw_ref, cb_ref, c_ref, s_ref, kre_ref, kim_ref, nyq_ref, skip_ref, o_ref):
    L = z_ref.shape[0]
    row = lax.broadcasted_iota(jnp.int32, (L, HY_DIM), 0)

    def conv(j):
        cols = slice(j * HY_DIM, (j + 1) * HY_DIM)
        z = z_ref[:, cols].astype(F32)
        zprev = jnp.where(row == 0, 0.0, pltpu.roll(z, 1, 0))
        znext = jnp.where(row == L - 1, 0.0, pltpu.roll(z, L - 1, 0))
        return zprev * cw_ref[0:1, cols] + z * cw_ref[1:2, cols] + znext * cw_ref[2:3, cols] + cb_ref[:, cols]

    u = conv(2) * conv(1)
    ub = u.astype(BF16)
    a = jnp.dot(c_ref[...], ub, preferred_element_type=F32)
    b = jnp.dot(s_ref[...], ub, preferred_element_type=F32)
    kre = kre_ref[...]
    kim = kim_ref[...]
    p = (a * kre + b * kim).astype(BF16)
    q = (b * kre - a * kim).astype(BF16)
    y = jnp.dot(c_ref[...], p, preferred_element_type=F32) + jnp.dot(s_ref[...], q, preferred_element_type=F32)
    alt = jnp.where(row % 2 == 0, 1.0, -1.0)
    vnyq = jnp.sum(u * alt, axis=0, keepdims=True)
    y = y + alt * (vnyq * nyq_ref[...])
    y = y + u * skip_ref[...]
    o_ref[...] = (y * conv(0)).astype(o_ref.dtype)


def _hyena(zhy, nb, L, conv_w, conv_b, cmat_bf, smat_bf, kre, kim, nyq, skip):
    t = zhy.shape[0]
    return pl.pallas_call(
        _hyena_kernel,
        grid=(nb,),
        in_specs=[pl.BlockSpec((L, 3 * HY_DIM), lambda b: (b, 0)),
                  _const_spec((3, 3 * HY_DIM)), _const_spec((1, 3 * HY_DIM)),
                  pl.BlockSpec((L, L), lambda b: (0, 0), pipeline_mode=pl.Buffered(1)),
                  pl.BlockSpec((L, L), lambda b: (0, 0), pipeline_mode=pl.Buffered(1)),
                  _const_spec((L, HY_DIM)), _const_spec((L, HY_DIM)),
                  _const_spec((1, HY_DIM)), _const_spec((1, HY_DIM))],
        out_specs=pl.BlockSpec((L, HY_DIM), lambda b: (b, 0)),
        out_shape=jax.ShapeDtypeStruct((t, HY_DIM), BF16),
        compiler_params=_cparams("arbitrary"),
        name="hyena_conv",
    )(zhy, conv_w, conv_b.reshape(1, -1), cmat_bf, smat_bf, kre, kim, nyq, skip.reshape(1, -1))


SUBLANES = 8


def _split_bf16(w):
    hi = w.astype(BF16)
    return jnp.stack([hi, (w - hi.astype(F32)).astype(BF16)])


def _store_token_tiles(ref, val):
    n = val.shape[0]
    for j in range(val.shape[1] // LANES):
        ref[pl.ds(j, n, stride=SUBLANES), :] = val[:, j * LANES:(j + 1) * LANES]


def _load_token_tiles(ref, n):
    return jnp.concatenate([ref[pl.ds(j, n, stride=SUBLANES), :] for j in range(SUBLANES)], axis=1)


def _merge_kernel(*refs, alpha, n_alias):
    (zgm_ref, yb_ref, yc_ref, gate_ref, x_ref, g1_ref, sh2_ref, sc2_ref, lng_ref, lnb_ref, ws_ref, bs_ref,
     pa_ref, pb_ref, pc_ref, wo_ref, l1g_ref, l1b_ref, wr_ref, br_ref) = refs[:20]
    x1_ref, h2_ref, rt_ref = refs[20 + n_alias:]
    tm = x_ref.shape[0]
    d = x_ref.shape[1]
    gm = _gelu_tanh(zgm_ref[...].astype(F32))
    u = gm[:, :GM_DIM]
    v = _layer_norm(gm[:, GM_DIM:], lng_ref[...], lnb_ref[...]).astype(BF16)
    lane_group = lax.broadcasted_iota(jnp.int32, (GM_CHUNK, GM_DIM), 1) // (GM_DIM // GM_GROUPS)
    ya = []
    for cidx in range(tm // GM_CHUNK):
        rows = slice(cidx * GM_CHUNK, (cidx + 1) * GM_CHUNK)
        r = jnp.dot(ws_ref[...], v[rows], preferred_element_type=F32)
        vv = bs_ref[...]
        for g in range(GM_GROUPS):
            vv = vv + jnp.where(lane_group == g, r[g * GM_CHUNK:(g + 1) * GM_CHUNK], 0.0)
        ya.append(u[rows] * vv)
    ya = jnp.concatenate(ya, axis=0) if len(ya) > 1 else ya[0]
    ma = jnp.dot(ya.astype(BF16), pa_ref[...], preferred_element_type=F32)
    mb = jnp.dot(yb_ref[...], pb_ref[...], preferred_element_type=F32)
    mc = jnp.dot(yc_ref[...], pc_ref[...], preferred_element_type=F32)
    merged = (_sigmoid(gate_ref[:, 0:d].astype(F32)) * ma
              + _sigmoid(gate_ref[:, d:2 * d].astype(F32)) * mb
              + _sigmoid(gate_ref[:, 2 * d:3 * d].astype(F32)) * mc)
    out = jnp.dot(merged.astype(BF16), wo_ref[...], preferred_element_type=F32)
    x1 = _layer_norm(alpha * x_ref[...] + g1_ref[...] * out, l1g_ref[...], l1b_ref[...])
    x1_ref[...] = x1
    h2 = x1 * (1.0 + sc2_ref[...]) + sh2_ref[...]
    _store_token_tiles(h2_ref, h2)
    h2_hi = h2.astype(BF16)
    h2_lo = (h2 - h2_hi.astype(F32)).astype(BF16)
    lg = (jnp.dot(h2_hi, wr_ref[0], preferred_element_type=F32)
          + jnp.dot(h2_hi, wr_ref[1], preferred_element_type=F32)
          + jnp.dot(h2_lo, wr_ref[0], preferred_element_type=F32) + br_ref[...])
    rt_ref[...] = _route(lg)


ROUTE_E0, ROUTE_E1, ROUTE_W0, ROUTE_W1 = 0, 1, 2, 3


def _route(lg):
    neg = jnp.float32(-3.0e38)
    lane_i = lax.broadcasted_iota(jnp.int32, lg.shape, 1)
    lane = lane_i.astype(F32)
    big = jnp.float32(LANES)
    is_g = lane_i < MOE_GROUPS
    gl = jnp.where(is_g, lg, neg)
    gmax = jnp.max(gl, axis=-1, keepdims=True)
    g_idx = jnp.min(jnp.where(gl == gmax, lane, big), axis=-1, keepdims=True)
    g_prob = 1.0 / jnp.sum(jnp.where(is_g, jnp.exp(gl - gmax), 0.0), axis=-1, keepdims=True)
    e_lo = MOE_GROUPS + MOE_EXPERTS_PER_GROUP * g_idx
    el = jnp.where(lane >= e_lo, jnp.where(lane < e_lo + MOE_EXPERTS_PER_GROUP, lg, neg), neg)
    v1 = jnp.max(el, axis=-1, keepdims=True)
    i1 = jnp.min(jnp.where(el == v1, lane, big), axis=-1, keepdims=True)
    el2 = jnp.where(lane == i1, neg, el)
    v2 = jnp.max(el2, axis=-1, keepdims=True)
    i2 = jnp.min(jnp.where(el2 == v2, lane, big), axis=-1, keepdims=True)
    e21 = jnp.exp(v2 - v1)
    w1 = g_prob / (1.0 + e21)
    w2 = w1 * e21
    rec = jnp.where(lane_i == ROUTE_E0, i1 - MOE_GROUPS, 0.0)
    rec = jnp.where(lane_i == ROUTE_E1, i2 - MOE_GROUPS, rec)
    rec = jnp.where(lane_i == ROUTE_W0, w1, rec)
    return jnp.where(lane_i == ROUTE_W1, w2, rec)


def _merge(zgm, yb, yc, gate, x2d, mod3, mod_row0, rows_per_mod, lp, alpha, tm, t_all, row0, prev):
    t, d = x2d.shape
    tiles_per_mod = rows_per_mod // tm
    off = row0 // tm

    def mod_map(piece):
        return lambda i: (mod_row0 + i // tiles_per_mod, 0, piece)

    row = lambda w: pl.BlockSpec((tm, w), lambda i: (i, 0))
    row_off = lambda w: pl.BlockSpec((tm, w), lambda i: (off + i, 0))
    consts = [lp["gm_ln_g"], lp["gm_ln_b"], lp["gm_ws"], lp["gm_bs"], lp["p_a"], lp["p_b"], lp["p_c"],
              lp["w_out"], lp["ln1_g"], lp["ln1_b"], lp["w_router"], lp["b_router"]]
    in_specs = [row(2 * GM_DIM), row(HY_DIM), row(DA_V_W), row(N_BRANCH * d), row(d),
                pl.BlockSpec((None, 1, d), mod_map(2)), pl.BlockSpec((None, 1, d), mod_map(3)),
                pl.BlockSpec((None, 1, d), mod_map(4))] + [_const_spec(a.shape) for a in consts]
    args = [zgm, yb, yc, gate, x2d, mod3, mod3, mod3, *consts]
    aliases = {}
    if prev is not None:
        aliases = {len(args): 1, len(args) + 1: 2}
        in_specs += [pl.BlockSpec(memory_space=pl.ANY)] * 2
        args += list(prev)
    return pl.pallas_call(
        functools.partial(_merge_kernel, alpha=alpha, n_alias=len(aliases)),
        grid=(t // tm,),
        in_specs=in_specs,
        out_specs=[row(d), pl.BlockSpec((tm * SUBLANES, LANES), lambda i: (off + i, 0)), row_off(LANES)],
        out_shape=[jax.ShapeDtypeStruct((t, d), F32), jax.ShapeDtypeStruct((t_all * SUBLANES, LANES), F32),
                   jax.ShapeDtypeStruct((t_all, LANES), F32)],
        input_output_aliases=aliases,
        compiler_params=_cparams("arbitrary"),
        name="merge_ln1",
    )(*args)


FLAG_FIRST, FLAG_LAST, FLAG_FINAL = 1, 2, 4


def _tile_copy(src, src_row, dst, dst_row, sem):
    s0 = pl.multiple_of(src_row * SUBLANES, SUBLANES)
    d0 = pl.multiple_of(dst_row * SUBLANES, SUBLANES)
    return pltpu.make_async_copy(src.at[pl.ds(s0, SUBLANES)], dst.at[pl.ds(d0, SUBLANES)], sem)


def _expert_kernel(vt_ref, ve_ref, vlo_ref, vhi_ref, vflag_ref, src_ref, nsrc_ref, dst_ref, h2_hbm,
                   wg_ref, wu_ref, wd_ref, y_hbm, xbuf, acc, ybuf, gsem, ssem, *, n_tiles):
    v = pl.program_id(0)
    tile, lo, hi, flag = vt_ref[v], vlo_ref[v], vhi_ref[v], vflag_ref[v]
    tm = acc.shape[0]
    slot = tile % 2
    first = (flag & FLAG_FIRST) != 0

    def issue_gather(idx_ref, to_slot):
        def body(i, carry):
            for j in range(SUBLANES):
                r = i * SUBLANES + j
                _tile_copy(h2_hbm, idx_ref[0, r], xbuf.at[to_slot], r, gsem.at[to_slot]).start()
            return carry

        lax.fori_loop(0, tm // SUBLANES, body, 0)

    @pl.when(first)
    def _():
        @pl.when(tile == 0)
        def _():
            issue_gather(src_ref, slot)

        pltpu.make_async_copy(h2_hbm.at[pl.ds(0, tm * SUBLANES)], xbuf.at[slot], gsem.at[slot]).wait()

        @pl.when(tile + 1 < n_tiles)
        def _():
            issue_gather(nsrc_ref, 1 - slot)

    @pl.when(hi > lo)
    def _():
        xb = _load_token_tiles(xbuf.at[slot], tm).astype(BF16)
        g = jnp.dot(xb, wg_ref[...], preferred_element_type=F32)
        u = jnp.dot(xb, wu_ref[...], preferred_element_type=F32)
        hmid = (g * _sigmoid(g) * u).astype(BF16)
        y = jnp.dot(hmid, wd_ref[...], preferred_element_type=F32)
        row = lax.broadcasted_iota(jnp.int32, (tm, 1), 0)
        y = jnp.where((row >= lo) & (row < hi), y, 0.0)

        @pl.when(first)
        def _():
            acc[...] = y

        @pl.when(jnp.logical_not(first))
        def _():
            acc[...] += y

    @pl.when((flag & FLAG_LAST) != 0)
    def _():
        whole = pltpu.make_async_copy(ybuf, y_hbm.at[pl.ds(0, tm * SUBLANES)], ssem)

        @pl.when(tile > 0)
        def _():
            whole.wait()

        _store_token_tiles(ybuf, acc[...])

        def body(i, carry):
            for j in range(SUBLANES):
                r = i * SUBLANES + j
                _tile_copy(ybuf, r, y_hbm, dst_ref[0, r], ssem).start()
            return carry

        lax.fori_loop(0, tm // SUBLANES, body, 0)

        @pl.when((flag & FLAG_FINAL) != 0)
        def _():
            whole.wait()


def _experts(h2, plan, w_gate, w_up, w_down, tm):
    t = h2.shape[0] // SUBLANES
    vt, ve, vlo, vhi, vflag, src, dst = plan
    n_vis = vt.shape[0]
    n_tiles = src.shape[0] // tm
    d, hid = w_gate.shape[-2:]
    assert d == SUBLANES * LANES
    idx_spec = lambda nxt: pl.BlockSpec(
        (None, 1, tm), lambda v, vt, *_: (jnp.minimum(vt[v] + nxt, n_tiles - 1), 0, 0), memory_space=pltpu.SMEM)
    w_spec = lambda shape: pl.BlockSpec((None,) + shape, lambda v, vt, ve, *_: (ve[v], 0, 0))
    grid_spec = pltpu.PrefetchScalarGridSpec(
        num_scalar_prefetch=5,
        grid=(n_vis,),
        in_specs=[idx_spec(0), idx_spec(1), idx_spec(0), pl.BlockSpec(memory_space=pl.ANY),
                  w_spec((d, hid)), w_spec((d, hid)), w_spec((hid, d))],
        out_specs=pl.BlockSpec(memory_space=pl.ANY),
        scratch_shapes=[pltpu.VMEM((2, tm * SUBLANES, LANES), F32), pltpu.VMEM((tm, d), F32),
                        pltpu.VMEM((tm * SUBLANES, LANES), F32),
                        pltpu.SemaphoreType.DMA((2,)), pltpu.SemaphoreType.DMA(())],
    )
    src3 = src.reshape(n_tiles, 1, tm)
    return pl.pallas_call(
        functools.partial(_expert_kernel, n_tiles=n_tiles),
        grid_spec=grid_spec,
        out_shape=jax.ShapeDtypeStruct((MOE_TOP_K * t * SUBLANES, LANES), F32),
        compiler_params=_cparams("arbitrary"),
        name="moe_experts",
    )(vt, ve, vlo, vhi, vflag, src3, src3, dst.reshape(n_tiles, 1, tm), h2, w_gate, w_up, w_down)


def _visit_plan(route, tm):
    t = route.shape[0]
    a = t * MOE_TOP_K
    i32 = jnp.int32
    eid = route[:, ROUTE_E0:ROUTE_E1 + 1].astype(i32).reshape(a)
    _, order = lax.sort((eid, lax.iota(i32, a)), num_keys=1)
    tok = order // MOE_TOP_K
    src = tok
    dst = (order % MOE_TOP_K) * t + tok
    counts = jnp.sum((eid[:, None] == jnp.arange(MOE_N_EXPERTS, dtype=i32)[None, :]).astype(i32), axis=0)
    ends = jnp.cumsum(counts)
    starts = ends - counts
    n_tiles = a // tm
    first_t = starts // tm
    last_t = jnp.maximum(ends - 1, 0) // tm
    nvis = jnp.where(counts > 0, last_t - first_t + 1, 0)
    cv_end = jnp.cumsum(nvis)
    cv_start = cv_end - nvis
    n_vis = n_tiles + MOE_N_EXPERTS
    v = jnp.arange(n_vis, dtype=i32)
    active = v < cv_end[-1]
    e = jnp.minimum(jnp.sum((cv_end[None, :] <= v[:, None]).astype(i32), axis=1), MOE_N_EXPERTS - 1)
    e_last = jnp.max(jnp.where(counts > 0, jnp.arange(MOE_N_EXPERTS, dtype=i32), 0))
    e = jnp.where(active, e, e_last)
    tile = jnp.where(active, first_t[e] + v - cv_start[e], n_tiles - 1)
    lo = jnp.where(active, jnp.clip(starts[e] - tile * tm, 0, tm), 0)
    hi = jnp.where(active, jnp.clip(ends[e] - tile * tm, 0, tm), 0)
    prev_t = jnp.concatenate([jnp.full((1,), -1, i32), tile[:-1]])
    next_t = jnp.concatenate([tile[1:], jnp.full((1,), -1, i32)])
    is_final = v == cv_end[-1] - 1
    is_last = (next_t != tile) | is_final
    flag = jnp.where(active, (prev_t != tile) * FLAG_FIRST + is_last * FLAG_LAST + is_final * FLAG_FINAL, 0)
    cast = lambda z: z.astype(i32)
    return cast(tile), cast(e), cast(lo), cast(hi), cast(flag), cast(src), cast(dst)


def _combine_kernel(x_ref, y0_ref, y1_ref, rt_ref, g2_ref, lg_ref, lb_ref, o_ref, *, alpha):
    rt = rt_ref[...]
    tm = x_ref.shape[0]
    y = (rt[:, ROUTE_W0:ROUTE_W0 + 1] * _load_token_tiles(y0_ref, tm)
         + rt[:, ROUTE_W1:ROUTE_W1 + 1] * _load_token_tiles(y1_ref, tm))
    o_ref[...] = _layer_norm(alpha * x_ref[...] + g2_ref[...] * y, lg_ref[...], lb_ref[...])


def _combine(x1, y2, route, row0, t_all, mod3, mod_row0, rows_per_mod, ln_g, ln_b, alpha, tm):
    t, d = x1.shape
    tiles_per_mod = rows_per_mod // tm
    t0 = row0 // tm
    t1 = (t_all + row0) // tm
    return pl.pallas_call(
        functools.partial(_combine_kernel, alpha=alpha),
        grid=(t // tm,),
        in_specs=[pl.BlockSpec((tm, d), lambda i: (i, 0)),
                  pl.BlockSpec((tm * SUBLANES, LANES), lambda i: (t0 + i, 0)),
                  pl.BlockSpec((tm * SUBLANES, LANES), lambda i: (t1 + i, 0)),
                  pl.BlockSpec((tm, LANES), lambda i: (t0 + i, 0)),
                  pl.BlockSpec((None, 1, d), lambda i: (mod_row0 + i // tiles_per_mod, 0, 5)),
                  _const_spec((1, d)), _const_spec((1, d))],
        out_specs=pl.BlockSpec((tm, d), lambda i: (i, 0)),
        out_shape=jax.ShapeDtypeStruct((t, d), F32),
        compiler_params=_cparams("arbitrary"),
        name="combine_ln2",
    )(x1, y2, y2, route, mod3, ln_g.reshape(1, d), ln_b.reshape(1, d))


def _pick_tile(n, pref):
    tm = min(pref, n)
    while n % tm:
        tm //= 2
    return tm


def kernel(x, c, ctx, c_ctx, ada_w, ada_b, w_in, gm_ln_g, gm_ln_b, gm_ws, gm_bs, hy_conv_w, hy_conv_b,
           hy_f_w1, hy_f_b1, hy_f_w2, hy_f_b2, hy_f_w3, hy_f_b3, hy_skip, da_lq1, da_lk1, da_lq2, da_lk2,
           da_norm_g, p_a, p_b, p_c, w_out, ln1_g, ln1_b, moe_wg, moe_bg, moe_we, moe_be,
           ex_w_gate, ex_w_up, ex_w_down, ln2_g, ln2_b):
    B, L, D = x.shape
    Lc = ctx.shape[1]
    depth = ada_w.shape[0]
    alpha = (2.0 * depth) ** 0.25
    T, Tc = B * L, B * Lc
    moe_tm = 512 if T >= 8192 else 64

    mp = -(-(B + 1) // 8) * 8
    c_all = jnp.zeros((mp, D), F32).at[:B].set(c).at[B].set(c_ctx)
    mod = _modulation(c_all, ada_w, ada_b)

    rope_tabs = _rope_tables(L // GRID_W)
    cm, sm = _dft_tables(L)
    cm_bf, sm_bf = cm.astype(BF16), sm.astype(BF16)
    cmc, smc = _dft_tables(Lc)
    cmc_bf, smc_bf = cmc.astype(BF16), smc.astype(BF16)

    seg_all = ((OFF_GM, OFF_HY, "gm"), (OFF_HY, OFF_Q, "hy"), (OFF_Q, OFF_K, "q"), (OFF_K, OFF_V, "k"),
               (OFF_V, OFF_GATE, "v"), (OFF_GATE, OFF_GATE + N_BRANCH * D, "gate"))
    seg_kv = ((0, DA_QK_W, "k"), (DA_QK_W, DA_QK_W + DA_V_W, "v"))

    tm_l = _pick_tile(L, 256)
    tm_c = _pick_tile(Lc, 256)
    tq_l = _pick_tile(L, 256)
    tq_c = _pick_tile(Lc, 256)

    xs = x.reshape(T, D)
    xc = ctx.reshape(Tc, D)
    for l in range(depth):
        last = l == depth - 1
        lam_init = 0.8 - 0.6 * math.exp(-0.3 * l)
        mod3 = mod[l].reshape(mp, 1, 6 * D)
        w_l = w_in[l].astype(BF16)
        lparams = [a[l].reshape(1, DA_HEAD_DIM) for a in (da_lq1, da_lk1, da_lq2, da_lk2)]
        norm_g = da_norm_g[l].reshape(1, DA_V_DIM)
        lp = {
            "gm_ln_g": gm_ln_g[l].reshape(1, GM_DIM), "gm_ln_b": gm_ln_b[l].reshape(1, GM_DIM),
            "gm_ws": gm_ws[l].reshape(GM_GROUPS * GM_CHUNK, GM_CHUNK).astype(BF16),
            "gm_bs": jnp.repeat(jnp.transpose(gm_bs[l]), GM_DIM // GM_GROUPS, axis=1),
            "p_a": p_a[l].astype(BF16), "p_b": p_b[l].astype(BF16), "p_c": p_c[l].astype(BF16),
            "w_out": w_out[l].astype(BF16),
            "ln1_g": ln1_g[l].reshape(1, D), "ln1_b": ln1_b[l].reshape(1, D),
            "w_router": _split_bf16(jnp.zeros((D, LANES), F32).at[:, :MOE_GROUPS].set(moe_wg[l])
                                    .at[:, MOE_GROUPS:MOE_GROUPS + MOE_N_EXPERTS].set(moe_we[l])),
            "b_router": jnp.zeros((1, LANES), F32).at[0, :MOE_GROUPS].set(moe_bg[l])
                           .at[0, MOE_GROUPS:MOE_GROUPS + MOE_N_EXPERTS].set(moe_be[l]),
        }
        fw = (hy_f_w1[l], hy_f_b1[l], hy_f_w2[l], hy_f_b2[l], hy_f_w3[l], hy_f_b3[l])

        zgm, zhy, q, k, v, gate = _inproj(xs, mod3, 0, L, w_l, seg_all, rope_tabs, L, tm_l)
        if last:
            k_c, v_c = _inproj(xc, mod3, B, Tc, w_l[:, OFF_K:OFF_GATE], seg_kv, None, Lc, tm_c)
        else:
            zgm_c, zhy_c, q_c, k_c, v_c, gate_c = _inproj(xc, mod3, B, Tc, w_l, seg_all, None, Lc, tm_c)
        y_c = _attention(q, [(k, v, L), (k_c, v_c, Lc)], lparams, norm_g, lam_init, B, L, tq_l)
        kre, kim, nyq = _hyena_filter_spectrum(L, cm, sm, *fw)
        y_b = _hyena(zhy, B, L, hy_conv_w[l], hy_conv_b[l], cm_bf, sm_bf, kre, kim, nyq, hy_skip[l])
        t_all = T if last else T + Tc
        x1, h2, route = _merge(zgm, y_b, y_c, gate, xs, mod3, 0, L, lp, alpha, tm_l, T, 0, None)

        if not last:
            yc_c = _attention(q_c, [(k_c, v_c, Lc)], lparams, norm_g, lam_init, B, Lc, tq_c)
            kre_c, kim_c, nyq_c = _hyena_filter_spectrum(Lc, cmc, smc, *fw)
            yb_c = _hyena(zhy_c, B, Lc, hy_conv_w[l], hy_conv_b[l], cmc_bf, smc_bf, kre_c, kim_c, nyq_c,
                          hy_skip[l])
            x1c, h2c, route_c = _merge(zgm_c, yb_c, yc_c, gate_c, xc, mod3, B, Tc, lp, alpha, tm_c, Tc, 0, None)
            h2 = jnp.concatenate([h2, h2c], axis=0)
            route = jnp.concatenate([route, route_c], axis=0)

        plan = _visit_plan(route, moe_tm)
        y2 = _experts(h2, plan, ex_w_gate[l].astype(BF16), ex_w_up[l].astype(BF16),
                      ex_w_down[l].astype(BF16), moe_tm)
        xs = _combine(x1, y2, route, 0, t_all, mod3, 0, L, ln2_g[l], ln2_b[l], alpha, tm_l)
        if not last:
            xc = _combine(x1c, y2, route, T, t_all, mod3, B, Tc, ln2_g[l], ln2_b[l], alpha, tm_c)
    return xs.reshape(B, L, D)
```

```python
import functools
import math

import numpy as np
import jax
import jax.numpy as jnp
from jax import lax
from jax.experimental import pallas as pl
from jax.experimental.pallas import tpu as pltpu

F32 = jnp.float32
BF16 = jnp.bfloat16
HIGHEST = lax.Precision.HIGHEST

GRID_W = 64
GM_DIM = 256
GM_GROUPS = 4
GM_CHUNK = 128
HY_DIM = 256
HY_EMB = 33
HY_BANDS = (HY_EMB - 1) // 2
HY_DECAY_FAST = 0.3
HY_DECAY_SLOW = 1.5
HY_DECAY_TARGET = 1e-2
HY_DECAY_SHIFT = 0.05
DA_HEADS = 4
DA_HEAD_DIM = 64
DA_V_DIM = 2 * DA_HEAD_DIM
DA_QK_W = DA_HEADS * 2 * DA_HEAD_DIM
DA_V_W = DA_HEADS * DA_V_DIM
ROPE_BASE = 10000.0
N_BRANCH = 3
OFF_GM = 0
OFF_HY = OFF_GM + 2 * GM_DIM
OFF_Q = OFF_HY + 3 * HY_DIM
OFF_K = OFF_Q + DA_QK_W
OFF_V = OFF_K + DA_QK_W
OFF_GATE = OFF_V + DA_V_W
MOE_GROUPS = 4
MOE_EXPERTS_PER_GROUP = 8
MOE_N_EXPERTS = MOE_GROUPS * MOE_EXPERTS_PER_GROUP
MOE_TOP_K = 2
LN_EPS = 1e-5
LANES = 128
VMEM_LIMIT = 56 * 1024 * 1024


def _cparams(*sem):
    return pltpu.CompilerParams(dimension_semantics=sem, vmem_limit_bytes=VMEM_LIMIT)


def _sigmoid(x):
    return 1.0 / (1.0 + jnp.exp(-x))


def _layer_norm(x, g, b):
    mu = jnp.mean(x, axis=-1, keepdims=True)
    xc = x - mu
    var = jnp.mean(xc * xc, axis=-1, keepdims=True)
    return xc * lax.rsqrt(var + LN_EPS) * g + b


def _gelu_tanh(x):
    return 0.5 * x * (1.0 + jnp.tanh(math.sqrt(2.0 / math.pi) * (x + 0.044715 * (x * x * x))))


def _const_spec(shape):
    nd = len(shape)
    return pl.BlockSpec(shape, lambda *_: (0,) * nd)


def _mod_kernel(c_ref, w_ref, b_ref, o_ref):
    c = c_ref[...]
    s = c * _sigmoid(c)
    o_ref[...] = jnp.dot(s, w_ref[...], precision=HIGHEST, preferred_element_type=F32) + b_ref[...]


def _modulation(c_all, ada_w, ada_b):
    depth, d, n = ada_w.shape
    mp = c_all.shape[0]
    tn = 512
    return pl.pallas_call(
        _mod_kernel,
        grid=(depth, n // tn),
        in_specs=[pl.BlockSpec((mp, d), lambda l, j: (0, 0)),
                  pl.BlockSpec((None, d, tn), lambda l, j: (l, 0, j)),
                  pl.BlockSpec((None, 1, tn), lambda l, j: (l, 0, j))],
        out_specs=pl.BlockSpec((None, mp, tn), lambda l, j: (l, 0, j)),
        out_shape=jax.ShapeDtypeStruct((depth, mp, n), F32),
        compiler_params=_cparams("arbitrary", "arbitrary"),
        name="adaln_mod",
    )(c_all, ada_w, ada_b.reshape(depth, 1, n))


def _rope_tables(rows):
    n_freq = DA_HEAD_DIM // 4
    row = jnp.broadcast_to(jnp.arange(rows)[:, None], (rows, GRID_W)).reshape(-1).astype(F32)
    col = jnp.broadcast_to(jnp.arange(GRID_W)[None, :], (rows, GRID_W)).reshape(-1).astype(F32)
    inv = ROPE_BASE ** (-jnp.arange(n_freq, dtype=F32) / n_freq)
    ang_r = row[:, None] * inv
    ang_c = col[:, None] * inv
    c64 = jnp.concatenate([jnp.cos(ang_r), jnp.cos(ang_r), jnp.cos(ang_c), jnp.cos(ang_c)], axis=-1)
    s64 = jnp.concatenate([-jnp.sin(ang_r), jnp.sin(ang_r), -jnp.sin(ang_c), jnp.sin(ang_c)], axis=-1)
    return jnp.tile(c64, (1, LANES // DA_HEAD_DIM)), jnp.tile(s64, (1, LANES // DA_HEAD_DIM))


def _rope_block(xb, cos, sin):
    lane = lax.broadcasted_iota(jnp.int32, xb.shape, 1)
    n_freq = DA_HEAD_DIM // 4
    first_half = (lane % (2 * n_freq)) < n_freq
    partner = jnp.where(first_half, pltpu.roll(xb, LANES - n_freq, 1), pltpu.roll(xb, n_freq, 1))
    return xb * cos + partner * sin


def _inproj_kernel(*refs, segs, use_rope, n_chunk):
    if use_rope:
        x_ref, sh_ref, sc_ref, w_ref, cos_ref, sin_ref = refs[:6]
        out_refs = refs[6:]
    else:
        x_ref, sh_ref, sc_ref, w_ref = refs[:4]
        out_refs = refs[4:]
    h = (x_ref[...] * (1.0 + sc_ref[...]) + sh_ref[...]).astype(BF16)
    for (a, b, kind), o_ref in zip(segs, out_refs):
        for c0 in range(a, b, n_chunk):
            c1 = min(c0 + n_chunk, b)
            acc = jnp.dot(h, w_ref[:, c0:c1], preferred_element_type=F32)
            if kind == "q":
                acc = acc * (DA_HEAD_DIM ** -0.5 * math.log2(math.e))
            if use_rope and kind in ("q", "k"):
                cos = cos_ref[...]
                sin = sin_ref[...]
                for j in range((c1 - c0) // LANES):
                    blk = _rope_block(acc[:, j * LANES:(j + 1) * LANES], cos, sin)
                    o_ref[:, c0 - a + j * LANES:c0 - a + (j + 1) * LANES] = blk.astype(o_ref.dtype)
            else:
                o_ref[:, c0 - a:c1 - a] = acc.astype(o_ref.dtype)


def _inproj(x2d, mod3, mod_row0, rows_per_mod, w, segs, rope_tabs, seq_len, tm):
    t, d = x2d.shape
    n = w.shape[1]
    use_rope = rope_tabs is not None
    tiles_per_mod = rows_per_mod // tm
    tiles_per_seq = seq_len // tm

    def mod_map(piece):
        return lambda i: (mod_row0 + i // tiles_per_mod, 0, piece)

    in_specs = [pl.BlockSpec((tm, d), lambda i: (i, 0)),
                pl.BlockSpec((None, 1, d), mod_map(0)),
                pl.BlockSpec((None, 1, d), mod_map(1)),
                pl.BlockSpec((d, n), lambda i: (0, 0), pipeline_mode=pl.Buffered(1))]
    args = [x2d, mod3, mod3, w]
    if use_rope:
        in_specs += [pl.BlockSpec((tm, LANES), lambda i: (i % tiles_per_seq, 0))] * 2
        args += list(rope_tabs)
    out_specs = [pl.BlockSpec((tm, b - a), lambda i: (i, 0)) for a, b, _ in segs]
    out_shape = [jax.ShapeDtypeStruct((t, b - a), BF16) for a, b, _ in segs]
    return pl.pallas_call(
        functools.partial(_inproj_kernel, segs=segs, use_rope=use_rope, n_chunk=512),
        grid=(t // tm,),
        in_specs=in_specs, out_specs=out_specs, out_shape=out_shape,
        compiler_params=_cparams("arbitrary"),
        name="inproj",
    )(*args)


ATTN_KEY_CHUNK = 256


def _attn_kernel(*refs, src_lens, lam_init):
    n_src = len(src_lens)
    lq1, lk1, lq2, lk2, g_ref, q_ref = refs[:6]
    kv_refs = refs[6:6 + 2 * n_src]
    o_ref, s_scr = refs[6 + 2 * n_src:]
    lam = (jnp.exp(jnp.sum(lq1[...] * lk1[...], axis=-1, keepdims=True))
           - jnp.exp(jnp.sum(lq2[...] * lk2[...], axis=-1, keepdims=True)) + lam_init)
    tq = q_ref.shape[0]
    lane = lax.broadcasted_iota(jnp.int32, (tq, LANES), 1)
    dn = (((1,), (1,)), ((), ()))
    chunks = []
    off = 0
    for j, n in enumerate(src_lens):
        kc = min(ATTN_KEY_CHUNK, n)
        for st in range(0, n, kc):
            chunks.append((j, st, kc, off))
            off += kc
    for h in range(DA_HEADS):
        cols = slice(h * LANES, (h + 1) * LANES)
        qh = q_ref[:, cols]
        zero = jnp.zeros_like(qh)
        om = []
        for m in range(2):
            qm = jnp.where(lane < DA_HEAD_DIM if m == 0 else lane >= DA_HEAD_DIM, qh, zero)
            mlane = None
            for j, st, kc, off in chunks:
                s_c = lax.dot_general(qm, kv_refs[2 * j][st:st + kc, cols], dn, preferred_element_type=F32)
                s_scr[:, off:off + kc] = s_c
                for b in range(kc // LANES):
                    blk = s_c[:, b * LANES:(b + 1) * LANES]
                    mlane = blk if mlane is None else jnp.maximum(mlane, blk)
            mx = jnp.max(mlane, axis=-1, keepdims=True)
            acc = None
            for j, st, kc, off in chunks:
                p = jnp.exp2(s_scr[:, off:off + kc] - mx).astype(BF16)
                v_aug = jnp.concatenate([kv_refs[2 * j + 1][st:st + kc, cols], jnp.ones((kc, LANES), BF16)],
                                        axis=1)
                d = jnp.dot(p, v_aug, preferred_element_type=F32)
                acc = d if acc is None else acc + d
            om.append(acc[:, :LANES] * (1.0 / acc[:, LANES:LANES + 1]))
        o = om[0] - lam * om[1]
        ms = jnp.mean(o * o, axis=-1, keepdims=True)
        o = o * lax.rsqrt(ms + LN_EPS) * g_ref[...] * (1.0 - lam_init)
        o_ref[:, cols] = o.astype(o_ref.dtype)


def _attention(q, kvs, lparams, norm_g, lam_init, nb, lq, tq):
    t = q.shape[0]
    qt = lq // tq
    in_specs = [_const_spec((1, DA_HEAD_DIM))] * 4 + [_const_spec((1, DA_V_DIM))]
    in_specs.append(pl.BlockSpec((tq, DA_QK_W), lambda b, i: (b * qt + i, 0)))
    args = list(lparams) + [norm_g, q]
    for k, v, lk in kvs:
        in_specs += [pl.BlockSpec((lk, DA_QK_W), lambda b, i: (b, 0)),
                     pl.BlockSpec((lk, DA_V_W), lambda b, i: (b, 0))]
        args += [k, v]
    src_lens = tuple(lk for _, _, lk in kvs)
    return pl.pallas_call(
        functools.partial(_attn_kernel, src_lens=src_lens, lam_init=lam_init),
        grid=(nb, qt),
        in_specs=in_specs,
        out_specs=pl.BlockSpec((tq, DA_V_W), lambda b, i: (b * qt + i, 0)),
        out_shape=jax.ShapeDtypeStruct((t, DA_V_W), BF16),
        scratch_shapes=[pltpu.VMEM((tq, sum(src_lens)), F32)],
        compiler_params=_cparams("arbitrary", "arbitrary"),
        name="diff_attn",
    )(*args)


def _dft_tables(L):
    k = jnp.arange(L, dtype=jnp.int32)
    m = (k[:, None] * k[None, :]) % (2 * L)
    ang = m.astype(F32) * (math.pi / L)
    return jnp.cos(ang), jnp.sin(ang)


def _filter_consts(L):
    t = jnp.linspace(0.0, 1.0, L, dtype=F32)[:, None]
    w = 2.0 * math.pi * jnp.arange(L, dtype=F32)[:, None] / L
    f = jnp.linspace(1e-4, HY_BANDS - 1, HY_BANDS, dtype=F32)[None, :]
    emb = jnp.concatenate([t, jnp.cos(f * w), -jnp.sin(f * w)], axis=-1)
    max_decay = math.log(HY_DECAY_TARGET) / HY_DECAY_FAST
    min_decay = math.log(HY_DECAY_TARGET) / HY_DECAY_SLOW
    deltas = jnp.abs(jnp.linspace(min_decay, max_decay, HY_DIM, dtype=F32))
    window = jnp.exp(-t * deltas[None, :]) + HY_DECAY_SHIFT
    return emb, window


def _filter_kernel(emb_ref, win_ref, w1, b1, w2, b2, w3, b3, hs_ref, hd_ref, nyq_ref):
    h = jnp.sin(jnp.dot(emb_ref[...], w1[...], precision=HIGHEST, preferred_element_type=F32) + b1[...])
    h = jnp.sin(jnp.dot(h, w2[...], precision=HIGHEST, preferred_element_type=F32) + b2[...])
    h = jnp.dot(h, w3[...], precision=HIGHEST, preferred_element_type=F32) + b3[...]
    win = win_ref[...]
    hf = h[:, :HY_DIM] * win
    hb = h[:, HY_DIM:] * win
    row = lax.broadcasted_iota(jnp.int32, hf.shape, 0)
    hb = jnp.where(row == 0, 0.0, hb)
    alt = jnp.where(row % 2 == 0, 1.0, -1.0)
    hs_ref[...] = hf + hb
    hd_ref[...] = hf - hb
    nyq_ref[...] = jnp.sum((hf + hb) * alt, axis=0, keepdims=True)


def _spectrum_kernel(c_ref, s_ref, hs_ref, hd_ref, kre_ref, kim_ref, *, n_fft):
    i = pl.program_id(0)
    tk = c_ref.shape[0]
    kidx = i * tk + lax.broadcasted_iota(jnp.int32, (tk, 1), 0)
    scale = jnp.where(kidx == 0, 1.0 / n_fft, 2.0 / n_fft)
    kre = jnp.dot(c_ref[...], hs_ref[...], precision=HIGHEST, preferred_element_type=F32)
    kim = -jnp.dot(s_ref[...], hd_ref[...], precision=HIGHEST, preferred_element_type=F32)
    kre_ref[...] = kre * scale
    kim_ref[...] = kim * scale


def _hyena_filter_spectrum(L, cmat, smat, w1, b1, w2, b2, w3, b3):
    emb, window = _filter_consts(L)
    full = lambda a: _const_spec(a.shape)
    ins = [emb, window, w1, b1.reshape(1, -1), w2, b2.reshape(1, -1), w3, b3.reshape(1, -1)]
    hs, hd, nyq = pl.pallas_call(
        _filter_kernel,
        grid=(1,),
        in_specs=[full(a) for a in ins],
        out_specs=[_const_spec((L, HY_DIM)), _const_spec((L, HY_DIM)), _const_spec((1, HY_DIM))],
        out_shape=[jax.ShapeDtypeStruct((L, HY_DIM), F32), jax.ShapeDtypeStruct((L, HY_DIM), F32),
                   jax.ShapeDtypeStruct((1, HY_DIM), F32)],
        compiler_params=_cparams("arbitrary"),
        name="hyena_filter",
    )(*ins)
    tk = min(256, L)
    kre, kim = pl.pallas_call(
        functools.partial(_spectrum_kernel, n_fft=2 * L),
        grid=(L // tk,),
        in_specs=[pl.BlockSpec((tk, L), lambda i: (i, 0)), pl.BlockSpec((tk, L), lambda i: (i, 0)),
                  _const_spec((L, HY_DIM)), _const_spec((L, HY_DIM))],
        out_specs=[pl.BlockSpec((tk, HY_DIM), lambda i: (i, 0))] * 2,
        out_shape=[jax.ShapeDtypeStruct((L, HY_DIM), F32)] * 2,
        compiler_params=_cparams("arbitrary"),
        name="hyena_spectrum",
    )(cmat, smat, hs, hd)
    return kre, kim, nyq * (1.0 / (2 * L))


def _hyena_kernel(z_ref, cw_ref, cb_ref, c_ref, s_ref, kre_ref, kim_ref, nyq_ref, skip_ref, o_ref):
    L = z_ref.shape[0]
    row = lax.broadcasted_iota(jnp.int32, (L, HY_DIM), 0)

    def conv(j):
        cols = slice(j * HY_DIM, (j + 1) * HY_DIM)
        z = z_ref[:, cols].astype(F32)
        zprev = jnp.where(row == 0, 0.0, pltpu.roll(z, 1, 0))
        znext = jnp.where(row == L - 1, 0.0, pltpu.roll(z, L - 1, 0))
        return zprev * cw_ref[0:1, cols] + z * cw_ref[1:2, cols] + znext * cw_ref[2:3, cols] + cb_ref[:, cols]

    u = conv(2) * conv(1)
    ub = u.astype(BF16)
    a = jnp.dot(c_ref[...], ub, preferred_element_type=F32)
    b = jnp.dot(s_ref[...], ub, preferred_element_type=F32)
    kre = kre_ref[...]
    kim = kim_ref[...]
    p = (a * kre + b * kim).astype(BF16)
    q = (b * kre - a * kim).astype(BF16)
    y = jnp.dot(c_ref[...], p, preferred_element_type=F32) + jnp.dot(s_ref[...], q, preferred_element_type=F32)
    alt = jnp.where(row % 2 == 0, 1.0, -1.0)
    vnyq = jnp.sum(u * alt, axis=0, keepdims=True)
    y = y + alt * (vnyq * nyq_ref[...])
    y = y + u * skip_ref[...]
    o_ref[...] = (y * conv(0)).astype(o_ref.dtype)


def _hyena(zhy, nb, L, conv_w, conv_b, cmat_bf, smat_bf, kre, kim, nyq, skip):
    t = zhy.shape[0]
    return pl.pallas_call(
        _hyena_kernel,
        grid=(nb,),
        in_specs=[pl.BlockSpec((L, 3 * HY_DIM), lambda b: (b, 0)),
                  _const_spec((3, 3 * HY_DIM)), _const_spec((1, 3 * HY_DIM)),
                  pl.BlockSpec((L, L), lambda b: (0, 0), pipeline_mode=pl.Buffered(1)),
                  pl.BlockSpec((L, L), lambda b: (0, 0), pipeline_mode=pl.Buffered(1)),
                  _const_spec((L, HY_DIM)), _const_spec((L, HY_DIM)),
                  _const_spec((1, HY_DIM)), _const_spec((1, HY_DIM))],
        out_specs=pl.BlockSpec((L, HY_DIM), lambda b: (b, 0)),
        out_shape=jax.ShapeDtypeStruct((t, HY_DIM), BF16),
        compiler_params=_cparams("arbitrary"),
        name="hyena_conv",
    )(zhy, conv_w, conv_b.reshape(1, -1), cmat_bf, smat_bf, kre, kim, nyq, skip.reshape(1, -1))


SUBLANES = 8


def _split_bf16(w):
    hi = w.astype(BF16)
    return jnp.stack([hi, (w - hi.astype(F32)).astype(BF16)])


def _store_token_tiles(ref, val):
    n = val.shape[0]
    for j in range(val.shape[1] // LANES):
        ref[pl.ds(j, n, stride=SUBLANES), :] = val[:, j * LANES:(j + 1) * LANES]


def _load_token_tiles(ref, n):
    return jnp.concatenate([ref[pl.ds(j, n, stride=SUBLANES), :] for j in range(SUBLANES)], axis=1)


def _merge_kernel(*refs, alpha, n_alias):
    (zgm_ref, yb_ref, yc_ref, gate_ref, x_ref, g1_ref, sh2_ref, sc2_ref, lng_ref, lnb_ref, ws_ref, bs_ref,
     pa_ref, pb_ref, pc_ref, wo_ref, l1g_ref, l1b_ref, wr_ref, br_ref) = refs[:20]
    x1_ref, h2_ref, rt_ref = refs[20 + n_alias:]
    tm = x_ref.shape[0]
    d = x_ref.shape[1]
    gm = _gelu_tanh(zgm_ref[...].astype(F32))
    u = gm[:, :GM_DIM]
    v = _layer_norm(gm[:, GM_DIM:], lng_ref[...], lnb_ref[...]).astype(BF16)
    lane_group = lax.broadcasted_iota(jnp.int32, (GM_CHUNK, GM_DIM), 1) // (GM_DIM // GM_GROUPS)
    ya = []
    for cidx in range(tm // GM_CHUNK):
        rows = slice(cidx * GM_CHUNK, (cidx + 1) * GM_CHUNK)
        r = jnp.dot(ws_ref[...], v[rows], preferred_element_type=F32)
        vv = bs_ref[...]
        for g in range(GM_GROUPS):
            vv = vv + jnp.where(lane_group == g, r[g * GM_CHUNK:(g + 1) * GM_CHUNK], 0.0)
        ya.append(u[rows] * vv)
    ya = jnp.concatenate(ya, axis=0) if len(ya) > 1 else ya[0]
    ma = jnp.dot(ya.astype(BF16), pa_ref[...], preferred_element_type=F32)
    mb = jnp.dot(yb_ref[...], pb_ref[...], preferred_element_type=F32)
    mc = jnp.dot(yc_ref[...], pc_ref[...], preferred_element_type=F32)
    merged = (_sigmoid(gate_ref[:, 0:d].astype(F32)) * ma
              + _sigmoid(gate_ref[:, d:2 * d].astype(F32)) * mb
              + _sigmoid(gate_ref[:, 2 * d:3 * d].astype(F32)) * mc)
    out = jnp.dot(merged.astype(BF16), wo_ref[...], preferred_element_type=F32)
    x1 = _layer_norm(alpha * x_ref[...] + g1_ref[...] * out, l1g_ref[...], l1b_ref[...])
    x1_ref[...] = x1
    h2 = x1 * (1.0 + sc2_ref[...]) + sh2_ref[...]
    _store_token_tiles(h2_ref, h2)
    h2_hi = h2.astype(BF16)
    h2_lo = (h2 - h2_hi.astype(F32)).astype(BF16)
    lg = (jnp.dot(h2_hi, wr_ref[0], preferred_element_type=F32)
          + jnp.dot(h2_hi, wr_ref[1], preferred_element_type=F32)
          + jnp.dot(h2_lo, wr_ref[0], preferred_element_type=F32) + br_ref[...])
    rt_ref[...] = _route(lg)


ROUTE_E0, ROUTE_E1, ROUTE_W0, ROUTE_W1 = 0, 1, 2, 3


def _route(lg):
    neg = jnp.float32(-3.0e38)
    lane_i = lax.broadcasted_iota(jnp.int32, lg.shape, 1)
    lane = lane_i.astype(F32)
    big = jnp.float32(LANES)
    is_g = lane_i < MOE_GROUPS
    gl = jnp.where(is_g, lg, neg)
    gmax = jnp.max(gl, axis=-1, keepdims=True)
    g_idx = jnp.min(jnp.where(gl == gmax, lane, big), axis=-1, keepdims=True)
    g_prob = 1.0 / jnp.sum(jnp.where(is_g, jnp.exp(gl - gmax), 0.0), axis=-1, keepdims=True)
    e_lo = MOE_GROUPS + MOE_EXPERTS_PER_GROUP * g_idx
    el = jnp.where(lane >= e_lo, jnp.where(lane < e_lo + MOE_EXPERTS_PER_GROUP, lg, neg), neg)
    v1 = jnp.max(el, axis=-1, keepdims=True)
    i1 = jnp.min(jnp.where(el == v1, lane, big), axis=-1, keepdims=True)
    el2 = jnp.where(lane == i1, neg, el)
    v2 = jnp.max(el2, axis=-1, keepdims=True)
    i2 = jnp.min(jnp.where(el2 == v2, lane, big), axis=-1, keepdims=True)
    e21 = jnp.exp(v2 - v1)
    w1 = g_prob / (1.0 + e21)
    w2 = w1 * e21
    rec = jnp.where(lane_i == ROUTE_E0, i1 - MOE_GROUPS, 0.0)
    rec = jnp.where(lane_i == ROUTE_E1, i2 - MOE_GROUPS, rec)
    rec = jnp.where(lane_i == ROUTE_W0, w1, rec)
    return jnp.where(lane_i == ROUTE_W1, w2, rec)


def _merge(zgm, yb, yc, gate, x2d, mod3, mod_row0, rows_per_mod, lp, alpha, tm, t_all, row0, prev):
    t, d = x2d.shape
    tiles_per_mod = rows_per_mod // tm
    off = row0 // tm

    def mod_map(piece):
        return lambda i: (mod_row0 + i // tiles_per_mod, 0, piece)

    row = lambda w: pl.BlockSpec((tm, w), lambda i: (i, 0))
    row_off = lambda w: pl.BlockSpec((tm, w), lambda i: (off + i, 0))
    consts = [lp["gm_ln_g"], lp["gm_ln_b"], lp["gm_ws"], lp["gm_bs"], lp["p_a"], lp["p_b"], lp["p_c"],
              lp["w_out"], lp["ln1_g"], lp["ln1_b"], lp["w_router"], lp["b_router"]]
    in_specs = [row(2 * GM_DIM), row(HY_DIM), row(DA_V_W), row(N_BRANCH * d), row(d),
                pl.BlockSpec((None, 1, d), mod_map(2)), pl.BlockSpec((None, 1, d), mod_map(3)),
                pl.BlockSpec((None, 1, d), mod_map(4))] + [_const_spec(a.shape) for a in consts]
    args = [zgm, yb, yc, gate, x2d, mod3, mod3, mod3, *consts]
    aliases = {}
    if prev is not None:
        aliases = {len(args): 1, len(args) + 1: 2}
        in_specs += [pl.BlockSpec(memory_space=pl.ANY)] * 2
        args += list(prev)
    return pl.pallas_call(
        functools.partial(_merge_kernel, alpha=alpha, n_alias=len(aliases)),
        grid=(t // tm,),
        in_specs=in_specs,
        out_specs=[row(d), pl.BlockSpec((tm * SUBLANES, LANES), lambda i: (off + i, 0)), row_off(LANES)],
        out_shape=[jax.ShapeDtypeStruct((t, d), F32), jax.ShapeDtypeStruct((t_all * SUBLANES, LANES), F32),
                   jax.ShapeDtypeStruct((t_all, LANES), F32)],
        input_output_aliases=aliases,
        compiler_params=_cparams("arbitrary"),
        name="merge_ln1",
    )(*args)


FLAG_FIRST, FLAG_LAST, FLAG_FINAL = 1, 2, 4


def _tile_copy(src, src_row, dst, dst_row, sem):
    s0 = pl.multiple_of(src_row * SUBLANES, SUBLANES)
    d0 = pl.multiple_of(dst_row * SUBLANES, SUBLANES)
    return pltpu.make_async_copy(src.at[pl.ds(s0, SUBLANES)], dst.at[pl.ds(d0, SUBLANES)], sem)


def _expert_kernel(vt_ref, ve_ref, vlo_ref, vhi_ref, vflag_ref, src_ref, nsrc_ref, dst_ref, pdst_ref, h2_hbm,
                   wg_ref, wu_ref, wd_ref, y_hbm, xbuf, acc, ybuf, gsem, ssem, *, n_tiles):
    v = pl.program_id(0)
    tile, lo, hi, flag = vt_ref[v], vlo_ref[v], vhi_ref[v], vflag_ref[v]
    tm = acc.shape[0]
    slot = tile % 2
    first = (flag & FLAG_FIRST) != 0

    def issue_gather(idx_ref, to_slot):
        def body(i, carry):
            for j in range(SUBLANES):
                r = i * SUBLANES + j
                _tile_copy(h2_hbm, idx_ref[0, r], xbuf.at[to_slot], r, gsem.at[to_slot]).start()
            return carry

        lax.fori_loop(0, tm // SUBLANES, body, 0)

    def issue_scatter(idx_ref):
        def body(i, carry):
            for j in range(SUBLANES):
                r = i * SUBLANES + j
                _tile_copy(ybuf, r, y_hbm, idx_ref[0, r], ssem).start()
            return carry

        lax.fori_loop(0, tm // SUBLANES, body, 0)

    has_next = tile + 1 < n_tiles
    prefetch = first & (tile > 0) & has_next

    @pl.when(first)
    def _():
        @pl.when(tile == 0)
        def _():
            issue_gather(src_ref, slot)

        pltpu.make_async_copy(h2_hbm.at[pl.ds(0, tm * SUBLANES)], xbuf.at[slot], gsem.at[slot]).wait()

        @pl.when(jnp.logical_not(prefetch))
        def _():
            @pl.when(has_next)
            def _():
                issue_gather(nsrc_ref, 1 - slot)

            @pl.when(tile > 0)
            def _():
                issue_scatter(pdst_ref)

    def compute(with_prefetch):
        xb = _load_token_tiles(xbuf.at[slot], tm).astype(BF16)
        if with_prefetch:
            for r in range(tm):
                _tile_copy(h2_hbm, nsrc_ref[0, r], xbuf.at[1 - slot], r, gsem.at[1 - slot]).start()
                _tile_copy(ybuf, r, y_hbm, pdst_ref[0, r], ssem).start()
        g = jnp.dot(xb, wg_ref[...], preferred_element_type=F32)
        u = jnp.dot(xb, wu_ref[...], preferred_element_type=F32)
        hmid = (g * _sigmoid(g) * u).astype(BF16)
        y = jnp.dot(hmid, wd_ref[...], preferred_element_type=F32)
        row = lax.broadcasted_iota(jnp.int32, (tm, 1), 0)
        y = jnp.where((row >= lo) & (row < hi), y, 0.0)
        if with_prefetch:
            acc[...] = y
            return

        @pl.when(first)
        def _():
            acc[...] = y

        @pl.when(jnp.logical_not(first))
        def _():
            acc[...] += y

    @pl.when(prefetch)
    def _():
        compute(True)

    @pl.when(jnp.logical_not(prefetch) & (hi > lo))
    def _():
        compute(False)

    @pl.when((flag & FLAG_LAST) != 0)
    def _():
        whole = pltpu.make_async_copy(ybuf, y_hbm.at[pl.ds(0, tm * SUBLANES)], ssem)

        @pl.when(tile > 0)
        def _():
            whole.wait()

        _store_token_tiles(ybuf, acc[...])

        @pl.when((flag & FLAG_FINAL) != 0)
        def _():
            issue_scatter(dst_ref)
            whole.wait()


def _experts(h2, plan, w_gate, w_up, w_down, tm):
    t = h2.shape[0] // SUBLANES
    vt, ve, vlo, vhi, vflag, src, dst = plan
    n_vis = vt.shape[0]
    n_tiles = src.shape[0] // tm
    d, hid = w_gate.shape[-2:]
    assert d == SUBLANES * LANES
    idx_spec = lambda nxt: pl.BlockSpec(
        (None, 1, tm), lambda v, vt, *_: (jnp.clip(vt[v] + nxt, 0, n_tiles - 1), 0, 0), memory_space=pltpu.SMEM)
    w_spec = lambda shape: pl.BlockSpec((None,) + shape, lambda v, vt, ve, *_: (ve[v], 0, 0))
    grid_spec = pltpu.PrefetchScalarGridSpec(
        num_scalar_prefetch=5,
        grid=(n_vis,),
        in_specs=[idx_spec(0), idx_spec(1), idx_spec(0), idx_spec(-1), pl.BlockSpec(memory_space=pl.ANY),
                  w_spec((d, hid)), w_spec((d, hid)), w_spec((hid, d))],
        out_specs=pl.BlockSpec(memory_space=pl.ANY),
        scratch_shapes=[pltpu.VMEM((2, tm * SUBLANES, LANES), F32), pltpu.VMEM((tm, d), F32),
                        pltpu.VMEM((tm * SUBLANES, LANES), F32),
                        pltpu.SemaphoreType.DMA((2,)), pltpu.SemaphoreType.DMA(())],
    )
    src3 = src.reshape(n_tiles, 1, tm)
    dst3 = dst.reshape(n_tiles, 1, tm)
    return pl.pallas_call(
        functools.partial(_expert_kernel, n_tiles=n_tiles),
        grid_spec=grid_spec,
        out_shape=jax.ShapeDtypeStruct((MOE_TOP_K * t * SUBLANES, LANES), F32),
        compiler_params=_cparams("arbitrary"),
        name="moe_experts",
    )(vt, ve, vlo, vhi, vflag, src3, src3, dst3, dst3, h2, w_gate, w_up, w_down)


def _visit_plan(route, tm):
    t = route.shape[0]
    a = t * MOE_TOP_K
    i32 = jnp.int32
    eid = route[:, ROUTE_E0:ROUTE_E1 + 1].astype(i32).reshape(a)
    _, order = lax.sort((eid, lax.iota(i32, a)), num_keys=1)
    tok = order // MOE_TOP_K
    src = tok
    dst = (order % MOE_TOP_K) * t + tok
    counts = jnp.sum((eid[:, None] == jnp.arange(MOE_N_EXPERTS, dtype=i32)[None, :]).astype(i32), axis=0)
    ends = jnp.cumsum(counts)
    starts = ends - counts
    n_tiles = a // tm
    first_t = starts // tm
    last_t = jnp.maximum(ends - 1, 0) // tm
    nvis = jnp.where(counts > 0, last_t - first_t + 1, 0)
    cv_end = jnp.cumsum(nvis)
    cv_start = cv_end - nvis
    n_vis = n_tiles + MOE_N_EXPERTS
    v = jnp.arange(n_vis, dtype=i32)
    active = v < cv_end[-1]
    e = jnp.minimum(jnp.sum((cv_end[None, :] <= v[:, None]).astype(i32), axis=1), MOE_N_EXPERTS - 1)
    e_last = jnp.max(jnp.where(counts > 0, jnp.arange(MOE_N_EXPERTS, dtype=i32), 0))
    e = jnp.where(active, e, e_last)
    tile = jnp.where(active, first_t[e] + v - cv_start[e], n_tiles - 1)
    lo = jnp.where(active, jnp.clip(starts[e] - tile * tm, 0, tm), 0)
    hi = jnp.where(active, jnp.clip(ends[e] - tile * tm, 0, tm), 0)
    prev_t = jnp.concatenate([jnp.full((1,), -1, i32), tile[:-1]])
    next_t = jnp.concatenate([tile[1:], jnp.full((1,), -1, i32)])
    is_final = v == cv_end[-1] - 1
    is_last = (next_t != tile) | is_final
    flag = jnp.where(active, (prev_t != tile) * FLAG_FIRST + is_last * FLAG_LAST + is_final * FLAG_FINAL, 0)
    cast = lambda z: z.astype(i32)
    return cast(tile), cast(e), cast(lo), cast(hi), cast(flag), cast(src), cast(dst)


def _combine_kernel(x_ref, y0_ref, y1_ref, rt_ref, g2_ref, lg_ref, lb_ref, o_ref, *, alpha):
    rt = rt_ref[...]
    tm = x_ref.shape[0]
    y = (rt[:, ROUTE_W0:ROUTE_W0 + 1] * _load_token_tiles(y0_ref, tm)
         + rt[:, ROUTE_W1:ROUTE_W1 + 1] * _load_token_tiles(y1_ref, tm))
    o_ref[...] = _layer_norm(alpha * x_ref[...] + g2_ref[...] * y, lg_ref[...], lb_ref[...])


def _combine(x1, y2, route, row0, t_all, mod3, mod_row0, rows_per_mod, ln_g, ln_b, alpha, tm):
    t, d = x1.shape
    tiles_per_mod = rows_per_mod // tm
    t0 = row0 // tm
    t1 = (t_all + row0) // tm
    return pl.pallas_call(
        functools.partial(_combine_kernel, alpha=alpha),
        grid=(t // tm,),
        in_specs=[pl.BlockSpec((tm, d), lambda i: (i, 0)),
                  pl.BlockSpec((tm * SUBLANES, LANES), lambda i: (t0 + i, 0)),
                  pl.BlockSpec((tm * SUBLANES, LANES), lambda i: (t1 + i, 0)),
                  pl.BlockSpec((tm, LANES), lambda i: (t0 + i, 0)),
                  pl.BlockSpec((None, 1, d), lambda i: (mod_row0 + i // tiles_per_mod, 0, 5)),
                  _const_spec((1, d)), _const_spec((1, d))],
        out_specs=pl.BlockSpec((tm, d), lambda i: (i, 0)),
        out_shape=jax.ShapeDtypeStruct((t, d), F32),
        compiler_params=_cparams("arbitrary"),
        name="combine_ln2",
    )(x1, y2, y2, route, mod3, ln_g.reshape(1, d), ln_b.reshape(1, d))


def _pick_tile(n, pref):
    tm = min(pref, n)
    while n % tm:
        tm //= 2
    return tm


def kernel(x, c, ctx, c_ctx, ada_w, ada_b, w_in, gm_ln_g, gm_ln_b, gm_ws, gm_bs, hy_conv_w, hy_conv_b,
           hy_f_w1, hy_f_b1, hy_f_w2, hy_f_b2, hy_f_w3, hy_f_b3, hy_skip, da_lq1, da_lk1, da_lq2, da_lk2,
           da_norm_g, p_a, p_b, p_c, w_out, ln1_g, ln1_b, moe_wg, moe_bg, moe_we, moe_be,
           ex_w_gate, ex_w_up, ex_w_down, ln2_g, ln2_b):
    B, L, D = x.shape
    Lc = ctx.shape[1]
    depth = ada_w.shape[0]
    alpha = (2.0 * depth) ** 0.25
    T, Tc = B * L, B * Lc
    moe_tm = 512 if T >= 8192 else 64

    mp = -(-(B + 1) // 8) * 8
    c_all = jnp.zeros((mp, D), F32).at[:B].set(c).at[B].set(c_ctx)
    mod = _modulation(c_all, ada_w, ada_b)

    rope_tabs = _rope_tables(L // GRID_W)
    cm, sm = _dft_tables(L)
    cm_bf, sm_bf = cm.astype(BF16), sm.astype(BF16)
    cmc, smc = _dft_tables(Lc)
    cmc_bf, smc_bf = cmc.astype(BF16), smc.astype(BF16)

    seg_all = ((OFF_GM, OFF_HY, "gm"), (OFF_HY, OFF_Q, "hy"), (OFF_Q, OFF_K, "q"), (OFF_K, OFF_V, "k"),
               (OFF_V, OFF_GATE, "v"), (OFF_GATE, OFF_GATE + N_BRANCH * D, "gate"))
    seg_kv = ((0, DA_QK_W, "k"), (DA_QK_W, DA_QK_W + DA_V_W, "v"))

    tm_l = _pick_tile(L, 256)
    tm_c = _pick_tile(Lc, 256)
    tq_l = _pick_tile(L, 256)
    tq_c = _pick_tile(Lc, 256)

    xs = x.reshape(T, D)
    xc = ctx.reshape(Tc, D)
    for l in range(depth):
        last = l == depth - 1
        lam_init = 0.8 - 0.6 * math.exp(-0.3 * l)
        mod3 = mod[l].reshape(mp, 1, 6 * D)
        w_l = w_in[l].astype(BF16)
        lparams = [a[l].reshape(1, DA_HEAD_DIM) for a in (da_lq1, da_lk1, da_lq2, da_lk2)]
        norm_g = da_norm_g[l].reshape(1, DA_V_DIM)
        lp = {
            "gm_ln_g": gm_ln_g[l].reshape(1, GM_DIM), "gm_ln_b": gm_ln_b[l].reshape(1, GM_DIM),
            "gm_ws": gm_ws[l].reshape(GM_GROUPS * GM_CHUNK, GM_CHUNK).astype(BF16),
            "gm_bs": jnp.repeat(jnp.transpose(gm_bs[l]), GM_DIM // GM_GROUPS, axis=1),
            "p_a": p_a[l].astype(BF16), "p_b": p_b[l].astype(BF16), "p_c": p_c[l].astype(BF16),
            "w_out": w_out[l].astype(BF16),
            "ln1_g": ln1_g[l].reshape(1, D), "ln1_b": ln1_b[l].reshape(1, D),
            "w_router": _split_bf16(jnp.zeros((D, LANES), F32).at[:, :MOE_GROUPS].set(moe_wg[l])
                                    .at[:, MOE_GROUPS:MOE_GROUPS + MOE_N_EXPERTS].set(moe_we[l])),
            "b_router": jnp.zeros((1, LANES), F32).at[0, :MOE_GROUPS].set(moe_bg[l])
                           .at[0, MOE_GROUPS:MOE_GROUPS + MOE_N_EXPERTS].set(moe_be[l]),
        }
        fw = (hy_f_w1[l], hy_f_b1[l], hy_f_w2[l], hy_f_b2[l], hy_f_w3[l], hy_f_b3[l])

        zgm, zhy, q, k, v, gate = _inproj(xs, mod3, 0, L, w_l, seg_all, rope_tabs, L, tm_l)
        if last:
            k_c, v_c = _inproj(xc, mod3, B, Tc, w_l[:, OFF_K:OFF_GATE], seg_kv, None, Lc, tm_c)
        else:
            zgm_c, zhy_c, q_c, k_c, v_c, gate_c = _inproj(xc, mod3, B, Tc, w_l, seg_all, None, Lc, tm_c)
        y_c = _attention(q, [(k, v, L), (k_c, v_c, Lc)], lparams, norm_g, lam_init, B, L, tq_l)
        kre, kim, nyq = _hyena_filter_spectrum(L, cm, sm, *fw)
        y_b = _hyena(zhy, B, L, hy_conv_w[l], hy_conv_b[l], cm_bf, sm_bf, kre, kim, nyq, hy_skip[l])
        t_all = T if last else T + Tc
        x1, h2, route = _merge(zgm, y_b, y_c, gate, xs, mod3, 0, L, lp, alpha, tm_l, T, 0, None)

        if not last:
            yc_c = _attention(q_c, [(k_c, v_c, Lc)], lparams, norm_g, lam_init, B, Lc, tq_c)
            kre_c, kim_c, nyq_c = _hyena_filter_spectrum(Lc, cmc, smc, *fw)
            yb_c = _hyena(zhy_c, B, Lc, hy_conv_w[l], hy_conv_b[l], cmc_bf, smc_bf, kre_c, kim_c, nyq_c,
                          hy_skip[l])
            x1c, h2c, route_c = _merge(zgm_c, yb_c, yc_c, gate_c, xc, mod3, B, Tc, lp, alpha, tm_c, Tc, 0, None)
            h2 = jnp.concatenate([h2, h2c], axis=0)
            route = jnp.concatenate([route, route_c], axis=0)

        plan = _visit_plan(route, moe_tm)
        y2 = _experts(h2, plan, ex_w_gate[l].astype(BF16), ex_w_up[l].astype(BF16),
                      ex_w_down[l].astype(BF16), moe_tm)
        xs = _combine(x1, y2, route, 0, t_all, mod3, 0, L, ln2_g[l], ln2_b[l], alpha, tm_l)
        if not last:
            xc = _combine(x1c, y2, route, T, t_all, mod3, B, Tc, ln2_g[l], ln2_b[l], alpha, tm_c)
    return xs.reshape(B, L, D)
```

```python
import functools
import math

import numpy as np
import jax
import jax.numpy as jnp
from jax import lax
from jax.experimental import pallas as pl
from jax.experimental.pallas import tpu as pltpu

F32 = jnp.float32
BF16 = jnp.bfloat16
HIGHEST = lax.Precision.HIGHEST

GRID_W = 64
GM_DIM = 256
GM_GROUPS = 4
GM_CHUNK = 128
HY_DIM = 256
HY_EMB = 33
HY_BANDS = (HY_EMB - 1) // 2
HY_DECAY_FAST = 0.3
HY_DECAY_SLOW = 1.5
HY_DECAY_TARGET = 1e-2
HY_DECAY_SHIFT = 0.05
DA_HEADS = 4
DA_HEAD_DIM = 64
DA_V_DIM = 2 * DA_HEAD_DIM
DA_QK_W = DA_HEADS * 2 * DA_HEAD_DIM
DA_V_W = DA_HEADS * DA_V_DIM
ROPE_BASE = 10000.0
N_BRANCH = 3
OFF_GM = 0
OFF_HY = OFF_GM + 2 * GM_DIM
OFF_Q = OFF_HY + 3 * HY_DIM
OFF_K = OFF_Q + DA_QK_W
OFF_V = OFF_K + DA_QK_W
OFF_GATE = OFF_V + DA_V_W
MOE_GROUPS = 4
MOE_EXPERTS_PER_GROUP = 8
MOE_N_EXPERTS = MOE_GROUPS * MOE_EXPERTS_PER_GROUP
MOE_TOP_K = 2
LN_EPS = 1e-5
LANES = 128
VMEM_LIMIT = 56 * 1024 * 1024


def _cparams(*sem):
    return pltpu.CompilerParams(dimension_semantics=sem, vmem_limit_bytes=VMEM_LIMIT)


def _sigmoid(x):
    return 1.0 / (1.0 + jnp.exp(-x))


def _layer_norm(x, g, b):
    mu = jnp.mean(x, axis=-1, keepdims=True)
    xc = x - mu
    var = jnp.mean(xc * xc, axis=-1, keepdims=True)
    return xc * lax.rsqrt(var + LN_EPS) * g + b


def _gelu_tanh(x):
    return 0.5 * x * (1.0 + jnp.tanh(math.sqrt(2.0 / math.pi) * (x + 0.044715 * (x * x * x))))


def _const_spec(shape):
    nd = len(shape)
    return pl.BlockSpec(shape, lambda *_: (0,) * nd)


def _mod_kernel(c_ref, w_ref, b_ref, o_ref):
    c = c_ref[...]
    s = c * _sigmoid(c)
    o_ref[...] = jnp.dot(s, w_ref[...], precision=HIGHEST, preferred_element_type=F32) + b_ref[...]


def _modulation(c_all, ada_w, ada_b):
    depth, d, n = ada_w.shape
    mp = c_all.shape[0]
    tn = 512
    return pl.pallas_call(
        _mod_kernel,
        grid=(depth, n // tn),
        in_specs=[pl.BlockSpec((mp, d), lambda l, j: (0, 0)),
                  pl.BlockSpec((None, d, tn), lambda l, j: (l, 0, j)),
                  pl.BlockSpec((None, 1, tn), lambda l, j: (l, 0, j))],
        out_specs=pl.BlockSpec((None, mp, tn), lambda l, j: (l, 0, j)),
        out_shape=jax.ShapeDtypeStruct((depth, mp, n), F32),
        compiler_params=_cparams("arbitrary", "arbitrary"),
        name="adaln_mod",
    )(c_all, ada_w, ada_b.reshape(depth, 1, n))


def _rope_tables(rows):
    n_freq = DA_HEAD_DIM // 4
    row = jnp.broadcast_to(jnp.arange(rows)[:, None], (rows, GRID_W)).reshape(-1).astype(F32)
    col = jnp.broadcast_to(jnp.arange(GRID_W)[None, :], (rows, GRID_W)).reshape(-1).astype(F32)
    inv = ROPE_BASE ** (-jnp.arange(n_freq, dtype=F32) / n_freq)
    ang_r = row[:, None] * inv
    ang_c = col[:, None] * inv
    c64 = jnp.concatenate([jnp.cos(ang_r), jnp.cos(ang_r), jnp.cos(ang_c), jnp.cos(ang_c)], axis=-1)
    s64 = jnp.concatenate([-jnp.sin(ang_r), jnp.sin(ang_r), -jnp.sin(ang_c), jnp.sin(ang_c)], axis=-1)
    return jnp.tile(c64, (1, LANES // DA_HEAD_DIM)), jnp.tile(s64, (1, LANES // DA_HEAD_DIM))


def _rope_block(xb, cos, sin):
    lane = lax.broadcasted_iota(jnp.int32, xb.shape, 1)
    n_freq = DA_HEAD_DIM // 4
    first_half = (lane % (2 * n_freq)) < n_freq
    partner = jnp.where(first_half, pltpu.roll(xb, LANES - n_freq, 1), pltpu.roll(xb, n_freq, 1))
    return xb * cos + partner * sin


def _inproj_kernel(*refs, segs, use_rope, n_chunk):
    if use_rope:
        x_ref, sh_ref, sc_ref, w_ref, cos_ref, sin_ref = refs[:6]
        out_refs = refs[6:]
    else:
        x_ref, sh_ref, sc_ref, w_ref = refs[:4]
        out_refs = refs[4:]
    h = (x_ref[...] * (1.0 + sc_ref[...]) + sh_ref[...]).astype(BF16)
    for (a, b, kind), o_ref in zip(segs, out_refs):
        for c0 in range(a, b, n_chunk):
            c1 = min(c0 + n_chunk, b)
            acc = jnp.dot(h, w_ref[:, c0:c1], preferred_element_type=F32)
            if kind == "q":
                acc = acc * (DA_HEAD_DIM ** -0.5 * math.log2(math.e))
            if use_rope and kind in ("q", "k"):
                cos = cos_ref[...]
                sin = sin_ref[...]
                for j in range((c1 - c0) // LANES):
                    blk = _rope_block(acc[:, j * LANES:(j + 1) * LANES], cos, sin)
                    o_ref[:, c0 - a + j * LANES:c0 - a + (j + 1) * LANES] = blk.astype(o_ref.dtype)
            else:
                o_ref[:, c0 - a:c1 - a] = acc.astype(o_ref.dtype)


def _inproj(x2d, mod3, mod_row0, rows_per_mod, w, segs, rope_tabs, seq_len, tm):
    t, d = x2d.shape
    n = w.shape[1]
    use_rope = rope_tabs is not None
    tiles_per_mod = rows_per_mod // tm
    tiles_per_seq = seq_len // tm

    def mod_map(piece):
        return lambda i: (mod_row0 + i // tiles_per_mod, 0, piece)

    in_specs = [pl.BlockSpec((tm, d), lambda i: (i, 0)),
                pl.BlockSpec((None, 1, d), mod_map(0)),
                pl.BlockSpec((None, 1, d), mod_map(1)),
                pl.BlockSpec((d, n), lambda i: (0, 0), pipeline_mode=pl.Buffered(1))]
    args = [x2d, mod3, mod3, w]
    if use_rope:
        in_specs += [pl.BlockSpec((tm, LANES), lambda i: (i % tiles_per_seq, 0))] * 2
        args += list(rope_tabs)
    out_specs = [pl.BlockSpec((tm, b - a), lambda i: (i, 0)) for a, b, _ in segs]
    out_shape = [jax.ShapeDtypeStruct((t, b - a), BF16) for a, b, _ in segs]
    return pl.pallas_call(
        functools.partial(_inproj_kernel, segs=segs, use_rope=use_rope, n_chunk=512),
        grid=(t // tm,),
        in_specs=in_specs, out_specs=out_specs, out_shape=out_shape,
        compiler_params=_cparams("arbitrary"),
        name="inproj",
    )(*args)


ATTN_KEY_CHUNK = 256


def _attn_kernel(*refs, src_lens, lam_init):
    n_src = len(src_lens)
    lq1, lk1, lq2, lk2, g_ref, q_ref = refs[:6]
    kv_refs = refs[6:6 + 2 * n_src]
    o_ref, s_scr = refs[6 + 2 * n_src:]
    lam = (jnp.exp(jnp.sum(lq1[...] * lk1[...], axis=-1, keepdims=True))
           - jnp.exp(jnp.sum(lq2[...] * lk2[...], axis=-1, keepdims=True)) + lam_init)
    tq = q_ref.shape[0]
    lane = lax.broadcasted_iota(jnp.int32, (tq, LANES), 1)
    dn = (((1,), (1,)), ((), ()))
    chunks = []
    off = 0
    for j, n in enumerate(src_lens):
        kc = min(ATTN_KEY_CHUNK, n)
        for st in range(0, n, kc):
            chunks.append((j, st, kc, off))
            off += kc
    for h in range(DA_HEADS):
        cols = slice(h * LANES, (h + 1) * LANES)
        qh = q_ref[:, cols]
        zero = jnp.zeros_like(qh)
        om = []
        for m in range(2):
            qm = jnp.where(lane < DA_HEAD_DIM if m == 0 else lane >= DA_HEAD_DIM, qh, zero)
            mlane = None
            for j, st, kc, off in chunks:
                s_c = lax.dot_general(qm, kv_refs[2 * j][st:st + kc, cols], dn, preferred_element_type=F32)
                s_scr[:, off:off + kc] = s_c
                for b in range(kc // LANES):
                    blk = s_c[:, b * LANES:(b + 1) * LANES]
                    mlane = blk if mlane is None else jnp.maximum(mlane, blk)
            mx = jnp.max(mlane, axis=-1, keepdims=True)
            acc = None
            for j, st, kc, off in chunks:
                p = jnp.exp2(s_scr[:, off:off + kc] - mx).astype(BF16)
                v_aug = jnp.concatenate([kv_refs[2 * j + 1][st:st + kc, cols], jnp.ones((kc, LANES), BF16)],
                                        axis=1)
                d = jnp.dot(p, v_aug, preferred_element_type=F32)
                acc = d if acc is None else acc + d
            om.append(acc[:, :LANES] * (1.0 / acc[:, LANES:LANES + 1]))
        o = om[0] - lam * om[1]
        ms = jnp.mean(o * o, axis=-1, keepdims=True)
        o = o * lax.rsqrt(ms + LN_EPS) * g_ref[...] * (1.0 - lam_init)
        o_ref[:, cols] = o.astype(o_ref.dtype)


def _attention(q, kvs, lparams, norm_g, lam_init, nb, lq, tq):
    t = q.shape[0]
    qt = lq // tq
    in_specs = [_const_spec((1, DA_HEAD_DIM))] * 4 + [_const_spec((1, DA_V_DIM))]
    in_specs.append(pl.BlockSpec((tq, DA_QK_W), lambda b, i: (b * qt + i, 0)))
    args = list(lparams) + [norm_g, q]
    for k, v, lk in kvs:
        in_specs += [pl.BlockSpec((lk, DA_QK_W), lambda b, i: (b, 0)),
                     pl.BlockSpec((lk, DA_V_W), lambda b, i: (b, 0))]
        args += [k, v]
    src_lens = tuple(lk for _, _, lk in kvs)
    return pl.pallas_call(
        functools.partial(_attn_kernel, src_lens=src_lens, lam_init=lam_init),
        grid=(nb, qt),
        in_specs=in_specs,
        out_specs=pl.BlockSpec((tq, DA_V_W), lambda b, i: (b * qt + i, 0)),
        out_shape=jax.ShapeDtypeStruct((t, DA_V_W), BF16),
        scratch_shapes=[pltpu.VMEM((tq, sum(src_lens)), F32)],
        compiler_params=_cparams("arbitrary", "arbitrary"),
        name="diff_attn",
    )(*args)


def _dft_tables(L):
    k = jnp.arange(L, dtype=jnp.int32)
    m = (k[:, None] * k[None, :]) % (2 * L)
    ang = m.astype(F32) * (math.pi / L)
    return jnp.cos(ang), jnp.sin(ang)


def _filter_consts(L):
    t = jnp.linspace(0.0, 1.0, L, dtype=F32)[:, None]
    w = 2.0 * math.pi * jnp.arange(L, dtype=F32)[:, None] / L
    f = jnp.linspace(1e-4, HY_BANDS - 1, HY_BANDS, dtype=F32)[None, :]
    emb = jnp.concatenate([t, jnp.cos(f * w), -jnp.sin(f * w)], axis=-1)
    max_decay = math.log(HY_DECAY_TARGET) / HY_DECAY_FAST
    min_decay = math.log(HY_DECAY_TARGET) / HY_DECAY_SLOW
    deltas = jnp.abs(jnp.linspace(min_decay, max_decay, HY_DIM, dtype=F32))
    window = jnp.exp(-t * deltas[None, :]) + HY_DECAY_SHIFT
    return emb, window


def _filter_kernel(emb_ref, win_ref, w1, b1, w2, b2, w3, b3, hs_ref, hd_ref, nyq_ref):
    h = jnp.sin(jnp.dot(emb_ref[...], w1[...], precision=HIGHEST, preferred_element_type=F32) + b1[...])
    h = jnp.sin(jnp.dot(h, w2[...], precision=HIGHEST, preferred_element_type=F32) + b2[...])
    h = jnp.dot(h, w3[...], precision=HIGHEST, preferred_element_type=F32) + b3[...]
    win = win_ref[...]
    hf = h[:, :HY_DIM] * win
    hb = h[:, HY_DIM:] * win
    row = lax.broadcasted_iota(jnp.int32, hf.shape, 0)
    hb = jnp.where(row == 0, 0.0, hb)
    alt = jnp.where(row % 2 == 0, 1.0, -1.0)
    hs_ref[...] = hf + hb
    hd_ref[...] = hf - hb
    nyq_ref[...] = jnp.sum((hf + hb) * alt, axis=0, keepdims=True)


def _spectrum_kernel(c_ref, s_ref, hs_ref, hd_ref, kre_ref, kim_ref, *, n_fft):
    i = pl.program_id(0)
    tk = c_ref.shape[0]
    kidx = i * tk + lax.broadcasted_iota(jnp.int32, (tk, 1), 0)
    scale = jnp.where(kidx == 0, 1.0 / n_fft, 2.0 / n_fft)
    kre = jnp.dot(c_ref[...], hs_ref[...], precision=HIGHEST, preferred_element_type=F32)
    kim = -jnp.dot(s_ref[...], hd_ref[...], precision=HIGHEST, preferred_element_type=F32)
    kre_ref[...] = kre * scale
    kim_ref[...] = kim * scale


def _hyena_filter_spectrum(L, cmat, smat, w1, b1, w2, b2, w3, b3):
    emb, window = _filter_consts(L)
    full = lambda a: _const_spec(a.shape)
    ins = [emb, window, w1, b1.reshape(1, -1), w2, b2.reshape(1, -1), w3, b3.reshape(1, -1)]
    hs, hd, nyq = pl.pallas_call(
        _filter_kernel,
        grid=(1,),
        in_specs=[full(a) for a in ins],
        out_specs=[_const_spec((L, HY_DIM)), _const_spec((L, HY_DIM)), _const_spec((1, HY_DIM))],
        out_shape=[jax.ShapeDtypeStruct((L, HY_DIM), F32), jax.ShapeDtypeStruct((L, HY_DIM), F32),
                   jax.ShapeDtypeStruct((1, HY_DIM), F32)],
        compiler_params=_cparams("arbitrary"),
        name="hyena_filter",
    )(*ins)
    tk = min(256, L)
    kre, kim = pl.pallas_call(
        functools.partial(_spectrum_kernel, n_fft=2 * L),
        grid=(L // tk,),
        in_specs=[pl.BlockSpec((tk, L), lambda i: (i, 0)), pl.BlockSpec((tk, L), lambda i: (i, 0)),
                  _const_spec((L, HY_DIM)), _const_spec((L, HY_DIM))],
        out_specs=[pl.BlockSpec((tk, HY_DIM), lambda i: (i, 0))] * 2,
        out_shape=[jax.ShapeDtypeStruct((L, HY_DIM), F32)] * 2,
        compiler_params=_cparams("arbitrary"),
        name="hyena_spectrum",
    )(cmat, smat, hs, hd)
    return kre, kim, nyq * (1.0 / (2 * L))


HY_ROW_BLOCK = 512


def _hyena_kernel(z_ref, cw_ref, cb_ref, c_ref, s_ref, kre_ref, kim_ref, nyq_ref, skip_ref, o_ref,
                  u_ref, x0_ref, p_ref, q_ref):
    L = z_ref.shape[0]
    row = lax.broadcasted_iota(jnp.int32, (L, HY_DIM), 0)

    def conv(j):
        cols = slice(j * HY_DIM, (j + 1) * HY_DIM)
        z = z_ref[:, cols].astype(F32)
        zprev = jnp.where(row == 0, 0.0, pltpu.roll(z, 1, 0))
        znext = jnp.where(row == L - 1, 0.0, pltpu.roll(z, L - 1, 0))
        return zprev * cw_ref[0:1, cols] + z * cw_ref[1:2, cols] + znext * cw_ref[2:3, cols] + cb_ref[:, cols]

    u = conv(2) * conv(1)
    ub = u.astype(BF16)
    u_ref[...] = u
    alt = jnp.where(row % 2 == 0, 1.0, -1.0)
    nyq_term = jnp.sum(u * alt, axis=0, keepdims=True) * nyq_ref[...]
    x0_ref[...] = conv(0)
    blk = min(HY_ROW_BLOCK, L)
    for r in range(0, L, blk):
        rows = slice(r, r + blk)
        a = jnp.dot(c_ref[rows, :], ub, preferred_element_type=F32)
        b = jnp.dot(s_ref[rows, :], ub, preferred_element_type=F32)
        kre = kre_ref[rows, :]
        kim = kim_ref[rows, :]
        p_ref[rows, :] = (a * kre + b * kim).astype(BF16)
        q_ref[rows, :] = (b * kre - a * kim).astype(BF16)
    for r in range(0, L, blk):
        rows = slice(r, r + blk)
        y = (jnp.dot(c_ref[rows, :], p_ref[...], preferred_element_type=F32)
             + jnp.dot(s_ref[rows, :], q_ref[...], preferred_element_type=F32))
        ub_rows = u_ref[rows, :]
        row_b = lax.broadcasted_iota(jnp.int32, (blk, HY_DIM), 0)
        y = y + jnp.where(row_b % 2 == 0, nyq_term, -nyq_term) + ub_rows * skip_ref[...]
        o_ref[rows, :] = (y * x0_ref[rows, :]).astype(o_ref.dtype)


def _hyena(zhy, nb, L, conv_w, conv_b, cmat_bf, smat_bf, kre, kim, nyq, skip):
    t = zhy.shape[0]
    return pl.pallas_call(
        _hyena_kernel,
        grid=(nb,),
        in_specs=[pl.BlockSpec((L, 3 * HY_DIM), lambda b: (b, 0)),
                  _const_spec((3, 3 * HY_DIM)), _const_spec((1, 3 * HY_DIM)),
                  pl.BlockSpec((L, L), lambda b: (0, 0), pipeline_mode=pl.Buffered(1)),
                  pl.BlockSpec((L, L), lambda b: (0, 0), pipeline_mode=pl.Buffered(1)),
                  _const_spec((L, HY_DIM)), _const_spec((L, HY_DIM)),
                  _const_spec((1, HY_DIM)), _const_spec((1, HY_DIM))],
        out_specs=pl.BlockSpec((L, HY_DIM), lambda b: (b, 0)),
        out_shape=jax.ShapeDtypeStruct((t, HY_DIM), BF16),
        scratch_shapes=[pltpu.VMEM((L, HY_DIM), F32), pltpu.VMEM((L, HY_DIM), F32),
                        pltpu.VMEM((L, HY_DIM), BF16), pltpu.VMEM((L, HY_DIM), BF16)],
        compiler_params=_cparams("arbitrary"),
        name="hyena_conv",
    )(zhy, conv_w, conv_b.reshape(1, -1), cmat_bf, smat_bf, kre, kim, nyq, skip.reshape(1, -1))


SUBLANES = 8


def _split_bf16(w):
    hi = w.astype(BF16)
    return jnp.stack([hi, (w - hi.astype(F32)).astype(BF16)])


def _store_token_tiles(ref, val):
    n = val.shape[0]
    for j in range(val.shape[1] // LANES):
        ref[pl.ds(j, n, stride=SUBLANES), :] = val[:, j * LANES:(j + 1) * LANES]


def _load_token_tiles(ref, n):
    return jnp.concatenate([ref[pl.ds(j, n, stride=SUBLANES), :] for j in range(SUBLANES)], axis=1)


def _merge_kernel(*refs, alpha, n_alias):
    (zgm_ref, yb_ref, yc_ref, gate_ref, x_ref, g1_ref, sh2_ref, sc2_ref, lng_ref, lnb_ref, ws_ref, bs_ref,
     pa_ref, pb_ref, pc_ref, wo_ref, l1g_ref, l1b_ref, wr_ref, br_ref) = refs[:20]
    x1_ref, h2_ref, rt_ref = refs[20 + n_alias:]
    tm = x_ref.shape[0]
    d = x_ref.shape[1]
    gm = _gelu_tanh(zgm_ref[...].astype(F32))
    u = gm[:, :GM_DIM]
    v = _layer_norm(gm[:, GM_DIM:], lng_ref[...], lnb_ref[...]).astype(BF16)
    lane_group = lax.broadcasted_iota(jnp.int32, (GM_CHUNK, GM_DIM), 1) // (GM_DIM // GM_GROUPS)
    ya = []
    for cidx in range(tm // GM_CHUNK):
        rows = slice(cidx * GM_CHUNK, (cidx + 1) * GM_CHUNK)
        r = jnp.dot(ws_ref[...], v[rows], preferred_element_type=F32)
        vv = bs_ref[...]
        for g in range(GM_GROUPS):
            vv = vv + jnp.where(lane_group == g, r[g * GM_CHUNK:(g + 1) * GM_CHUNK], 0.0)
        ya.append(u[rows] * vv)
    ya = jnp.concatenate(ya, axis=0) if len(ya) > 1 else ya[0]
    ma = jnp.dot(ya.astype(BF16), pa_ref[...], preferred_element_type=F32)
    mb = jnp.dot(yb_ref[...], pb_ref[...], preferred_element_type=F32)
    mc = jnp.dot(yc_ref[...], pc_ref[...], preferred_element_type=F32)
    merged = (_sigmoid(gate_ref[:, 0:d].astype(F32)) * ma
              + _sigmoid(gate_ref[:, d:2 * d].astype(F32)) * mb
              + _sigmoid(gate_ref[:, 2 * d:3 * d].astype(F32)) * mc)
    out = jnp.dot(merged.astype(BF16), wo_ref[...], preferred_element_type=F32)
    x1 = _layer_norm(alpha * x_ref[...] + g1_ref[...] * out, l1g_ref[...], l1b_ref[...])
    x1_ref[...] = x1
    h2 = x1 * (1.0 + sc2_ref[...]) + sh2_ref[...]
    _store_token_tiles(h2_ref, h2)
    h2_hi = h2.astype(BF16)
    h2_lo = (h2 - h2_hi.astype(F32)).astype(BF16)
    lg = (jnp.dot(h2_hi, wr_ref[0], preferred_element_type=F32)
          + jnp.dot(h2_hi, wr_ref[1], preferred_element_type=F32)
          + jnp.dot(h2_lo, wr_ref[0], preferred_element_type=F32) + br_ref[...])
    rt_ref[...] = _route(lg)


ROUTE_E0, ROUTE_E1, ROUTE_W0, ROUTE_W1 = 0, 1, 2, 3


def _route(lg):
    neg = jnp.float32(-3.0e38)
    lane_i = lax.broadcasted_iota(jnp.int32, lg.shape, 1)
    lane = lane_i.astype(F32)
    big = jnp.float32(LANES)
    is_g = lane_i < MOE_GROUPS
    gl = jnp.where(is_g, lg, neg)
    gmax = jnp.max(gl, axis=-1, keepdims=True)
    g_idx = jnp.min(jnp.where(gl == gmax, lane, big), axis=-1, keepdims=True)
    g_prob = 1.0 / jnp.sum(jnp.where(is_g, jnp.exp(gl - gmax), 0.0), axis=-1, keepdims=True)
    e_lo = MOE_GROUPS + MOE_EXPERTS_PER_GROUP * g_idx
    el = jnp.where(lane >= e_lo, jnp.where(lane < e_lo + MOE_EXPERTS_PER_GROUP, lg, neg), neg)
    v1 = jnp.max(el, axis=-1, keepdims=True)
    i1 = jnp.min(jnp.where(el == v1, lane, big), axis=-1, keepdims=True)
    el2 = jnp.where(lane == i1, neg, el)
    v2 = jnp.max(el2, axis=-1, keepdims=True)
    i2 = jnp.min(jnp.where(el2 == v2, lane, big), axis=-1, keepdims=True)
    e21 = jnp.exp(v2 - v1)
    w1 = g_prob / (1.0 + e21)
    w2 = w1 * e21
    rec = jnp.where(lane_i == ROUTE_E0, i1 - MOE_GROUPS, 0.0)
    rec = jnp.where(lane_i == ROUTE_E1, i2 - MOE_GROUPS, rec)
    rec = jnp.where(lane_i == ROUTE_W0, w1, rec)
    return jnp.where(lane_i == ROUTE_W1, w2, rec)


def _merge(zgm, yb, yc, gate, x2d, mod3, mod_row0, rows_per_mod, lp, alpha, tm, t_all, row0, prev):
    t, d = x2d.shape
    tiles_per_mod = rows_per_mod // tm
    off = row0 // tm

    def mod_map(piece):
        return lambda i: (mod_row0 + i // tiles_per_mod, 0, piece)

    row = lambda w: pl.BlockSpec((tm, w), lambda i: (i, 0))
    row_off = lambda w: pl.BlockSpec((tm, w), lambda i: (off + i, 0))
    consts = [lp["gm_ln_g"], lp["gm_ln_b"], lp["gm_ws"], lp["gm_bs"], lp["p_a"], lp["p_b"], lp["p_c"],
              lp["w_out"], lp["ln1_g"], lp["ln1_b"], lp["w_router"], lp["b_router"]]
    in_specs = [row(2 * GM_DIM), row(HY_DIM), row(DA_V_W), row(N_BRANCH * d), row(d),
                pl.BlockSpec((None, 1, d), mod_map(2)), pl.BlockSpec((None, 1, d), mod_map(3)),
                pl.BlockSpec((None, 1, d), mod_map(4))] + [_const_spec(a.shape) for a in consts]
    args = [zgm, yb, yc, gate, x2d, mod3, mod3, mod3, *consts]
    aliases = {}
    if prev is not None:
        aliases = {len(args): 1, len(args) + 1: 2}
        in_specs += [pl.BlockSpec(memory_space=pl.ANY)] * 2
        args += list(prev)
    return pl.pallas_call(
        functools.partial(_merge_kernel, alpha=alpha, n_alias=len(aliases)),
        grid=(t // tm,),
        in_specs=in_specs,
        out_specs=[row(d), pl.BlockSpec((tm * SUBLANES, LANES), lambda i: (off + i, 0)), row_off(LANES)],
        out_shape=[jax.ShapeDtypeStruct((t, d), F32), jax.ShapeDtypeStruct((t_all * SUBLANES, LANES), F32),
                   jax.ShapeDtypeStruct((t_all, LANES), F32)],
        input_output_aliases=aliases,
        compiler_params=_cparams("arbitrary"),
        name="merge_ln1",
    )(*args)


FLAG_FIRST, FLAG_LAST, FLAG_FINAL, FLAG_NEW_EXPERT = 1, 2, 4, 8


def _tile_copy(src, src_row, dst, dst_row, sem):
    s0 = pl.multiple_of(src_row * SUBLANES, SUBLANES)
    d0 = pl.multiple_of(dst_row * SUBLANES, SUBLANES)
    return pltpu.make_async_copy(src.at[pl.ds(s0, SUBLANES)], dst.at[pl.ds(d0, SUBLANES)], sem)


def _expert_kernel(vt_ref, ve_ref, vlo_ref, vhi_ref, vflag_ref, src_ref, nsrc_ref, dst_ref, pdst_ref, h2_hbm,
                   wg32_ref, wu32_ref, wd32_ref, y_hbm, xbuf, acc, ybuf, wg_ref, wu_ref, wd_ref, gsem, ssem,
                   *, n_tiles):
    v = pl.program_id(0)
    tile, lo, hi, flag = vt_ref[v], vlo_ref[v], vhi_ref[v], vflag_ref[v]
    tm = acc.shape[0]
    slot = tile % 2
    first = (flag & FLAG_FIRST) != 0

    def issue_gather(idx_ref, to_slot):
        def body(i, carry):
            for j in range(SUBLANES):
                r = i * SUBLANES + j
                _tile_copy(h2_hbm, idx_ref[0, r], xbuf.at[to_slot], r, gsem.at[to_slot]).start()
            return carry

        lax.fori_loop(0, tm // SUBLANES, body, 0)

    def issue_scatter(idx_ref):
        def body(i, carry):
            for j in range(SUBLANES):
                r = i * SUBLANES + j
                _tile_copy(ybuf, r, y_hbm, idx_ref[0, r], ssem).start()
            return carry

        lax.fori_loop(0, tm // SUBLANES, body, 0)

    @pl.when((flag & FLAG_NEW_EXPERT) != 0)
    def _():
        wg_ref[...] = wg32_ref[...].astype(BF16)
        wu_ref[...] = wu32_ref[...].astype(BF16)
        wd_ref[...] = wd32_ref[...].astype(BF16)

    has_next = tile + 1 < n_tiles
    prefetch = first & (tile > 0) & has_next

    @pl.when(first)
    def _():
        @pl.when(tile == 0)
        def _():
            issue_gather(src_ref, slot)

        pltpu.make_async_copy(h2_hbm.at[pl.ds(0, tm * SUBLANES)], xbuf.at[slot], gsem.at[slot]).wait()

        @pl.when(jnp.logical_not(prefetch))
        def _():
            @pl.when(has_next)
            def _():
                issue_gather(nsrc_ref, 1 - slot)

            @pl.when(tile > 0)
            def _():
                issue_scatter(pdst_ref)

    def compute(with_prefetch):
        xb = _load_token_tiles(xbuf.at[slot], tm).astype(BF16)
        if with_prefetch:
            for r in range(tm):
                _tile_copy(h2_hbm, nsrc_ref[0, r], xbuf.at[1 - slot], r, gsem.at[1 - slot]).start()
                _tile_copy(ybuf, r, y_hbm, pdst_ref[0, r], ssem).start()
        g = jnp.dot(xb, wg_ref[...], preferred_element_type=F32)
        u = jnp.dot(xb, wu_ref[...], preferred_element_type=F32)
        hmid = (g * _sigmoid(g) * u).astype(BF16)
        y = jnp.dot(hmid, wd_ref[...], preferred_element_type=F32)
        row = lax.broadcasted_iota(jnp.int32, (tm, 1), 0)
        y = jnp.where((row >= lo) & (row < hi), y, 0.0)
        if with_prefetch:
            acc[...] = y
            return

        @pl.when(first)
        def _():
            acc[...] = y

        @pl.when(jnp.logical_not(first))
        def _():
            acc[...] += y

    @pl.when(prefetch)
    def _():
        compute(True)

    @pl.when(jnp.logical_not(prefetch) & (hi > lo))
    def _():
        compute(False)

    @pl.when((flag & FLAG_LAST) != 0)
    def _():
        whole = pltpu.make_async_copy(ybuf, y_hbm.at[pl.ds(0, tm * SUBLANES)], ssem)

        @pl.when(tile > 0)
        def _():
            whole.wait()

        _store_token_tiles(ybuf, acc[...])

        @pl.when((flag & FLAG_FINAL) != 0)
        def _():
            issue_scatter(dst_ref)
            whole.wait()


def _experts(h2, plan, w_gate, w_up, w_down, layer, tm):
    t = h2.shape[0] // SUBLANES
    vt, ve, vlo, vhi, vflag, src, dst = plan
    n_vis = vt.shape[0]
    n_tiles = src.shape[0] // tm
    d, hid = w_gate.shape[-2:]
    assert d == SUBLANES * LANES
    idx_spec = lambda nxt: pl.BlockSpec(
        (None, 1, tm), lambda v, vt, *_: (jnp.clip(vt[v] + nxt, 0, n_tiles - 1), 0, 0), memory_space=pltpu.SMEM)
    w_spec = lambda shape: pl.BlockSpec((None, None) + shape, lambda v, vt, ve, *_: (layer, ve[v], 0, 0))
    grid_spec = pltpu.PrefetchScalarGridSpec(
        num_scalar_prefetch=5,
        grid=(n_vis,),
        in_specs=[idx_spec(0), idx_spec(1), idx_spec(0), idx_spec(-1), pl.BlockSpec(memory_space=pl.ANY),
                  w_spec((d, hid)), w_spec((d, hid)), w_spec((hid, d))],
        out_specs=pl.BlockSpec(memory_space=pl.ANY),
        scratch_shapes=[pltpu.VMEM((2, tm * SUBLANES, LANES), F32), pltpu.VMEM((tm, d), F32),
                        pltpu.VMEM((tm * SUBLANES, LANES), F32),
                        pltpu.VMEM((d, hid), BF16), pltpu.VMEM((d, hid), BF16), pltpu.VMEM((hid, d), BF16),
                        pltpu.SemaphoreType.DMA((2,)), pltpu.SemaphoreType.DMA(())],
    )
    src3 = src.reshape(n_tiles, 1, tm)
    dst3 = dst.reshape(n_tiles, 1, tm)
    return pl.pallas_call(
        functools.partial(_expert_kernel, n_tiles=n_tiles),
        grid_spec=grid_spec,
        out_shape=jax.ShapeDtypeStruct((MOE_TOP_K * t * SUBLANES, LANES), F32),
        compiler_params=_cparams("arbitrary"),
        name="moe_experts",
    )(vt, ve, vlo, vhi, vflag, src3, src3, dst3, dst3, h2, w_gate, w_up, w_down)


def _visit_plan(route, tm):
    t = route.shape[0]
    a = t * MOE_TOP_K
    i32 = jnp.int32
    eid = route[:, ROUTE_E0:ROUTE_E1 + 1].astype(i32).reshape(a)
    _, order = lax.sort((eid, lax.iota(i32, a)), num_keys=1)
    tok = order // MOE_TOP_K
    src = tok
    dst = (order % MOE_TOP_K) * t + tok
    counts = jnp.sum((eid[:, None] == jnp.arange(MOE_N_EXPERTS, dtype=i32)[None, :]).astype(i32), axis=0)
    ends = jnp.cumsum(counts)
    starts = ends - counts
    n_tiles = a // tm
    first_t = starts // tm
    last_t = jnp.maximum(ends - 1, 0) // tm
    nvis = jnp.where(counts > 0, last_t - first_t + 1, 0)
    cv_end = jnp.cumsum(nvis)
    cv_start = cv_end - nvis
    n_vis = n_tiles + MOE_N_EXPERTS
    v = jnp.arange(n_vis, dtype=i32)
    active = v < cv_end[-1]
    e = jnp.minimum(jnp.sum((cv_end[None, :] <= v[:, None]).astype(i32), axis=1), MOE_N_EXPERTS - 1)
    e_last = jnp.max(jnp.where(counts > 0, jnp.arange(MOE_N_EXPERTS, dtype=i32), 0))
    e = jnp.where(active, e, e_last)
    tile = jnp.where(active, first_t[e] + v - cv_start[e], n_tiles - 1)
    lo = jnp.where(active, jnp.clip(starts[e] - tile * tm, 0, tm), 0)
    hi = jnp.where(active, jnp.clip(ends[e] - tile * tm, 0, tm), 0)
    prev_t = jnp.concatenate([jnp.full((1,), -1, i32), tile[:-1]])
    next_t = jnp.concatenate([tile[1:], jnp.full((1,), -1, i32)])
    is_final = v == cv_end[-1] - 1
    is_last = (next_t != tile) | is_final
    prev_e = jnp.concatenate([jnp.full((1,), -1, i32), e[:-1]])
    flag = jnp.where(active, (prev_t != tile) * FLAG_FIRST + is_last * FLAG_LAST + is_final * FLAG_FINAL
                     + (prev_e != e) * FLAG_NEW_EXPERT, 0)
    cast = lambda z: z.astype(i32)
    return cast(tile), cast(e), cast(lo), cast(hi), cast(flag), cast(src), cast(dst)


def _combine_kernel(x_ref, y0_ref, y1_ref, rt_ref, g2_ref, lg_ref, lb_ref, o_ref, *, alpha):
    rt = rt_ref[...]
    tm = x_ref.shape[0]
    y = (rt[:, ROUTE_W0:ROUTE_W0 + 1] * _load_token_tiles(y0_ref, tm)
         + rt[:, ROUTE_W1:ROUTE_W1 + 1] * _load_token_tiles(y1_ref, tm))
    o_ref[...] = _layer_norm(alpha * x_ref[...] + g2_ref[...] * y, lg_ref[...], lb_ref[...])


def _combine(x1, y2, route, row0, t_all, mod3, mod_row0, rows_per_mod, ln_g, ln_b, alpha, tm):
    t, d = x1.shape
    tiles_per_mod = rows_per_mod // tm
    t0 = row0 // tm
    t1 = (t_all + row0) // tm
    return pl.pallas_call(
        functools.partial(_combine_kernel, alpha=alpha),
        grid=(t // tm,),
        in_specs=[pl.BlockSpec((tm, d), lambda i: (i, 0)),
                  pl.BlockSpec((tm * SUBLANES, LANES), lambda i: (t0 + i, 0)),
                  pl.BlockSpec((tm * SUBLANES, LANES), lambda i: (t1 + i, 0)),
                  pl.BlockSpec((tm, LANES), lambda i: (t0 + i, 0)),
                  pl.BlockSpec((None, 1, d), lambda i: (mod_row0 + i // tiles_per_mod, 0, 5)),
                  _const_spec((1, d)), _const_spec((1, d))],
        out_specs=pl.BlockSpec((tm, d), lambda i: (i, 0)),
        out_shape=jax.ShapeDtypeStruct((t, d), F32),
        compiler_params=_cparams("arbitrary"),
        name="combine_ln2",
    )(x1, y2, y2, route, mod3, ln_g.reshape(1, d), ln_b.reshape(1, d))


def _pick_tile(n, pref):
    tm = min(pref, n)
    while n % tm:
        tm //= 2
    return tm


def kernel(x, c, ctx, c_ctx, ada_w, ada_b, w_in, gm_ln_g, gm_ln_b, gm_ws, gm_bs, hy_conv_w, hy_conv_b,
           hy_f_w1, hy_f_b1, hy_f_w2, hy_f_b2, hy_f_w3, hy_f_b3, hy_skip, da_lq1, da_lk1, da_lq2, da_lk2,
           da_norm_g, p_a, p_b, p_c, w_out, ln1_g, ln1_b, moe_wg, moe_bg, moe_we, moe_be,
           ex_w_gate, ex_w_up, ex_w_down, ln2_g, ln2_b):
    B, L, D = x.shape
    Lc = ctx.shape[1]
    depth = ada_w.shape[0]
    alpha = (2.0 * depth) ** 0.25
    T, Tc = B * L, B * Lc
    moe_tm = 512 if T >= 8192 else 64

    mp = -(-(B + 1) // 8) * 8
    c_all = jnp.zeros((mp, D), F32).at[:B].set(c).at[B].set(c_ctx)
    mod = _modulation(c_all, ada_w, ada_b)

    rope_tabs = _rope_tables(L // GRID_W)
    cm, sm = _dft_tables(L)
    cm_bf, sm_bf = cm.astype(BF16), sm.astype(BF16)
    cmc, smc = _dft_tables(Lc)
    cmc_bf, smc_bf = cmc.astype(BF16), smc.astype(BF16)

    seg_all = ((OFF_GM, OFF_HY, "gm"), (OFF_HY, OFF_Q, "hy"), (OFF_Q, OFF_K, "q"), (OFF_K, OFF_V, "k"),
               (OFF_V, OFF_GATE, "v"), (OFF_GATE, OFF_GATE + N_BRANCH * D, "gate"))
    seg_kv = ((0, DA_QK_W, "k"), (DA_QK_W, DA_QK_W + DA_V_W, "v"))

    tm_l = _pick_tile(L, 256)
    tm_c = _pick_tile(Lc, 256)
    tq_l = _pick_tile(L, 256)
    tq_c = _pick_tile(Lc, 256)
    tm_m = _pick_tile(L, 512)
    tm_mc = _pick_tile(Tc, 512)

    xs = x.reshape(T, D)
    xc = ctx.reshape(Tc, D)
    for l in range(depth):
        last = l == depth - 1
        lam_init = 0.8 - 0.6 * math.exp(-0.3 * l)
        mod3 = mod[l].reshape(mp, 1, 6 * D)
        w_l = w_in[l].astype(BF16)
        lparams = [a[l].reshape(1, DA_HEAD_DIM) for a in (da_lq1, da_lk1, da_lq2, da_lk2)]
        norm_g = da_norm_g[l].reshape(1, DA_V_DIM)
        lp = {
            "gm_ln_g": gm_ln_g[l].reshape(1, GM_DIM), "gm_ln_b": gm_ln_b[l].reshape(1, GM_DIM),
            "gm_ws": gm_ws[l].reshape(GM_GROUPS * GM_CHUNK, GM_CHUNK).astype(BF16),
            "gm_bs": jnp.repeat(jnp.transpose(gm_bs[l]), GM_DIM // GM_GROUPS, axis=1),
            "p_a": p_a[l].astype(BF16), "p_b": p_b[l].astype(BF16), "p_c": p_c[l].astype(BF16),
            "w_out": w_out[l].astype(BF16),
            "ln1_g": ln1_g[l].reshape(1, D), "ln1_b": ln1_b[l].reshape(1, D),
            "w_router": _split_bf16(jnp.zeros((D, LANES), F32).at[:, :MOE_GROUPS].set(moe_wg[l])
                                    .at[:, MOE_GROUPS:MOE_GROUPS + MOE_N_EXPERTS].set(moe_we[l])),
            "b_router": jnp.zeros((1, LANES), F32).at[0, :MOE_GROUPS].set(moe_bg[l])
                           .at[0, MOE_GROUPS:MOE_GROUPS + MOE_N_EXPERTS].set(moe_be[l]),
        }
        fw = (hy_f_w1[l], hy_f_b1[l], hy_f_w2[l], hy_f_b2[l], hy_f_w3[l], hy_f_b3[l])

        zgm, zhy, q, k, v, gate = _inproj(xs, mod3, 0, L, w_l, seg_all, rope_tabs, L, tm_l)
        if last:
            k_c, v_c = _inproj(xc, mod3, B, Tc, w_l[:, OFF_K:OFF_GATE], seg_kv, None, Lc, tm_c)
        else:
            zgm_c, zhy_c, q_c, k_c, v_c, gate_c = _inproj(xc, mod3, B, Tc, w_l, seg_all, None, Lc, tm_c)
        y_c = _attention(q, [(k, v, L), (k_c, v_c, Lc)], lparams, norm_g, lam_init, B, L, tq_l)
        kre, kim, nyq = _hyena_filter_spectrum(L, cm, sm, *fw)
        y_b = _hyena(zhy, B, L, hy_conv_w[l], hy_conv_b[l], cm_bf, sm_bf, kre, kim, nyq, hy_skip[l])
        t_all = T if last else T + Tc
        x1, h2, route = _merge(zgm, y_b, y_c, gate, xs, mod3, 0, L, lp, alpha, tm_m, T, 0, None)

        if not last:
            yc_c = _attention(q_c, [(k_c, v_c, Lc)], lparams, norm_g, lam_init, B, Lc, tq_c)
            kre_c, kim_c, nyq_c = _hyena_filter_spectrum(Lc, cmc, smc, *fw)
            yb_c = _hyena(zhy_c, B, Lc, hy_conv_w[l], hy_conv_b[l], cmc_bf, smc_bf, kre_c, kim_c, nyq_c,
                          hy_skip[l])
            x1c, h2c, route_c = _merge(zgm_c, yb_c, yc_c, gate_c, xc, mod3, B, Tc, lp, alpha, tm_mc, Tc, 0, None)
            h2 = jnp.concatenate([h2, h2c], axis=0)
            route = jnp.concatenate([route, route_c], axis=0)

        plan = _visit_plan(route, moe_tm)
        y2 = _experts(h2, plan, ex_w_gate, ex_w_up, ex_w_down, l, moe_tm)
        xs = _combine(x1, y2, route, 0, t_all, mod3, 0, L, ln2_g[l], ln2_b[l], alpha, tm_l)
        if not last:
            xc = _combine(x1c, y2, route, T, t_all, mod3, B, Tc, ln2_g[l], ln2_b[l], alpha, tm_c)
    return xs.reshape(B, L, D)
```

```python
import functools
import math

import numpy as np
import jax
import jax.numpy as jnp
from jax import lax
from jax.experimental import pallas as pl
from jax.experimental.pallas import tpu as pltpu

F32 = jnp.float32
BF16 = jnp.bfloat16
HIGHEST = lax.Precision.HIGHEST

GRID_W = 64
GM_DIM = 256
GM_GROUPS = 4
GM_CHUNK = 128
HY_DIM = 256
HY_EMB = 33
HY_BANDS = (HY_EMB - 1) // 2
HY_DECAY_FAST = 0.3
HY_DECAY_SLOW = 1.5
HY_DECAY_TARGET = 1e-2
HY_DECAY_SHIFT = 0.05
DA_HEADS = 4
DA_HEAD_DIM = 64
DA_V_DIM = 2 * DA_HEAD_DIM
DA_QK_W = DA_HEADS * 2 * DA_HEAD_DIM
DA_V_W = DA_HEADS * DA_V_DIM
ROPE_BASE = 10000.0
N_BRANCH = 3
OFF_GM = 0
OFF_HY = OFF_GM + 2 * GM_DIM
OFF_Q = OFF_HY + 3 * HY_DIM
OFF_K = OFF_Q + DA_QK_W
OFF_V = OFF_K + DA_QK_W
OFF_GATE = OFF_V + DA_V_W
MOE_GROUPS = 4
MOE_EXPERTS_PER_GROUP = 8
MOE_N_EXPERTS = MOE_GROUPS * MOE_EXPERTS_PER_GROUP
MOE_TOP_K = 2
LN_EPS = 1e-5
LANES = 128
VMEM_LIMIT = 56 * 1024 * 1024


def _cparams(*sem):
    return pltpu.CompilerParams(dimension_semantics=sem, vmem_limit_bytes=VMEM_LIMIT)


def _sigmoid(x):
    return 1.0 / (1.0 + jnp.exp(-x))


def _layer_norm(x, g, b):
    mu = jnp.mean(x, axis=-1, keepdims=True)
    xc = x - mu
    var = jnp.mean(xc * xc, axis=-1, keepdims=True)
    return xc * lax.rsqrt(var + LN_EPS) * g + b


def _gelu_tanh(x):
    return 0.5 * x * (1.0 + jnp.tanh(math.sqrt(2.0 / math.pi) * (x + 0.044715 * (x * x * x))))


def _const_spec(shape):
    nd = len(shape)
    return pl.BlockSpec(shape, lambda *_: (0,) * nd)


def _mod_kernel(c_ref, w_ref, b_ref, o_ref):
    c = c_ref[...]
    s = c * _sigmoid(c)
    o_ref[...] = jnp.dot(s, w_ref[...], precision=HIGHEST, preferred_element_type=F32) + b_ref[...]


def _modulation(c_all, ada_w, ada_b):
    depth, d, n = ada_w.shape
    mp = c_all.shape[0]
    tn = 512
    return pl.pallas_call(
        _mod_kernel,
        grid=(depth, n // tn),
        in_specs=[pl.BlockSpec((mp, d), lambda l, j: (0, 0)),
                  pl.BlockSpec((None, d, tn), lambda l, j: (l, 0, j)),
                  pl.BlockSpec((None, 1, tn), lambda l, j: (l, 0, j))],
        out_specs=pl.BlockSpec((None, mp, tn), lambda l, j: (l, 0, j)),
        out_shape=jax.ShapeDtypeStruct((depth, mp, n), F32),
        compiler_params=_cparams("arbitrary", "arbitrary"),
        name="adaln_mod",
    )(c_all, ada_w, ada_b.reshape(depth, 1, n))


def _rope_tables(rows):
    n_freq = DA_HEAD_DIM // 4
    row = jnp.broadcast_to(jnp.arange(rows)[:, None], (rows, GRID_W)).reshape(-1).astype(F32)
    col = jnp.broadcast_to(jnp.arange(GRID_W)[None, :], (rows, GRID_W)).reshape(-1).astype(F32)
    inv = ROPE_BASE ** (-jnp.arange(n_freq, dtype=F32) / n_freq)
    ang_r = row[:, None] * inv
    ang_c = col[:, None] * inv
    c64 = jnp.concatenate([jnp.cos(ang_r), jnp.cos(ang_r), jnp.cos(ang_c), jnp.cos(ang_c)], axis=-1)
    s64 = jnp.concatenate([-jnp.sin(ang_r), jnp.sin(ang_r), -jnp.sin(ang_c), jnp.sin(ang_c)], axis=-1)
    return jnp.tile(c64, (1, LANES // DA_HEAD_DIM)), jnp.tile(s64, (1, LANES // DA_HEAD_DIM))


def _rope_block(xb, cos, sin):
    lane = lax.broadcasted_iota(jnp.int32, xb.shape, 1)
    n_freq = DA_HEAD_DIM // 4
    first_half = (lane % (2 * n_freq)) < n_freq
    partner = jnp.where(first_half, pltpu.roll(xb, LANES - n_freq, 1), pltpu.roll(xb, n_freq, 1))
    return xb * cos + partner * sin


def _inproj_kernel(*refs, segs, use_rope, n_chunk):
    if use_rope:
        x_ref, sh_ref, sc_ref, w_ref, cos_ref, sin_ref = refs[:6]
        out_refs = refs[6:]
    else:
        x_ref, sh_ref, sc_ref, w_ref = refs[:4]
        out_refs = refs[4:]
    h = (x_ref[...] * (1.0 + sc_ref[...]) + sh_ref[...]).astype(BF16)
    for (a, b, kind), o_ref in zip(segs, out_refs):
        for c0 in range(a, b, n_chunk):
            c1 = min(c0 + n_chunk, b)
            acc = jnp.dot(h, w_ref[:, c0:c1], preferred_element_type=F32)
            if kind == "q":
                acc = acc * (DA_HEAD_DIM ** -0.5 * math.log2(math.e))
            if use_rope and kind in ("q", "k"):
                cos = cos_ref[...]
                sin = sin_ref[...]
                for j in range((c1 - c0) // LANES):
                    blk = _rope_block(acc[:, j * LANES:(j + 1) * LANES], cos, sin)
                    o_ref[:, c0 - a + j * LANES:c0 - a + (j + 1) * LANES] = blk.astype(o_ref.dtype)
            else:
                o_ref[:, c0 - a:c1 - a] = acc.astype(o_ref.dtype)


def _inproj(x2d, mod3, mod_row0, rows_per_mod, w, segs, rope_tabs, seq_len, tm):
    t, d = x2d.shape
    n = w.shape[1]
    use_rope = rope_tabs is not None
    tiles_per_mod = rows_per_mod // tm
    tiles_per_seq = seq_len // tm

    def mod_map(piece):
        return lambda i: (mod_row0 + i // tiles_per_mod, 0, piece)

    in_specs = [pl.BlockSpec((tm, d), lambda i: (i, 0)),
                pl.BlockSpec((None, 1, d), mod_map(0)),
                pl.BlockSpec((None, 1, d), mod_map(1)),
                pl.BlockSpec((d, n), lambda i: (0, 0), pipeline_mode=pl.Buffered(1))]
    args = [x2d, mod3, mod3, w]
    if use_rope:
        in_specs += [pl.BlockSpec((tm, LANES), lambda i: (i % tiles_per_seq, 0))] * 2
        args += list(rope_tabs)
    out_specs = [pl.BlockSpec((tm, b - a), lambda i: (i, 0)) for a, b, _ in segs]
    out_shape = [jax.ShapeDtypeStruct((t, b - a), BF16) for a, b, _ in segs]
    return pl.pallas_call(
        functools.partial(_inproj_kernel, segs=segs, use_rope=use_rope, n_chunk=512),
        grid=(t // tm,),
        in_specs=in_specs, out_specs=out_specs, out_shape=out_shape,
        compiler_params=_cparams("arbitrary"),
        name="inproj",
    )(*args)


ATTN_KEY_CHUNK = 256


def _attn_kernel(*refs, src_lens, lam_init):
    n_src = len(src_lens)
    lq1, lk1, lq2, lk2, g_ref, q_ref = refs[:6]
    kv_refs = refs[6:6 + 2 * n_src]
    o_ref, s_scr = refs[6 + 2 * n_src:]
    lam = (jnp.exp(jnp.sum(lq1[...] * lk1[...], axis=-1, keepdims=True))
           - jnp.exp(jnp.sum(lq2[...] * lk2[...], axis=-1, keepdims=True)) + lam_init)
    tq = q_ref.shape[0]
    lane = lax.broadcasted_iota(jnp.int32, (tq, LANES), 1)
    dn = (((1,), (1,)), ((), ()))
    chunks = []
    off = 0
    for j, n in enumerate(src_lens):
        kc = min(ATTN_KEY_CHUNK, n)
        for st in range(0, n, kc):
            chunks.append((j, st, kc, off))
            off += kc
    for h in range(DA_HEADS):
        cols = slice(h * LANES, (h + 1) * LANES)
        qh = q_ref[:, cols]
        zero = jnp.zeros_like(qh)
        om = []
        for m in range(2):
            qm = jnp.where(lane < DA_HEAD_DIM if m == 0 else lane >= DA_HEAD_DIM, qh, zero)
            mlane = None
            for j, st, kc, off in chunks:
                s_c = lax.dot_general(qm, kv_refs[2 * j][st:st + kc, cols], dn, preferred_element_type=F32)
                s_scr[:, off:off + kc] = s_c
                for b in range(kc // LANES):
                    blk = s_c[:, b * LANES:(b + 1) * LANES]
                    mlane = blk if mlane is None else jnp.maximum(mlane, blk)
            mx = jnp.max(mlane, axis=-1, keepdims=True)
            acc = None
            for j, st, kc, off in chunks:
                p = jnp.exp2(s_scr[:, off:off + kc] - mx).astype(BF16)
                v_aug = jnp.concatenate([kv_refs[2 * j + 1][st:st + kc, cols], jnp.ones((kc, LANES), BF16)],
                                        axis=1)
                d = jnp.dot(p, v_aug, preferred_element_type=F32)
                acc = d if acc is None else acc + d
            om.append(acc[:, :LANES] * (1.0 / acc[:, LANES:LANES + 1]))
        o = om[0] - lam * om[1]
        ms = jnp.mean(o * o, axis=-1, keepdims=True)
        o = o * lax.rsqrt(ms + LN_EPS) * g_ref[...] * (1.0 - lam_init)
        o_ref[:, cols] = o.astype(o_ref.dtype)


def _attention(q, kvs, lparams, norm_g, lam_init, nb, lq, tq):
    t = q.shape[0]
    qt = lq // tq
    in_specs = [_const_spec((1, DA_HEAD_DIM))] * 4 + [_const_spec((1, DA_V_DIM))]
    in_specs.append(pl.BlockSpec((tq, DA_QK_W), lambda b, i: (b * qt + i, 0)))
    args = list(lparams) + [norm_g, q]
    for k, v, lk in kvs:
        in_specs += [pl.BlockSpec((lk, DA_QK_W), lambda b, i: (b, 0)),
                     pl.BlockSpec((lk, DA_V_W), lambda b, i: (b, 0))]
        args += [k, v]
    src_lens = tuple(lk for _, _, lk in kvs)
    return pl.pallas_call(
        functools.partial(_attn_kernel, src_lens=src_lens, lam_init=lam_init),
        grid=(nb, qt),
        in_specs=in_specs,
        out_specs=pl.BlockSpec((tq, DA_V_W), lambda b, i: (b * qt + i, 0)),
        out_shape=jax.ShapeDtypeStruct((t, DA_V_W), BF16),
        scratch_shapes=[pltpu.VMEM((tq, sum(src_lens)), F32)],
        compiler_params=_cparams("arbitrary", "arbitrary"),
        name="diff_attn",
    )(*args)


def _dft_tables(L):
    k = jnp.arange(L, dtype=jnp.int32)
    m = (k[:, None] * k[None, :]) % (2 * L)
    ang = m.astype(F32) * (math.pi / L)
    return jnp.cos(ang), jnp.sin(ang)


def _filter_consts(L):
    t = jnp.linspace(0.0, 1.0, L, dtype=F32)[:, None]
    w = 2.0 * math.pi * jnp.arange(L, dtype=F32)[:, None] / L
    f = jnp.linspace(1e-4, HY_BANDS - 1, HY_BANDS, dtype=F32)[None, :]
    emb = jnp.concatenate([t, jnp.cos(f * w), -jnp.sin(f * w)], axis=-1)
    max_decay = math.log(HY_DECAY_TARGET) / HY_DECAY_FAST
    min_decay = math.log(HY_DECAY_TARGET) / HY_DECAY_SLOW
    deltas = jnp.abs(jnp.linspace(min_decay, max_decay, HY_DIM, dtype=F32))
    window = jnp.exp(-t * deltas[None, :]) + HY_DECAY_SHIFT
    return emb, window


def _filter_kernel(emb_ref, win_ref, w1, b1, w2, b2, w3, b3, hs_ref, hd_ref, nyq_ref):
    h = jnp.sin(jnp.dot(emb_ref[...], w1[...], precision=HIGHEST, preferred_element_type=F32) + b1[...])
    h = jnp.sin(jnp.dot(h, w2[...], precision=HIGHEST, preferred_element_type=F32) + b2[...])
    h = jnp.dot(h, w3[...], precision=HIGHEST, preferred_element_type=F32) + b3[...]
    win = win_ref[...]
    hf = h[:, :HY_DIM] * win
    hb = h[:, HY_DIM:] * win
    row = lax.broadcasted_iota(jnp.int32, hf.shape, 0)
    hb = jnp.where(row == 0, 0.0, hb)
    alt = jnp.where(row % 2 == 0, 1.0, -1.0)
    hs_ref[...] = hf + hb
    hd_ref[...] = hf - hb
    nyq_ref[...] = jnp.sum((hf + hb) * alt, axis=0, keepdims=True)


def _spectrum_kernel(c_ref, s_ref, hs_ref, hd_ref, kre_ref, kim_ref, *, n_fft):
    i = pl.program_id(0)
    tk = c_ref.shape[0]
    kidx = i * tk + lax.broadcasted_iota(jnp.int32, (tk, 1), 0)
    scale = jnp.where(kidx == 0, 1.0 / n_fft, 2.0 / n_fft)
    kre = jnp.dot(c_ref[...], hs_ref[...], precision=HIGHEST, preferred_element_type=F32)
    kim = -jnp.dot(s_ref[...], hd_ref[...], precision=HIGHEST, preferred_element_type=F32)
    kre_ref[...] = kre * scale
    kim_ref[...] = kim * scale


def _hyena_filter_spectrum(L, cmat, smat, w1, b1, w2, b2, w3, b3):
    emb, window = _filter_consts(L)
    full = lambda a: _const_spec(a.shape)
    ins = [emb, window, w1, b1.reshape(1, -1), w2, b2.reshape(1, -1), w3, b3.reshape(1, -1)]
    hs, hd, nyq = pl.pallas_call(
        _filter_kernel,
        grid=(1,),
        in_specs=[full(a) for a in ins],
        out_specs=[_const_spec((L, HY_DIM)), _const_spec((L, HY_DIM)), _const_spec((1, HY_DIM))],
        out_shape=[jax.ShapeDtypeStruct((L, HY_DIM), F32), jax.ShapeDtypeStruct((L, HY_DIM), F32),
                   jax.ShapeDtypeStruct((1, HY_DIM), F32)],
        compiler_params=_cparams("arbitrary"),
        name="hyena_filter",
    )(*ins)
    tk = min(256, L)
    kre, kim = pl.pallas_call(
        functools.partial(_spectrum_kernel, n_fft=2 * L),
        grid=(L // tk,),
        in_specs=[pl.BlockSpec((tk, L), lambda i: (i, 0)), pl.BlockSpec((tk, L), lambda i: (i, 0)),
                  _const_spec((L, HY_DIM)), _const_spec((L, HY_DIM))],
        out_specs=[pl.BlockSpec((tk, HY_DIM), lambda i: (i, 0))] * 2,
        out_shape=[jax.ShapeDtypeStruct((L, HY_DIM), F32)] * 2,
        compiler_params=_cparams("arbitrary"),
        name="hyena_spectrum",
    )(cmat, smat, hs, hd)
    return kre, kim, nyq * (1.0 / (2 * L))


HY_ROW_BLOCK = 512


def _hyena_kernel(z_ref, cw_ref, cb_ref, c_ref, s_ref, kre_ref, kim_ref, nyq_ref, skip_ref, o_ref,
                  u_ref, x0_ref, p_ref, q_ref):
    L = z_ref.shape[0]
    row = lax.broadcasted_iota(jnp.int32, (L, HY_DIM), 0)

    def conv(j):
        cols = slice(j * HY_DIM, (j + 1) * HY_DIM)
        z = z_ref[:, cols].astype(F32)
        zprev = jnp.where(row == 0, 0.0, pltpu.roll(z, 1, 0))
        znext = jnp.where(row == L - 1, 0.0, pltpu.roll(z, L - 1, 0))
        return zprev * cw_ref[0:1, cols] + z * cw_ref[1:2, cols] + znext * cw_ref[2:3, cols] + cb_ref[:, cols]

    u = conv(2) * conv(1)
    ub = u.astype(BF16)
    u_ref[...] = u
    alt = jnp.where(row % 2 == 0, 1.0, -1.0)
    nyq_term = jnp.sum(u * alt, axis=0, keepdims=True) * nyq_ref[...]
    x0_ref[...] = conv(0)
    blk = min(HY_ROW_BLOCK, L)
    for r in range(0, L, blk):
        rows = slice(r, r + blk)
        a = jnp.dot(c_ref[rows, :], ub, preferred_element_type=F32)
        b = jnp.dot(s_ref[rows, :], ub, preferred_element_type=F32)
        kre = kre_ref[rows, :]
        kim = kim_ref[rows, :]
        p_ref[rows, :] = (a * kre + b * kim).astype(BF16)
        q_ref[rows, :] = (b * kre - a * kim).astype(BF16)
    for r in range(0, L, blk):
        rows = slice(r, r + blk)
        y = (jnp.dot(c_ref[rows, :], p_ref[...], preferred_element_type=F32)
             + jnp.dot(s_ref[rows, :], q_ref[...], preferred_element_type=F32))
        ub_rows = u_ref[rows, :]
        row_b = lax.broadcasted_iota(jnp.int32, (blk, HY_DIM), 0)
        y = y + jnp.where(row_b % 2 == 0, nyq_term, -nyq_term) + ub_rows * skip_ref[...]
        o_ref[rows, :] = (y * x0_ref[rows, :]).astype(o_ref.dtype)


def _hyena(zhy, nb, L, conv_w, conv_b, cmat_bf, smat_bf, kre, kim, nyq, skip):
    t = zhy.shape[0]
    return pl.pallas_call(
        _hyena_kernel,
        grid=(nb,),
        in_specs=[pl.BlockSpec((L, 3 * HY_DIM), lambda b: (b, 0)),
                  _const_spec((3, 3 * HY_DIM)), _const_spec((1, 3 * HY_DIM)),
                  pl.BlockSpec((L, L), lambda b: (0, 0), pipeline_mode=pl.Buffered(1)),
                  pl.BlockSpec((L, L), lambda b: (0, 0), pipeline_mode=pl.Buffered(1)),
                  _const_spec((L, HY_DIM)), _const_spec((L, HY_DIM)),
                  _const_spec((1, HY_DIM)), _const_spec((1, HY_DIM))],
        out_specs=pl.BlockSpec((L, HY_DIM), lambda b: (b, 0)),
        out_shape=jax.ShapeDtypeStruct((t, HY_DIM), BF16),
        scratch_shapes=[pltpu.VMEM((L, HY_DIM), F32), pltpu.VMEM((L, HY_DIM), F32),
                        pltpu.VMEM((L, HY_DIM), BF16), pltpu.VMEM((L, HY_DIM), BF16)],
        compiler_params=_cparams("arbitrary"),
        name="hyena_conv",
    )(zhy, conv_w, conv_b.reshape(1, -1), cmat_bf, smat_bf, kre, kim, nyq, skip.reshape(1, -1))


SUBLANES = 8


def _split_bf16(w):
    hi = w.astype(BF16)
    return jnp.stack([hi, (w - hi.astype(F32)).astype(BF16)])


def _store_token_tiles(ref, val):
    n = val.shape[0]
    for j in range(val.shape[1] // LANES):
        ref[pl.ds(j, n, stride=SUBLANES), :] = val[:, j * LANES:(j + 1) * LANES]


def _load_token_tiles(ref, n):
    return jnp.concatenate([ref[pl.ds(j, n, stride=SUBLANES), :] for j in range(SUBLANES)], axis=1)


def _merge_kernel(*refs, alpha, n_alias):
    (zgm_ref, yb_ref, yc_ref, gate_ref, x_ref, g1_ref, sh2_ref, sc2_ref, lng_ref, lnb_ref, ws_ref, bs_ref,
     pa_ref, pb_ref, pc_ref, wo_ref, l1g_ref, l1b_ref, wr_ref, br_ref) = refs[:20]
    x1_ref, h2_ref, rt_ref = refs[20 + n_alias:]
    tm = x_ref.shape[0]
    d = x_ref.shape[1]
    gm = _gelu_tanh(zgm_ref[...].astype(F32))
    u = gm[:, :GM_DIM]
    v = _layer_norm(gm[:, GM_DIM:], lng_ref[...], lnb_ref[...]).astype(BF16)
    lane_group = lax.broadcasted_iota(jnp.int32, (GM_CHUNK, GM_DIM), 1) // (GM_DIM // GM_GROUPS)
    ya = []
    for cidx in range(tm // GM_CHUNK):
        rows = slice(cidx * GM_CHUNK, (cidx + 1) * GM_CHUNK)
        r = jnp.dot(ws_ref[...], v[rows], preferred_element_type=F32)
        vv = bs_ref[...]
        for g in range(GM_GROUPS):
            vv = vv + jnp.where(lane_group == g, r[g * GM_CHUNK:(g + 1) * GM_CHUNK], 0.0)
        ya.append(u[rows] * vv)
    ya = jnp.concatenate(ya, axis=0) if len(ya) > 1 else ya[0]
    ma = jnp.dot(ya.astype(BF16), pa_ref[...], preferred_element_type=F32)
    mb = jnp.dot(yb_ref[...], pb_ref[...], preferred_element_type=F32)
    mc = jnp.dot(yc_ref[...], pc_ref[...], preferred_element_type=F32)
    merged = (_sigmoid(gate_ref[:, 0:d].astype(F32)) * ma
              + _sigmoid(gate_ref[:, d:2 * d].astype(F32)) * mb
              + _sigmoid(gate_ref[:, 2 * d:3 * d].astype(F32)) * mc)
    out = jnp.dot(merged.astype(BF16), wo_ref[...], preferred_element_type=F32)
    x1 = _layer_norm(alpha * x_ref[...] + g1_ref[...] * out, l1g_ref[...], l1b_ref[...])
    x1_ref[...] = x1
    h2 = x1 * (1.0 + sc2_ref[...]) + sh2_ref[...]
    _store_token_tiles(h2_ref, h2)
    h2_hi = h2.astype(BF16)
    h2_lo = (h2 - h2_hi.astype(F32)).astype(BF16)
    lg = (jnp.dot(h2_hi, wr_ref[0], preferred_element_type=F32)
          + jnp.dot(h2_hi, wr_ref[1], preferred_element_type=F32)
          + jnp.dot(h2_lo, wr_ref[0], preferred_element_type=F32) + br_ref[...])
    rt_ref[...] = _route(lg)


ROUTE_E0, ROUTE_E1, ROUTE_W0, ROUTE_W1 = 0, 1, 2, 3


def _route(lg):
    neg = jnp.float32(-3.0e38)
    lane_i = lax.broadcasted_iota(jnp.int32, lg.shape, 1)
    lane = lane_i.astype(F32)
    big = jnp.float32(LANES)
    is_g = lane_i < MOE_GROUPS
    gl = jnp.where(is_g, lg, neg)
    gmax = jnp.max(gl, axis=-1, keepdims=True)
    g_idx = jnp.min(jnp.where(gl == gmax, lane, big), axis=-1, keepdims=True)
    g_prob = 1.0 / jnp.sum(jnp.where(is_g, jnp.exp(gl - gmax), 0.0), axis=-1, keepdims=True)
    e_lo = MOE_GROUPS + MOE_EXPERTS_PER_GROUP * g_idx
    el = jnp.where(lane >= e_lo, jnp.where(lane < e_lo + MOE_EXPERTS_PER_GROUP, lg, neg), neg)
    v1 = jnp.max(el, axis=-1, keepdims=True)
    i1 = jnp.min(jnp.where(el == v1, lane, big), axis=-1, keepdims=True)
    el2 = jnp.where(lane == i1, neg, el)
    v2 = jnp.max(el2, axis=-1, keepdims=True)
    i2 = jnp.min(jnp.where(el2 == v2, lane, big), axis=-1, keepdims=True)
    e21 = jnp.exp(v2 - v1)
    w1 = g_prob / (1.0 + e21)
    w2 = w1 * e21
    rec = jnp.where(lane_i == ROUTE_E0, i1 - MOE_GROUPS, 0.0)
    rec = jnp.where(lane_i == ROUTE_E1, i2 - MOE_GROUPS, rec)
    rec = jnp.where(lane_i == ROUTE_W0, w1, rec)
    return jnp.where(lane_i == ROUTE_W1, w2, rec)


def _merge(zgm, yb, yc, gate, x2d, mod3, mod_row0, rows_per_mod, lp, alpha, tm, t_all, row0, prev):
    t, d = x2d.shape
    tiles_per_mod = rows_per_mod // tm
    off = row0 // tm

    def mod_map(piece):
        return lambda i: (mod_row0 + i // tiles_per_mod, 0, piece)

    row = lambda w: pl.BlockSpec((tm, w), lambda i: (i, 0))
    row_off = lambda w: pl.BlockSpec((tm, w), lambda i: (off + i, 0))
    consts = [lp["gm_ln_g"], lp["gm_ln_b"], lp["gm_ws"], lp["gm_bs"], lp["p_a"], lp["p_b"], lp["p_c"],
              lp["w_out"], lp["ln1_g"], lp["ln1_b"], lp["w_router"], lp["b_router"]]
    in_specs = [row(2 * GM_DIM), row(HY_DIM), row(DA_V_W), row(N_BRANCH * d), row(d),
                pl.BlockSpec((None, 1, d), mod_map(2)), pl.BlockSpec((None, 1, d), mod_map(3)),
                pl.BlockSpec((None, 1, d), mod_map(4))] + [_const_spec(a.shape) for a in consts]
    args = [zgm, yb, yc, gate, x2d, mod3, mod3, mod3, *consts]
    aliases = {}
    if prev is not None:
        aliases = {len(args): 1, len(args) + 1: 2}
        in_specs += [pl.BlockSpec(memory_space=pl.ANY)] * 2
        args += list(prev)
    return pl.pallas_call(
        functools.partial(_merge_kernel, alpha=alpha, n_alias=len(aliases)),
        grid=(t // tm,),
        in_specs=in_specs,
        out_specs=[row(d), pl.BlockSpec((tm * SUBLANES, LANES), lambda i: (off + i, 0)), row_off(LANES)],
        out_shape=[jax.ShapeDtypeStruct((t, d), F32), jax.ShapeDtypeStruct((t_all * SUBLANES, LANES), F32),
                   jax.ShapeDtypeStruct((t_all, LANES), F32)],
        input_output_aliases=aliases,
        compiler_params=_cparams("arbitrary"),
        name="merge_ln1",
    )(*args)


FLAG_FIRST, FLAG_LAST, FLAG_FINAL, FLAG_NEW_EXPERT = 1, 2, 4, 8


def _tile_copy(src, src_row, dst, dst_row, sem):
    s0 = pl.multiple_of(src_row * SUBLANES, SUBLANES)
    d0 = pl.multiple_of(dst_row * SUBLANES, SUBLANES)
    return pltpu.make_async_copy(src.at[pl.ds(s0, SUBLANES)], dst.at[pl.ds(d0, SUBLANES)], sem)


def _expert_kernel(vt_ref, ve_ref, vlo_ref, vhi_ref, vflag_ref, src_ref, nsrc_ref, dst_ref, pdst_ref, h2_hbm,
                   wg32_ref, wu32_ref, wd32_ref, y_hbm, xbuf, acc, ybuf, wg_ref, wu_ref, wd_ref, gsem, ssem,
                   *, n_tiles):
    v = pl.program_id(0)
    tile, lo, hi, flag = vt_ref[v], vlo_ref[v], vhi_ref[v], vflag_ref[v]
    tm = acc.shape[0]
    slot = tile % 2
    first = (flag & FLAG_FIRST) != 0

    def issue_gather(idx_ref, to_slot):
        def body(i, carry):
            for j in range(SUBLANES):
                r = i * SUBLANES + j
                _tile_copy(h2_hbm, idx_ref[0, r], xbuf.at[to_slot], r, gsem.at[to_slot]).start(priority=j % 2)
            return carry

        lax.fori_loop(0, tm // SUBLANES, body, 0)

    def issue_scatter(idx_ref):
        def body(i, carry):
            for j in range(SUBLANES):
                r = i * SUBLANES + j
                _tile_copy(ybuf, r, y_hbm, idx_ref[0, r], ssem).start(priority=j % 2)
            return carry

        lax.fori_loop(0, tm // SUBLANES, body, 0)

    @pl.when((flag & FLAG_NEW_EXPERT) != 0)
    def _():
        wg_ref[...] = wg32_ref[...].astype(BF16)
        wu_ref[...] = wu32_ref[...].astype(BF16)
        wd_ref[...] = wd32_ref[...].astype(BF16)

    has_next = tile + 1 < n_tiles
    prefetch = first & (tile > 0) & has_next

    @pl.when(first)
    def _():
        @pl.when(tile == 0)
        def _():
            issue_gather(src_ref, slot)

        pltpu.make_async_copy(h2_hbm.at[pl.ds(0, tm * SUBLANES)], xbuf.at[slot], gsem.at[slot]).wait()

        @pl.when(jnp.logical_not(prefetch))
        def _():
            @pl.when(has_next)
            def _():
                issue_gather(nsrc_ref, 1 - slot)

            @pl.when(tile > 0)
            def _():
                issue_scatter(pdst_ref)

    def compute(with_prefetch):
        xb = _load_token_tiles(xbuf.at[slot], tm).astype(BF16)
        if with_prefetch:
            for r in range(tm):
                _tile_copy(h2_hbm, nsrc_ref[0, r], xbuf.at[1 - slot], r, gsem.at[1 - slot]).start(priority=r % 2)
                _tile_copy(ybuf, r, y_hbm, pdst_ref[0, r], ssem).start(priority=r % 2)
        g = jnp.dot(xb, wg_ref[...], preferred_element_type=F32)
        u = jnp.dot(xb, wu_ref[...], preferred_element_type=F32)
        hmid = (g * _sigmoid(g) * u).astype(BF16)
        y = jnp.dot(hmid, wd_ref[...], preferred_element_type=F32)
        row = lax.broadcasted_iota(jnp.int32, (tm, 1), 0)
        y = jnp.where((row >= lo) & (row < hi), y, 0.0)
        if with_prefetch:
            acc[...] = y
            return

        @pl.when(first)
        def _():
            acc[...] = y

        @pl.when(jnp.logical_not(first))
        def _():
            acc[...] += y

    @pl.when(prefetch)
    def _():
        compute(True)

    @pl.when(jnp.logical_not(prefetch) & (hi > lo))
    def _():
        compute(False)

    @pl.when((flag & FLAG_LAST) != 0)
    def _():
        whole = pltpu.make_async_copy(ybuf, y_hbm.at[pl.ds(0, tm * SUBLANES)], ssem)

        @pl.when(tile > 0)
        def _():
            whole.wait()

        _store_token_tiles(ybuf, acc[...])

        @pl.when((flag & FLAG_FINAL) != 0)
        def _():
            issue_scatter(dst_ref)
            whole.wait()


def _experts(h2, plan, w_gate, w_up, w_down, layer, tm):
    t = h2.shape[0] // SUBLANES
    vt, ve, vlo, vhi, vflag, src, dst = plan
    n_vis = vt.shape[0]
    n_tiles = src.shape[0] // tm
    d, hid = w_gate.shape[-2:]
    assert d == SUBLANES * LANES
    idx_spec = lambda nxt: pl.BlockSpec(
        (None, 1, tm), lambda v, vt, *_: (jnp.clip(vt[v] + nxt, 0, n_tiles - 1), 0, 0), memory_space=pltpu.SMEM)
    w_spec = lambda shape: pl.BlockSpec((None, None) + shape, lambda v, vt, ve, *_: (layer, ve[v], 0, 0))
    grid_spec = pltpu.PrefetchScalarGridSpec(
        num_scalar_prefetch=5,
        grid=(n_vis,),
        in_specs=[idx_spec(0), idx_spec(1), idx_spec(0), idx_spec(-1), pl.BlockSpec(memory_space=pl.ANY),
                  w_spec((d, hid)), w_spec((d, hid)), w_spec((hid, d))],
        out_specs=pl.BlockSpec(memory_space=pl.ANY),
        scratch_shapes=[pltpu.VMEM((2, tm * SUBLANES, LANES), F32), pltpu.VMEM((tm, d), F32),
                        pltpu.VMEM((tm * SUBLANES, LANES), F32),
                        pltpu.VMEM((d, hid), BF16), pltpu.VMEM((d, hid), BF16), pltpu.VMEM((hid, d), BF16),
                        pltpu.SemaphoreType.DMA((2,)), pltpu.SemaphoreType.DMA(())],
    )
    src3 = src.reshape(n_tiles, 1, tm)
    dst3 = dst.reshape(n_tiles, 1, tm)
    return pl.pallas_call(
        functools.partial(_expert_kernel, n_tiles=n_tiles),
        grid_spec=grid_spec,
        out_shape=jax.ShapeDtypeStruct((MOE_TOP_K * t * SUBLANES, LANES), F32),
        compiler_params=_cparams("arbitrary"),
        name="moe_experts",
    )(vt, ve, vlo, vhi, vflag, src3, src3, dst3, dst3, h2, w_gate, w_up, w_down)


def _visit_plan(route, tm):
    t = route.shape[0]
    a = t * MOE_TOP_K
    i32 = jnp.int32
    eid = route[:, ROUTE_E0:ROUTE_E1 + 1].astype(i32).reshape(a)
    idx_bits = max(1, (a - 1).bit_length())
    assert MOE_N_EXPERTS << idx_bits <= 2 ** 31
    order = lax.sort(eid * (1 << idx_bits) + lax.iota(i32, a)) & ((1 << idx_bits) - 1)
    tok = order // MOE_TOP_K
    src = tok
    dst = (order % MOE_TOP_K) * t + tok
    counts = jnp.sum((eid[:, None] == jnp.arange(MOE_N_EXPERTS, dtype=i32)[None, :]).astype(i32), axis=0)
    ends = jnp.cumsum(counts)
    starts = ends - counts
    n_tiles = a // tm
    first_t = starts // tm
    last_t = jnp.maximum(ends - 1, 0) // tm
    nvis = jnp.where(counts > 0, last_t - first_t + 1, 0)
    cv_end = jnp.cumsum(nvis)
    cv_start = cv_end - nvis
    n_vis = n_tiles + MOE_N_EXPERTS
    v = jnp.arange(n_vis, dtype=i32)
    active = v < cv_end[-1]
    e = jnp.minimum(jnp.sum((cv_end[None, :] <= v[:, None]).astype(i32), axis=1), MOE_N_EXPERTS - 1)
    e_last = jnp.max(jnp.where(counts > 0, jnp.arange(MOE_N_EXPERTS, dtype=i32), 0))
    e = jnp.where(active, e, e_last)
    tile = jnp.where(active, first_t[e] + v - cv_start[e], n_tiles - 1)
    lo = jnp.where(active, jnp.clip(starts[e] - tile * tm, 0, tm), 0)
    hi = jnp.where(active, jnp.clip(ends[e] - tile * tm, 0, tm), 0)
    prev_t = jnp.concatenate([jnp.full((1,), -1, i32), tile[:-1]])
    next_t = jnp.concatenate([tile[1:], jnp.full((1,), -1, i32)])
    is_final = v == cv_end[-1] - 1
    is_last = (next_t != tile) | is_final
    prev_e = jnp.concatenate([jnp.full((1,), -1, i32), e[:-1]])
    flag = jnp.where(active, (prev_t != tile) * FLAG_FIRST + is_last * FLAG_LAST + is_final * FLAG_FINAL
                     + (prev_e != e) * FLAG_NEW_EXPERT, 0)
    cast = lambda z: z.astype(i32)
    return cast(tile), cast(e), cast(lo), cast(hi), cast(flag), cast(src), cast(dst)


def _combine_kernel(x_ref, y0_ref, y1_ref, rt_ref, g2_ref, lg_ref, lb_ref, o_ref, *, alpha):
    rt = rt_ref[...]
    tm = x_ref.shape[0]
    y = (rt[:, ROUTE_W0:ROUTE_W0 + 1] * _load_token_tiles(y0_ref, tm)
         + rt[:, ROUTE_W1:ROUTE_W1 + 1] * _load_token_tiles(y1_ref, tm))
    o_ref[...] = _layer_norm(alpha * x_ref[...] + g2_ref[...] * y, lg_ref[...], lb_ref[...])


def _combine(x1, y2, route, row0, t_all, mod3, mod_row0, rows_per_mod, ln_g, ln_b, alpha, tm):
    t, d = x1.shape
    tiles_per_mod = rows_per_mod // tm
    t0 = row0 // tm
    t1 = (t_all + row0) // tm
    return pl.pallas_call(
        functools.partial(_combine_kernel, alpha=alpha),
        grid=(t // tm,),
        in_specs=[pl.BlockSpec((tm, d), lambda i: (i, 0)),
                  pl.BlockSpec((tm * SUBLANES, LANES), lambda i: (t0 + i, 0)),
                  pl.BlockSpec((tm * SUBLANES, LANES), lambda i: (t1 + i, 0)),
                  pl.BlockSpec((tm, LANES), lambda i: (t0 + i, 0)),
                  pl.BlockSpec((None, 1, d), lambda i: (mod_row0 + i // tiles_per_mod, 0, 5)),
                  _const_spec((1, d)), _const_spec((1, d))],
        out_specs=pl.BlockSpec((tm, d), lambda i: (i, 0)),
        out_shape=jax.ShapeDtypeStruct((t, d), F32),
        compiler_params=_cparams("arbitrary"),
        name="combine_ln2",
    )(x1, y2, y2, route, mod3, ln_g.reshape(1, d), ln_b.reshape(1, d))


def _pick_tile(n, pref):
    tm = min(pref, n)
    while n % tm:
        tm //= 2
    return tm


def kernel(x, c, ctx, c_ctx, ada_w, ada_b, w_in, gm_ln_g, gm_ln_b, gm_ws, gm_bs, hy_conv_w, hy_conv_b,
           hy_f_w1, hy_f_b1, hy_f_w2, hy_f_b2, hy_f_w3, hy_f_b3, hy_skip, da_lq1, da_lk1, da_lq2, da_lk2,
           da_norm_g, p_a, p_b, p_c, w_out, ln1_g, ln1_b, moe_wg, moe_bg, moe_we, moe_be,
           ex_w_gate, ex_w_up, ex_w_down, ln2_g, ln2_b):
    B, L, D = x.shape
    Lc = ctx.shape[1]
    depth = ada_w.shape[0]
    alpha = (2.0 * depth) ** 0.25
    T, Tc = B * L, B * Lc
    moe_tm = 512 if T >= 8192 else 64

    mp = -(-(B + 1) // 8) * 8
    c_all = jnp.zeros((mp, D), F32).at[:B].set(c).at[B].set(c_ctx)
    mod = _modulation(c_all, ada_w, ada_b)

    rope_tabs = _rope_tables(L // GRID_W)
    cm, sm = _dft_tables(L)
    cm_bf, sm_bf = cm.astype(BF16), sm.astype(BF16)
    cmc, smc = _dft_tables(Lc)
    cmc_bf, smc_bf = cmc.astype(BF16), smc.astype(BF16)

    seg_all = ((OFF_GM, OFF_HY, "gm"), (OFF_HY, OFF_Q, "hy"), (OFF_Q, OFF_K, "q"), (OFF_K, OFF_V, "k"),
               (OFF_V, OFF_GATE, "v"), (OFF_GATE, OFF_GATE + N_BRANCH * D, "gate"))
    seg_kv = ((0, DA_QK_W, "k"), (DA_QK_W, DA_QK_W + DA_V_W, "v"))

    tm_l = _pick_tile(L, 512)
    tm_c = _pick_tile(Lc, 256)
    tq_l = _pick_tile(L, 512)
    tq_c = _pick_tile(Lc, 256)
    tm_m = _pick_tile(L, 512)
    tm_mc = _pick_tile(Tc, 512)

    xs = x.reshape(T, D)
    xc = ctx.reshape(Tc, D)
    for l in range(depth):
        last = l == depth - 1
        lam_init = 0.8 - 0.6 * math.exp(-0.3 * l)
        mod3 = mod[l].reshape(mp, 1, 6 * D)
        w_l = w_in[l].astype(BF16)
        lparams = [a[l].reshape(1, DA_HEAD_DIM) for a in (da_lq1, da_lk1, da_lq2, da_lk2)]
        norm_g = da_norm_g[l].reshape(1, DA_V_DIM)
        lp = {
            "gm_ln_g": gm_ln_g[l].reshape(1, GM_DIM), "gm_ln_b": gm_ln_b[l].reshape(1, GM_DIM),
            "gm_ws": gm_ws[l].reshape(GM_GROUPS * GM_CHUNK, GM_CHUNK).astype(BF16),
            "gm_bs": jnp.repeat(jnp.transpose(gm_bs[l]), GM_DIM // GM_GROUPS, axis=1),
            "p_a": p_a[l].astype(BF16), "p_b": p_b[l].astype(BF16), "p_c": p_c[l].astype(BF16),
            "w_out": w_out[l].astype(BF16),
            "ln1_g": ln1_g[l].reshape(1, D), "ln1_b": ln1_b[l].reshape(1, D),
            "w_router": _split_bf16(jnp.zeros((D, LANES), F32).at[:, :MOE_GROUPS].set(moe_wg[l])
                                    .at[:, MOE_GROUPS:MOE_GROUPS + MOE_N_EXPERTS].set(moe_we[l])),
            "b_router": jnp.zeros((1, LANES), F32).at[0, :MOE_GROUPS].set(moe_bg[l])
                           .at[0, MOE_GROUPS:MOE_GROUPS + MOE_N_EXPERTS].set(moe_be[l]),
        }
        fw = (hy_f_w1[l], hy_f_b1[l], hy_f_w2[l], hy_f_b2[l], hy_f_w3[l], hy_f_b3[l])

        zgm, zhy, q, k, v, gate = _inproj(xs, mod3, 0, L, w_l, seg_all, rope_tabs, L, tm_l)
        if last:
            k_c, v_c = _inproj(xc, mod3, B, Tc, w_l[:, OFF_K:OFF_GATE], seg_kv, None, Lc, tm_c)
        else:
            zgm_c, zhy_c, q_c, k_c, v_c, gate_c = _inproj(xc, mod3, B, Tc, w_l, seg_all, None, Lc, tm_c)
        y_c = _attention(q, [(k, v, L), (k_c, v_c, Lc)], lparams, norm_g, lam_init, B, L, tq_l)
        kre, kim, nyq = _hyena_filter_spectrum(L, cm, sm, *fw)
        y_b = _hyena(zhy, B, L, hy_conv_w[l], hy_conv_b[l], cm_bf, sm_bf, kre, kim, nyq, hy_skip[l])
        t_all = T if last else T + Tc
        x1, h2, route = _merge(zgm, y_b, y_c, gate, xs, mod3, 0, L, lp, alpha, tm_m, T, 0, None)

        if not last:
            yc_c = _attention(q_c, [(k_c, v_c, Lc)], lparams, norm_g, lam_init, B, Lc, tq_c)
            kre_c, kim_c, nyq_c = _hyena_filter_spectrum(Lc, cmc, smc, *fw)
            yb_c = _hyena(zhy_c, B, Lc, hy_conv_w[l], hy_conv_b[l], cmc_bf, smc_bf, kre_c, kim_c, nyq_c,
                          hy_skip[l])
            x1c, h2c, route_c = _merge(zgm_c, yb_c, yc_c, gate_c, xc, mod3, B, Tc, lp, alpha, tm_mc, Tc, 0, None)
            h2 = jnp.concatenate([h2, h2c], axis=0)
            route = jnp.concatenate([route, route_c], axis=0)

        plan = _visit_plan(route, moe_tm)
        y2 = _experts(h2, plan, ex_w_gate, ex_w_up, ex_w_down, l, moe_tm)
        xs = _combine(x1, y2, route, 0, t_all, mod3, 0, L, ln2_g[l], ln2_b[l], alpha, tm_c)
        if not last:
            xc = _combine(x1c, y2, route, T, t_all, mod3, B, Tc, ln2_g[l], ln2_b[l], alpha, tm_c)
    return xs.reshape(B, L, D)
```

```python
import functools
import math

import numpy as np
import jax
import jax.numpy as jnp
from jax import lax
from jax.experimental import pallas as pl
from jax.experimental.pallas import tpu as pltpu

F32 = jnp.float32
BF16 = jnp.bfloat16
HIGHEST = lax.Precision.HIGHEST

GRID_W = 64
GM_DIM = 256
GM_GROUPS = 4
GM_CHUNK = 128
HY_DIM = 256
HY_EMB = 33
HY_BANDS = (HY_EMB - 1) // 2
HY_DECAY_FAST = 0.3
HY_DECAY_SLOW = 1.5
HY_DECAY_TARGET = 1e-2
HY_DECAY_SHIFT = 0.05
DA_HEADS = 4
DA_HEAD_DIM = 64
DA_V_DIM = 2 * DA_HEAD_DIM
DA_QK_W = DA_HEADS * 2 * DA_HEAD_DIM
DA_V_W = DA_HEADS * DA_V_DIM
ROPE_BASE = 10000.0
N_BRANCH = 3
OFF_GM = 0
OFF_HY = OFF_GM + 2 * GM_DIM
OFF_Q = OFF_HY + 3 * HY_DIM
OFF_K = OFF_Q + DA_QK_W
OFF_V = OFF_K + DA_QK_W
OFF_GATE = OFF_V + DA_V_W
MOE_GROUPS = 4
MOE_EXPERTS_PER_GROUP = 8
MOE_N_EXPERTS = MOE_GROUPS * MOE_EXPERTS_PER_GROUP
MOE_TOP_K = 2
LN_EPS = 1e-5
LANES = 128
VMEM_LIMIT = 56 * 1024 * 1024


def _cparams(*sem):
    return pltpu.CompilerParams(dimension_semantics=sem, vmem_limit_bytes=VMEM_LIMIT)


def _sigmoid(x):
    return 1.0 / (1.0 + jnp.exp(-x))


def _layer_norm(x, g, b):
    mu = jnp.mean(x, axis=-1, keepdims=True)
    xc = x - mu
    var = jnp.mean(xc * xc, axis=-1, keepdims=True)
    return xc * lax.rsqrt(var + LN_EPS) * g + b


def _gelu_tanh(x):
    return 0.5 * x * (1.0 + jnp.tanh(math.sqrt(2.0 / math.pi) * (x + 0.044715 * (x * x * x))))


def _const_spec(shape):
    nd = len(shape)
    return pl.BlockSpec(shape, lambda *_: (0,) * nd)


def _mod_kernel(c_ref, w_ref, b_ref, o_ref):
    c = c_ref[...]
    s = c * _sigmoid(c)
    o_ref[...] = jnp.dot(s, w_ref[...], precision=HIGHEST, preferred_element_type=F32) + b_ref[...]


def _modulation(c_all, ada_w, ada_b):
    depth, d, n = ada_w.shape
    mp = c_all.shape[0]
    tn = 512
    return pl.pallas_call(
        _mod_kernel,
        grid=(depth, n // tn),
        in_specs=[pl.BlockSpec((mp, d), lambda l, j: (0, 0)),
                  pl.BlockSpec((None, d, tn), lambda l, j: (l, 0, j)),
                  pl.BlockSpec((None, 1, tn), lambda l, j: (l, 0, j))],
        out_specs=pl.BlockSpec((None, mp, tn), lambda l, j: (l, 0, j)),
        out_shape=jax.ShapeDtypeStruct((depth, mp, n), F32),
        compiler_params=_cparams("arbitrary", "arbitrary"),
        name="adaln_mod",
    )(c_all, ada_w, ada_b.reshape(depth, 1, n))


def _rope_tables(rows):
    n_freq = DA_HEAD_DIM // 4
    row = jnp.broadcast_to(jnp.arange(rows)[:, None], (rows, GRID_W)).reshape(-1).astype(F32)
    col = jnp.broadcast_to(jnp.arange(GRID_W)[None, :], (rows, GRID_W)).reshape(-1).astype(F32)
    inv = ROPE_BASE ** (-jnp.arange(n_freq, dtype=F32) / n_freq)
    ang_r = row[:, None] * inv
    ang_c = col[:, None] * inv
    c64 = jnp.concatenate([jnp.cos(ang_r), jnp.cos(ang_r), jnp.cos(ang_c), jnp.cos(ang_c)], axis=-1)
    s64 = jnp.concatenate([-jnp.sin(ang_r), jnp.sin(ang_r), -jnp.sin(ang_c), jnp.sin(ang_c)], axis=-1)
    return jnp.tile(c64, (1, LANES // DA_HEAD_DIM)), jnp.tile(s64, (1, LANES // DA_HEAD_DIM))


def _rope_block(xb, cos, sin):
    lane = lax.broadcasted_iota(jnp.int32, xb.shape, 1)
    n_freq = DA_HEAD_DIM // 4
    first_half = (lane % (2 * n_freq)) < n_freq
    partner = jnp.where(first_half, pltpu.roll(xb, LANES - n_freq, 1), pltpu.roll(xb, n_freq, 1))
    return xb * cos + partner * sin


def _inproj_kernel(*refs, segs, use_rope, n_chunk):
    if use_rope:
        x_ref, sh_ref, sc_ref, w_ref, cos_ref, sin_ref = refs[:6]
        out_refs = refs[6:]
    else:
        x_ref, sh_ref, sc_ref, w_ref = refs[:4]
        out_refs = refs[4:]
    h = (x_ref[...] * (1.0 + sc_ref[...]) + sh_ref[...]).astype(BF16)
    for (a, b, kind), o_ref in zip(segs, out_refs):
        for c0 in range(a, b, n_chunk):
            c1 = min(c0 + n_chunk, b)
            acc = jnp.dot(h, w_ref[:, c0:c1], preferred_element_type=F32)
            if kind == "q":
                acc = acc * (DA_HEAD_DIM ** -0.5 * math.log2(math.e))
            if use_rope and kind in ("q", "k"):
                cos = cos_ref[...]
                sin = sin_ref[...]
                for j in range((c1 - c0) // LANES):
                    blk = _rope_block(acc[:, j * LANES:(j + 1) * LANES], cos, sin)
                    o_ref[:, c0 - a + j * LANES:c0 - a + (j + 1) * LANES] = blk.astype(o_ref.dtype)
            else:
                o_ref[:, c0 - a:c1 - a] = acc.astype(o_ref.dtype)


def _inproj(x2d, mod3, mod_row0, rows_per_mod, w, segs, rope_tabs, seq_len, tm):
    t, d = x2d.shape
    n = w.shape[1]
    use_rope = rope_tabs is not None
    tiles_per_mod = rows_per_mod // tm
    tiles_per_seq = seq_len // tm

    def mod_map(piece):
        return lambda i: (mod_row0 + i // tiles_per_mod, 0, piece)

    in_specs = [pl.BlockSpec((tm, d), lambda i: (i, 0)),
                pl.BlockSpec((None, 1, d), mod_map(0)),
                pl.BlockSpec((None, 1, d), mod_map(1)),
                pl.BlockSpec((d, n), lambda i: (0, 0), pipeline_mode=pl.Buffered(1))]
    args = [x2d, mod3, mod3, w]
    if use_rope:
        in_specs += [pl.BlockSpec((tm, LANES), lambda i: (i % tiles_per_seq, 0))] * 2
        args += list(rope_tabs)
    out_specs = [pl.BlockSpec((tm, b - a), lambda i: (i, 0)) for a, b, _ in segs]
    out_shape = [jax.ShapeDtypeStruct((t, b - a), BF16) for a, b, _ in segs]
    return pl.pallas_call(
        functools.partial(_inproj_kernel, segs=segs, use_rope=use_rope, n_chunk=512),
        grid=(t // tm,),
        in_specs=in_specs, out_specs=out_specs, out_shape=out_shape,
        compiler_params=_cparams("arbitrary"),
        name="inproj",
    )(*args)


ATTN_KEY_CHUNK = 256


def _attn_kernel(*refs, src_lens, lam_init):
    n_src = len(src_lens)
    lq1, lk1, lq2, lk2, g_ref, q_ref = refs[:6]
    kv_refs = refs[6:6 + 2 * n_src]
    o_ref, s_scr = refs[6 + 2 * n_src:]
    lam = (jnp.exp(jnp.sum(lq1[...] * lk1[...], axis=-1, keepdims=True))
           - jnp.exp(jnp.sum(lq2[...] * lk2[...], axis=-1, keepdims=True)) + lam_init)
    tq = q_ref.shape[0]
    lane = lax.broadcasted_iota(jnp.int32, (tq, LANES), 1)
    dn = (((1,), (1,)), ((), ()))
    chunks = []
    off = 0
    for j, n in enumerate(src_lens):
        kc = min(ATTN_KEY_CHUNK, n)
        for st in range(0, n, kc):
            chunks.append((j, st, kc, off))
            off += kc
    for h in range(DA_HEADS):
        cols = slice(h * LANES, (h + 1) * LANES)
        qh = q_ref[:, cols]
        zero = jnp.zeros_like(qh)
        om = []
        for m in range(2):
            qm = jnp.where(lane < DA_HEAD_DIM if m == 0 else lane >= DA_HEAD_DIM, qh, zero)
            mlane = None
            for j, st, kc, off in chunks:
                s_c = lax.dot_general(qm, kv_refs[2 * j][st:st + kc, cols], dn, preferred_element_type=F32)
                s_scr[:, off:off + kc] = s_c
                for b in range(kc // LANES):
                    blk = s_c[:, b * LANES:(b + 1) * LANES]
                    mlane = blk if mlane is None else jnp.maximum(mlane, blk)
            mx = jnp.max(mlane, axis=-1, keepdims=True)
            acc = None
            for j, st, kc, off in chunks:
                p = jnp.exp2(s_scr[:, off:off + kc] - mx).astype(BF16)
                v_aug = jnp.concatenate([kv_refs[2 * j + 1][st:st + kc, cols], jnp.ones((kc, LANES), BF16)],
                                        axis=1)
                d = jnp.dot(p, v_aug, preferred_element_type=F32)
                acc = d if acc is None else acc + d
            om.append(acc[:, :LANES] * (1.0 / acc[:, LANES:LANES + 1]))
        o = om[0] - lam * om[1]
        ms = jnp.mean(o * o, axis=-1, keepdims=True)
        o = o * lax.rsqrt(ms + LN_EPS) * g_ref[...] * (1.0 - lam_init)
        o_ref[:, cols] = o.astype(o_ref.dtype)


def _attention(q, kvs, lparams, norm_g, lam_init, nb, lq, tq):
    t = q.shape[0]
    qt = lq // tq
    in_specs = [_const_spec((1, DA_HEAD_DIM))] * 4 + [_const_spec((1, DA_V_DIM))]
    in_specs.append(pl.BlockSpec((tq, DA_QK_W), lambda b, i: (b * qt + i, 0)))
    args = list(lparams) + [norm_g, q]
    for k, v, lk in kvs:
        in_specs += [pl.BlockSpec((lk, DA_QK_W), lambda b, i: (b, 0)),
                     pl.BlockSpec((lk, DA_V_W), lambda b, i: (b, 0))]
        args += [k, v]
    src_lens = tuple(lk for _, _, lk in kvs)
    return pl.pallas_call(
        functools.partial(_attn_kernel, src_lens=src_lens, lam_init=lam_init),
        grid=(nb, qt),
        in_specs=in_specs,
        out_specs=pl.BlockSpec((tq, DA_V_W), lambda b, i: (b * qt + i, 0)),
        out_shape=jax.ShapeDtypeStruct((t, DA_V_W), BF16),
        scratch_shapes=[pltpu.VMEM((tq, sum(src_lens)), F32)],
        compiler_params=_cparams("arbitrary", "arbitrary"),
        name="diff_attn",
    )(*args)


DFT_SPLIT = 64


def _dft_tables(L):
    n = jnp.arange(L, dtype=jnp.int32)[None, :]
    k1 = jnp.arange(L // DFT_SPLIT, dtype=jnp.int32)[:, None] * DFT_SPLIT
    k0 = jnp.arange(DFT_SPLIT, dtype=jnp.int32)[:, None]
    ang_a = ((k1 * n) % (2 * L)).astype(F32) * (math.pi / L)
    ang_b = ((k0 * n) % (2 * L)).astype(F32) * (math.pi / L)
    ca, sa = jnp.cos(ang_a)[:, None, :], jnp.sin(ang_a)[:, None, :]
    cb, sb = jnp.cos(ang_b)[None, :, :], jnp.sin(ang_b)[None, :, :]
    return (ca * cb - sa * sb).reshape(L, L), (sa * cb + ca * sb).reshape(L, L)


def _filter_consts(L):
    t = jnp.linspace(0.0, 1.0, L, dtype=F32)[:, None]
    w = 2.0 * math.pi * jnp.arange(L, dtype=F32)[:, None] / L
    f = jnp.linspace(1e-4, HY_BANDS - 1, HY_BANDS, dtype=F32)[None, :]
    emb = jnp.concatenate([t, jnp.cos(f * w), -jnp.sin(f * w)], axis=-1)
    max_decay = math.log(HY_DECAY_TARGET) / HY_DECAY_FAST
    min_decay = math.log(HY_DECAY_TARGET) / HY_DECAY_SLOW
    deltas = jnp.abs(jnp.linspace(min_decay, max_decay, HY_DIM, dtype=F32))
    window = jnp.exp(-t * deltas[None, :]) + HY_DECAY_SHIFT
    return emb, window


def _filter_kernel(emb_ref, win_ref, w1, b1, w2, b2, w3, b3, hs_ref, hd_ref, nyq_ref):
    h = jnp.sin(jnp.dot(emb_ref[...], w1[...], precision=HIGHEST, preferred_element_type=F32) + b1[...])
    h = jnp.sin(jnp.dot(h, w2[...], precision=HIGHEST, preferred_element_type=F32) + b2[...])
    h = jnp.dot(h, w3[...], precision=HIGHEST, preferred_element_type=F32) + b3[...]
    win = win_ref[...]
    hf = h[:, :HY_DIM] * win
    hb = h[:, HY_DIM:] * win
    row = lax.broadcasted_iota(jnp.int32, hf.shape, 0)
    hb = jnp.where(row == 0, 0.0, hb)
    alt = jnp.where(row % 2 == 0, 1.0, -1.0)
    hs_ref[...] = hf + hb
    hd_ref[...] = hf - hb
    nyq_ref[...] = jnp.sum((hf + hb) * alt, axis=0, keepdims=True)


def _spectrum_kernel(c_ref, s_ref, hs_ref, hd_ref, kre_ref, kim_ref, *, n_fft):
    i = pl.program_id(0)
    tk = c_ref.shape[0]
    kidx = i * tk + lax.broadcasted_iota(jnp.int32, (tk, 1), 0)
    scale = jnp.where(kidx == 0, 1.0 / n_fft, 2.0 / n_fft)
    kre = jnp.dot(c_ref[...], hs_ref[...], precision=HIGHEST, preferred_element_type=F32)
    kim = -jnp.dot(s_ref[...], hd_ref[...], precision=HIGHEST, preferred_element_type=F32)
    kre_ref[...] = kre * scale
    kim_ref[...] = kim * scale


def _hyena_filter_spectrum(L, cmat, smat, w1, b1, w2, b2, w3, b3):
    emb, window = _filter_consts(L)
    full = lambda a: _const_spec(a.shape)
    ins = [emb, window, w1, b1.reshape(1, -1), w2, b2.reshape(1, -1), w3, b3.reshape(1, -1)]
    hs, hd, nyq = pl.pallas_call(
        _filter_kernel,
        grid=(1,),
        in_specs=[full(a) for a in ins],
        out_specs=[_const_spec((L, HY_DIM)), _const_spec((L, HY_DIM)), _const_spec((1, HY_DIM))],
        out_shape=[jax.ShapeDtypeStruct((L, HY_DIM), F32), jax.ShapeDtypeStruct((L, HY_DIM), F32),
                   jax.ShapeDtypeStruct((1, HY_DIM), F32)],
        compiler_params=_cparams("arbitrary"),
        name="hyena_filter",
    )(*ins)
    tk = min(256, L)
    kre, kim = pl.pallas_call(
        functools.partial(_spectrum_kernel, n_fft=2 * L),
        grid=(L // tk,),
        in_specs=[pl.BlockSpec((tk, L), lambda i: (i, 0)), pl.BlockSpec((tk, L), lambda i: (i, 0)),
                  _const_spec((L, HY_DIM)), _const_spec((L, HY_DIM))],
        out_specs=[pl.BlockSpec((tk, HY_DIM), lambda i: (i, 0))] * 2,
        out_shape=[jax.ShapeDtypeStruct((L, HY_DIM), F32)] * 2,
        compiler_params=_cparams("arbitrary"),
        name="hyena_spectrum",
    )(cmat, smat, hs, hd)
    return kre, kim, nyq * (1.0 / (2 * L))


HY_ROW_BLOCK = 1024


def _hyena_kernel(z_ref, cw_ref, cb_ref, c_ref, s_ref, kre_ref, kim_ref, nyq_ref, skip_ref, o_ref,
                  u_ref, x0_ref, p_ref, q_ref):
    L = z_ref.shape[0]
    row = lax.broadcasted_iota(jnp.int32, (L, HY_DIM), 0)

    def conv(j):
        cols = slice(j * HY_DIM, (j + 1) * HY_DIM)
        z = z_ref[:, cols].astype(F32)
        zprev = jnp.where(row == 0, 0.0, pltpu.roll(z, 1, 0))
        znext = jnp.where(row == L - 1, 0.0, pltpu.roll(z, L - 1, 0))
        return zprev * cw_ref[0:1, cols] + z * cw_ref[1:2, cols] + znext * cw_ref[2:3, cols] + cb_ref[:, cols]

    u = conv(2) * conv(1)
    ub = u.astype(BF16)
    u_ref[...] = u
    alt = jnp.where(row % 2 == 0, 1.0, -1.0)
    nyq_term = jnp.sum(u * alt, axis=0, keepdims=True) * nyq_ref[...]
    x0_ref[...] = conv(0)
    blk = min(HY_ROW_BLOCK, L)
    for r in range(0, L, blk):
        rows = slice(r, r + blk)
        a = jnp.dot(c_ref[rows, :], ub, preferred_element_type=F32)
        b = jnp.dot(s_ref[rows, :], ub, preferred_element_type=F32)
        kre = kre_ref[rows, :]
        kim = kim_ref[rows, :]
        p_ref[rows, :] = (a * kre + b * kim).astype(BF16)
        q_ref[rows, :] = (b * kre - a * kim).astype(BF16)
    for r in range(0, L, blk):
        rows = slice(r, r + blk)
        y = (jnp.dot(c_ref[rows, :], p_ref[...], preferred_element_type=F32)
             + jnp.dot(s_ref[rows, :], q_ref[...], preferred_element_type=F32))
        ub_rows = u_ref[rows, :]
        row_b = lax.broadcasted_iota(jnp.int32, (blk, HY_DIM), 0)
        y = y + jnp.where(row_b % 2 == 0, nyq_term, -nyq_term) + ub_rows * skip_ref[...]
        o_ref[rows, :] = (y * x0_ref[rows, :]).astype(o_ref.dtype)


def _hyena(zhy, nb, L, conv_w, conv_b, cmat_bf, smat_bf, kre, kim, nyq, skip):
    t = zhy.shape[0]
    return pl.pallas_call(
        _hyena_kernel,
        grid=(nb,),
        in_specs=[pl.BlockSpec((L, 3 * HY_DIM), lambda b: (b, 0)),
                  _const_spec((3, 3 * HY_DIM)), _const_spec((1, 3 * HY_DIM)),
                  pl.BlockSpec((L, L), lambda b: (0, 0), pipeline_mode=pl.Buffered(1)),
                  pl.BlockSpec((L, L), lambda b: (0, 0), pipeline_mode=pl.Buffered(1)),
                  _const_spec((L, HY_DIM)), _const_spec((L, HY_DIM)),
                  _const_spec((1, HY_DIM)), _const_spec((1, HY_DIM))],
        out_specs=pl.BlockSpec((L, HY_DIM), lambda b: (b, 0)),
        out_shape=jax.ShapeDtypeStruct((t, HY_DIM), BF16),
        scratch_shapes=[pltpu.VMEM((L, HY_DIM), F32), pltpu.VMEM((L, HY_DIM), F32),
                        pltpu.VMEM((L, HY_DIM), BF16), pltpu.VMEM((L, HY_DIM), BF16)],
        compiler_params=_cparams("arbitrary"),
        name="hyena_conv",
    )(zhy, conv_w, conv_b.reshape(1, -1), cmat_bf, smat_bf, kre, kim, nyq, skip.reshape(1, -1))


SUBLANES = 8


def _split_bf16(w):
    hi = w.astype(BF16)
    return jnp.stack([hi, (w - hi.astype(F32)).astype(BF16)])


def _store_token_tiles(ref, val):
    n = val.shape[0]
    for j in range(val.shape[1] // LANES):
        ref[pl.ds(j, n, stride=SUBLANES), :] = val[:, j * LANES:(j + 1) * LANES]


def _load_token_tiles(ref, n):
    return jnp.concatenate([ref[pl.ds(j, n, stride=SUBLANES), :] for j in range(SUBLANES)], axis=1)


def _merge_kernel(*refs, alpha, n_alias):
    (zgm_ref, yb_ref, yc_ref, gate_ref, x_ref, g1_ref, sh2_ref, sc2_ref, lng_ref, lnb_ref, ws_ref, bs_ref,
     pa_ref, pb_ref, pc_ref, wo_ref, l1g_ref, l1b_ref, wr_ref, br_ref) = refs[:20]
    x1_ref, h2_ref, rt_ref = refs[20 + n_alias:]
    tm = x_ref.shape[0]
    d = x_ref.shape[1]
    gm = _gelu_tanh(zgm_ref[...].astype(F32))
    u = gm[:, :GM_DIM]
    v = _layer_norm(gm[:, GM_DIM:], lng_ref[...], lnb_ref[...]).astype(BF16)
    lane_group = lax.broadcasted_iota(jnp.int32, (GM_CHUNK, GM_DIM), 1) // (GM_DIM // GM_GROUPS)
    ya = []
    for cidx in range(tm // GM_CHUNK):
        rows = slice(cidx * GM_CHUNK, (cidx + 1) * GM_CHUNK)
        r = jnp.dot(ws_ref[...], v[rows], preferred_element_type=F32)
        vv = bs_ref[...]
        for g in range(GM_GROUPS):
            vv = vv + jnp.where(lane_group == g, r[g * GM_CHUNK:(g + 1) * GM_CHUNK], 0.0)
        ya.append(u[rows] * vv)
    ya = jnp.concatenate(ya, axis=0) if len(ya) > 1 else ya[0]
    ma = jnp.dot(ya.astype(BF16), pa_ref[...], preferred_element_type=F32)
    mb = jnp.dot(yb_ref[...], pb_ref[...], preferred_element_type=F32)
    mc = jnp.dot(yc_ref[...], pc_ref[...], preferred_element_type=F32)
    merged = (_sigmoid(gate_ref[:, 0:d].astype(F32)) * ma
              + _sigmoid(gate_ref[:, d:2 * d].astype(F32)) * mb
              + _sigmoid(gate_ref[:, 2 * d:3 * d].astype(F32)) * mc)
    out = jnp.dot(merged.astype(BF16), wo_ref[...], preferred_element_type=F32)
    x1 = _layer_norm(alpha * x_ref[...] + g1_ref[...] * out, l1g_ref[...], l1b_ref[...])
    x1_ref[...] = x1
    h2 = x1 * (1.0 + sc2_ref[...]) + sh2_ref[...]
    _store_token_tiles(h2_ref, h2)
    h2_hi = h2.astype(BF16)
    h2_lo = (h2 - h2_hi.astype(F32)).astype(BF16)
    lg = (jnp.dot(h2_hi, wr_ref[0], preferred_element_type=F32)
          + jnp.dot(h2_hi, wr_ref[1], preferred_element_type=F32)
          + jnp.dot(h2_lo, wr_ref[0], preferred_element_type=F32) + br_ref[...])
    rt_ref[...] = _route(lg)


ROUTE_E0, ROUTE_E1, ROUTE_W0, ROUTE_W1 = 0, 1, 2, 3


def _route(lg):
    neg = jnp.float32(-3.0e38)
    lane_i = lax.broadcasted_iota(jnp.int32, lg.shape, 1)
    lane = lane_i.astype(F32)
    big = jnp.float32(LANES)
    is_g = lane_i < MOE_GROUPS
    gl = jnp.where(is_g, lg, neg)
    gmax = jnp.max(gl, axis=-1, keepdims=True)
    g_idx = jnp.min(jnp.where(gl == gmax, lane, big), axis=-1, keepdims=True)
    g_prob = 1.0 / jnp.sum(jnp.where(is_g, jnp.exp(gl - gmax), 0.0), axis=-1, keepdims=True)
    e_lo = MOE_GROUPS + MOE_EXPERTS_PER_GROUP * g_idx
    el = jnp.where(lane >= e_lo, jnp.where(lane < e_lo + MOE_EXPERTS_PER_GROUP, lg, neg), neg)
    v1 = jnp.max(el, axis=-1, keepdims=True)
    i1 = jnp.min(jnp.where(el == v1, lane, big), axis=-1, keepdims=True)
    el2 = jnp.where(lane == i1, neg, el)
    v2 = jnp.max(el2, axis=-1, keepdims=True)
    i2 = jnp.min(jnp.where(el2 == v2, lane, big), axis=-1, keepdims=True)
    e21 = jnp.exp(v2 - v1)
    w1 = g_prob / (1.0 + e21)
    w2 = w1 * e21
    swap = i2 < i1
    rec = jnp.where(lane_i == ROUTE_E0, jnp.minimum(i1, i2) - MOE_GROUPS, 0.0)
    rec = jnp.where(lane_i == ROUTE_E1, jnp.maximum(i1, i2) - MOE_GROUPS, rec)
    rec = jnp.where(lane_i == ROUTE_W0, jnp.where(swap, w2, w1), rec)
    return jnp.where(lane_i == ROUTE_W1, jnp.where(swap, w1, w2), rec)


def _merge(zgm, yb, yc, gate, x2d, mod3, mod_row0, rows_per_mod, lp, alpha, tm, t_all, row0, prev):
    t, d = x2d.shape
    tiles_per_mod = rows_per_mod // tm
    off = row0 // tm

    def mod_map(piece):
        return lambda i: (mod_row0 + i // tiles_per_mod, 0, piece)

    row = lambda w: pl.BlockSpec((tm, w), lambda i: (i, 0))
    row_off = lambda w: pl.BlockSpec((tm, w), lambda i: (off + i, 0))
    consts = [lp["gm_ln_g"], lp["gm_ln_b"], lp["gm_ws"], lp["gm_bs"], lp["p_a"], lp["p_b"], lp["p_c"],
              lp["w_out"], lp["ln1_g"], lp["ln1_b"], lp["w_router"], lp["b_router"]]
    in_specs = [row(2 * GM_DIM), row(HY_DIM), row(DA_V_W), row(N_BRANCH * d), row(d),
                pl.BlockSpec((None, 1, d), mod_map(2)), pl.BlockSpec((None, 1, d), mod_map(3)),
                pl.BlockSpec((None, 1, d), mod_map(4))] + [_const_spec(a.shape) for a in consts]
    args = [zgm, yb, yc, gate, x2d, mod3, mod3, mod3, *consts]
    aliases = {}
    if prev is not None:
        aliases = {len(args): 1, len(args) + 1: 2}
        in_specs += [pl.BlockSpec(memory_space=pl.ANY)] * 2
        args += list(prev)
    return pl.pallas_call(
        functools.partial(_merge_kernel, alpha=alpha, n_alias=len(aliases)),
        grid=(t // tm,),
        in_specs=in_specs,
        out_specs=[row(d), pl.BlockSpec((tm * SUBLANES, LANES), lambda i: (off + i, 0)), row_off(LANES)],
        out_shape=[jax.ShapeDtypeStruct((t, d), F32), jax.ShapeDtypeStruct((t_all * SUBLANES, LANES), F32),
                   jax.ShapeDtypeStruct((t_all, LANES), F32)],
        input_output_aliases=aliases,
        compiler_params=_cparams("arbitrary"),
        name="merge_ln1",
    )(*args)


PAIR_CLASSES = tuple((MOE_EXPERTS_PER_GROUP * g + a, MOE_EXPERTS_PER_GROUP * g + b)
                     for g in range(MOE_GROUPS)
                     for a in range(MOE_EXPERTS_PER_GROUP) for b in range(a + 1, MOE_EXPERTS_PER_GROUP))
FLAG_FIRST, FLAG_LAST, FLAG_FINAL, FLAG_NEW_GROUP, FLAG_HI = 1, 2, 4, 8, 16


def _tile_copy(src, src_row, dst, dst_row, sem):
    s0 = pl.multiple_of(src_row * SUBLANES, SUBLANES)
    d0 = pl.multiple_of(dst_row * SUBLANES, SUBLANES)
    return pltpu.make_async_copy(src.at[pl.ds(s0, SUBLANES)], dst.at[pl.ds(d0, SUBLANES)], sem)


def _expert_kernel(vt_ref, ve_ref, vlo_ref, vhi_ref, vflag_ref, src_ref, nsrc_ref, dst_ref, pdst_ref,
                   wlo_ref, whi_ref, h2_hbm, wg_hbm, wu_hbm, wd_hbm, y_hbm,
                   xbuf, acc, ybuf, wg_grp, wu_grp, wd_grp, gsem, ssem, wsem, *, n_tiles):
    v = pl.program_id(0)
    tile, expert, lo, hi, flag = vt_ref[v], ve_ref[v], vlo_ref[v], vhi_ref[v], vflag_ref[v]
    tm = acc.shape[0]
    slot = tile % 2
    first = (flag & FLAG_FIRST) != 0
    group0 = pl.multiple_of((expert // MOE_EXPERTS_PER_GROUP) * MOE_EXPERTS_PER_GROUP, MOE_EXPERTS_PER_GROUP)
    e_in_group = expert - group0

    @pl.when((flag & FLAG_NEW_GROUP) != 0)
    def _():
        copies = [pltpu.make_async_copy(w_hbm.at[pl.ds(group0, MOE_EXPERTS_PER_GROUP)], w_grp, wsem.at[i])
                  for i, (w_hbm, w_grp) in enumerate(((wg_hbm, wg_grp), (wu_hbm, wu_grp), (wd_hbm, wd_grp)))]
        for cp in copies:
            cp.start()
        for cp in copies:
            cp.wait()

    def issue_gather(idx_ref, to_slot):
        def body(i, carry):
            for j in range(SUBLANES):
                r = i * SUBLANES + j
                _tile_copy(h2_hbm, idx_ref[0, r], xbuf.at[to_slot], r, gsem.at[to_slot]).start(priority=j % 2)
            return carry

        lax.fori_loop(0, tm // SUBLANES, body, 0)

    def issue_scatter(idx_ref):
        def body(i, carry):
            for j in range(SUBLANES):
                r = i * SUBLANES + j
                _tile_copy(ybuf, r, y_hbm, idx_ref[0, r], ssem).start(priority=j % 2)
            return carry

        lax.fori_loop(0, tm // SUBLANES, body, 0)

    has_next = tile + 1 < n_tiles
    prefetch = first & (tile > 0) & has_next

    @pl.when(first)
    def _():
        @pl.when(tile == 0)
        def _():
            issue_gather(src_ref, slot)

        pltpu.make_async_copy(h2_hbm.at[pl.ds(0, tm * SUBLANES)], xbuf.at[slot], gsem.at[slot]).wait()

        @pl.when(jnp.logical_not(prefetch))
        def _():
            @pl.when(has_next)
            def _():
                issue_gather(nsrc_ref, 1 - slot)

            @pl.when(tile > 0)
            def _():
                issue_scatter(pdst_ref)

    def compute(with_prefetch):
        xb = _load_token_tiles(xbuf.at[slot], tm).astype(BF16)
        if with_prefetch:
            for r in range(tm):
                _tile_copy(h2_hbm, nsrc_ref[0, r], xbuf.at[1 - slot], r, gsem.at[1 - slot]).start(priority=r % 2)
                _tile_copy(ybuf, r, y_hbm, pdst_ref[0, r], ssem).start(priority=r % 2)
        g = jnp.dot(xb, wg_grp[e_in_group], preferred_element_type=F32)
        u = jnp.dot(xb, wu_grp[e_in_group], preferred_element_type=F32)
        hmid = (g * _sigmoid(g) * u).astype(BF16)
        y = jnp.dot(hmid, wd_grp[e_in_group], preferred_element_type=F32)
        w_b = jnp.where((flag & FLAG_HI) != 0, whi_ref[...], wlo_ref[...])
        row = lax.broadcasted_iota(jnp.int32, (tm, 1), 0)
        y = jnp.where((row >= lo) & (row < hi), y * jnp.concatenate([w_b] * SUBLANES, axis=1), 0.0)
        if with_prefetch:
            acc[...] = y
            return

        @pl.when(first)
        def _():
            acc[...] = y

        @pl.when(jnp.logical_not(first))
        def _():
            acc[...] += y

    @pl.when(prefetch)
    def _():
        compute(True)

    @pl.when(jnp.logical_not(prefetch) & (hi > lo))
    def _():
        compute(False)

    @pl.when((flag & FLAG_LAST) != 0)
    def _():
        whole = pltpu.make_async_copy(ybuf, y_hbm.at[pl.ds(0, tm * SUBLANES)], ssem)

        @pl.when(tile > 0)
        def _():
            whole.wait()

        _store_token_tiles(ybuf, acc[...])

        @pl.when((flag & FLAG_FINAL) != 0)
        def _():
            issue_scatter(dst_ref)
            whole.wait()


def _experts(h2, plan, w_gate, w_up, w_down, tm):
    t = h2.shape[0] // SUBLANES
    vt, ve, vlo, vhi, vflag, order, wlo_b, whi_b = plan
    n_vis = vt.shape[0]
    n_tiles = t // tm
    d, hid = w_gate.shape[-2:]
    grp = MOE_EXPERTS_PER_GROUP
    assert d == SUBLANES * LANES
    idx_spec = lambda nxt: pl.BlockSpec(
        (None, 1, tm), lambda v, vt, *_: (jnp.clip(vt[v] + nxt, 0, n_tiles - 1), 0, 0), memory_space=pltpu.SMEM)
    w_tile = pl.BlockSpec((tm, LANES), lambda v, vt, *_: (vt[v], 0))
    any_spec = pl.BlockSpec(memory_space=pl.ANY)
    grid_spec = pltpu.PrefetchScalarGridSpec(
        num_scalar_prefetch=5,
        grid=(n_vis,),
        in_specs=[idx_spec(0), idx_spec(1), idx_spec(0), idx_spec(-1), w_tile, w_tile,
                  any_spec, any_spec, any_spec, any_spec],
        out_specs=any_spec,
        scratch_shapes=[pltpu.VMEM((2, tm * SUBLANES, LANES), F32), pltpu.VMEM((tm, d), F32),
                        pltpu.VMEM((tm * SUBLANES, LANES), F32),
                        pltpu.VMEM((grp, d, hid), BF16), pltpu.VMEM((grp, d, hid), BF16),
                        pltpu.VMEM((grp, hid, d), BF16),
                        pltpu.SemaphoreType.DMA((2,)), pltpu.SemaphoreType.DMA(()), pltpu.SemaphoreType.DMA((3,))],
    )
    idx3 = order.reshape(n_tiles, 1, tm)
    return pl.pallas_call(
        functools.partial(_expert_kernel, n_tiles=n_tiles),
        grid_spec=grid_spec,
        out_shape=jax.ShapeDtypeStruct((t * SUBLANES, LANES), F32),
        compiler_params=_cparams("arbitrary"),
        name="moe_experts",
    )(vt, ve, vlo, vhi, vflag, idx3, idx3, idx3, idx3, wlo_b, whi_b, h2, w_gate, w_up, w_down)


def _visit_plan(route, tm):
    t = route.shape[0]
    i32 = jnp.int32
    n_cls = len(PAIR_CLASSES)
    cls_lo = jnp.asarray(np.array([p[0] for p in PAIR_CLASSES], np.int32))
    cls_hi = jnp.asarray(np.array([p[1] for p in PAIR_CLASSES], np.int32))
    cls_key = cls_lo * MOE_N_EXPERTS + cls_hi
    key = route[:, ROUTE_E0].astype(i32) * MOE_N_EXPERTS + route[:, ROUTE_E1].astype(i32)
    idx_bits = max(1, (t - 1).bit_length())
    assert (MOE_N_EXPERTS * MOE_N_EXPERTS) << idx_bits <= 2 ** 31
    packed, wlo, whi = lax.sort((key * (1 << idx_bits) + lax.iota(i32, t), route[:, ROUTE_W0], route[:, ROUTE_W1]),
                                num_keys=1)
    order = packed & ((1 << idx_bits) - 1)
    counts = jnp.sum((key[:, None] == cls_key[None, :]).astype(i32), axis=0)
    ends = jnp.cumsum(counts)
    starts = ends - counts
    n_tiles = t // tm
    first_t = starts // tm
    last_t = jnp.maximum(ends - 1, 0) // tm
    nvis = jnp.where(counts > 0, 2 * (last_t - first_t + 1), 0)
    cv_end = jnp.cumsum(nvis)
    cv_start = cv_end - nvis
    n_vis = 2 * (n_tiles + n_cls)
    v = jnp.arange(n_vis, dtype=i32)
    active = v < cv_end[-1]
    c = jnp.minimum(jnp.sum((cv_end[None, :] <= v[:, None]).astype(i32), axis=1), n_cls - 1)
    c_last = jnp.max(jnp.where(counts > 0, jnp.arange(n_cls, dtype=i32), 0))
    c = jnp.where(active, c, c_last)
    within = v - cv_start[c]
    tile = jnp.where(active, first_t[c] + within // 2, n_tiles - 1)
    is_hi = jnp.where(active, within % 2, 1)
    e = jnp.where(is_hi == 1, cls_hi[c], cls_lo[c])
    lo = jnp.where(active, jnp.clip(starts[c] - tile * tm, 0, tm), 0)
    hi = jnp.where(active, jnp.clip(ends[c] - tile * tm, 0, tm), 0)
    prev_t = jnp.concatenate([jnp.full((1,), -1, i32), tile[:-1]])
    next_t = jnp.concatenate([tile[1:], jnp.full((1,), -1, i32)])
    is_final = v == cv_end[-1] - 1
    is_last = (next_t != tile) | is_final
    group = e // MOE_EXPERTS_PER_GROUP
    prev_g = jnp.concatenate([jnp.full((1,), -1, i32), group[:-1]])
    flag = jnp.where(active, (prev_t != tile) * FLAG_FIRST + is_last * FLAG_LAST + is_final * FLAG_FINAL
                     + (prev_g != group) * FLAG_NEW_GROUP + is_hi * FLAG_HI, 0)
    cast = lambda z: z.astype(i32)
    bcast = lambda w: jnp.broadcast_to(w[:, None], (t, LANES))
    return cast(tile), cast(e), cast(lo), cast(hi), cast(flag), cast(order), bcast(wlo), bcast(whi)


def _combine_kernel(x_ref, y_ref, g2_ref, lg_ref, lb_ref, o_ref, *, alpha):
    y = _load_token_tiles(y_ref, x_ref.shape[0])
    o_ref[...] = _layer_norm(alpha * x_ref[...] + g2_ref[...] * y, lg_ref[...], lb_ref[...])


def _combine(x1, y, row0, mod3, mod_row0, rows_per_mod, ln_g, ln_b, alpha, tm):
    t, d = x1.shape
    tiles_per_mod = rows_per_mod // tm
    t0 = row0 // tm
    return pl.pallas_call(
        functools.partial(_combine_kernel, alpha=alpha),
        grid=(t // tm,),
        in_specs=[pl.BlockSpec((tm, d), lambda i: (i, 0)),
                  pl.BlockSpec((tm * SUBLANES, LANES), lambda i: (t0 + i, 0)),
                  pl.BlockSpec((None, 1, d), lambda i: (mod_row0 + i // tiles_per_mod, 0, 5)),
                  _const_spec((1, d)), _const_spec((1, d))],
        out_specs=pl.BlockSpec((tm, d), lambda i: (i, 0)),
        out_shape=jax.ShapeDtypeStruct((t, d), F32),
        compiler_params=_cparams("arbitrary"),
        name="combine_ln2",
    )(x1, y, mod3, ln_g.reshape(1, d), ln_b.reshape(1, d))


def _pick_tile(n, pref):
    tm = min(pref, n)
    while n % tm:
        tm //= 2
    return tm


def kernel(x, c, ctx, c_ctx, ada_w, ada_b, w_in, gm_ln_g, gm_ln_b, gm_ws, gm_bs, hy_conv_w, hy_conv_b,
           hy_f_w1, hy_f_b1, hy_f_w2, hy_f_b2, hy_f_w3, hy_f_b3, hy_skip, da_lq1, da_lk1, da_lq2, da_lk2,
           da_norm_g, p_a, p_b, p_c, w_out, ln1_g, ln1_b, moe_wg, moe_bg, moe_we, moe_be,
           ex_w_gate, ex_w_up, ex_w_down, ln2_g, ln2_b):
    B, L, D = x.shape
    Lc = ctx.shape[1]
    depth = ada_w.shape[0]
    alpha = (2.0 * depth) ** 0.25
    T, Tc = B * L, B * Lc
    moe_tm = 512 if T >= 8192 else 64

    mp = -(-(B + 1) // 8) * 8
    c_all = jnp.zeros((mp, D), F32).at[:B].set(c).at[B].set(c_ctx)
    mod = _modulation(c_all, ada_w, ada_b)

    rope_tabs = _rope_tables(L // GRID_W)
    cm, sm = _dft_tables(L)
    cm_bf, sm_bf = cm.astype(BF16), sm.astype(BF16)
    cmc, smc = _dft_tables(Lc)
    cmc_bf, smc_bf = cmc.astype(BF16), smc.astype(BF16)

    seg_all = ((OFF_GM, OFF_HY, "gm"), (OFF_HY, OFF_Q, "hy"), (OFF_Q, OFF_K, "q"), (OFF_K, OFF_V, "k"),
               (OFF_V, OFF_GATE, "v"), (OFF_GATE, OFF_GATE + N_BRANCH * D, "gate"))
    seg_kv = ((0, DA_QK_W, "k"), (DA_QK_W, DA_QK_W + DA_V_W, "v"))

    tm_l = _pick_tile(L, 512)
    tm_c = _pick_tile(Lc, 256)
    tq_l = _pick_tile(L, 512)
    tq_c = _pick_tile(Lc, 256)
    tm_m = _pick_tile(L, 512)
    tm_mc = _pick_tile(Tc, 512)

    xs = x.reshape(T, D)
    xc = ctx.reshape(Tc, D)
    for l in range(depth):
        last = l == depth - 1
        lam_init = 0.8 - 0.6 * math.exp(-0.3 * l)
        mod3 = mod[l].reshape(mp, 1, 6 * D)
        w_l = w_in[l].astype(BF16)
        lparams = [a[l].reshape(1, DA_HEAD_DIM) for a in (da_lq1, da_lk1, da_lq2, da_lk2)]
        norm_g = da_norm_g[l].reshape(1, DA_V_DIM)
        lp = {
            "gm_ln_g": gm_ln_g[l].reshape(1, GM_DIM), "gm_ln_b": gm_ln_b[l].reshape(1, GM_DIM),
            "gm_ws": gm_ws[l].reshape(GM_GROUPS * GM_CHUNK, GM_CHUNK).astype(BF16),
            "gm_bs": jnp.repeat(jnp.transpose(gm_bs[l]), GM_DIM // GM_GROUPS, axis=1),
            "p_a": p_a[l].astype(BF16), "p_b": p_b[l].astype(BF16), "p_c": p_c[l].astype(BF16),
            "w_out": w_out[l].astype(BF16),
            "ln1_g": ln1_g[l].reshape(1, D), "ln1_b": ln1_b[l].reshape(1, D),
            "w_router": _split_bf16(jnp.zeros((D, LANES), F32).at[:, :MOE_GROUPS].set(moe_wg[l])
                                    .at[:, MOE_GROUPS:MOE_GROUPS + MOE_N_EXPERTS].set(moe_we[l])),
            "b_router": jnp.zeros((1, LANES), F32).at[0, :MOE_GROUPS].set(moe_bg[l])
                           .at[0, MOE_GROUPS:MOE_GROUPS + MOE_N_EXPERTS].set(moe_be[l]),
        }
        fw = (hy_f_w1[l], hy_f_b1[l], hy_f_w2[l], hy_f_b2[l], hy_f_w3[l], hy_f_b3[l])

        zgm, zhy, q, k, v, gate = _inproj(xs, mod3, 0, L, w_l, seg_all, rope_tabs, L, tm_l)
        if last:
            k_c, v_c = _inproj(xc, mod3, B, Tc, w_l[:, OFF_K:OFF_GATE], seg_kv, None, Lc, tm_c)
        else:
            zgm_c, zhy_c, q_c, k_c, v_c, gate_c = _inproj(xc, mod3, B, Tc, w_l, seg_all, None, Lc, tm_c)
        y_c = _attention(q, [(k, v, L), (k_c, v_c, Lc)], lparams, norm_g, lam_init, B, L, tq_l)
        kre, kim, nyq = _hyena_filter_spectrum(L, cm, sm, *fw)
        y_b = _hyena(zhy, B, L, hy_conv_w[l], hy_conv_b[l], cm_bf, sm_bf, kre, kim, nyq, hy_skip[l])
        t_all = T if last else T + Tc
        x1, h2, route = _merge(zgm, y_b, y_c, gate, xs, mod3, 0, L, lp, alpha, tm_m, T, 0, None)

        if not last:
            yc_c = _attention(q_c, [(k_c, v_c, Lc)], lparams, norm_g, lam_init, B, Lc, tq_c)
            kre_c, kim_c, nyq_c = _hyena_filter_spectrum(Lc, cmc, smc, *fw)
            yb_c = _hyena(zhy_c, B, Lc, hy_conv_w[l], hy_conv_b[l], cmc_bf, smc_bf, kre_c, kim_c, nyq_c,
                          hy_skip[l])
            x1c, h2c, route_c = _merge(zgm_c, yb_c, yc_c, gate_c, xc, mod3, B, Tc, lp, alpha, tm_mc, Tc, 0, None)
            h2 = jnp.concatenate([h2, h2c], axis=0)
            route = jnp.concatenate([route, route_c], axis=0)

        plan = _visit_plan(route, moe_tm)
        y = _experts(h2, plan, ex_w_gate[l].astype(BF16), ex_w_up[l].astype(BF16), ex_w_down[l].astype(BF16),
                     moe_tm)
        xs = _combine(x1, y, 0, mod3, 0, L, ln2_g[l], ln2_b[l], alpha, tm_c)
        if not last:
            xc = _combine(x1c, y, T, mod3, B, Tc, ln2_g[l], ln2_b[l], alpha, tm_c)
    return xs.reshape(B, L, D)
```

```python
import functools
import math

import numpy as np
import jax
import jax.numpy as jnp
from jax import lax
from jax.experimental import pallas as pl
from jax.experimental.pallas import tpu as pltpu

F32 = jnp.float32
BF16 = jnp.bfloat16
HIGHEST = lax.Precision.HIGHEST

GRID_W = 64
GM_DIM = 256
GM_GROUPS = 4
GM_CHUNK = 128
HY_DIM = 256
HY_EMB = 33
HY_BANDS = (HY_EMB - 1) // 2
HY_DECAY_FAST = 0.3
HY_DECAY_SLOW = 1.5
HY_DECAY_TARGET = 1e-2
HY_DECAY_SHIFT = 0.05
DA_HEADS = 4
DA_HEAD_DIM = 64
DA_V_DIM = 2 * DA_HEAD_DIM
DA_QK_W = DA_HEADS * 2 * DA_HEAD_DIM
DA_V_W = DA_HEADS * DA_V_DIM
ROPE_BASE = 10000.0
N_BRANCH = 3
OFF_GM = 0
OFF_HY = OFF_GM + 2 * GM_DIM
OFF_Q = OFF_HY + 3 * HY_DIM
OFF_K = OFF_Q + DA_QK_W
OFF_V = OFF_K + DA_QK_W
OFF_GATE = OFF_V + DA_V_W
MOE_GROUPS = 4
MOE_EXPERTS_PER_GROUP = 8
MOE_N_EXPERTS = MOE_GROUPS * MOE_EXPERTS_PER_GROUP
MOE_TOP_K = 2
LN_EPS = 1e-5
LANES = 128
VMEM_LIMIT = 56 * 1024 * 1024


def _cparams(*sem):
    return pltpu.CompilerParams(dimension_semantics=sem, vmem_limit_bytes=VMEM_LIMIT)


def _sigmoid(x):
    return 1.0 / (1.0 + jnp.exp(-x))


def _layer_norm(x, g, b):
    mu = jnp.mean(x, axis=-1, keepdims=True)
    xc = x - mu
    var = jnp.mean(xc * xc, axis=-1, keepdims=True)
    return xc * lax.rsqrt(var + LN_EPS) * g + b


def _gelu_tanh(x):
    return 0.5 * x * (1.0 + jnp.tanh(math.sqrt(2.0 / math.pi) * (x + 0.044715 * (x * x * x))))


def _const_spec(shape):
    nd = len(shape)
    return pl.BlockSpec(shape, lambda *_: (0,) * nd)


def _mod_kernel(c_ref, w_ref, b_ref, o_ref):
    c = c_ref[...]
    s = c * _sigmoid(c)
    o_ref[...] = jnp.dot(s, w_ref[...], precision=HIGHEST, preferred_element_type=F32) + b_ref[...]


def _modulation(c_all, ada_w, ada_b):
    depth, d, n = ada_w.shape
    mp = c_all.shape[0]
    tn = 512
    return pl.pallas_call(
        _mod_kernel,
        grid=(depth, n // tn),
        in_specs=[pl.BlockSpec((mp, d), lambda l, j: (0, 0)),
                  pl.BlockSpec((None, d, tn), lambda l, j: (l, 0, j)),
                  pl.BlockSpec((None, 1, tn), lambda l, j: (l, 0, j))],
        out_specs=pl.BlockSpec((None, mp, tn), lambda l, j: (l, 0, j)),
        out_shape=jax.ShapeDtypeStruct((depth, mp, n), F32),
        compiler_params=_cparams("arbitrary", "arbitrary"),
        name="adaln_mod",
    )(c_all, ada_w, ada_b.reshape(depth, 1, n))


def _rope_tables(rows):
    n_freq = DA_HEAD_DIM // 4
    row = jnp.broadcast_to(jnp.arange(rows)[:, None], (rows, GRID_W)).reshape(-1).astype(F32)
    col = jnp.broadcast_to(jnp.arange(GRID_W)[None, :], (rows, GRID_W)).reshape(-1).astype(F32)
    inv = ROPE_BASE ** (-jnp.arange(n_freq, dtype=F32) / n_freq)
    ang_r = row[:, None] * inv
    ang_c = col[:, None] * inv
    c64 = jnp.concatenate([jnp.cos(ang_r), jnp.cos(ang_r), jnp.cos(ang_c), jnp.cos(ang_c)], axis=-1)
    s64 = jnp.concatenate([-jnp.sin(ang_r), jnp.sin(ang_r), -jnp.sin(ang_c), jnp.sin(ang_c)], axis=-1)
    return jnp.tile(c64, (1, LANES // DA_HEAD_DIM)), jnp.tile(s64, (1, LANES // DA_HEAD_DIM))


def _rope_block(xb, cos, sin):
    lane = lax.broadcasted_iota(jnp.int32, xb.shape, 1)
    n_freq = DA_HEAD_DIM // 4
    first_half = (lane % (2 * n_freq)) < n_freq
    partner = jnp.where(first_half, pltpu.roll(xb, LANES - n_freq, 1), pltpu.roll(xb, n_freq, 1))
    return xb * cos + partner * sin


def _inproj_kernel(*refs, segs, use_rope, n_chunk):
    if use_rope:
        x_ref, sh_ref, sc_ref, w_ref, cos_ref, sin_ref = refs[:6]
        out_refs = refs[6:]
    else:
        x_ref, sh_ref, sc_ref, w_ref = refs[:4]
        out_refs = refs[4:]
    h = (x_ref[...] * (1.0 + sc_ref[...]) + sh_ref[...]).astype(BF16)
    for (a, b, kind), o_ref in zip(segs, out_refs):
        for c0 in range(a, b, n_chunk):
            c1 = min(c0 + n_chunk, b)
            acc = jnp.dot(h, w_ref[:, c0:c1], preferred_element_type=F32)
            if kind == "q":
                acc = acc * (DA_HEAD_DIM ** -0.5 * math.log2(math.e))
            if use_rope and kind in ("q", "k"):
                cos = cos_ref[...]
                sin = sin_ref[...]
                for j in range((c1 - c0) // LANES):
                    blk = _rope_block(acc[:, j * LANES:(j + 1) * LANES], cos, sin)
                    o_ref[:, c0 - a + j * LANES:c0 - a + (j + 1) * LANES] = blk.astype(o_ref.dtype)
            else:
                o_ref[:, c0 - a:c1 - a] = acc.astype(o_ref.dtype)


def _inproj(x2d, mod3, mod_row0, rows_per_mod, w, segs, rope_tabs, seq_len, tm):
    t, d = x2d.shape
    n = w.shape[1]
    use_rope = rope_tabs is not None
    tiles_per_mod = rows_per_mod // tm
    tiles_per_seq = seq_len // tm

    def mod_map(piece):
        return lambda i: (mod_row0 + i // tiles_per_mod, 0, piece)

    in_specs = [pl.BlockSpec((tm, d), lambda i: (i, 0)),
                pl.BlockSpec((None, 1, d), mod_map(0)),
                pl.BlockSpec((None, 1, d), mod_map(1)),
                pl.BlockSpec((d, n), lambda i: (0, 0), pipeline_mode=pl.Buffered(1))]
    args = [x2d, mod3, mod3, w]
    if use_rope:
        in_specs += [pl.BlockSpec((tm, LANES), lambda i: (i % tiles_per_seq, 0))] * 2
        args += list(rope_tabs)
    out_specs = [pl.BlockSpec((tm, b - a), lambda i: (i, 0)) for a, b, _ in segs]
    out_shape = [jax.ShapeDtypeStruct((t, b - a), BF16) for a, b, _ in segs]
    return pl.pallas_call(
        functools.partial(_inproj_kernel, segs=segs, use_rope=use_rope, n_chunk=512),
        grid=(t // tm,),
        in_specs=in_specs, out_specs=out_specs, out_shape=out_shape,
        compiler_params=_cparams("arbitrary"),
        name="inproj",
    )(*args)


ATTN_KEY_CHUNK = 256


def _attn_kernel(*refs, src_lens, lam_init):
    n_src = len(src_lens)
    lq1, lk1, lq2, lk2, g_ref, q_ref = refs[:6]
    kv_refs = refs[6:6 + 2 * n_src]
    o_ref, s_scr = refs[6 + 2 * n_src:]
    lam = (jnp.exp(jnp.sum(lq1[...] * lk1[...], axis=-1, keepdims=True))
           - jnp.exp(jnp.sum(lq2[...] * lk2[...], axis=-1, keepdims=True)) + lam_init)
    tq = q_ref.shape[0]
    lane = lax.broadcasted_iota(jnp.int32, (tq, LANES), 1)
    dn = (((1,), (1,)), ((), ()))
    chunks = []
    off = 0
    for j, n in enumerate(src_lens):
        kc = min(ATTN_KEY_CHUNK, n)
        for st in range(0, n, kc):
            chunks.append((j, st, kc, off))
            off += kc
    for h in range(DA_HEADS):
        cols = slice(h * LANES, (h + 1) * LANES)
        qh = q_ref[:, cols]
        zero = jnp.zeros_like(qh)
        om = []
        for m in range(2):
            qm = jnp.where(lane < DA_HEAD_DIM if m == 0 else lane >= DA_HEAD_DIM, qh, zero)
            mlane = None
            for j, st, kc, off in chunks:
                s_c = lax.dot_general(qm, kv_refs[2 * j][st:st + kc, cols], dn, preferred_element_type=F32)
                s_scr[:, off:off + kc] = s_c
                for b in range(kc // LANES):
                    blk = s_c[:, b * LANES:(b + 1) * LANES]
                    mlane = blk if mlane is None else jnp.maximum(mlane, blk)
            mx = jnp.max(mlane, axis=-1, keepdims=True)
            acc = None
            for j, st, kc, off in chunks:
                p = jnp.exp2(s_scr[:, off:off + kc] - mx).astype(BF16)
                v_aug = jnp.concatenate([kv_refs[2 * j + 1][st:st + kc, cols], jnp.ones((kc, LANES), BF16)],
                                        axis=1)
                d = jnp.dot(p, v_aug, preferred_element_type=F32)
                acc = d if acc is None else acc + d
            om.append(acc[:, :LANES] * (1.0 / acc[:, LANES:LANES + 1]))
        o = om[0] - lam * om[1]
        ms = jnp.mean(o * o, axis=-1, keepdims=True)
        o = o * lax.rsqrt(ms + LN_EPS) * g_ref[...] * (1.0 - lam_init)
        o_ref[:, cols] = o.astype(o_ref.dtype)


def _attention(q, kvs, lparams, norm_g, lam_init, nb, lq, tq):
    t = q.shape[0]
    qt = lq // tq
    in_specs = [_const_spec((1, DA_HEAD_DIM))] * 4 + [_const_spec((1, DA_V_DIM))]
    in_specs.append(pl.BlockSpec((tq, DA_QK_W), lambda b, i: (b * qt + i, 0)))
    args = list(lparams) + [norm_g, q]
    for k, v, lk in kvs:
        in_specs += [pl.BlockSpec((lk, DA_QK_W), lambda b, i: (b, 0)),
                     pl.BlockSpec((lk, DA_V_W), lambda b, i: (b, 0))]
        args += [k, v]
    src_lens = tuple(lk for _, _, lk in kvs)
    return pl.pallas_call(
        functools.partial(_attn_kernel, src_lens=src_lens, lam_init=lam_init),
        grid=(nb, qt),
        in_specs=in_specs,
        out_specs=pl.BlockSpec((tq, DA_V_W), lambda b, i: (b * qt + i, 0)),
        out_shape=jax.ShapeDtypeStruct((t, DA_V_W), BF16),
        scratch_shapes=[pltpu.VMEM((tq, sum(src_lens)), F32)],
        compiler_params=_cparams("arbitrary", "arbitrary"),
        name="diff_attn",
    )(*args)


DFT_SPLIT = 64


def _dft_tables(L):
    n = jnp.arange(L, dtype=jnp.int32)[None, :]
    k1 = jnp.arange(L // DFT_SPLIT, dtype=jnp.int32)[:, None] * DFT_SPLIT
    k0 = jnp.arange(DFT_SPLIT, dtype=jnp.int32)[:, None]
    ang_a = ((k1 * n) % (2 * L)).astype(F32) * (math.pi / L)
    ang_b = ((k0 * n) % (2 * L)).astype(F32) * (math.pi / L)
    ca, sa = jnp.cos(ang_a)[:, None, :], jnp.sin(ang_a)[:, None, :]
    cb, sb = jnp.cos(ang_b)[None, :, :], jnp.sin(ang_b)[None, :, :]
    return (ca * cb - sa * sb).reshape(L, L), (sa * cb + ca * sb).reshape(L, L)


def _filter_consts(L):
    t = jnp.linspace(0.0, 1.0, L, dtype=F32)[:, None]
    w = 2.0 * math.pi * jnp.arange(L, dtype=F32)[:, None] / L
    f = jnp.linspace(1e-4, HY_BANDS - 1, HY_BANDS, dtype=F32)[None, :]
    emb = jnp.concatenate([t, jnp.cos(f * w), -jnp.sin(f * w)], axis=-1)
    max_decay = math.log(HY_DECAY_TARGET) / HY_DECAY_FAST
    min_decay = math.log(HY_DECAY_TARGET) / HY_DECAY_SLOW
    deltas = jnp.abs(jnp.linspace(min_decay, max_decay, HY_DIM, dtype=F32))
    window = jnp.exp(-t * deltas[None, :]) + HY_DECAY_SHIFT
    return emb, window


def _filter_kernel(emb_ref, win_ref, w1, b1, w2, b2, w3, b3, hs_ref, hd_ref, nyq_ref):
    h = jnp.sin(jnp.dot(emb_ref[...], w1[...], precision=HIGHEST, preferred_element_type=F32) + b1[...])
    h = jnp.sin(jnp.dot(h, w2[...], precision=HIGHEST, preferred_element_type=F32) + b2[...])
    h = jnp.dot(h, w3[...], precision=HIGHEST, preferred_element_type=F32) + b3[...]
    win = win_ref[...]
    hf = h[:, :HY_DIM] * win
    hb = h[:, HY_DIM:] * win
    row = lax.broadcasted_iota(jnp.int32, hf.shape, 0)
    hb = jnp.where(row == 0, 0.0, hb)
    alt = jnp.where(row % 2 == 0, 1.0, -1.0)
    hs_ref[...] = hf + hb
    hd_ref[...] = hf - hb
    nyq_ref[...] = jnp.sum((hf + hb) * alt, axis=0, keepdims=True)


def _spectrum_kernel(c_ref, s_ref, hs_ref, hd_ref, kre_ref, kim_ref, *, n_fft):
    i = pl.program_id(0)
    tk = c_ref.shape[0]
    kidx = i * tk + lax.broadcasted_iota(jnp.int32, (tk, 1), 0)
    scale = jnp.where(kidx == 0, 1.0 / n_fft, 2.0 / n_fft)
    kre = jnp.dot(c_ref[...], hs_ref[...], precision=HIGHEST, preferred_element_type=F32)
    kim = -jnp.dot(s_ref[...], hd_ref[...], precision=HIGHEST, preferred_element_type=F32)
    kre_ref[...] = kre * scale
    kim_ref[...] = kim * scale


def _hyena_filter_spectrum(L, cmat, smat, w1, b1, w2, b2, w3, b3):
    emb, window = _filter_consts(L)
    full = lambda a: _const_spec(a.shape)
    ins = [emb, window, w1, b1.reshape(1, -1), w2, b2.reshape(1, -1), w3, b3.reshape(1, -1)]
    hs, hd, nyq = pl.pallas_call(
        _filter_kernel,
        grid=(1,),
        in_specs=[full(a) for a in ins],
        out_specs=[_const_spec((L, HY_DIM)), _const_spec((L, HY_DIM)), _const_spec((1, HY_DIM))],
        out_shape=[jax.ShapeDtypeStruct((L, HY_DIM), F32), jax.ShapeDtypeStruct((L, HY_DIM), F32),
                   jax.ShapeDtypeStruct((1, HY_DIM), F32)],
        compiler_params=_cparams("arbitrary"),
        name="hyena_filter",
    )(*ins)
    tk = min(256, L)
    kre, kim = pl.pallas_call(
        functools.partial(_spectrum_kernel, n_fft=2 * L),
        grid=(L // tk,),
        in_specs=[pl.BlockSpec((tk, L), lambda i: (i, 0)), pl.BlockSpec((tk, L), lambda i: (i, 0)),
                  _const_spec((L, HY_DIM)), _const_spec((L, HY_DIM))],
        out_specs=[pl.BlockSpec((tk, HY_DIM), lambda i: (i, 0))] * 2,
        out_shape=[jax.ShapeDtypeStruct((L, HY_DIM), F32)] * 2,
        compiler_params=_cparams("arbitrary"),
        name="hyena_spectrum",
    )(cmat, smat, hs, hd)
    return kre, kim, nyq * (1.0 / (2 * L))


HY_ROW_BLOCK = 1024


def _hyena_kernel(z_ref, cw_ref, cb_ref, c_ref, s_ref, kre_ref, kim_ref, nyq_ref, skip_ref, o_ref,
                  u_ref, x0_ref, p_ref, q_ref):
    L = z_ref.shape[0]
    row = lax.broadcasted_iota(jnp.int32, (L, HY_DIM), 0)

    def conv(j):
        cols = slice(j * HY_DIM, (j + 1) * HY_DIM)
        z = z_ref[:, cols].astype(F32)
        zprev = jnp.where(row == 0, 0.0, pltpu.roll(z, 1, 0))
        znext = jnp.where(row == L - 1, 0.0, pltpu.roll(z, L - 1, 0))
        return zprev * cw_ref[0:1, cols] + z * cw_ref[1:2, cols] + znext * cw_ref[2:3, cols] + cb_ref[:, cols]

    u = conv(2) * conv(1)
    ub = u.astype(BF16)
    u_ref[...] = u
    alt = jnp.where(row % 2 == 0, 1.0, -1.0)
    nyq_term = jnp.sum(u * alt, axis=0, keepdims=True) * nyq_ref[...]
    x0_ref[...] = conv(0)
    blk = min(HY_ROW_BLOCK, L)
    for r in range(0, L, blk):
        rows = slice(r, r + blk)
        a = jnp.dot(c_ref[rows, :], ub, preferred_element_type=F32)
        b = jnp.dot(s_ref[rows, :], ub, preferred_element_type=F32)
        kre = kre_ref[rows, :]
        kim = kim_ref[rows, :]
        p_ref[rows, :] = (a * kre + b * kim).astype(BF16)
        q_ref[rows, :] = (b * kre - a * kim).astype(BF16)
    for r in range(0, L, blk):
        rows = slice(r, r + blk)
        y = (jnp.dot(c_ref[rows, :], p_ref[...], preferred_element_type=F32)
             + jnp.dot(s_ref[rows, :], q_ref[...], preferred_element_type=F32))
        ub_rows = u_ref[rows, :]
        row_b = lax.broadcasted_iota(jnp.int32, (blk, HY_DIM), 0)
        y = y + jnp.where(row_b % 2 == 0, nyq_term, -nyq_term) + ub_rows * skip_ref[...]
        o_ref[rows, :] = (y * x0_ref[rows, :]).astype(o_ref.dtype)


def _hyena(zhy, nb, L, conv_w, conv_b, cmat_bf, smat_bf, kre, kim, nyq, skip):
    t = zhy.shape[0]
    return pl.pallas_call(
        _hyena_kernel,
        grid=(nb,),
        in_specs=[pl.BlockSpec((L, 3 * HY_DIM), lambda b: (b, 0)),
                  _const_spec((3, 3 * HY_DIM)), _const_spec((1, 3 * HY_DIM)),
                  pl.BlockSpec((L, L), lambda b: (0, 0), pipeline_mode=pl.Buffered(1)),
                  pl.BlockSpec((L, L), lambda b: (0, 0), pipeline_mode=pl.Buffered(1)),
                  _const_spec((L, HY_DIM)), _const_spec((L, HY_DIM)),
                  _const_spec((1, HY_DIM)), _const_spec((1, HY_DIM))],
        out_specs=pl.BlockSpec((L, HY_DIM), lambda b: (b, 0)),
        out_shape=jax.ShapeDtypeStruct((t, HY_DIM), BF16),
        scratch_shapes=[pltpu.VMEM((L, HY_DIM), F32), pltpu.VMEM((L, HY_DIM), F32),
                        pltpu.VMEM((L, HY_DIM), BF16), pltpu.VMEM((L, HY_DIM), BF16)],
        compiler_params=_cparams("arbitrary"),
        name="hyena_conv",
    )(zhy, conv_w, conv_b.reshape(1, -1), cmat_bf, smat_bf, kre, kim, nyq, skip.reshape(1, -1))


SUBLANES = 8


def _split_bf16(w):
    hi = w.astype(BF16)
    return jnp.stack([hi, (w - hi.astype(F32)).astype(BF16)])


def _store_token_tiles(ref, val):
    n = val.shape[0]
    for j in range(val.shape[1] // LANES):
        ref[pl.ds(j, n, stride=SUBLANES), :] = val[:, j * LANES:(j + 1) * LANES]


def _load_token_tiles(ref, n):
    return jnp.concatenate([ref[pl.ds(j, n, stride=SUBLANES), :] for j in range(SUBLANES)], axis=1)


def _merge_kernel(*refs, alpha, n_alias):
    (zgm_ref, yb_ref, yc_ref, gate_ref, x_ref, g1_ref, sh2_ref, sc2_ref, lng_ref, lnb_ref, ws_ref, bs_ref,
     pa_ref, pb_ref, pc_ref, wo_ref, l1g_ref, l1b_ref, wr_ref, br_ref) = refs[:20]
    x1_ref, h2_ref, rt_ref = refs[20 + n_alias:]
    tm = x_ref.shape[0]
    d = x_ref.shape[1]
    gm = _gelu_tanh(zgm_ref[...].astype(F32))
    u = gm[:, :GM_DIM]
    v = _layer_norm(gm[:, GM_DIM:], lng_ref[...], lnb_ref[...]).astype(BF16)
    lane_group = lax.broadcasted_iota(jnp.int32, (GM_CHUNK, GM_DIM), 1) // (GM_DIM // GM_GROUPS)
    ya = []
    for cidx in range(tm // GM_CHUNK):
        rows = slice(cidx * GM_CHUNK, (cidx + 1) * GM_CHUNK)
        r = jnp.dot(ws_ref[...], v[rows], preferred_element_type=F32)
        vv = bs_ref[...]
        for g in range(GM_GROUPS):
            vv = vv + jnp.where(lane_group == g, r[g * GM_CHUNK:(g + 1) * GM_CHUNK], 0.0)
        ya.append(u[rows] * vv)
    ya = jnp.concatenate(ya, axis=0) if len(ya) > 1 else ya[0]
    ma = jnp.dot(ya.astype(BF16), pa_ref[...], preferred_element_type=F32)
    mb = jnp.dot(yb_ref[...], pb_ref[...], preferred_element_type=F32)
    mc = jnp.dot(yc_ref[...], pc_ref[...], preferred_element_type=F32)
    merged = (_sigmoid(gate_ref[:, 0:d].astype(F32)) * ma
              + _sigmoid(gate_ref[:, d:2 * d].astype(F32)) * mb
              + _sigmoid(gate_ref[:, 2 * d:3 * d].astype(F32)) * mc)
    out = jnp.dot(merged.astype(BF16), wo_ref[...], preferred_element_type=F32)
    x1 = _layer_norm(alpha * x_ref[...] + g1_ref[...] * out, l1g_ref[...], l1b_ref[...])
    x1_ref[...] = x1
    h2 = x1 * (1.0 + sc2_ref[...]) + sh2_ref[...]
    _store_token_tiles(h2_ref, h2)
    h2_hi = h2.astype(BF16)
    h2_lo = (h2 - h2_hi.astype(F32)).astype(BF16)
    lg = (jnp.dot(h2_hi, wr_ref[0], preferred_element_type=F32)
          + jnp.dot(h2_hi, wr_ref[1], preferred_element_type=F32)
          + jnp.dot(h2_lo, wr_ref[0], preferred_element_type=F32) + br_ref[...])
    rt_ref[...] = _route(lg)


ROUTE_E0, ROUTE_E1, ROUTE_W0, ROUTE_W1 = 0, 1, 2, 3


def _route(lg):
    neg = jnp.float32(-3.0e38)
    lane_i = lax.broadcasted_iota(jnp.int32, lg.shape, 1)
    lane = lane_i.astype(F32)
    big = jnp.float32(LANES)
    is_g = lane_i < MOE_GROUPS
    gl = jnp.where(is_g, lg, neg)
    gmax = jnp.max(gl, axis=-1, keepdims=True)
    g_idx = jnp.min(jnp.where(gl == gmax, lane, big), axis=-1, keepdims=True)
    g_prob = 1.0 / jnp.sum(jnp.where(is_g, jnp.exp(gl - gmax), 0.0), axis=-1, keepdims=True)
    e_lo = MOE_GROUPS + MOE_EXPERTS_PER_GROUP * g_idx
    el = jnp.where(lane >= e_lo, jnp.where(lane < e_lo + MOE_EXPERTS_PER_GROUP, lg, neg), neg)
    v1 = jnp.max(el, axis=-1, keepdims=True)
    i1 = jnp.min(jnp.where(el == v1, lane, big), axis=-1, keepdims=True)
    el2 = jnp.where(lane == i1, neg, el)
    v2 = jnp.max(el2, axis=-1, keepdims=True)
    i2 = jnp.min(jnp.where(el2 == v2, lane, big), axis=-1, keepdims=True)
    e21 = jnp.exp(v2 - v1)
    w1 = g_prob / (1.0 + e21)
    w2 = w1 * e21
    swap = i2 < i1
    rec = jnp.where(lane_i == ROUTE_E0, jnp.minimum(i1, i2) - MOE_GROUPS, 0.0)
    rec = jnp.where(lane_i == ROUTE_E1, jnp.maximum(i1, i2) - MOE_GROUPS, rec)
    rec = jnp.where(lane_i == ROUTE_W0, jnp.where(swap, w2, w1), rec)
    return jnp.where(lane_i == ROUTE_W1, jnp.where(swap, w1, w2), rec)


def _merge(zgm, yb, yc, gate, x2d, mod3, mod_row0, rows_per_mod, lp, alpha, tm, t_all, row0, prev):
    t, d = x2d.shape
    tiles_per_mod = rows_per_mod // tm
    off = row0 // tm

    def mod_map(piece):
        return lambda i: (mod_row0 + i // tiles_per_mod, 0, piece)

    row = lambda w: pl.BlockSpec((tm, w), lambda i: (i, 0))
    row_off = lambda w: pl.BlockSpec((tm, w), lambda i: (off + i, 0))
    consts = [lp["gm_ln_g"], lp["gm_ln_b"], lp["gm_ws"], lp["gm_bs"], lp["p_a"], lp["p_b"], lp["p_c"],
              lp["w_out"], lp["ln1_g"], lp["ln1_b"], lp["w_router"], lp["b_router"]]
    in_specs = [row(2 * GM_DIM), row(HY_DIM), row(DA_V_W), row(N_BRANCH * d), row(d),
                pl.BlockSpec((None, 1, d), mod_map(2)), pl.BlockSpec((None, 1, d), mod_map(3)),
                pl.BlockSpec((None, 1, d), mod_map(4))] + [_const_spec(a.shape) for a in consts]
    args = [zgm, yb, yc, gate, x2d, mod3, mod3, mod3, *consts]
    aliases = {}
    if prev is not None:
        aliases = {len(args): 1, len(args) + 1: 2}
        in_specs += [pl.BlockSpec(memory_space=pl.ANY)] * 2
        args += list(prev)
    return pl.pallas_call(
        functools.partial(_merge_kernel, alpha=alpha, n_alias=len(aliases)),
        grid=(t // tm,),
        in_specs=in_specs,
        out_specs=[row(d), pl.BlockSpec((tm * SUBLANES, LANES), lambda i: (off + i, 0)), row_off(LANES)],
        out_shape=[jax.ShapeDtypeStruct((t, d), F32), jax.ShapeDtypeStruct((t_all * SUBLANES, LANES), F32),
                   jax.ShapeDtypeStruct((t_all, LANES), F32)],
        input_output_aliases=aliases,
        compiler_params=_cparams("arbitrary"),
        name="merge_ln1",
    )(*args)


PAIR_CLASSES = tuple((MOE_EXPERTS_PER_GROUP * g + a, MOE_EXPERTS_PER_GROUP * g + b)
                     for g in range(MOE_GROUPS)
                     for a in range(MOE_EXPERTS_PER_GROUP) for b in range(a + 1, MOE_EXPERTS_PER_GROUP))
FLAG_FIRST, FLAG_LAST, FLAG_FINAL, FLAG_NEW_GROUP, FLAG_HI = 1, 2, 4, 8, 16


def _tile_copy(src, src_row, dst, dst_row, sem):
    s0 = pl.multiple_of(src_row * SUBLANES, SUBLANES)
    d0 = pl.multiple_of(dst_row * SUBLANES, SUBLANES)
    return pltpu.make_async_copy(src.at[pl.ds(s0, SUBLANES)], dst.at[pl.ds(d0, SUBLANES)], sem)


def _expert_kernel(vt_ref, ve_ref, vlo_ref, vhi_ref, vflag_ref, src_ref, nsrc_ref, dst_ref, pdst_ref,
                   wlo_ref, whi_ref, h2_hbm, wg_hbm, wu_hbm, wd_hbm, y_hbm,
                   xbuf, acc, ybuf, wg_grp, wu_grp, wd_grp, wg_stage, wu_stage, wd_stage, gsem, ssem, wsem,
                   *, n_tiles, layer):
    v = pl.program_id(0)
    tile, expert, lo, hi, flag = vt_ref[v], ve_ref[v], vlo_ref[v], vhi_ref[v], vflag_ref[v]
    tm = acc.shape[0]
    slot = tile % 2
    first = (flag & FLAG_FIRST) != 0
    group0 = pl.multiple_of((expert // MOE_EXPERTS_PER_GROUP) * MOE_EXPERTS_PER_GROUP, MOE_EXPERTS_PER_GROUP)
    e_in_group = expert - group0

    @pl.when((flag & FLAG_NEW_GROUP) != 0)
    def _():
        streams = ((wg_hbm, wg_stage, wg_grp), (wu_hbm, wu_stage, wu_grp), (wd_hbm, wd_stage, wd_grp))
        for k in range(MOE_EXPERTS_PER_GROUP):
            copies = [pltpu.make_async_copy(w_hbm.at[layer, group0 + k], stage, wsem.at[i])
                      for i, (w_hbm, stage, _) in enumerate(streams)]
            for cp in copies:
                cp.start()
            for cp, (_, stage, w_grp) in zip(copies, streams):
                cp.wait()
                w_grp[k] = stage[...].astype(BF16)

    def issue_gather(idx_ref, to_slot):
        def body(i, carry):
            for j in range(SUBLANES):
                r = i * SUBLANES + j
                _tile_copy(h2_hbm, idx_ref[0, r], xbuf.at[to_slot], r, gsem.at[to_slot]).start(priority=j % 2)
            return carry

        lax.fori_loop(0, tm // SUBLANES, body, 0)

    def issue_scatter(idx_ref):
        def body(i, carry):
            for j in range(SUBLANES):
                r = i * SUBLANES + j
                _tile_copy(ybuf, r, y_hbm, idx_ref[0, r], ssem).start(priority=j % 2)
            return carry

        lax.fori_loop(0, tm // SUBLANES, body, 0)

    has_next = tile + 1 < n_tiles
    prefetch = first & (tile > 0) & has_next

    @pl.when(first)
    def _():
        @pl.when(tile == 0)
        def _():
            issue_gather(src_ref, slot)

        pltpu.make_async_copy(h2_hbm.at[pl.ds(0, tm * SUBLANES)], xbuf.at[slot], gsem.at[slot]).wait()

        @pl.when(jnp.logical_not(prefetch))
        def _():
            @pl.when(has_next)
            def _():
                issue_gather(nsrc_ref, 1 - slot)

            @pl.when(tile > 0)
            def _():
                issue_scatter(pdst_ref)

    def compute(with_prefetch):
        xb = _load_token_tiles(xbuf.at[slot], tm).astype(BF16)
        if with_prefetch:
            for r in range(tm):
                _tile_copy(h2_hbm, nsrc_ref[0, r], xbuf.at[1 - slot], r, gsem.at[1 - slot]).start(priority=r % 2)
                _tile_copy(ybuf, r, y_hbm, pdst_ref[0, r], ssem).start(priority=r % 2)
        g = jnp.dot(xb, wg_grp[e_in_group], preferred_element_type=F32)
        u = jnp.dot(xb, wu_grp[e_in_group], preferred_element_type=F32)
        hmid = (g * _sigmoid(g) * u).astype(BF16)
        y = jnp.dot(hmid, wd_grp[e_in_group], preferred_element_type=F32)
        w_b = jnp.where((flag & FLAG_HI) != 0, whi_ref[...], wlo_ref[...])
        row = lax.broadcasted_iota(jnp.int32, (tm, 1), 0)
        y = jnp.where((row >= lo) & (row < hi), y * jnp.concatenate([w_b] * SUBLANES, axis=1), 0.0)
        if with_prefetch:
            acc[...] = y
            return

        @pl.when(first)
        def _():
            acc[...] = y

        @pl.when(jnp.logical_not(first))
        def _():
            acc[...] += y

    @pl.when(prefetch)
    def _():
        compute(True)

    @pl.when(jnp.logical_not(prefetch) & (hi > lo))
    def _():
        compute(False)

    @pl.when((flag & FLAG_LAST) != 0)
    def _():
        whole = pltpu.make_async_copy(ybuf, y_hbm.at[pl.ds(0, tm * SUBLANES)], ssem)

        @pl.when(tile > 0)
        def _():
            whole.wait()

        _store_token_tiles(ybuf, acc[...])

        @pl.when((flag & FLAG_FINAL) != 0)
        def _():
            issue_scatter(dst_ref)
            whole.wait()


def _experts(h2, plan, w_gate, w_up, w_down, layer, tm):
    t = h2.shape[0] // SUBLANES
    vt, ve, vlo, vhi, vflag, order, wlo_b, whi_b = plan
    n_vis = vt.shape[0]
    n_tiles = t // tm
    d, hid = w_gate.shape[-2:]
    grp = MOE_EXPERTS_PER_GROUP
    assert d == SUBLANES * LANES
    idx_spec = lambda nxt: pl.BlockSpec(
        (None, 1, tm), lambda v, vt, *_: (jnp.clip(vt[v] + nxt, 0, n_tiles - 1), 0, 0), memory_space=pltpu.SMEM)
    w_tile = pl.BlockSpec((tm, LANES), lambda v, vt, *_: (vt[v], 0))
    any_spec = pl.BlockSpec(memory_space=pl.ANY)
    grid_spec = pltpu.PrefetchScalarGridSpec(
        num_scalar_prefetch=5,
        grid=(n_vis,),
        in_specs=[idx_spec(0), idx_spec(1), idx_spec(0), idx_spec(-1), w_tile, w_tile,
                  any_spec, any_spec, any_spec, any_spec],
        out_specs=any_spec,
        scratch_shapes=[pltpu.VMEM((2, tm * SUBLANES, LANES), F32), pltpu.VMEM((tm, d), F32),
                        pltpu.VMEM((tm * SUBLANES, LANES), F32),
                        pltpu.VMEM((grp, d, hid), BF16), pltpu.VMEM((grp, d, hid), BF16),
                        pltpu.VMEM((grp, hid, d), BF16),
                        pltpu.VMEM((d, hid), F32), pltpu.VMEM((d, hid), F32), pltpu.VMEM((hid, d), F32),
                        pltpu.SemaphoreType.DMA((2,)), pltpu.SemaphoreType.DMA(()), pltpu.SemaphoreType.DMA((3,))],
    )
    idx3 = order.reshape(n_tiles, 1, tm)
    return pl.pallas_call(
        functools.partial(_expert_kernel, n_tiles=n_tiles, layer=layer),
        grid_spec=grid_spec,
        out_shape=jax.ShapeDtypeStruct((t * SUBLANES, LANES), F32),
        compiler_params=_cparams("arbitrary"),
        name="moe_experts",
    )(vt, ve, vlo, vhi, vflag, idx3, idx3, idx3, idx3, wlo_b, whi_b, h2, w_gate, w_up, w_down)


def _visit_plan(route, tm):
    t = route.shape[0]
    i32 = jnp.int32
    n_cls = len(PAIR_CLASSES)
    cls_lo = jnp.asarray(np.array([p[0] for p in PAIR_CLASSES], np.int32))
    cls_hi = jnp.asarray(np.array([p[1] for p in PAIR_CLASSES], np.int32))
    cls_key = cls_lo * MOE_N_EXPERTS + cls_hi
    key = route[:, ROUTE_E0].astype(i32) * MOE_N_EXPERTS + route[:, ROUTE_E1].astype(i32)
    idx_bits = max(1, (t - 1).bit_length())
    assert (MOE_N_EXPERTS * MOE_N_EXPERTS) << idx_bits <= 2 ** 31
    packed, wlo, whi = lax.sort((key * (1 << idx_bits) + lax.iota(i32, t), route[:, ROUTE_W0], route[:, ROUTE_W1]),
                                num_keys=1)
    order = packed & ((1 << idx_bits) - 1)
    counts = jnp.sum((key[:, None] == cls_key[None, :]).astype(i32), axis=0)
    n_tiles = t // tm

    def segment_visits(seg_counts, seg_expert):
        n_seg = seg_counts.shape[0]
        ends = jnp.cumsum(seg_counts)
        starts = ends - seg_counts
        first_t = starts // tm
        nvis = jnp.where(seg_counts > 0, jnp.maximum(ends - 1, 0) // tm - first_t + 1, 0)
        cv_end = jnp.cumsum(nvis)
        cv_start = cv_end - nvis
        v = jnp.arange(n_tiles + n_seg, dtype=i32)
        active = v < cv_end[-1]
        s = jnp.minimum(jnp.sum((cv_end[None, :] <= v[:, None]).astype(i32), axis=1), n_seg - 1)
        tile = first_t[s] + v - cv_start[s]
        lo = jnp.clip(starts[s] - tile * tm, 0, tm)
        hi = jnp.clip(ends[s] - tile * tm, 0, tm)
        return tile, seg_expert[s], lo, hi, active

    run_id = np.cumsum([0] + [int(a[0] != b[0]) for a, b in zip(PAIR_CLASSES[:-1], PAIR_CLASSES[1:])])
    run_lo = jnp.asarray(np.array([PAIR_CLASSES[list(run_id).index(r)][0] for r in range(run_id[-1] + 1)], np.int32))
    in_run = jnp.asarray(run_id[None, :] == np.arange(run_id[-1] + 1)[:, None])
    run_counts = jnp.sum(jnp.where(in_run, counts[None, :], 0), axis=1)
    parts = [segment_visits(run_counts, run_lo) + (0,), segment_visits(counts, cls_hi) + (1,)]
    tile = jnp.concatenate([p[0] for p in parts])
    e = jnp.concatenate([p[1] for p in parts])
    lo = jnp.concatenate([p[2] for p in parts])
    hi = jnp.concatenate([p[3] for p in parts])
    active = jnp.concatenate([p[4] for p in parts])
    is_hi = jnp.concatenate([jnp.full(p[0].shape, p[5], i32) for p in parts])
    assert tm * 2 + 2 <= 2048
    order_key = jnp.where(active, tile * 2048 + lo * 2 + is_hi, jnp.iinfo(jnp.int32).max)
    _, tile, e, lo, hi, is_hi, active = lax.sort((order_key, tile, e, lo, hi, is_hi, active.astype(i32)), num_keys=1)
    active = active == 1
    n_active = jnp.sum(active.astype(i32))
    v = jnp.arange(tile.shape[0], dtype=i32)
    e_last = jnp.max(jnp.where(active, v, -1))
    e = jnp.where(active, e, e[e_last])
    tile = jnp.where(active, tile, n_tiles - 1)
    lo = jnp.where(active, lo, 0)
    hi = jnp.where(active, hi, 0)
    prev_t = jnp.concatenate([jnp.full((1,), -1, i32), tile[:-1]])
    next_t = jnp.concatenate([tile[1:], jnp.full((1,), -1, i32)])
    is_final = v == n_active - 1
    is_last = (next_t != tile) | is_final
    group = e // MOE_EXPERTS_PER_GROUP
    prev_g = jnp.concatenate([jnp.full((1,), -1, i32), group[:-1]])
    flag = jnp.where(active, (prev_t != tile) * FLAG_FIRST + is_last * FLAG_LAST + is_final * FLAG_FINAL
                     + (prev_g != group) * FLAG_NEW_GROUP + is_hi * FLAG_HI, 0)
    cast = lambda z: z.astype(i32)
    bcast = lambda w: jnp.broadcast_to(w[:, None], (t, LANES))
    return cast(tile), cast(e), cast(lo), cast(hi), cast(flag), cast(order), bcast(wlo), bcast(whi)


def _combine_kernel(x_ref, y_ref, g2_ref, lg_ref, lb_ref, o_ref, *, alpha):
    y = _load_token_tiles(y_ref, x_ref.shape[0])
    o_ref[...] = _layer_norm(alpha * x_ref[...] + g2_ref[...] * y, lg_ref[...], lb_ref[...])


def _combine(x1, y, row0, mod3, mod_row0, rows_per_mod, ln_g, ln_b, alpha, tm):
    t, d = x1.shape
    tiles_per_mod = rows_per_mod // tm
    t0 = row0 // tm
    return pl.pallas_call(
        functools.partial(_combine_kernel, alpha=alpha),
        grid=(t // tm,),
        in_specs=[pl.BlockSpec((tm, d), lambda i: (i, 0)),
                  pl.BlockSpec((tm * SUBLANES, LANES), lambda i: (t0 + i, 0)),
                  pl.BlockSpec((None, 1, d), lambda i: (mod_row0 + i // tiles_per_mod, 0, 5)),
                  _const_spec((1, d)), _const_spec((1, d))],
        out_specs=pl.BlockSpec((tm, d), lambda i: (i, 0)),
        out_shape=jax.ShapeDtypeStruct((t, d), F32),
        compiler_params=_cparams("arbitrary"),
        name="combine_ln2",
    )(x1, y, mod3, ln_g.reshape(1, d), ln_b.reshape(1, d))


def _pick_tile(n, pref):
    tm = min(pref, n)
    while n % tm:
        tm //= 2
    return tm


def kernel(x, c, ctx, c_ctx, ada_w, ada_b, w_in, gm_ln_g, gm_ln_b, gm_ws, gm_bs, hy_conv_w, hy_conv_b,
           hy_f_w1, hy_f_b1, hy_f_w2, hy_f_b2, hy_f_w3, hy_f_b3, hy_skip, da_lq1, da_lk1, da_lq2, da_lk2,
           da_norm_g, p_a, p_b, p_c, w_out, ln1_g, ln1_b, moe_wg, moe_bg, moe_we, moe_be,
           ex_w_gate, ex_w_up, ex_w_down, ln2_g, ln2_b):
    B, L, D = x.shape
    Lc = ctx.shape[1]
    depth = ada_w.shape[0]
    alpha = (2.0 * depth) ** 0.25
    T, Tc = B * L, B * Lc
    moe_tm = 512 if T >= 8192 else 64

    mp = -(-(B + 1) // 8) * 8
    c_all = jnp.zeros((mp, D), F32).at[:B].set(c).at[B].set(c_ctx)
    mod = _modulation(c_all, ada_w, ada_b)

    rope_tabs = _rope_tables(L // GRID_W)
    cm, sm = _dft_tables(L)
    cm_bf, sm_bf = cm.astype(BF16), sm.astype(BF16)
    cmc, smc = _dft_tables(Lc)
    cmc_bf, smc_bf = cmc.astype(BF16), smc.astype(BF16)

    seg_all = ((OFF_GM, OFF_HY, "gm"), (OFF_HY, OFF_Q, "hy"), (OFF_Q, OFF_K, "q"), (OFF_K, OFF_V, "k"),
               (OFF_V, OFF_GATE, "v"), (OFF_GATE, OFF_GATE + N_BRANCH * D, "gate"))
    seg_kv = ((0, DA_QK_W, "k"), (DA_QK_W, DA_QK_W + DA_V_W, "v"))

    tm_l = _pick_tile(L, 512)
    tm_c = _pick_tile(Lc, 256)
    tq_l = _pick_tile(L, 512)
    tq_c = _pick_tile(Lc, 256)
    tm_m = _pick_tile(L, 512)
    tm_mc = _pick_tile(Tc, 512)

    xs = x.reshape(T, D)
    xc = ctx.reshape(Tc, D)
    for l in range(depth):
        last = l == depth - 1
        lam_init = 0.8 - 0.6 * math.exp(-0.3 * l)
        mod3 = mod[l].reshape(mp, 1, 6 * D)
        w_l = w_in[l].astype(BF16)
        lparams = [a[l].reshape(1, DA_HEAD_DIM) for a in (da_lq1, da_lk1, da_lq2, da_lk2)]
        norm_g = da_norm_g[l].reshape(1, DA_V_DIM)
        lp = {
            "gm_ln_g": gm_ln_g[l].reshape(1, GM_DIM), "gm_ln_b": gm_ln_b[l].reshape(1, GM_DIM),
            "gm_ws": gm_ws[l].reshape(GM_GROUPS * GM_CHUNK, GM_CHUNK).astype(BF16),
            "gm_bs": jnp.repeat(jnp.transpose(gm_bs[l]), GM_DIM // GM_GROUPS, axis=1),
            "p_a": p_a[l].astype(BF16), "p_b": p_b[l].astype(BF16), "p_c": p_c[l].astype(BF16),
            "w_out": w_out[l].astype(BF16),
            "ln1_g": ln1_g[l].reshape(1, D), "ln1_b": ln1_b[l].reshape(1, D),
            "w_router": _split_bf16(jnp.zeros((D, LANES), F32).at[:, :MOE_GROUPS].set(moe_wg[l])
                                    .at[:, MOE_GROUPS:MOE_GROUPS + MOE_N_EXPERTS].set(moe_we[l])),
            "b_router": jnp.zeros((1, LANES), F32).at[0, :MOE_GROUPS].set(moe_bg[l])
                           .at[0, MOE_GROUPS:MOE_GROUPS + MOE_N_EXPERTS].set(moe_be[l]),
        }
        fw = (hy_f_w1[l], hy_f_b1[l], hy_f_w2[l], hy_f_b2[l], hy_f_w3[l], hy_f_b3[l])

        zgm, zhy, q, k, v, gate = _inproj(xs, mod3, 0, L, w_l, seg_all, rope_tabs, L, tm_l)
        if last:
            k_c, v_c = _inproj(xc, mod3, B, Tc, w_l[:, OFF_K:OFF_GATE], seg_kv, None, Lc, tm_c)
        else:
            zgm_c, zhy_c, q_c, k_c, v_c, gate_c = _inproj(xc, mod3, B, Tc, w_l, seg_all, None, Lc, tm_c)
        y_c = _attention(q, [(k, v, L), (k_c, v_c, Lc)], lparams, norm_g, lam_init, B, L, tq_l)
        kre, kim, nyq = _hyena_filter_spectrum(L, cm, sm, *fw)
        y_b = _hyena(zhy, B, L, hy_conv_w[l], hy_conv_b[l], cm_bf, sm_bf, kre, kim, nyq, hy_skip[l])
        t_all = T if last else T + Tc
        x1, h2, route = _merge(zgm, y_b, y_c, gate, xs, mod3, 0, L, lp, alpha, tm_m, T, 0, None)

        if not last:
            yc_c = _attention(q_c, [(k_c, v_c, Lc)], lparams, norm_g, lam_init, B, Lc, tq_c)
            kre_c, kim_c, nyq_c = _hyena_filter_spectrum(Lc, cmc, smc, *fw)
            yb_c = _hyena(zhy_c, B, Lc, hy_conv_w[l], hy_conv_b[l], cmc_bf, smc_bf, kre_c, kim_c, nyq_c,
                          hy_skip[l])
            x1c, h2c, route_c = _merge(zgm_c, yb_c, yc_c, gate_c, xc, mod3, B, Tc, lp, alpha, tm_mc, Tc, 0, None)
            h2 = jnp.concatenate([h2, h2c], axis=0)
            route = jnp.concatenate([route, route_c], axis=0)

        plan = _visit_plan(route, moe_tm)
        y = _experts(h2, plan, ex_w_gate, ex_w_up, ex_w_down, l, moe_tm)
        xs = _combine(x1, y, 0, mod3, 0, L, ln2_g[l], ln2_b[l], alpha, tm_c)
        if not last:
            xc = _combine(x1c, y, T, mod3, B, Tc, ln2_g[l], ln2_b[l], alpha, tm_c)
    return xs.reshape(B, L, D)
```

```python
import functools
import math

import numpy as np
import jax
import jax.numpy as jnp
from jax import lax
from jax.experimental import pallas as pl
from jax.experimental.pallas import tpu as pltpu

F32 = jnp.float32
BF16 = jnp.bfloat16
HIGHEST = lax.Precision.HIGHEST

GRID_W = 64
GM_DIM = 256
GM_GROUPS = 4
GM_CHUNK = 128
HY_DIM = 256
HY_EMB = 33
HY_BANDS = (HY_EMB - 1) // 2
HY_DECAY_FAST = 0.3
HY_DECAY_SLOW = 1.5
HY_DECAY_TARGET = 1e-2
HY_DECAY_SHIFT = 0.05
DA_HEADS = 4
DA_HEAD_DIM = 64
DA_V_DIM = 2 * DA_HEAD_DIM
DA_QK_W = DA_HEADS * 2 * DA_HEAD_DIM
DA_V_W = DA_HEADS * DA_V_DIM
ROPE_BASE = 10000.0
N_BRANCH = 3
OFF_GM = 0
OFF_HY = OFF_GM + 2 * GM_DIM
OFF_Q = OFF_HY + 3 * HY_DIM
OFF_K = OFF_Q + DA_QK_W
OFF_V = OFF_K + DA_QK_W
OFF_GATE = OFF_V + DA_V_W
MOE_GROUPS = 4
MOE_EXPERTS_PER_GROUP = 8
MOE_N_EXPERTS = MOE_GROUPS * MOE_EXPERTS_PER_GROUP
MOE_TOP_K = 2
LN_EPS = 1e-5
LANES = 128
VMEM_LIMIT = 56 * 1024 * 1024


def _cparams(*sem):
    return pltpu.CompilerParams(dimension_semantics=sem, vmem_limit_bytes=VMEM_LIMIT)


def _sigmoid(x):
    return 1.0 / (1.0 + jnp.exp(-x))


def _layer_norm(x, g, b):
    mu = jnp.mean(x, axis=-1, keepdims=True)
    xc = x - mu
    var = jnp.mean(xc * xc, axis=-1, keepdims=True)
    return xc * lax.rsqrt(var + LN_EPS) * g + b


def _gelu_tanh(x):
    return 0.5 * x * (1.0 + jnp.tanh(math.sqrt(2.0 / math.pi) * (x + 0.044715 * (x * x * x))))


def _const_spec(shape):
    nd = len(shape)
    return pl.BlockSpec(shape, lambda *_: (0,) * nd)


def _mod_kernel(c_ref, w_ref, b_ref, o_ref):
    c = c_ref[...]
    s = c * _sigmoid(c)
    o_ref[...] = jnp.dot(s, w_ref[...], precision=HIGHEST, preferred_element_type=F32) + b_ref[...]


def _modulation(c_all, ada_w, ada_b):
    depth, d, n = ada_w.shape
    mp = c_all.shape[0]
    tn = 512
    return pl.pallas_call(
        _mod_kernel,
        grid=(depth, n // tn),
        in_specs=[pl.BlockSpec((mp, d), lambda l, j: (0, 0)),
                  pl.BlockSpec((None, d, tn), lambda l, j: (l, 0, j)),
                  pl.BlockSpec((None, 1, tn), lambda l, j: (l, 0, j))],
        out_specs=pl.BlockSpec((None, mp, tn), lambda l, j: (l, 0, j)),
        out_shape=jax.ShapeDtypeStruct((depth, mp, n), F32),
        compiler_params=_cparams("arbitrary", "arbitrary"),
        name="adaln_mod",
    )(c_all, ada_w, ada_b.reshape(depth, 1, n))


def _rope_tables(rows):
    n_freq = DA_HEAD_DIM // 4
    row = jnp.broadcast_to(jnp.arange(rows)[:, None], (rows, GRID_W)).reshape(-1).astype(F32)
    col = jnp.broadcast_to(jnp.arange(GRID_W)[None, :], (rows, GRID_W)).reshape(-1).astype(F32)
    inv = ROPE_BASE ** (-jnp.arange(n_freq, dtype=F32) / n_freq)
    ang_r = row[:, None] * inv
    ang_c = col[:, None] * inv
    c64 = jnp.concatenate([jnp.cos(ang_r), jnp.cos(ang_r), jnp.cos(ang_c), jnp.cos(ang_c)], axis=-1)
    s64 = jnp.concatenate([-jnp.sin(ang_r), jnp.sin(ang_r), -jnp.sin(ang_c), jnp.sin(ang_c)], axis=-1)
    return jnp.tile(c64, (1, LANES // DA_HEAD_DIM)), jnp.tile(s64, (1, LANES // DA_HEAD_DIM))


def _rope_block(xb, cos, sin):
    lane = lax.broadcasted_iota(jnp.int32, xb.shape, 1)
    n_freq = DA_HEAD_DIM // 4
    first_half = (lane % (2 * n_freq)) < n_freq
    partner = jnp.where(first_half, pltpu.roll(xb, LANES - n_freq, 1), pltpu.roll(xb, n_freq, 1))
    return xb * cos + partner * sin


def _inproj_kernel(*refs, segs, use_rope, n_chunk):
    if use_rope:
        x_ref, sh_ref, sc_ref, w_ref, cos_ref, sin_ref = refs[:6]
        out_refs = refs[6:]
    else:
        x_ref, sh_ref, sc_ref, w_ref = refs[:4]
        out_refs = refs[4:]
    h = (x_ref[...] * (1.0 + sc_ref[...]) + sh_ref[...]).astype(BF16)
    for (a, b, kind), o_ref in zip(segs, out_refs):
        for c0 in range(a, b, n_chunk):
            c1 = min(c0 + n_chunk, b)
            acc = jnp.dot(h, w_ref[:, c0:c1], preferred_element_type=F32)
            if kind == "q":
                acc = acc * (DA_HEAD_DIM ** -0.5 * math.log2(math.e))
            if use_rope and kind in ("q", "k"):
                cos = cos_ref[...]
                sin = sin_ref[...]
                for j in range((c1 - c0) // LANES):
                    blk = _rope_block(acc[:, j * LANES:(j + 1) * LANES], cos, sin)
                    o_ref[:, c0 - a + j * LANES:c0 - a + (j + 1) * LANES] = blk.astype(o_ref.dtype)
            else:
                o_ref[:, c0 - a:c1 - a] = acc.astype(o_ref.dtype)


def _inproj(x2d, mod3, mod_row0, rows_per_mod, w, segs, rope_tabs, seq_len, tm):
    t, d = x2d.shape
    n = w.shape[1]
    use_rope = rope_tabs is not None
    tiles_per_mod = rows_per_mod // tm
    tiles_per_seq = seq_len // tm

    def mod_map(piece):
        return lambda i: (mod_row0 + i // tiles_per_mod, 0, piece)

    in_specs = [pl.BlockSpec((tm, d), lambda i: (i, 0)),
                pl.BlockSpec((None, 1, d), mod_map(0)),
                pl.BlockSpec((None, 1, d), mod_map(1)),
                pl.BlockSpec((d, n), lambda i: (0, 0), pipeline_mode=pl.Buffered(1))]
    args = [x2d, mod3, mod3, w]
    if use_rope:
        in_specs += [pl.BlockSpec((tm, LANES), lambda i: (i % tiles_per_seq, 0))] * 2
        args += list(rope_tabs)
    out_specs = [pl.BlockSpec((tm, b - a), lambda i: (i, 0)) for a, b, _ in segs]
    out_shape = [jax.ShapeDtypeStruct((t, b - a), BF16) for a, b, _ in segs]
    return pl.pallas_call(
        functools.partial(_inproj_kernel, segs=segs, use_rope=use_rope, n_chunk=512),
        grid=(t // tm,),
        in_specs=in_specs, out_specs=out_specs, out_shape=out_shape,
        compiler_params=_cparams("arbitrary"),
        name="inproj",
    )(*args)


ATTN_KEY_CHUNK = 256


def _attn_kernel(*refs, src_lens, lam_init):
    n_src = len(src_lens)
    lq1, lk1, lq2, lk2, g_ref, q_ref = refs[:6]
    kv_refs = refs[6:6 + 2 * n_src]
    o_ref, s_scr = refs[6 + 2 * n_src:]
    lam = (jnp.exp(jnp.sum(lq1[...] * lk1[...], axis=-1, keepdims=True))
           - jnp.exp(jnp.sum(lq2[...] * lk2[...], axis=-1, keepdims=True)) + lam_init)
    tq = q_ref.shape[0]
    lane = lax.broadcasted_iota(jnp.int32, (tq, LANES), 1)
    dn = (((1,), (1,)), ((), ()))
    chunks = []
    off = 0
    for j, n in enumerate(src_lens):
        kc = min(ATTN_KEY_CHUNK, n)
        for st in range(0, n, kc):
            chunks.append((j, st, kc, off))
            off += kc
    for h in range(DA_HEADS):
        cols = slice(h * LANES, (h + 1) * LANES)
        qh = q_ref[:, cols]
        zero = jnp.zeros_like(qh)
        om = []
        for m in range(2):
            qm = jnp.where(lane < DA_HEAD_DIM if m == 0 else lane >= DA_HEAD_DIM, qh, zero)
            mlane = None
            for j, st, kc, off in chunks:
                s_c = lax.dot_general(qm, kv_refs[2 * j][st:st + kc, cols], dn, preferred_element_type=F32)
                s_scr[:, off:off + kc] = s_c
                for b in range(kc // LANES):
                    blk = s_c[:, b * LANES:(b + 1) * LANES]
                    mlane = blk if mlane is None else jnp.maximum(mlane, blk)
            mx = jnp.max(mlane, axis=-1, keepdims=True)
            acc = None
            for j, st, kc, off in chunks:
                p = jnp.exp2(s_scr[:, off:off + kc] - mx).astype(BF16)
                v_aug = jnp.concatenate([kv_refs[2 * j + 1][st:st + kc, cols], jnp.ones((kc, LANES), BF16)],
                                        axis=1)
                d = jnp.dot(p, v_aug, preferred_element_type=F32)
                acc = d if acc is None else acc + d
            om.append(acc[:, :LANES] * (1.0 / acc[:, LANES:LANES + 1]))
        o = om[0] - lam * om[1]
        ms = jnp.mean(o * o, axis=-1, keepdims=True)
        o = o * lax.rsqrt(ms + LN_EPS) * g_ref[...] * (1.0 - lam_init)
        o_ref[:, cols] = o.astype(o_ref.dtype)


def _attention(q, kvs, lparams, norm_g, lam_init, nb, lq, tq):
    t = q.shape[0]
    qt = lq // tq
    in_specs = [_const_spec((1, DA_HEAD_DIM))] * 4 + [_const_spec((1, DA_V_DIM))]
    in_specs.append(pl.BlockSpec((tq, DA_QK_W), lambda b, i: (b * qt + i, 0)))
    args = list(lparams) + [norm_g, q]
    for k, v, lk in kvs:
        in_specs += [pl.BlockSpec((lk, DA_QK_W), lambda b, i: (b, 0)),
                     pl.BlockSpec((lk, DA_V_W), lambda b, i: (b, 0))]
        args += [k, v]
    src_lens = tuple(lk for _, _, lk in kvs)
    return pl.pallas_call(
        functools.partial(_attn_kernel, src_lens=src_lens, lam_init=lam_init),
        grid=(nb, qt),
        in_specs=in_specs,
        out_specs=pl.BlockSpec((tq, DA_V_W), lambda b, i: (b * qt + i, 0)),
        out_shape=jax.ShapeDtypeStruct((t, DA_V_W), BF16),
        scratch_shapes=[pltpu.VMEM((tq, sum(src_lens)), F32)],
        compiler_params=_cparams("arbitrary", "arbitrary"),
        name="diff_attn",
    )(*args)


DFT_SPLIT = 64


def _dft_tables(L):
    n = jnp.arange(L, dtype=jnp.int32)[None, :]
    k1 = jnp.arange(L // DFT_SPLIT, dtype=jnp.int32)[:, None] * DFT_SPLIT
    k0 = jnp.arange(DFT_SPLIT, dtype=jnp.int32)[:, None]
    ang_a = ((k1 * n) % (2 * L)).astype(F32) * (math.pi / L)
    ang_b = ((k0 * n) % (2 * L)).astype(F32) * (math.pi / L)
    ca, sa = jnp.cos(ang_a)[:, None, :], jnp.sin(ang_a)[:, None, :]
    cb, sb = jnp.cos(ang_b)[None, :, :], jnp.sin(ang_b)[None, :, :]
    return (ca * cb - sa * sb).reshape(L, L), (sa * cb + ca * sb).reshape(L, L)


def _filter_consts(L):
    t = jnp.linspace(0.0, 1.0, L, dtype=F32)[:, None]
    w = 2.0 * math.pi * jnp.arange(L, dtype=F32)[:, None] / L
    f = jnp.linspace(1e-4, HY_BANDS - 1, HY_BANDS, dtype=F32)[None, :]
    emb = jnp.concatenate([t, jnp.cos(f * w), -jnp.sin(f * w)], axis=-1)
    max_decay = math.log(HY_DECAY_TARGET) / HY_DECAY_FAST
    min_decay = math.log(HY_DECAY_TARGET) / HY_DECAY_SLOW
    deltas = jnp.abs(jnp.linspace(min_decay, max_decay, HY_DIM, dtype=F32))
    window = jnp.exp(-t * deltas[None, :]) + HY_DECAY_SHIFT
    return emb, window


def _filter_kernel(emb_ref, win_ref, w1, b1, w2, b2, w3, b3, hs_ref, hd_ref, nyq_ref):
    h = jnp.sin(jnp.dot(emb_ref[...], w1[...], precision=HIGHEST, preferred_element_type=F32) + b1[...])
    h = jnp.sin(jnp.dot(h, w2[...], precision=HIGHEST, preferred_element_type=F32) + b2[...])
    h = jnp.dot(h, w3[...], precision=HIGHEST, preferred_element_type=F32) + b3[...]
    win = win_ref[...]
    hf = h[:, :HY_DIM] * win
    hb = h[:, HY_DIM:] * win
    row = lax.broadcasted_iota(jnp.int32, hf.shape, 0)
    hb = jnp.where(row == 0, 0.0, hb)
    alt = jnp.where(row % 2 == 0, 1.0, -1.0)
    hs_ref[...] = hf + hb
    hd_ref[...] = hf - hb
    nyq_ref[...] = jnp.sum((hf + hb) * alt, axis=0, keepdims=True)


def _spectrum_kernel(c_ref, s_ref, hs_ref, hd_ref, kre_ref, kim_ref, *, n_fft):
    i = pl.program_id(0)
    tk = c_ref.shape[0]
    kidx = i * tk + lax.broadcasted_iota(jnp.int32, (tk, 1), 0)
    scale = jnp.where(kidx == 0, 1.0 / n_fft, 2.0 / n_fft)
    kre = jnp.dot(c_ref[...], hs_ref[...], precision=HIGHEST, preferred_element_type=F32)
    kim = -jnp.dot(s_ref[...], hd_ref[...], precision=HIGHEST, preferred_element_type=F32)
    kre_ref[...] = kre * scale
    kim_ref[...] = kim * scale


def _hyena_filter_spectrum(L, cmat, smat, w1, b1, w2, b2, w3, b3):
    emb, window = _filter_consts(L)
    full = lambda a: _const_spec(a.shape)
    ins = [emb, window, w1, b1.reshape(1, -1), w2, b2.reshape(1, -1), w3, b3.reshape(1, -1)]
    hs, hd, nyq = pl.pallas_call(
        _filter_kernel,
        grid=(1,),
        in_specs=[full(a) for a in ins],
        out_specs=[_const_spec((L, HY_DIM)), _const_spec((L, HY_DIM)), _const_spec((1, HY_DIM))],
        out_shape=[jax.ShapeDtypeStruct((L, HY_DIM), F32), jax.ShapeDtypeStruct((L, HY_DIM), F32),
                   jax.ShapeDtypeStruct((1, HY_DIM), F32)],
        compiler_params=_cparams("arbitrary"),
        name="hyena_filter",
    )(*ins)
    tk = min(256, L)
    kre, kim = pl.pallas_call(
        functools.partial(_spectrum_kernel, n_fft=2 * L),
        grid=(L // tk,),
        in_specs=[pl.BlockSpec((tk, L), lambda i: (i, 0)), pl.BlockSpec((tk, L), lambda i: (i, 0)),
                  _const_spec((L, HY_DIM)), _const_spec((L, HY_DIM))],
        out_specs=[pl.BlockSpec((tk, HY_DIM), lambda i: (i, 0))] * 2,
        out_shape=[jax.ShapeDtypeStruct((L, HY_DIM), F32)] * 2,
        compiler_params=_cparams("arbitrary"),
        name="hyena_spectrum",
    )(cmat, smat, hs, hd)
    return kre, kim, nyq * (1.0 / (2 * L))


HY_ROW_BLOCK = 1024


def _hyena_kernel(z_ref, cw_ref, cb_ref, c_ref, s_ref, kre_ref, kim_ref, nyq_ref, skip_ref, o_ref,
                  u_ref, x0_ref, p_ref, q_ref):
    L = z_ref.shape[0]
    row = lax.broadcasted_iota(jnp.int32, (L, HY_DIM), 0)

    def conv(j):
        cols = slice(j * HY_DIM, (j + 1) * HY_DIM)
        z = z_ref[:, cols].astype(F32)
        zprev = jnp.where(row == 0, 0.0, pltpu.roll(z, 1, 0))
        znext = jnp.where(row == L - 1, 0.0, pltpu.roll(z, L - 1, 0))
        return zprev * cw_ref[0:1, cols] + z * cw_ref[1:2, cols] + znext * cw_ref[2:3, cols] + cb_ref[:, cols]

    u = conv(2) * conv(1)
    ub = u.astype(BF16)
    u_ref[...] = u
    alt = jnp.where(row % 2 == 0, 1.0, -1.0)
    nyq_term = jnp.sum(u * alt, axis=0, keepdims=True) * nyq_ref[...]
    x0_ref[...] = conv(0)
    blk = min(HY_ROW_BLOCK, L)
    for r in range(0, L, blk):
        rows = slice(r, r + blk)
        a = jnp.dot(c_ref[rows, :], ub, preferred_element_type=F32)
        b = jnp.dot(s_ref[rows, :], ub, preferred_element_type=F32)
        kre = kre_ref[rows, :]
        kim = kim_ref[rows, :]
        p_ref[rows, :] = (a * kre + b * kim).astype(BF16)
        q_ref[rows, :] = (b * kre - a * kim).astype(BF16)
    for r in range(0, L, blk):
        rows = slice(r, r + blk)
        y = (jnp.dot(c_ref[rows, :], p_ref[...], preferred_element_type=F32)
             + jnp.dot(s_ref[rows, :], q_ref[...], preferred_element_type=F32))
        ub_rows = u_ref[rows, :]
        row_b = lax.broadcasted_iota(jnp.int32, (blk, HY_DIM), 0)
        y = y + jnp.where(row_b % 2 == 0, nyq_term, -nyq_term) + ub_rows * skip_ref[...]
        o_ref[rows, :] = (y * x0_ref[rows, :]).astype(o_ref.dtype)


def _hyena(zhy, nb, L, conv_w, conv_b, cmat_bf, smat_bf, kre, kim, nyq, skip):
    t = zhy.shape[0]
    return pl.pallas_call(
        _hyena_kernel,
        grid=(nb,),
        in_specs=[pl.BlockSpec((L, 3 * HY_DIM), lambda b: (b, 0)),
                  _const_spec((3, 3 * HY_DIM)), _const_spec((1, 3 * HY_DIM)),
                  pl.BlockSpec((L, L), lambda b: (0, 0), pipeline_mode=pl.Buffered(1)),
                  pl.BlockSpec((L, L), lambda b: (0, 0), pipeline_mode=pl.Buffered(1)),
                  _const_spec((L, HY_DIM)), _const_spec((L, HY_DIM)),
                  _const_spec((1, HY_DIM)), _const_spec((1, HY_DIM))],
        out_specs=pl.BlockSpec((L, HY_DIM), lambda b: (b, 0)),
        out_shape=jax.ShapeDtypeStruct((t, HY_DIM), BF16),
        scratch_shapes=[pltpu.VMEM((L, HY_DIM), F32), pltpu.VMEM((L, HY_DIM), F32),
                        pltpu.VMEM((L, HY_DIM), BF16), pltpu.VMEM((L, HY_DIM), BF16)],
        compiler_params=_cparams("arbitrary"),
        name="hyena_conv",
    )(zhy, conv_w, conv_b.reshape(1, -1), cmat_bf, smat_bf, kre, kim, nyq, skip.reshape(1, -1))


SUBLANES = 8


def _split_bf16(w):
    hi = w.astype(BF16)
    return jnp.stack([hi, (w - hi.astype(F32)).astype(BF16)])


def _store_token_tiles(ref, val):
    n = val.shape[0]
    for j in range(val.shape[1] // LANES):
        ref[pl.ds(j, n, stride=SUBLANES), :] = val[:, j * LANES:(j + 1) * LANES]


def _load_token_tiles(ref, n):
    return jnp.concatenate([ref[pl.ds(j, n, stride=SUBLANES), :] for j in range(SUBLANES)], axis=1)


def _merge_kernel(*refs, alpha, n_alias, n_real):
    h2_ref, rt_ref = refs[-2:]
    i = pl.program_id(0)

    @pl.when(i < n_real)
    def _():
        _merge_tile(*refs, alpha=alpha, n_alias=n_alias)

    @pl.when(i >= n_real)
    def _():
        h2_ref[...] = jnp.zeros_like(h2_ref)
        rt_ref[...] = jnp.zeros_like(rt_ref)


def _merge_tile(*refs, alpha, n_alias):
    (zgm_ref, yb_ref, yc_ref, gate_ref, x_ref, g1_ref, sh2_ref, sc2_ref, lng_ref, lnb_ref, ws_ref, bs_ref,
     pa_ref, pb_ref, pc_ref, wo_ref, l1g_ref, l1b_ref, wr_ref, br_ref) = refs[:20]
    x1_ref, h2_ref, rt_ref = refs[20 + n_alias:]
    tm = x_ref.shape[0]
    d = x_ref.shape[1]
    gm = _gelu_tanh(zgm_ref[...].astype(F32))
    u = gm[:, :GM_DIM]
    v = _layer_norm(gm[:, GM_DIM:], lng_ref[...], lnb_ref[...]).astype(BF16)
    lane_group = lax.broadcasted_iota(jnp.int32, (GM_CHUNK, GM_DIM), 1) // (GM_DIM // GM_GROUPS)
    ya = []
    for cidx in range(tm // GM_CHUNK):
        rows = slice(cidx * GM_CHUNK, (cidx + 1) * GM_CHUNK)
        r = jnp.dot(ws_ref[...], v[rows], preferred_element_type=F32)
        vv = bs_ref[...]
        for g in range(GM_GROUPS):
            vv = vv + jnp.where(lane_group == g, r[g * GM_CHUNK:(g + 1) * GM_CHUNK], 0.0)
        ya.append(u[rows] * vv)
    ya = jnp.concatenate(ya, axis=0) if len(ya) > 1 else ya[0]
    ma = jnp.dot(ya.astype(BF16), pa_ref[...], preferred_element_type=F32)
    mb = jnp.dot(yb_ref[...], pb_ref[...], preferred_element_type=F32)
    mc = jnp.dot(yc_ref[...], pc_ref[...], preferred_element_type=F32)
    merged = (_sigmoid(gate_ref[:, 0:d].astype(F32)) * ma
              + _sigmoid(gate_ref[:, d:2 * d].astype(F32)) * mb
              + _sigmoid(gate_ref[:, 2 * d:3 * d].astype(F32)) * mc)
    out = jnp.dot(merged.astype(BF16), wo_ref[...], preferred_element_type=F32)
    x1 = _layer_norm(alpha * x_ref[...] + g1_ref[...] * out, l1g_ref[...], l1b_ref[...])
    x1_ref[...] = x1
    h2 = x1 * (1.0 + sc2_ref[...]) + sh2_ref[...]
    _store_token_tiles(h2_ref, h2)
    h2_hi = h2.astype(BF16)
    h2_lo = (h2 - h2_hi.astype(F32)).astype(BF16)
    lg = (jnp.dot(h2_hi, wr_ref[0], preferred_element_type=F32)
          + jnp.dot(h2_hi, wr_ref[1], preferred_element_type=F32)
          + jnp.dot(h2_lo, wr_ref[0], preferred_element_type=F32) + br_ref[...])
    rt_ref[...] = _route(lg)


ROUTE_E0, ROUTE_E1, ROUTE_W0, ROUTE_W1 = 0, 1, 2, 3


def _route(lg):
    neg = jnp.float32(-3.0e38)
    lane_i = lax.broadcasted_iota(jnp.int32, lg.shape, 1)
    lane = lane_i.astype(F32)
    big = jnp.float32(LANES)
    is_g = lane_i < MOE_GROUPS
    gl = jnp.where(is_g, lg, neg)
    gmax = jnp.max(gl, axis=-1, keepdims=True)
    g_idx = jnp.min(jnp.where(gl == gmax, lane, big), axis=-1, keepdims=True)
    g_prob = 1.0 / jnp.sum(jnp.where(is_g, jnp.exp(gl - gmax), 0.0), axis=-1, keepdims=True)
    e_lo = MOE_GROUPS + MOE_EXPERTS_PER_GROUP * g_idx
    el = jnp.where(lane >= e_lo, jnp.where(lane < e_lo + MOE_EXPERTS_PER_GROUP, lg, neg), neg)
    v1 = jnp.max(el, axis=-1, keepdims=True)
    i1 = jnp.min(jnp.where(el == v1, lane, big), axis=-1, keepdims=True)
    el2 = jnp.where(lane == i1, neg, el)
    v2 = jnp.max(el2, axis=-1, keepdims=True)
    i2 = jnp.min(jnp.where(el2 == v2, lane, big), axis=-1, keepdims=True)
    e21 = jnp.exp(v2 - v1)
    w1 = g_prob / (1.0 + e21)
    w2 = w1 * e21
    swap = i2 < i1
    rec = jnp.where(lane_i == ROUTE_E0, jnp.minimum(i1, i2) - MOE_GROUPS, 0.0)
    rec = jnp.where(lane_i == ROUTE_E1, jnp.maximum(i1, i2) - MOE_GROUPS, rec)
    rec = jnp.where(lane_i == ROUTE_W0, jnp.where(swap, w2, w1), rec)
    return jnp.where(lane_i == ROUTE_W1, jnp.where(swap, w1, w2), rec)


def _merge(zgm, yb, yc, gate, x2d, mod3, mod_row0, rows_per_mod, lp, alpha, tm, t_all, row0, prev):
    t, d = x2d.shape
    tiles_per_mod = rows_per_mod // tm
    off = row0 // tm
    n_real = t // tm
    n_fill = (t_all - row0 - t) // tm if prev is None else 0
    real = lambda i: jnp.minimum(i, n_real - 1)

    def mod_map(piece):
        return lambda i: (mod_row0 + real(i) // tiles_per_mod, 0, piece)

    row = lambda w: pl.BlockSpec((tm, w), lambda i: (real(i), 0))
    row_off = lambda w: pl.BlockSpec((tm, w), lambda i: (off + i, 0))
    consts = [lp["gm_ln_g"], lp["gm_ln_b"], lp["gm_ws"], lp["gm_bs"], lp["p_a"], lp["p_b"], lp["p_c"],
              lp["w_out"], lp["ln1_g"], lp["ln1_b"], lp["w_router"], lp["b_router"]]
    in_specs = [row(2 * GM_DIM), row(HY_DIM), row(DA_V_W), row(N_BRANCH * d), row(d),
                pl.BlockSpec((None, 1, d), mod_map(2)), pl.BlockSpec((None, 1, d), mod_map(3)),
                pl.BlockSpec((None, 1, d), mod_map(4))] + [_const_spec(a.shape) for a in consts]
    args = [zgm, yb, yc, gate, x2d, mod3, mod3, mod3, *consts]
    aliases = {}
    if prev is not None:
        aliases = {len(args): 1, len(args) + 1: 2}
        in_specs += [pl.BlockSpec(memory_space=pl.ANY)] * 2
        args += list(prev)
    return pl.pallas_call(
        functools.partial(_merge_kernel, alpha=alpha, n_alias=len(aliases), n_real=n_real),
        grid=(n_real + n_fill,),
        in_specs=in_specs,
        out_specs=[row(d), pl.BlockSpec((tm * SUBLANES, LANES), lambda i: (off + i, 0)), row_off(LANES)],
        out_shape=[jax.ShapeDtypeStruct((t, d), F32), jax.ShapeDtypeStruct((t_all * SUBLANES, LANES), F32),
                   jax.ShapeDtypeStruct((t_all, LANES), F32)],
        input_output_aliases=aliases,
        compiler_params=_cparams("arbitrary"),
        name="merge_ln1",
    )(*args)


PAIR_CLASSES = tuple((MOE_EXPERTS_PER_GROUP * g + a, MOE_EXPERTS_PER_GROUP * g + b)
                     for g in range(MOE_GROUPS)
                     for a in range(MOE_EXPERTS_PER_GROUP) for b in range(a + 1, MOE_EXPERTS_PER_GROUP))
FLAG_FIRST, FLAG_LAST, FLAG_FINAL, FLAG_NEW_GROUP, FLAG_HI = 1, 2, 4, 8, 16


def _tile_copy(src, src_row, dst, dst_row, sem):
    s0 = pl.multiple_of(src_row * SUBLANES, SUBLANES)
    d0 = pl.multiple_of(dst_row * SUBLANES, SUBLANES)
    return pltpu.make_async_copy(src.at[pl.ds(s0, SUBLANES)], dst.at[pl.ds(d0, SUBLANES)], sem)


def _expert_kernel(vt_ref, ve_ref, vlo_ref, vhi_ref, vflag_ref, src_ref, nsrc_ref, dst_ref, pdst_ref,
                   wlo_ref, whi_ref, h2_hbm, wg_hbm, wu_hbm, wd_hbm, y_hbm,
                   xbuf, acc, ybuf, wg_grp, wu_grp, wd_grp, wg_stage, wu_stage, wd_stage, gsem, ssem, wsem,
                   *, n_tiles, layer):
    v = pl.program_id(0)
    tile, expert, lo, hi, flag = vt_ref[v], ve_ref[v], vlo_ref[v], vhi_ref[v], vflag_ref[v]
    tm = acc.shape[0]
    slot = tile % 2
    first = (flag & FLAG_FIRST) != 0
    group0 = pl.multiple_of((expert // MOE_EXPERTS_PER_GROUP) * MOE_EXPERTS_PER_GROUP, MOE_EXPERTS_PER_GROUP)
    e_in_group = expert - group0

    @pl.when((flag & FLAG_NEW_GROUP) != 0)
    def _():
        streams = ((wg_hbm, wg_stage, wg_grp), (wu_hbm, wu_stage, wu_grp), (wd_hbm, wd_stage, wd_grp))
        for k in range(MOE_EXPERTS_PER_GROUP):
            copies = [pltpu.make_async_copy(w_hbm.at[layer, group0 + k], stage, wsem.at[i])
                      for i, (w_hbm, stage, _) in enumerate(streams)]
            for cp in copies:
                cp.start()
            for cp, (_, stage, w_grp) in zip(copies, streams):
                cp.wait()
                w_grp[k] = stage[...].astype(BF16)

    def issue_gather(idx_ref, to_slot):
        def body(i, carry):
            for j in range(SUBLANES):
                r = i * SUBLANES + j
                _tile_copy(h2_hbm, idx_ref[0, r], xbuf.at[to_slot], r, gsem.at[to_slot]).start(priority=j % 2)
            return carry

        lax.fori_loop(0, tm // SUBLANES, body, 0)

    def issue_scatter(idx_ref):
        def body(i, carry):
            for j in range(SUBLANES):
                r = i * SUBLANES + j
                _tile_copy(ybuf, r, y_hbm, idx_ref[0, r], ssem).start(priority=j % 2)
            return carry

        lax.fori_loop(0, tm // SUBLANES, body, 0)

    has_next = tile + 1 < n_tiles
    prefetch = first & (tile > 0) & has_next

    @pl.when(first)
    def _():
        @pl.when(tile == 0)
        def _():
            issue_gather(src_ref, slot)

        pltpu.make_async_copy(h2_hbm.at[pl.ds(0, tm * SUBLANES)], xbuf.at[slot], gsem.at[slot]).wait()

        @pl.when(jnp.logical_not(prefetch))
        def _():
            @pl.when(has_next)
            def _():
                issue_gather(nsrc_ref, 1 - slot)

            @pl.when(tile > 0)
            def _():
                issue_scatter(pdst_ref)

    def compute(with_prefetch):
        xb = _load_token_tiles(xbuf.at[slot], tm).astype(BF16)
        if with_prefetch:
            for r in range(tm):
                _tile_copy(h2_hbm, nsrc_ref[0, r], xbuf.at[1 - slot], r, gsem.at[1 - slot]).start(priority=r % 2)
                _tile_copy(ybuf, r, y_hbm, pdst_ref[0, r], ssem).start(priority=r % 2)
        g = jnp.dot(xb, wg_grp[e_in_group], preferred_element_type=F32)
        u = jnp.dot(xb, wu_grp[e_in_group], preferred_element_type=F32)
        hmid = (g * _sigmoid(g) * u).astype(BF16)
        y = jnp.dot(hmid, wd_grp[e_in_group], preferred_element_type=F32)
        w_b = jnp.where((flag & FLAG_HI) != 0, whi_ref[...], wlo_ref[...])
        row = lax.broadcasted_iota(jnp.int32, (tm, 1), 0)
        y = jnp.where((row >= lo) & (row < hi), y * jnp.concatenate([w_b] * SUBLANES, axis=1), 0.0)
        if with_prefetch:
            acc[...] = y
            return

        @pl.when(first)
        def _():
            acc[...] = y

        @pl.when(jnp.logical_not(first))
        def _():
            acc[...] += y

    @pl.when(prefetch)
    def _():
        compute(True)

    @pl.when(jnp.logical_not(prefetch) & (hi > lo))
    def _():
        compute(False)

    @pl.when((flag & FLAG_LAST) != 0)
    def _():
        whole = pltpu.make_async_copy(ybuf, y_hbm.at[pl.ds(0, tm * SUBLANES)], ssem)

        @pl.when(tile > 0)
        def _():
            whole.wait()

        _store_token_tiles(ybuf, acc[...])

        @pl.when((flag & FLAG_FINAL) != 0)
        def _():
            issue_scatter(dst_ref)
            whole.wait()


def _experts(h2, plan, w_gate, w_up, w_down, layer, tm):
    t = h2.shape[0] // SUBLANES
    vt, ve, vlo, vhi, vflag, order, wlo_b, whi_b = plan
    n_vis = vt.shape[0]
    n_tiles = t // tm
    d, hid = w_gate.shape[-2:]
    grp = MOE_EXPERTS_PER_GROUP
    assert d == SUBLANES * LANES
    idx_spec = lambda nxt: pl.BlockSpec(
        (None, 1, tm), lambda v, vt, *_: (jnp.clip(vt[v] + nxt, 0, n_tiles - 1), 0, 0), memory_space=pltpu.SMEM)
    w_tile = pl.BlockSpec((tm, LANES), lambda v, vt, *_: (vt[v], 0))
    any_spec = pl.BlockSpec(memory_space=pl.ANY)
    grid_spec = pltpu.PrefetchScalarGridSpec(
        num_scalar_prefetch=5,
        grid=(n_vis,),
        in_specs=[idx_spec(0), idx_spec(1), idx_spec(0), idx_spec(-1), w_tile, w_tile,
                  any_spec, any_spec, any_spec, any_spec],
        out_specs=any_spec,
        scratch_shapes=[pltpu.VMEM((2, tm * SUBLANES, LANES), F32), pltpu.VMEM((tm, d), F32),
                        pltpu.VMEM((tm * SUBLANES, LANES), F32),
                        pltpu.VMEM((grp, d, hid), BF16), pltpu.VMEM((grp, d, hid), BF16),
                        pltpu.VMEM((grp, hid, d), BF16),
                        pltpu.VMEM((d, hid), F32), pltpu.VMEM((d, hid), F32), pltpu.VMEM((hid, d), F32),
                        pltpu.SemaphoreType.DMA((2,)), pltpu.SemaphoreType.DMA(()), pltpu.SemaphoreType.DMA((3,))],
    )
    idx3 = order.reshape(n_tiles, 1, tm)
    return pl.pallas_call(
        functools.partial(_expert_kernel, n_tiles=n_tiles, layer=layer),
        grid_spec=grid_spec,
        out_shape=jax.ShapeDtypeStruct((t * SUBLANES, LANES), F32),
        compiler_params=_cparams("arbitrary"),
        name="moe_experts",
    )(vt, ve, vlo, vhi, vflag, idx3, idx3, idx3, idx3, wlo_b, whi_b, h2, w_gate, w_up, w_down)


def _visit_plan(route, tm):
    t = route.shape[0]
    i32 = jnp.int32
    n_cls = len(PAIR_CLASSES)
    cls_lo = jnp.asarray(np.array([p[0] for p in PAIR_CLASSES], np.int32))
    cls_hi = jnp.asarray(np.array([p[1] for p in PAIR_CLASSES], np.int32))
    cls_key = cls_lo * MOE_N_EXPERTS + cls_hi
    key = route[:, ROUTE_E0].astype(i32) * MOE_N_EXPERTS + route[:, ROUTE_E1].astype(i32)
    idx_bits = max(1, (t - 1).bit_length())
    assert (MOE_N_EXPERTS * MOE_N_EXPERTS) << idx_bits <= 2 ** 31
    packed, wlo, whi = lax.sort((key * (1 << idx_bits) + lax.iota(i32, t), route[:, ROUTE_W0], route[:, ROUTE_W1]),
                                num_keys=1)
    order = packed & ((1 << idx_bits) - 1)
    counts = jnp.sum((key[:, None] == cls_key[None, :]).astype(i32), axis=0)
    n_tiles = t // tm

    def segment_visits(seg_counts, seg_expert):
        n_seg = seg_counts.shape[0]
        ends = jnp.cumsum(seg_counts)
        starts = ends - seg_counts
        first_t = starts // tm
        nvis = jnp.where(seg_counts > 0, jnp.maximum(ends - 1, 0) // tm - first_t + 1, 0)
        cv_end = jnp.cumsum(nvis)
        cv_start = cv_end - nvis
        v = jnp.arange(n_tiles + n_seg, dtype=i32)
        active = v < cv_end[-1]
        s = jnp.minimum(jnp.sum((cv_end[None, :] <= v[:, None]).astype(i32), axis=1), n_seg - 1)
        tile = first_t[s] + v - cv_start[s]
        lo = jnp.clip(starts[s] - tile * tm, 0, tm)
        hi = jnp.clip(ends[s] - tile * tm, 0, tm)
        return tile, seg_expert[s], lo, hi, active

    run_id = np.cumsum([0] + [int(a[0] != b[0]) for a, b in zip(PAIR_CLASSES[:-1], PAIR_CLASSES[1:])])
    run_lo = jnp.asarray(np.array([PAIR_CLASSES[list(run_id).index(r)][0] for r in range(run_id[-1] + 1)], np.int32))
    in_run = jnp.asarray(run_id[None, :] == np.arange(run_id[-1] + 1)[:, None])
    run_counts = jnp.sum(jnp.where(in_run, counts[None, :], 0), axis=1)
    parts = [segment_visits(run_counts, run_lo) + (0,), segment_visits(counts, cls_hi) + (1,)]
    tile = jnp.concatenate([p[0] for p in parts])
    e = jnp.concatenate([p[1] for p in parts])
    lo = jnp.concatenate([p[2] for p in parts])
    hi = jnp.concatenate([p[3] for p in parts])
    active = jnp.concatenate([p[4] for p in parts])
    is_hi = jnp.concatenate([jnp.full(p[0].shape, p[5], i32) for p in parts])
    assert tm * 2 + 2 <= 2048
    order_key = jnp.where(active, tile * 2048 + lo * 2 + is_hi, jnp.iinfo(jnp.int32).max)
    _, tile, e, lo, hi, is_hi, active = lax.sort((order_key, tile, e, lo, hi, is_hi, active.astype(i32)), num_keys=1)
    active = active == 1
    n_active = jnp.sum(active.astype(i32))
    v = jnp.arange(tile.shape[0], dtype=i32)
    e_last = jnp.max(jnp.where(active, v, -1))
    e = jnp.where(active, e, e[e_last])
    tile = jnp.where(active, tile, n_tiles - 1)
    lo = jnp.where(active, lo, 0)
    hi = jnp.where(active, hi, 0)
    prev_t = jnp.concatenate([jnp.full((1,), -1, i32), tile[:-1]])
    next_t = jnp.concatenate([tile[1:], jnp.full((1,), -1, i32)])
    is_final = v == n_active - 1
    is_last = (next_t != tile) | is_final
    group = e // MOE_EXPERTS_PER_GROUP
    prev_g = jnp.concatenate([jnp.full((1,), -1, i32), group[:-1]])
    flag = jnp.where(active, (prev_t != tile) * FLAG_FIRST + is_last * FLAG_LAST + is_final * FLAG_FINAL
                     + (prev_g != group) * FLAG_NEW_GROUP + is_hi * FLAG_HI, 0)
    cast = lambda z: z.astype(i32)
    bcast = lambda w: jnp.broadcast_to(w[:, None], (t, LANES))
    return cast(tile), cast(e), cast(lo), cast(hi), cast(flag), cast(order), bcast(wlo), bcast(whi)


def _combine_kernel(x_ref, y_ref, g2_ref, lg_ref, lb_ref, o_ref, *, alpha):
    y = _load_token_tiles(y_ref, x_ref.shape[0])
    o_ref[...] = _layer_norm(alpha * x_ref[...] + g2_ref[...] * y, lg_ref[...], lb_ref[...])


def _combine(x1, y, row0, mod3, mod_row0, rows_per_mod, ln_g, ln_b, alpha, tm):
    t, d = x1.shape
    tiles_per_mod = rows_per_mod // tm
    t0 = row0 // tm
    return pl.pallas_call(
        functools.partial(_combine_kernel, alpha=alpha),
        grid=(t // tm,),
        in_specs=[pl.BlockSpec((tm, d), lambda i: (i, 0)),
                  pl.BlockSpec((tm * SUBLANES, LANES), lambda i: (t0 + i, 0)),
                  pl.BlockSpec((None, 1, d), lambda i: (mod_row0 + i // tiles_per_mod, 0, 5)),
                  _const_spec((1, d)), _const_spec((1, d))],
        out_specs=pl.BlockSpec((tm, d), lambda i: (i, 0)),
        out_shape=jax.ShapeDtypeStruct((t, d), F32),
        compiler_params=_cparams("arbitrary"),
        name="combine_ln2",
    )(x1, y, mod3, ln_g.reshape(1, d), ln_b.reshape(1, d))


def _pick_tile(n, pref):
    tm = min(pref, n)
    while n % tm:
        tm //= 2
    return tm


def kernel(x, c, ctx, c_ctx, ada_w, ada_b, w_in, gm_ln_g, gm_ln_b, gm_ws, gm_bs, hy_conv_w, hy_conv_b,
           hy_f_w1, hy_f_b1, hy_f_w2, hy_f_b2, hy_f_w3, hy_f_b3, hy_skip, da_lq1, da_lk1, da_lq2, da_lk2,
           da_norm_g, p_a, p_b, p_c, w_out, ln1_g, ln1_b, moe_wg, moe_bg, moe_we, moe_be,
           ex_w_gate, ex_w_up, ex_w_down, ln2_g, ln2_b):
    B, L, D = x.shape
    Lc = ctx.shape[1]
    depth = ada_w.shape[0]
    alpha = (2.0 * depth) ** 0.25
    T, Tc = B * L, B * Lc
    moe_tm = 512 if T >= 8192 else 64

    mp = -(-(B + 1) // 8) * 8
    c_all = jnp.zeros((mp, D), F32).at[:B].set(c).at[B].set(c_ctx)
    mod = _modulation(c_all, ada_w, ada_b)

    rope_tabs = _rope_tables(L // GRID_W)
    cm, sm = _dft_tables(L)
    cm_bf, sm_bf = cm.astype(BF16), sm.astype(BF16)
    cmc, smc = _dft_tables(Lc)
    cmc_bf, smc_bf = cmc.astype(BF16), smc.astype(BF16)

    seg_all = ((OFF_GM, OFF_HY, "gm"), (OFF_HY, OFF_Q, "hy"), (OFF_Q, OFF_K, "q"), (OFF_K, OFF_V, "k"),
               (OFF_V, OFF_GATE, "v"), (OFF_GATE, OFF_GATE + N_BRANCH * D, "gate"))
    seg_kv = ((0, DA_QK_W, "k"), (DA_QK_W, DA_QK_W + DA_V_W, "v"))

    tm_l = _pick_tile(L, 512)
    tm_c = _pick_tile(Lc, 256)
    tq_l = _pick_tile(L, 512)
    tq_c = _pick_tile(Lc, 256)
    tm_m = _pick_tile(L, 512)
    tm_mc = _pick_tile(Tc, 512)

    xs = x.reshape(T, D)
    xc = ctx.reshape(Tc, D)
    for l in range(depth):
        last = l == depth - 1
        lam_init = 0.8 - 0.6 * math.exp(-0.3 * l)
        mod3 = mod[l].reshape(mp, 1, 6 * D)
        w_l = w_in[l].astype(BF16)
        lparams = [a[l].reshape(1, DA_HEAD_DIM) for a in (da_lq1, da_lk1, da_lq2, da_lk2)]
        norm_g = da_norm_g[l].reshape(1, DA_V_DIM)
        lp = {
            "gm_ln_g": gm_ln_g[l].reshape(1, GM_DIM), "gm_ln_b": gm_ln_b[l].reshape(1, GM_DIM),
            "gm_ws": gm_ws[l].reshape(GM_GROUPS * GM_CHUNK, GM_CHUNK).astype(BF16),
            "gm_bs": jnp.repeat(jnp.transpose(gm_bs[l]), GM_DIM // GM_GROUPS, axis=1),
            "p_a": p_a[l].astype(BF16), "p_b": p_b[l].astype(BF16), "p_c": p_c[l].astype(BF16),
            "w_out": w_out[l].astype(BF16),
            "ln1_g": ln1_g[l].reshape(1, D), "ln1_b": ln1_b[l].reshape(1, D),
            "w_router": _split_bf16(jnp.zeros((D, LANES), F32).at[:, :MOE_GROUPS].set(moe_wg[l])
                                    .at[:, MOE_GROUPS:MOE_GROUPS + MOE_N_EXPERTS].set(moe_we[l])),
            "b_router": jnp.zeros((1, LANES), F32).at[0, :MOE_GROUPS].set(moe_bg[l])
                           .at[0, MOE_GROUPS:MOE_GROUPS + MOE_N_EXPERTS].set(moe_be[l]),
        }
        fw = (hy_f_w1[l], hy_f_b1[l], hy_f_w2[l], hy_f_b2[l], hy_f_w3[l], hy_f_b3[l])

        zgm, zhy, q, k, v, gate = _inproj(xs, mod3, 0, L, w_l, seg_all, rope_tabs, L, tm_l)
        if last:
            k_c, v_c = _inproj(xc, mod3, B, Tc, w_l[:, OFF_K:OFF_GATE], seg_kv, None, Lc, tm_c)
        else:
            zgm_c, zhy_c, q_c, k_c, v_c, gate_c = _inproj(xc, mod3, B, Tc, w_l, seg_all, None, Lc, tm_c)
        y_c = _attention(q, [(k, v, L), (k_c, v_c, Lc)], lparams, norm_g, lam_init, B, L, tq_l)
        kre, kim, nyq = _hyena_filter_spectrum(L, cm, sm, *fw)
        y_b = _hyena(zhy, B, L, hy_conv_w[l], hy_conv_b[l], cm_bf, sm_bf, kre, kim, nyq, hy_skip[l])
        t_all = T if last else T + Tc
        x1, h2, route = _merge(zgm, y_b, y_c, gate, xs, mod3, 0, L, lp, alpha, tm_m, t_all, 0, None)

        if not last:
            yc_c = _attention(q_c, [(k_c, v_c, Lc)], lparams, norm_g, lam_init, B, Lc, tq_c)
            kre_c, kim_c, nyq_c = _hyena_filter_spectrum(Lc, cmc, smc, *fw)
            yb_c = _hyena(zhy_c, B, Lc, hy_conv_w[l], hy_conv_b[l], cmc_bf, smc_bf, kre_c, kim_c, nyq_c,
                          hy_skip[l])
            x1c, h2, route = _merge(zgm_c, yb_c, yc_c, gate_c, xc, mod3, B, Tc, lp, alpha, tm_mc, t_all, T,
                                    (h2, route))

        plan = _visit_plan(route, moe_tm)
        y = _experts(h2, plan, ex_w_gate, ex_w_up, ex_w_down, l, moe_tm)
        xs = _combine(x1, y, 0, mod3, 0, L, ln2_g[l], ln2_b[l], alpha, tm_c)
        if not last:
            xc = _combine(x1c, y, T, mod3, B, Tc, ln2_g[l], ln2_b[l], alpha, tm_c)
    return xs.reshape(B, L, D)
```

```python
import functools
import math

import numpy as np
import jax
import jax.numpy as jnp
from jax import lax
from jax.experimental import pallas as pl
from jax.experimental.pallas import tpu as pltpu

F32 = jnp.float32
BF16 = jnp.bfloat16
HIGHEST = lax.Precision.HIGHEST

GRID_W = 64
GM_DIM = 256
GM_GROUPS = 4
GM_CHUNK = 128
HY_DIM = 256
HY_EMB = 33
HY_BANDS = (HY_EMB - 1) // 2
HY_DECAY_FAST = 0.3
HY_DECAY_SLOW = 1.5
HY_DECAY_TARGET = 1e-2
HY_DECAY_SHIFT = 0.05
DA_HEADS = 4
DA_HEAD_DIM = 64
DA_V_DIM = 2 * DA_HEAD_DIM
DA_QK_W = DA_HEADS * 2 * DA_HEAD_DIM
DA_V_W = DA_HEADS * DA_V_DIM
ROPE_BASE = 10000.0
N_BRANCH = 3
OFF_GM = 0
OFF_HY = OFF_GM + 2 * GM_DIM
OFF_Q = OFF_HY + 3 * HY_DIM
OFF_K = OFF_Q + DA_QK_W
OFF_V = OFF_K + DA_QK_W
OFF_GATE = OFF_V + DA_V_W
MOE_GROUPS = 4
MOE_EXPERTS_PER_GROUP = 8
MOE_N_EXPERTS = MOE_GROUPS * MOE_EXPERTS_PER_GROUP
MOE_TOP_K = 2
LN_EPS = 1e-5
LANES = 128
VMEM_LIMIT = 56 * 1024 * 1024


def _cparams(*sem):
    return pltpu.CompilerParams(dimension_semantics=sem, vmem_limit_bytes=VMEM_LIMIT)


def _sigmoid(x):
    return 1.0 / (1.0 + jnp.exp(-x))


def _layer_norm(x, g, b):
    mu = jnp.mean(x, axis=-1, keepdims=True)
    xc = x - mu
    var = jnp.mean(xc * xc, axis=-1, keepdims=True)
    return xc * lax.rsqrt(var + LN_EPS) * g + b


def _gelu_tanh(x):
    return 0.5 * x * (1.0 + jnp.tanh(math.sqrt(2.0 / math.pi) * (x + 0.044715 * (x * x * x))))


def _const_spec(shape):
    nd = len(shape)
    return pl.BlockSpec(shape, lambda *_: (0,) * nd)


def _mod_kernel(c_ref, w_ref, b_ref, o_ref):
    c = c_ref[...]
    s = c * _sigmoid(c)
    o_ref[...] = jnp.dot(s, w_ref[...], precision=HIGHEST, preferred_element_type=F32) + b_ref[...]


def _modulation(c_all, ada_w, ada_b):
    depth, d, n = ada_w.shape
    mp = c_all.shape[0]
    tn = 512
    return pl.pallas_call(
        _mod_kernel,
        grid=(depth, n // tn),
        in_specs=[pl.BlockSpec((mp, d), lambda l, j: (0, 0)),
                  pl.BlockSpec((None, d, tn), lambda l, j: (l, 0, j)),
                  pl.BlockSpec((None, 1, tn), lambda l, j: (l, 0, j))],
        out_specs=pl.BlockSpec((None, mp, tn), lambda l, j: (l, 0, j)),
        out_shape=jax.ShapeDtypeStruct((depth, mp, n), F32),
        compiler_params=_cparams("arbitrary", "arbitrary"),
        name="adaln_mod",
    )(c_all, ada_w, ada_b.reshape(depth, 1, n))


def _rope_tables(rows):
    n_freq = DA_HEAD_DIM // 4
    row = jnp.broadcast_to(jnp.arange(rows)[:, None], (rows, GRID_W)).reshape(-1).astype(F32)
    col = jnp.broadcast_to(jnp.arange(GRID_W)[None, :], (rows, GRID_W)).reshape(-1).astype(F32)
    inv = ROPE_BASE ** (-jnp.arange(n_freq, dtype=F32) / n_freq)
    ang_r = row[:, None] * inv
    ang_c = col[:, None] * inv
    c64 = jnp.concatenate([jnp.cos(ang_r), jnp.cos(ang_r), jnp.cos(ang_c), jnp.cos(ang_c)], axis=-1)
    s64 = jnp.concatenate([-jnp.sin(ang_r), jnp.sin(ang_r), -jnp.sin(ang_c), jnp.sin(ang_c)], axis=-1)
    return jnp.tile(c64, (1, LANES // DA_HEAD_DIM)), jnp.tile(s64, (1, LANES // DA_HEAD_DIM))


def _rope_block(xb, cos, sin):
    lane = lax.broadcasted_iota(jnp.int32, xb.shape, 1)
    n_freq = DA_HEAD_DIM // 4
    first_half = (lane % (2 * n_freq)) < n_freq
    partner = jnp.where(first_half, pltpu.roll(xb, LANES - n_freq, 1), pltpu.roll(xb, n_freq, 1))
    return xb * cos + partner * sin


def _inproj_kernel(*refs, segs, use_rope, n_chunk):
    if use_rope:
        x_ref, sh_ref, sc_ref, w_ref, cos_ref, sin_ref = refs[:6]
        out_refs = refs[6:]
    else:
        x_ref, sh_ref, sc_ref, w_ref = refs[:4]
        out_refs = refs[4:]
    h = (x_ref[...] * (1.0 + sc_ref[...]) + sh_ref[...]).astype(BF16)
    for (a, b, kind), o_ref in zip(segs, out_refs):
        for c0 in range(a, b, n_chunk):
            c1 = min(c0 + n_chunk, b)
            acc = jnp.dot(h, w_ref[:, c0:c1], preferred_element_type=F32)
            if kind == "q":
                acc = acc * (DA_HEAD_DIM ** -0.5 * math.log2(math.e))
            if use_rope and kind in ("q", "k"):
                cos = cos_ref[...]
                sin = sin_ref[...]
                for j in range((c1 - c0) // LANES):
                    blk = _rope_block(acc[:, j * LANES:(j + 1) * LANES], cos, sin)
                    o_ref[:, c0 - a + j * LANES:c0 - a + (j + 1) * LANES] = blk.astype(o_ref.dtype)
            else:
                o_ref[:, c0 - a:c1 - a] = acc.astype(o_ref.dtype)


def _inproj(x2d, mod3, mod_row0, rows_per_mod, w, segs, rope_tabs, seq_len, tm):
    t, d = x2d.shape
    n = w.shape[1]
    use_rope = rope_tabs is not None
    tiles_per_mod = rows_per_mod // tm
    tiles_per_seq = seq_len // tm

    def mod_map(piece):
        return lambda i: (mod_row0 + i // tiles_per_mod, 0, piece)

    in_specs = [pl.BlockSpec((tm, d), lambda i: (i, 0)),
                pl.BlockSpec((None, 1, d), mod_map(0)),
                pl.BlockSpec((None, 1, d), mod_map(1)),
                pl.BlockSpec((d, n), lambda i: (0, 0), pipeline_mode=pl.Buffered(1))]
    args = [x2d, mod3, mod3, w]
    if use_rope:
        in_specs += [pl.BlockSpec((tm, LANES), lambda i: (i % tiles_per_seq, 0))] * 2
        args += list(rope_tabs)
    out_specs = [pl.BlockSpec((tm, b - a), lambda i: (i, 0)) for a, b, _ in segs]
    out_shape = [jax.ShapeDtypeStruct((t, b - a), BF16) for a, b, _ in segs]
    return pl.pallas_call(
        functools.partial(_inproj_kernel, segs=segs, use_rope=use_rope, n_chunk=512),
        grid=(t // tm,),
        in_specs=in_specs, out_specs=out_specs, out_shape=out_shape,
        compiler_params=_cparams("arbitrary"),
        name="inproj",
    )(*args)


ATTN_KEY_CHUNK = 256


def _attn_kernel(*refs, src_lens, lam_init):
    n_src = len(src_lens)
    lq1, lk1, lq2, lk2, g_ref, q_ref = refs[:6]
    kv_refs = refs[6:6 + 2 * n_src]
    o_ref, s_scr = refs[6 + 2 * n_src:]
    lam = (jnp.exp(jnp.sum(lq1[...] * lk1[...], axis=-1, keepdims=True))
           - jnp.exp(jnp.sum(lq2[...] * lk2[...], axis=-1, keepdims=True)) + lam_init)
    tq = q_ref.shape[0]
    lane = lax.broadcasted_iota(jnp.int32, (tq, LANES), 1)
    dn = (((1,), (1,)), ((), ()))
    chunks = []
    off = 0
    for j, n in enumerate(src_lens):
        kc = min(ATTN_KEY_CHUNK, n)
        for st in range(0, n, kc):
            chunks.append((j, st, kc, off))
            off += kc
    for h in range(DA_HEADS):
        cols = slice(h * LANES, (h + 1) * LANES)
        qh = q_ref[:, cols]
        zero = jnp.zeros_like(qh)
        om = []
        for m in range(2):
            qm = jnp.where(lane < DA_HEAD_DIM if m == 0 else lane >= DA_HEAD_DIM, qh, zero)
            mlane = None
            for j, st, kc, off in chunks:
                s_c = lax.dot_general(qm, kv_refs[2 * j][st:st + kc, cols], dn, preferred_element_type=F32)
                s_scr[:, off:off + kc] = s_c
                for b in range(kc // LANES):
                    blk = s_c[:, b * LANES:(b + 1) * LANES]
                    mlane = blk if mlane is None else jnp.maximum(mlane, blk)
            mx = jnp.max(mlane, axis=-1, keepdims=True)
            acc = None
            for j, st, kc, off in chunks:
                p = jnp.exp2(s_scr[:, off:off + kc] - mx).astype(BF16)
                v_aug = jnp.concatenate([kv_refs[2 * j + 1][st:st + kc, cols], jnp.ones((kc, LANES), BF16)],
                                        axis=1)
                d = jnp.dot(p, v_aug, preferred_element_type=F32)
                acc = d if acc is None else acc + d
            om.append(acc[:, :LANES] * (1.0 / acc[:, LANES:LANES + 1]))
        o = om[0] - lam * om[1]
        ms = jnp.mean(o * o, axis=-1, keepdims=True)
        o = o * lax.rsqrt(ms + LN_EPS) * g_ref[...] * (1.0 - lam_init)
        o_ref[:, cols] = o.astype(o_ref.dtype)


def _attention(q, kvs, lparams, norm_g, lam_init, nb, lq, tq):
    t = q.shape[0]
    qt = lq // tq
    in_specs = [_const_spec((1, DA_HEAD_DIM))] * 4 + [_const_spec((1, DA_V_DIM))]
    in_specs.append(pl.BlockSpec((tq, DA_QK_W), lambda b, i: (b * qt + i, 0)))
    args = list(lparams) + [norm_g, q]
    for k, v, lk in kvs:
        in_specs += [pl.BlockSpec((lk, DA_QK_W), lambda b, i: (b, 0)),
                     pl.BlockSpec((lk, DA_V_W), lambda b, i: (b, 0))]
        args += [k, v]
    src_lens = tuple(lk for _, _, lk in kvs)
    return pl.pallas_call(
        functools.partial(_attn_kernel, src_lens=src_lens, lam_init=lam_init),
        grid=(nb, qt),
        in_specs=in_specs,
        out_specs=pl.BlockSpec((tq, DA_V_W), lambda b, i: (b * qt + i, 0)),
        out_shape=jax.ShapeDtypeStruct((t, DA_V_W), BF16),
        scratch_shapes=[pltpu.VMEM((tq, sum(src_lens)), F32)],
        compiler_params=_cparams("arbitrary", "arbitrary"),
        name="diff_attn",
    )(*args)


DFT_SPLIT = 64


def _dft_tables(L):
    n = jnp.arange(L, dtype=jnp.int32)[None, :]
    k1 = jnp.arange(L // DFT_SPLIT, dtype=jnp.int32)[:, None] * DFT_SPLIT
    k0 = jnp.arange(DFT_SPLIT, dtype=jnp.int32)[:, None]
    ang_a = ((k1 * n) % (2 * L)).astype(F32) * (math.pi / L)
    ang_b = ((k0 * n) % (2 * L)).astype(F32) * (math.pi / L)
    ca, sa = jnp.cos(ang_a)[:, None, :], jnp.sin(ang_a)[:, None, :]
    cb, sb = jnp.cos(ang_b)[None, :, :], jnp.sin(ang_b)[None, :, :]
    return (ca * cb - sa * sb).reshape(L, L), (sa * cb + ca * sb).reshape(L, L)


def _filter_consts(L):
    t = jnp.linspace(0.0, 1.0, L, dtype=F32)[:, None]
    w = 2.0 * math.pi * jnp.arange(L, dtype=F32)[:, None] / L
    f = jnp.linspace(1e-4, HY_BANDS - 1, HY_BANDS, dtype=F32)[None, :]
    emb = jnp.concatenate([t, jnp.cos(f * w), -jnp.sin(f * w)], axis=-1)
    max_decay = math.log(HY_DECAY_TARGET) / HY_DECAY_FAST
    min_decay = math.log(HY_DECAY_TARGET) / HY_DECAY_SLOW
    deltas = jnp.abs(jnp.linspace(min_decay, max_decay, HY_DIM, dtype=F32))
    window = jnp.exp(-t * deltas[None, :]) + HY_DECAY_SHIFT
    return emb, window


def _filter_kernel(emb_ref, win_ref, w1, b1, w2, b2, w3, b3, hs_ref, hd_ref, nyq_ref):
    h = jnp.sin(jnp.dot(emb_ref[...], w1[...], precision=HIGHEST, preferred_element_type=F32) + b1[...])
    h = jnp.sin(jnp.dot(h, w2[...], precision=HIGHEST, preferred_element_type=F32) + b2[...])
    h = jnp.dot(h, w3[...], precision=HIGHEST, preferred_element_type=F32) + b3[...]
    win = win_ref[...]
    hf = h[:, :HY_DIM] * win
    hb = h[:, HY_DIM:] * win
    row = lax.broadcasted_iota(jnp.int32, hf.shape, 0)
    hb = jnp.where(row == 0, 0.0, hb)
    alt = jnp.where(row % 2 == 0, 1.0, -1.0)
    hs_ref[...] = hf + hb
    hd_ref[...] = hf - hb
    nyq_ref[...] = jnp.sum((hf + hb) * alt, axis=0, keepdims=True)


def _spectrum_kernel(c_ref, s_ref, hs_ref, hd_ref, kre_ref, kim_ref, *, n_fft):
    i = pl.program_id(0)
    tk = c_ref.shape[0]
    kidx = i * tk + lax.broadcasted_iota(jnp.int32, (tk, 1), 0)
    scale = jnp.where(kidx == 0, 1.0 / n_fft, 2.0 / n_fft)
    kre = jnp.dot(c_ref[...], hs_ref[...], precision=HIGHEST, preferred_element_type=F32)
    kim = -jnp.dot(s_ref[...], hd_ref[...], precision=HIGHEST, preferred_element_type=F32)
    kre_ref[...] = kre * scale
    kim_ref[...] = kim * scale


def _hyena_filter_spectrum(L, cmat, smat, w1, b1, w2, b2, w3, b3):
    emb, window = _filter_consts(L)
    full = lambda a: _const_spec(a.shape)
    ins = [emb, window, w1, b1.reshape(1, -1), w2, b2.reshape(1, -1), w3, b3.reshape(1, -1)]
    hs, hd, nyq = pl.pallas_call(
        _filter_kernel,
        grid=(1,),
        in_specs=[full(a) for a in ins],
        out_specs=[_const_spec((L, HY_DIM)), _const_spec((L, HY_DIM)), _const_spec((1, HY_DIM))],
        out_shape=[jax.ShapeDtypeStruct((L, HY_DIM), F32), jax.ShapeDtypeStruct((L, HY_DIM), F32),
                   jax.ShapeDtypeStruct((1, HY_DIM), F32)],
        compiler_params=_cparams("arbitrary"),
        name="hyena_filter",
    )(*ins)
    tk = min(256, L)
    kre, kim = pl.pallas_call(
        functools.partial(_spectrum_kernel, n_fft=2 * L),
        grid=(L // tk,),
        in_specs=[pl.BlockSpec((tk, L), lambda i: (i, 0)), pl.BlockSpec((tk, L), lambda i: (i, 0)),
                  _const_spec((L, HY_DIM)), _const_spec((L, HY_DIM))],
        out_specs=[pl.BlockSpec((tk, HY_DIM), lambda i: (i, 0))] * 2,
        out_shape=[jax.ShapeDtypeStruct((L, HY_DIM), F32)] * 2,
        compiler_params=_cparams("arbitrary"),
        name="hyena_spectrum",
    )(cmat, smat, hs, hd)
    return kre, kim, nyq * (1.0 / (2 * L))


HY_ROW_BLOCK = 1024


def _hyena_kernel(z_ref, cw_ref, cb_ref, c_ref, s_ref, kre_ref, kim_ref, nyq_ref, skip_ref, o_ref,
                  u_ref, x0_ref, p_ref, q_ref):
    L = z_ref.shape[0]
    row = lax.broadcasted_iota(jnp.int32, (L, HY_DIM), 0)

    def conv(j):
        cols = slice(j * HY_DIM, (j + 1) * HY_DIM)
        z = z_ref[:, cols].astype(F32)
        zprev = jnp.where(row == 0, 0.0, pltpu.roll(z, 1, 0))
        znext = jnp.where(row == L - 1, 0.0, pltpu.roll(z, L - 1, 0))
        return zprev * cw_ref[0:1, cols] + z * cw_ref[1:2, cols] + znext * cw_ref[2:3, cols] + cb_ref[:, cols]

    u = conv(2) * conv(1)
    ub = u.astype(BF16)
    u_ref[...] = u
    alt = jnp.where(row % 2 == 0, 1.0, -1.0)
    nyq_term = jnp.sum(u * alt, axis=0, keepdims=True) * nyq_ref[...]
    x0_ref[...] = conv(0)
    blk = min(HY_ROW_BLOCK, L)
    for r in range(0, L, blk):
        rows = slice(r, r + blk)
        a = jnp.dot(c_ref[rows, :], ub, preferred_element_type=F32)
        b = jnp.dot(s_ref[rows, :], ub, preferred_element_type=F32)
        kre = kre_ref[rows, :]
        kim = kim_ref[rows, :]
        p_ref[rows, :] = (a * kre + b * kim).astype(BF16)
        q_ref[rows, :] = (b * kre - a * kim).astype(BF16)
    for r in range(0, L, blk):
        rows = slice(r, r + blk)
        y = (jnp.dot(c_ref[rows, :], p_ref[...], preferred_element_type=F32)
             + jnp.dot(s_ref[rows, :], q_ref[...], preferred_element_type=F32))
        ub_rows = u_ref[rows, :]
        row_b = lax.broadcasted_iota(jnp.int32, (blk, HY_DIM), 0)
        y = y + jnp.where(row_b % 2 == 0, nyq_term, -nyq_term) + ub_rows * skip_ref[...]
        o_ref[rows, :] = (y * x0_ref[rows, :]).astype(o_ref.dtype)


def _hyena(zhy, nb, L, conv_w, conv_b, cmat_bf, smat_bf, kre, kim, nyq, skip):
    t = zhy.shape[0]
    return pl.pallas_call(
        _hyena_kernel,
        grid=(nb,),
        in_specs=[pl.BlockSpec((L, 3 * HY_DIM), lambda b: (b, 0)),
                  _const_spec((3, 3 * HY_DIM)), _const_spec((1, 3 * HY_DIM)),
                  pl.BlockSpec((L, L), lambda b: (0, 0), pipeline_mode=pl.Buffered(1)),
                  pl.BlockSpec((L, L), lambda b: (0, 0), pipeline_mode=pl.Buffered(1)),
                  _const_spec((L, HY_DIM)), _const_spec((L, HY_DIM)),
                  _const_spec((1, HY_DIM)), _const_spec((1, HY_DIM))],
        out_specs=pl.BlockSpec((L, HY_DIM), lambda b: (b, 0)),
        out_shape=jax.ShapeDtypeStruct((t, HY_DIM), BF16),
        scratch_shapes=[pltpu.VMEM((L, HY_DIM), F32), pltpu.VMEM((L, HY_DIM), F32),
                        pltpu.VMEM((L, HY_DIM), BF16), pltpu.VMEM((L, HY_DIM), BF16)],
        compiler_params=_cparams("arbitrary"),
        name="hyena_conv",
    )(zhy, conv_w, conv_b.reshape(1, -1), cmat_bf, smat_bf, kre, kim, nyq, skip.reshape(1, -1))


SUBLANES = 8


def _split_bf16(w):
    hi = w.astype(BF16)
    return jnp.stack([hi, (w - hi.astype(F32)).astype(BF16)])


def _store_token_tiles(ref, val):
    n = val.shape[0]
    for j in range(val.shape[1] // LANES):
        ref[pl.ds(j, n, stride=SUBLANES), :] = val[:, j * LANES:(j + 1) * LANES]


def _load_token_tiles(ref, n):
    return jnp.concatenate([ref[pl.ds(j, n, stride=SUBLANES), :] for j in range(SUBLANES)], axis=1)


def _merge_kernel(*refs, alpha, n_alias, n_real):
    h2_ref, rt_ref = refs[-2:]
    i = pl.program_id(0)

    @pl.when(i < n_real)
    def _():
        _merge_tile(*refs, alpha=alpha, n_alias=n_alias)

    @pl.when(i >= n_real)
    def _():
        h2_ref[...] = jnp.zeros_like(h2_ref)
        rt_ref[...] = jnp.zeros_like(rt_ref)


def _merge_tile(*refs, alpha, n_alias):
    (zgm_ref, yb_ref, yc_ref, gate_ref, x_ref, g1_ref, sh2_ref, sc2_ref, lng_ref, lnb_ref, ws_ref, bs_ref,
     pa_ref, pb_ref, pc_ref, wo_ref, l1g_ref, l1b_ref, wr_ref, br_ref) = refs[:20]
    x1_ref, h2_ref, rt_ref = refs[20 + n_alias:]
    tm = x_ref.shape[0]
    d = x_ref.shape[1]
    gm = _gelu_tanh(zgm_ref[...].astype(F32))
    u = gm[:, :GM_DIM]
    v = _layer_norm(gm[:, GM_DIM:], lng_ref[...], lnb_ref[...]).astype(BF16)
    lane_group = lax.broadcasted_iota(jnp.int32, (GM_CHUNK, GM_DIM), 1) // (GM_DIM // GM_GROUPS)
    ya = []
    for cidx in range(tm // GM_CHUNK):
        rows = slice(cidx * GM_CHUNK, (cidx + 1) * GM_CHUNK)
        r = jnp.dot(ws_ref[...], v[rows], preferred_element_type=F32)
        vv = bs_ref[...]
        for g in range(GM_GROUPS):
            vv = vv + jnp.where(lane_group == g, r[g * GM_CHUNK:(g + 1) * GM_CHUNK], 0.0)
        ya.append(u[rows] * vv)
    ya = jnp.concatenate(ya, axis=0) if len(ya) > 1 else ya[0]
    ma = jnp.dot(ya.astype(BF16), pa_ref[...], preferred_element_type=F32)
    mb = jnp.dot(yb_ref[...], pb_ref[...], preferred_element_type=F32)
    mc = jnp.dot(yc_ref[...], pc_ref[...], preferred_element_type=F32)
    merged = (_sigmoid(gate_ref[:, 0:d].astype(F32)) * ma
              + _sigmoid(gate_ref[:, d:2 * d].astype(F32)) * mb
              + _sigmoid(gate_ref[:, 2 * d:3 * d].astype(F32)) * mc)
    out = jnp.dot(merged.astype(BF16), wo_ref[...], preferred_element_type=F32)
    x1 = _layer_norm(alpha * x_ref[...] + g1_ref[...] * out, l1g_ref[...], l1b_ref[...])
    x1_ref[...] = x1
    h2 = x1 * (1.0 + sc2_ref[...]) + sh2_ref[...]
    _store_token_tiles(h2_ref, h2)
    h2_hi = h2.astype(BF16)
    h2_lo = (h2 - h2_hi.astype(F32)).astype(BF16)
    lg = (jnp.dot(h2_hi, wr_ref[0], preferred_element_type=F32)
          + jnp.dot(h2_hi, wr_ref[1], preferred_element_type=F32)
          + jnp.dot(h2_lo, wr_ref[0], preferred_element_type=F32) + br_ref[...])
    rt_ref[...] = _route(lg)


ROUTE_E0, ROUTE_E1, ROUTE_W0, ROUTE_W1 = 0, 1, 2, 3


def _route(lg):
    neg = jnp.float32(-3.0e38)
    lane_i = lax.broadcasted_iota(jnp.int32, lg.shape, 1)
    lane = lane_i.astype(F32)
    big = jnp.float32(LANES)
    is_g = lane_i < MOE_GROUPS
    gl = jnp.where(is_g, lg, neg)
    gmax = jnp.max(gl, axis=-1, keepdims=True)
    g_idx = jnp.min(jnp.where(gl == gmax, lane, big), axis=-1, keepdims=True)
    g_prob = 1.0 / jnp.sum(jnp.where(is_g, jnp.exp(gl - gmax), 0.0), axis=-1, keepdims=True)
    e_lo = MOE_GROUPS + MOE_EXPERTS_PER_GROUP * g_idx
    el = jnp.where(lane >= e_lo, jnp.where(lane < e_lo + MOE_EXPERTS_PER_GROUP, lg, neg), neg)
    v1 = jnp.max(el, axis=-1, keepdims=True)
    i1 = jnp.min(jnp.where(el == v1, lane, big), axis=-1, keepdims=True)
    el2 = jnp.where(lane == i1, neg, el)
    v2 = jnp.max(el2, axis=-1, keepdims=True)
    i2 = jnp.min(jnp.where(el2 == v2, lane, big), axis=-1, keepdims=True)
    e21 = jnp.exp(v2 - v1)
    w1 = g_prob / (1.0 + e21)
    w2 = w1 * e21
    swap = i2 < i1
    rec = jnp.where(lane_i == ROUTE_E0, jnp.minimum(i1, i2) - MOE_GROUPS, 0.0)
    rec = jnp.where(lane_i == ROUTE_E1, jnp.maximum(i1, i2) - MOE_GROUPS, rec)
    rec = jnp.where(lane_i == ROUTE_W0, jnp.where(swap, w2, w1), rec)
    return jnp.where(lane_i == ROUTE_W1, jnp.where(swap, w1, w2), rec)


def _merge(zgm, yb, yc, gate, x2d, mod3, mod_row0, rows_per_mod, lp, alpha, tm, t_all, row0, prev):
    t, d = x2d.shape
    tiles_per_mod = rows_per_mod // tm
    off = row0 // tm
    n_real = t // tm
    n_fill = (t_all - row0 - t) // tm if prev is None else 0
    real = lambda i: jnp.minimum(i, n_real - 1)

    def mod_map(piece):
        return lambda i: (mod_row0 + real(i) // tiles_per_mod, 0, piece)

    row = lambda w: pl.BlockSpec((tm, w), lambda i: (real(i), 0))
    row_off = lambda w: pl.BlockSpec((tm, w), lambda i: (off + i, 0))
    consts = [lp["gm_ln_g"], lp["gm_ln_b"], lp["gm_ws"], lp["gm_bs"], lp["p_a"], lp["p_b"], lp["p_c"],
              lp["w_out"], lp["ln1_g"], lp["ln1_b"], lp["w_router"], lp["b_router"]]
    in_specs = [row(2 * GM_DIM), row(HY_DIM), row(DA_V_W), row(N_BRANCH * d), row(d),
                pl.BlockSpec((None, 1, d), mod_map(2)), pl.BlockSpec((None, 1, d), mod_map(3)),
                pl.BlockSpec((None, 1, d), mod_map(4))] + [_const_spec(a.shape) for a in consts]
    args = [zgm, yb, yc, gate, x2d, mod3, mod3, mod3, *consts]
    aliases = {}
    if prev is not None:
        aliases = {len(args): 1, len(args) + 1: 2}
        in_specs += [pl.BlockSpec(memory_space=pl.ANY)] * 2
        args += list(prev)
    return pl.pallas_call(
        functools.partial(_merge_kernel, alpha=alpha, n_alias=len(aliases), n_real=n_real),
        grid=(n_real + n_fill,),
        in_specs=in_specs,
        out_specs=[row(d), pl.BlockSpec((tm * SUBLANES, LANES), lambda i: (off + i, 0)), row_off(LANES)],
        out_shape=[jax.ShapeDtypeStruct((t, d), F32), jax.ShapeDtypeStruct((t_all * SUBLANES, LANES), F32),
                   jax.ShapeDtypeStruct((t_all, LANES), F32)],
        input_output_aliases=aliases,
        compiler_params=_cparams("arbitrary"),
        name="merge_ln1",
    )(*args)


PAIR_CLASSES = tuple((MOE_EXPERTS_PER_GROUP * g + a, MOE_EXPERTS_PER_GROUP * g + b)
                     for g in range(MOE_GROUPS)
                     for a in range(MOE_EXPERTS_PER_GROUP) for b in range(a + 1, MOE_EXPERTS_PER_GROUP))
FLAG_FIRST, FLAG_LAST, FLAG_FINAL, FLAG_NEW_GROUP, FLAG_HI = 1, 2, 4, 8, 16


def _tile_copy(src, src_row, dst, dst_row, sem):
    s0 = pl.multiple_of(src_row * SUBLANES, SUBLANES)
    d0 = pl.multiple_of(dst_row * SUBLANES, SUBLANES)
    return pltpu.make_async_copy(src.at[pl.ds(s0, SUBLANES)], dst.at[pl.ds(d0, SUBLANES)], sem)


def _expert_kernel(vt_ref, ve_ref, vlo_ref, vhi_ref, vflag_ref, src_ref, nsrc_ref, dst_ref, pdst_ref,
                   wlo_ref, whi_ref, h2_hbm, wg_hbm, wu_hbm, wd_hbm, y_hbm,
                   xbuf, acc, ybuf, wg_grp, wu_grp, wd_grp, wg_stage, wu_stage, wd_stage, gsem, ssem, wsem,
                   *, n_tiles, layer):
    v = pl.program_id(0)
    tile, expert, lo, hi, flag = vt_ref[v], ve_ref[v], vlo_ref[v], vhi_ref[v], vflag_ref[v]
    tm = acc.shape[0]
    slot = tile % 2
    first = (flag & FLAG_FIRST) != 0
    group0 = pl.multiple_of((expert // MOE_EXPERTS_PER_GROUP) * MOE_EXPERTS_PER_GROUP, MOE_EXPERTS_PER_GROUP)
    e_in_group = expert - group0

    @pl.when((flag & FLAG_NEW_GROUP) != 0)
    def _():
        streams = ((wg_hbm, wg_stage, wg_grp), (wu_hbm, wu_stage, wu_grp), (wd_hbm, wd_stage, wd_grp))
        for k in range(MOE_EXPERTS_PER_GROUP):
            copies = [pltpu.make_async_copy(w_hbm.at[layer, group0 + k], stage, wsem.at[i])
                      for i, (w_hbm, stage, _) in enumerate(streams)]
            for cp in copies:
                cp.start()
            for cp, (_, stage, w_grp) in zip(copies, streams):
                cp.wait()
                w_grp[k] = stage[...].astype(BF16)

    def issue_gather(idx_ref, to_slot):
        def body(i, carry):
            for j in range(SUBLANES):
                r = i * SUBLANES + j
                _tile_copy(h2_hbm, idx_ref[0, r], xbuf.at[to_slot], r, gsem.at[to_slot]).start(priority=j % 2)
            return carry

        lax.fori_loop(0, tm // SUBLANES, body, 0)

    def issue_scatter(idx_ref):
        def body(i, carry):
            for j in range(SUBLANES):
                r = i * SUBLANES + j
                _tile_copy(ybuf, r, y_hbm, idx_ref[0, r], ssem).start(priority=j % 2)
            return carry

        lax.fori_loop(0, tm // SUBLANES, body, 0)

    has_next = tile + 1 < n_tiles
    prefetch = first & (tile > 0) & has_next

    @pl.when(first)
    def _():
        @pl.when(tile == 0)
        def _():
            issue_gather(src_ref, slot)

        pltpu.make_async_copy(h2_hbm.at[pl.ds(0, tm * SUBLANES)], xbuf.at[slot], gsem.at[slot]).wait()

        @pl.when(jnp.logical_not(prefetch))
        def _():
            @pl.when(has_next)
            def _():
                issue_gather(nsrc_ref, 1 - slot)

            @pl.when(tile > 0)
            def _():
                issue_scatter(pdst_ref)

    def compute(with_prefetch):
        xb = _load_token_tiles(xbuf.at[slot], tm).astype(BF16)
        if with_prefetch:
            for r in range(tm):
                _tile_copy(h2_hbm, nsrc_ref[0, r], xbuf.at[1 - slot], r, gsem.at[1 - slot]).start(priority=r % 2)
                _tile_copy(ybuf, r, y_hbm, pdst_ref[0, r], ssem).start(priority=r % 2)
        g = jnp.dot(xb, wg_grp[e_in_group], preferred_element_type=F32)
        u = jnp.dot(xb, wu_grp[e_in_group], preferred_element_type=F32)
        hmid = (g * _sigmoid(g) * u).astype(BF16)
        y = jnp.dot(hmid, wd_grp[e_in_group], preferred_element_type=F32)
        w_b = jnp.where((flag & FLAG_HI) != 0, whi_ref[...], wlo_ref[...])
        row = lax.broadcasted_iota(jnp.int32, (tm, 1), 0)
        y = jnp.where((row >= lo) & (row < hi), y * jnp.concatenate([w_b] * SUBLANES, axis=1), 0.0)
        if with_prefetch:
            acc[...] = y
            return

        @pl.when(first)
        def _():
            acc[...] = y

        @pl.when(jnp.logical_not(first))
        def _():
            acc[...] += y

    @pl.when(prefetch)
    def _():
        compute(True)

    @pl.when(jnp.logical_not(prefetch) & (hi > lo))
    def _():
        compute(False)

    @pl.when((flag & FLAG_LAST) != 0)
    def _():
        whole = pltpu.make_async_copy(ybuf, y_hbm.at[pl.ds(0, tm * SUBLANES)], ssem)

        @pl.when(tile > 0)
        def _():
            whole.wait()

        _store_token_tiles(ybuf, acc[...])

        @pl.when((flag & FLAG_FINAL) != 0)
        def _():
            issue_scatter(dst_ref)
            whole.wait()


def _experts(h2, plan, w_gate, w_up, w_down, layer, tm):
    t = h2.shape[0] // SUBLANES
    vt, ve, vlo, vhi, vflag, order, wlo_b, whi_b = plan
    n_vis = vt.shape[0]
    n_tiles = t // tm
    d, hid = w_gate.shape[-2:]
    grp = MOE_EXPERTS_PER_GROUP
    assert d == SUBLANES * LANES
    idx_spec = lambda nxt: pl.BlockSpec(
        (None, 1, tm), lambda v, vt, *_: (jnp.clip(vt[v] + nxt, 0, n_tiles - 1), 0, 0), memory_space=pltpu.SMEM)
    w_tile = pl.BlockSpec((tm, LANES), lambda v, vt, *_: (vt[v], 0))
    any_spec = pl.BlockSpec(memory_space=pl.ANY)
    grid_spec = pltpu.PrefetchScalarGridSpec(
        num_scalar_prefetch=5,
        grid=(n_vis,),
        in_specs=[idx_spec(0), idx_spec(1), idx_spec(0), idx_spec(-1), w_tile, w_tile,
                  any_spec, any_spec, any_spec, any_spec],
        out_specs=any_spec,
        scratch_shapes=[pltpu.VMEM((2, tm * SUBLANES, LANES), F32), pltpu.VMEM((tm, d), F32),
                        pltpu.VMEM((tm * SUBLANES, LANES), F32),
                        pltpu.VMEM((grp, d, hid), BF16), pltpu.VMEM((grp, d, hid), BF16),
                        pltpu.VMEM((grp, hid, d), BF16),
                        pltpu.VMEM((d, hid), F32), pltpu.VMEM((d, hid), F32), pltpu.VMEM((hid, d), F32),
                        pltpu.SemaphoreType.DMA((2,)), pltpu.SemaphoreType.DMA(()), pltpu.SemaphoreType.DMA((3,))],
    )
    idx3 = order.reshape(n_tiles, 1, tm)
    return pl.pallas_call(
        functools.partial(_expert_kernel, n_tiles=n_tiles, layer=layer),
        grid_spec=grid_spec,
        out_shape=jax.ShapeDtypeStruct((t * SUBLANES, LANES), F32),
        compiler_params=_cparams("arbitrary"),
        name="moe_experts",
    )(vt, ve, vlo, vhi, vflag, idx3, idx3, idx3, idx3, wlo_b, whi_b, h2, w_gate, w_up, w_down)


def _visit_plan(route, tm):
    t = route.shape[0]
    i32 = jnp.int32
    cls_lo = jnp.asarray(np.array([p[0] for p in PAIR_CLASSES], np.int32))
    cls_hi = jnp.asarray(np.array([p[1] for p in PAIR_CLASSES], np.int32))
    cls_key = cls_lo * MOE_N_EXPERTS + cls_hi
    key = route[:, ROUTE_E0].astype(i32) * MOE_N_EXPERTS + route[:, ROUTE_E1].astype(i32)
    idx_bits = max(1, (t - 1).bit_length())
    assert (MOE_N_EXPERTS * MOE_N_EXPERTS) << idx_bits <= 2 ** 31
    packed, wlo, whi = lax.sort((key * (1 << idx_bits) + lax.iota(i32, t), route[:, ROUTE_W0], route[:, ROUTE_W1]),
                                num_keys=1)
    order = packed & ((1 << idx_bits) - 1)
    counts = jnp.sum((key[:, None] == cls_key[None, :]).astype(i32), axis=0)
    n_tiles = t // tm

    def segment_visits(seg_counts, seg_expert):
        n_seg = seg_counts.shape[0]
        ends = jnp.cumsum(seg_counts)
        starts = ends - seg_counts
        first_t = starts // tm
        nvis = jnp.where(seg_counts > 0, jnp.maximum(ends - 1, 0) // tm - first_t + 1, 0)
        cv_end = jnp.cumsum(nvis)
        cv_start = cv_end - nvis
        v = jnp.arange(n_tiles + n_seg, dtype=i32)
        active = v < cv_end[-1]
        s = jnp.minimum(jnp.sum((cv_end[None, :] <= v[:, None]).astype(i32), axis=1), n_seg - 1)
        tile = first_t[s] + v - cv_start[s]
        lo = jnp.clip(starts[s] - tile * tm, 0, tm)
        hi = jnp.clip(ends[s] - tile * tm, 0, tm)
        return tile, seg_expert[s], lo, hi, active

    run_id = np.cumsum([0] + [int(a[0] != b[0]) for a, b in zip(PAIR_CLASSES[:-1], PAIR_CLASSES[1:])])
    run_lo = jnp.asarray(np.array([PAIR_CLASSES[list(run_id).index(r)][0] for r in range(run_id[-1] + 1)], np.int32))
    in_run = jnp.asarray(run_id[None, :] == np.arange(run_id[-1] + 1)[:, None])
    run_counts = jnp.sum(jnp.where(in_run, counts[None, :], 0), axis=1)
    parts = [segment_visits(run_counts, run_lo) + (0,), segment_visits(counts, cls_hi) + (1,)]
    tile = jnp.concatenate([p[0] for p in parts])
    e = jnp.concatenate([p[1] for p in parts])
    lo = jnp.concatenate([p[2] for p in parts])
    hi = jnp.concatenate([p[3] for p in parts])
    active = jnp.concatenate([p[4] for p in parts])
    is_hi = jnp.concatenate([jnp.full(p[0].shape, p[5], i32) for p in parts])
    tile_stride = 2 * (tm + 1)
    order_key = jnp.where(active, tile * tile_stride + lo * 2 + is_hi, jnp.iinfo(jnp.int32).max)
    _, tile, e, lo, hi, is_hi, active = lax.sort((order_key, tile, e, lo, hi, is_hi, active.astype(i32)), num_keys=1)
    active = active == 1
    n_active = jnp.sum(active.astype(i32))
    v = jnp.arange(tile.shape[0], dtype=i32)
    e_last = jnp.max(jnp.where(active, v, -1))
    e = jnp.where(active, e, e[e_last])
    tile = jnp.where(active, tile, n_tiles - 1)
    lo = jnp.where(active, lo, 0)
    hi = jnp.where(active, hi, 0)
    prev_t = jnp.concatenate([jnp.full((1,), -1, i32), tile[:-1]])
    next_t = jnp.concatenate([tile[1:], jnp.full((1,), -1, i32)])
    is_final = v == n_active - 1
    is_last = (next_t != tile) | is_final
    group = e // MOE_EXPERTS_PER_GROUP
    prev_g = jnp.concatenate([jnp.full((1,), -1, i32), group[:-1]])
    flag = jnp.where(active, (prev_t != tile) * FLAG_FIRST + is_last * FLAG_LAST + is_final * FLAG_FINAL
                     + (prev_g != group) * FLAG_NEW_GROUP + is_hi * FLAG_HI, 0)
    cast = lambda z: z.astype(i32)
    bcast = lambda w: jnp.broadcast_to(w[:, None], (t, LANES))
    return cast(tile), cast(e), cast(lo), cast(hi), cast(flag), cast(order), bcast(wlo), bcast(whi)


def _combine_kernel(x_ref, y_ref, g2_ref, lg_ref, lb_ref, o_ref, *, alpha):
    y = _load_token_tiles(y_ref, x_ref.shape[0])
    o_ref[...] = _layer_norm(alpha * x_ref[...] + g2_ref[...] * y, lg_ref[...], lb_ref[...])


def _combine(x1, y, row0, mod3, mod_row0, rows_per_mod, ln_g, ln_b, alpha, tm):
    t, d = x1.shape
    tiles_per_mod = rows_per_mod // tm
    t0 = row0 // tm
    return pl.pallas_call(
        functools.partial(_combine_kernel, alpha=alpha),
        grid=(t // tm,),
        in_specs=[pl.BlockSpec((tm, d), lambda i: (i, 0)),
                  pl.BlockSpec((tm * SUBLANES, LANES), lambda i: (t0 + i, 0)),
                  pl.BlockSpec((None, 1, d), lambda i: (mod_row0 + i // tiles_per_mod, 0, 5)),
                  _const_spec((1, d)), _const_spec((1, d))],
        out_specs=pl.BlockSpec((tm, d), lambda i: (i, 0)),
        out_shape=jax.ShapeDtypeStruct((t, d), F32),
        compiler_params=_cparams("arbitrary"),
        name="combine_ln2",
    )(x1, y, mod3, ln_g.reshape(1, d), ln_b.reshape(1, d))


ROW_TILE = 512
SMALL_ROW_TILE = 256


def _pick_tile(n, pref):
    tm = min(pref, n)
    while n % tm:
        tm //= 2
    return tm


def kernel(x, c, ctx, c_ctx, ada_w, ada_b, w_in, gm_ln_g, gm_ln_b, gm_ws, gm_bs, hy_conv_w, hy_conv_b,
           hy_f_w1, hy_f_b1, hy_f_w2, hy_f_b2, hy_f_w3, hy_f_b3, hy_skip, da_lq1, da_lk1, da_lq2, da_lk2,
           da_norm_g, p_a, p_b, p_c, w_out, ln1_g, ln1_b, moe_wg, moe_bg, moe_we, moe_be,
           ex_w_gate, ex_w_up, ex_w_down, ln2_g, ln2_b):
    B, L, D = x.shape
    Lc = ctx.shape[1]
    depth = ada_w.shape[0]
    alpha = (2.0 * depth) ** 0.25
    T, Tc = B * L, B * Lc

    mp = -(-(B + 1) // 8) * 8
    c_all = jnp.zeros((mp, D), F32).at[:B].set(c).at[B].set(c_ctx)
    mod = _modulation(c_all, ada_w, ada_b)

    rope_tabs = _rope_tables(L // GRID_W)
    cm, sm = _dft_tables(L)
    cm_bf, sm_bf = cm.astype(BF16), sm.astype(BF16)
    cmc, smc = _dft_tables(Lc)
    cmc_bf, smc_bf = cmc.astype(BF16), smc.astype(BF16)

    seg_all = ((OFF_GM, OFF_HY, "gm"), (OFF_HY, OFF_Q, "hy"), (OFF_Q, OFF_K, "q"), (OFF_K, OFF_V, "k"),
               (OFF_V, OFF_GATE, "v"), (OFF_GATE, OFF_GATE + N_BRANCH * D, "gate"))
    seg_kv = ((0, DA_QK_W, "k"), (DA_QK_W, DA_QK_W + DA_V_W, "v"))

    tm_l = _pick_tile(L, ROW_TILE)
    tm_c = _pick_tile(Lc, SMALL_ROW_TILE)
    tq_l = _pick_tile(L, ROW_TILE)
    tq_c = _pick_tile(Lc, SMALL_ROW_TILE)
    tm_m = _pick_tile(L, ROW_TILE)
    tm_mc = _pick_tile(Tc, ROW_TILE)

    xs = x.reshape(T, D)
    xc = ctx.reshape(Tc, D)
    for l in range(depth):
        last = l == depth - 1
        lam_init = 0.8 - 0.6 * math.exp(-0.3 * l)
        mod3 = mod[l].reshape(mp, 1, 6 * D)
        w_l = w_in[l].astype(BF16)
        lparams = [a[l].reshape(1, DA_HEAD_DIM) for a in (da_lq1, da_lk1, da_lq2, da_lk2)]
        norm_g = da_norm_g[l].reshape(1, DA_V_DIM)
        lp = {
            "gm_ln_g": gm_ln_g[l].reshape(1, GM_DIM), "gm_ln_b": gm_ln_b[l].reshape(1, GM_DIM),
            "gm_ws": gm_ws[l].reshape(GM_GROUPS * GM_CHUNK, GM_CHUNK).astype(BF16),
            "gm_bs": jnp.repeat(jnp.transpose(gm_bs[l]), GM_DIM // GM_GROUPS, axis=1),
            "p_a": p_a[l].astype(BF16), "p_b": p_b[l].astype(BF16), "p_c": p_c[l].astype(BF16),
            "w_out": w_out[l].astype(BF16),
            "ln1_g": ln1_g[l].reshape(1, D), "ln1_b": ln1_b[l].reshape(1, D),
            "w_router": _split_bf16(jnp.zeros((D, LANES), F32).at[:, :MOE_GROUPS].set(moe_wg[l])
                                    .at[:, MOE_GROUPS:MOE_GROUPS + MOE_N_EXPERTS].set(moe_we[l])),
            "b_router": jnp.zeros((1, LANES), F32).at[0, :MOE_GROUPS].set(moe_bg[l])
                           .at[0, MOE_GROUPS:MOE_GROUPS + MOE_N_EXPERTS].set(moe_be[l]),
        }
        fw = (hy_f_w1[l], hy_f_b1[l], hy_f_w2[l], hy_f_b2[l], hy_f_w3[l], hy_f_b3[l])

        zgm, zhy, q, k, v, gate = _inproj(xs, mod3, 0, L, w_l, seg_all, rope_tabs, L, tm_l)
        if last:
            k_c, v_c = _inproj(xc, mod3, B, Tc, w_l[:, OFF_K:OFF_GATE], seg_kv, None, Lc, tm_c)
        else:
            zgm_c, zhy_c, q_c, k_c, v_c, gate_c = _inproj(xc, mod3, B, Tc, w_l, seg_all, None, Lc, tm_c)
        y_c = _attention(q, [(k, v, L), (k_c, v_c, Lc)], lparams, norm_g, lam_init, B, L, tq_l)
        kre, kim, nyq = _hyena_filter_spectrum(L, cm, sm, *fw)
        y_b = _hyena(zhy, B, L, hy_conv_w[l], hy_conv_b[l], cm_bf, sm_bf, kre, kim, nyq, hy_skip[l])
        t_all = T if last else T + Tc
        x1, h2, route = _merge(zgm, y_b, y_c, gate, xs, mod3, 0, L, lp, alpha, tm_m, t_all, 0, None)

        if not last:
            yc_c = _attention(q_c, [(k_c, v_c, Lc)], lparams, norm_g, lam_init, B, Lc, tq_c)
            kre_c, kim_c, nyq_c = _hyena_filter_spectrum(Lc, cmc, smc, *fw)
            yb_c = _hyena(zhy_c, B, Lc, hy_conv_w[l], hy_conv_b[l], cmc_bf, smc_bf, kre_c, kim_c, nyq_c,
                          hy_skip[l])
            x1c, h2, route = _merge(zgm_c, yb_c, yc_c, gate_c, xc, mod3, B, Tc, lp, alpha, tm_mc, t_all, T,
                                    (h2, route))

        moe_tm = _pick_tile(t_all, ROW_TILE)
        plan = _visit_plan(route, moe_tm)
        y = _experts(h2, plan, ex_w_gate, ex_w_up, ex_w_down, l, moe_tm)
        xs = _combine(x1, y, 0, mod3, 0, L, ln2_g[l], ln2_b[l], alpha, tm_c)
        if not last:
            xc = _combine(x1c, y, T, mod3, B, Tc, ln2_g[l], ln2_b[l], alpha, tm_c)
    return xs.reshape(B, L, D)
```

```python
import functools
import math

import numpy as np
import jax
import jax.numpy as jnp
from jax import lax
from jax.experimental import pallas as pl
from jax.experimental.pallas import tpu as pltpu

F32 = jnp.float32
BF16 = jnp.bfloat16
HIGHEST = lax.Precision.HIGHEST

GRID_W = 64
GM_DIM = 256
GM_GROUPS = 4
GM_CHUNK = 128
HY_DIM = 256
HY_EMB = 33
HY_BANDS = (HY_EMB - 1) // 2
HY_DECAY_FAST = 0.3
HY_DECAY_SLOW = 1.5
HY_DECAY_TARGET = 1e-2
HY_DECAY_SHIFT = 0.05
DA_HEADS = 4
DA_HEAD_DIM = 64
DA_V_DIM = 2 * DA_HEAD_DIM
DA_QK_W = DA_HEADS * 2 * DA_HEAD_DIM
DA_V_W = DA_HEADS * DA_V_DIM
ROPE_BASE = 10000.0
N_BRANCH = 3
OFF_GM = 0
OFF_HY = OFF_GM + 2 * GM_DIM
OFF_Q = OFF_HY + 3 * HY_DIM
OFF_K = OFF_Q + DA_QK_W
OFF_V = OFF_K + DA_QK_W
OFF_GATE = OFF_V + DA_V_W
MOE_GROUPS = 4
MOE_EXPERTS_PER_GROUP = 8
MOE_N_EXPERTS = MOE_GROUPS * MOE_EXPERTS_PER_GROUP
MOE_TOP_K = 2
LN_EPS = 1e-5
LANES = 128
VMEM_LIMIT = 56 * 1024 * 1024


def _cparams(*sem):
    return pltpu.CompilerParams(dimension_semantics=sem, vmem_limit_bytes=VMEM_LIMIT)


def _sigmoid(x):
    return 1.0 / (1.0 + jnp.exp(-x))


def _layer_norm(x, g, b):
    mu = jnp.mean(x, axis=-1, keepdims=True)
    xc = x - mu
    var = jnp.mean(xc * xc, axis=-1, keepdims=True)
    return xc * lax.rsqrt(var + LN_EPS) * g + b


def _gelu_tanh(x):
    return 0.5 * x * (1.0 + jnp.tanh(math.sqrt(2.0 / math.pi) * (x + 0.044715 * (x * x * x))))


def _const_spec(shape):
    nd = len(shape)
    return pl.BlockSpec(shape, lambda *_: (0,) * nd)


def _mod_kernel(c_ref, w_ref, b_ref, o_ref):
    c = c_ref[...]
    s = c * _sigmoid(c)
    o_ref[...] = jnp.dot(s, w_ref[...], precision=HIGHEST, preferred_element_type=F32) + b_ref[...]


def _modulation(c_all, ada_w, ada_b):
    depth, d, n = ada_w.shape
    mp = c_all.shape[0]
    tn = 512
    return pl.pallas_call(
        _mod_kernel,
        grid=(depth, n // tn),
        in_specs=[pl.BlockSpec((mp, d), lambda l, j: (0, 0)),
                  pl.BlockSpec((None, d, tn), lambda l, j: (l, 0, j)),
                  pl.BlockSpec((None, 1, tn), lambda l, j: (l, 0, j))],
        out_specs=pl.BlockSpec((None, mp, tn), lambda l, j: (l, 0, j)),
        out_shape=jax.ShapeDtypeStruct((depth, mp, n), F32),
        compiler_params=_cparams("arbitrary", "arbitrary"),
        name="adaln_mod",
    )(c_all, ada_w, ada_b.reshape(depth, 1, n))


def _rope_tables(rows):
    n_freq = DA_HEAD_DIM // 4
    row = jnp.broadcast_to(jnp.arange(rows)[:, None], (rows, GRID_W)).reshape(-1).astype(F32)
    col = jnp.broadcast_to(jnp.arange(GRID_W)[None, :], (rows, GRID_W)).reshape(-1).astype(F32)
    inv = ROPE_BASE ** (-jnp.arange(n_freq, dtype=F32) / n_freq)
    ang_r = row[:, None] * inv
    ang_c = col[:, None] * inv
    c64 = jnp.concatenate([jnp.cos(ang_r), jnp.cos(ang_r), jnp.cos(ang_c), jnp.cos(ang_c)], axis=-1)
    s64 = jnp.concatenate([-jnp.sin(ang_r), jnp.sin(ang_r), -jnp.sin(ang_c), jnp.sin(ang_c)], axis=-1)
    return jnp.tile(c64, (1, LANES // DA_HEAD_DIM)), jnp.tile(s64, (1, LANES // DA_HEAD_DIM))


def _rope_block(xb, cos, sin):
    lane = lax.broadcasted_iota(jnp.int32, xb.shape, 1)
    n_freq = DA_HEAD_DIM // 4
    first_half = (lane % (2 * n_freq)) < n_freq
    partner = jnp.where(first_half, pltpu.roll(xb, LANES - n_freq, 1), pltpu.roll(xb, n_freq, 1))
    return xb * cos + partner * sin


def _inproj_kernel(*refs, segs, use_rope, n_chunk):
    if use_rope:
        x_ref, sh_ref, sc_ref, w_ref, cos_ref, sin_ref = refs[:6]
        out_refs = refs[6:]
    else:
        x_ref, sh_ref, sc_ref, w_ref = refs[:4]
        out_refs = refs[4:]
    h = (x_ref[...] * (1.0 + sc_ref[...]) + sh_ref[...]).astype(BF16)
    for (a, b, kind), o_ref in zip(segs, out_refs):
        for c0 in range(a, b, n_chunk):
            c1 = min(c0 + n_chunk, b)
            acc = jnp.dot(h, w_ref[:, c0:c1], preferred_element_type=F32)
            if kind == "q":
                acc = acc * (DA_HEAD_DIM ** -0.5 * math.log2(math.e))
            if use_rope and kind in ("q", "k"):
                cos = cos_ref[...]
                sin = sin_ref[...]
                for j in range((c1 - c0) // LANES):
                    blk = _rope_block(acc[:, j * LANES:(j + 1) * LANES], cos, sin)
                    o_ref[:, c0 - a + j * LANES:c0 - a + (j + 1) * LANES] = blk.astype(o_ref.dtype)
            else:
                o_ref[:, c0 - a:c1 - a] = acc.astype(o_ref.dtype)


def _inproj(x2d, mod3, mod_row0, rows_per_mod, w, segs, rope_tabs, seq_len, tm):
    t, d = x2d.shape
    n = w.shape[1]
    use_rope = rope_tabs is not None
    tiles_per_mod = rows_per_mod // tm
    tiles_per_seq = seq_len // tm

    def mod_map(piece):
        return lambda i: (mod_row0 + i // tiles_per_mod, 0, piece)

    in_specs = [pl.BlockSpec((tm, d), lambda i: (i, 0)),
                pl.BlockSpec((None, 1, d), mod_map(0)),
                pl.BlockSpec((None, 1, d), mod_map(1)),
                pl.BlockSpec((d, n), lambda i: (0, 0), pipeline_mode=pl.Buffered(1))]
    args = [x2d, mod3, mod3, w]
    if use_rope:
        in_specs += [pl.BlockSpec((tm, LANES), lambda i: (i % tiles_per_seq, 0))] * 2
        args += list(rope_tabs)
    out_specs = [pl.BlockSpec((tm, b - a), lambda i: (i, 0)) for a, b, _ in segs]
    out_shape = [jax.ShapeDtypeStruct((t, b - a), BF16) for a, b, _ in segs]
    return pl.pallas_call(
        functools.partial(_inproj_kernel, segs=segs, use_rope=use_rope, n_chunk=512),
        grid=(t // tm,),
        in_specs=in_specs, out_specs=out_specs, out_shape=out_shape,
        compiler_params=_cparams("arbitrary"),
        name="inproj",
    )(*args)


ATTN_KEY_CHUNK = 256


def _attn_kernel(*refs, src_lens, lam_init):
    n_src = len(src_lens)
    lq1, lk1, lq2, lk2, g_ref, q_ref = refs[:6]
    kv_refs = refs[6:6 + 2 * n_src]
    o_ref, s_scr = refs[6 + 2 * n_src:]
    lam = (jnp.exp(jnp.sum(lq1[...] * lk1[...], axis=-1, keepdims=True))
           - jnp.exp(jnp.sum(lq2[...] * lk2[...], axis=-1, keepdims=True)) + lam_init)
    tq = q_ref.shape[0]
    lane = lax.broadcasted_iota(jnp.int32, (tq, LANES), 1)
    dn = (((1,), (1,)), ((), ()))
    chunks = []
    off = 0
    for j, n in enumerate(src_lens):
        kc = min(ATTN_KEY_CHUNK, n)
        for st in range(0, n, kc):
            chunks.append((j, st, kc, off))
            off += kc
    for h in range(DA_HEADS):
        cols = slice(h * LANES, (h + 1) * LANES)
        qh = q_ref[:, cols]
        zero = jnp.zeros_like(qh)
        om = []
        for m in range(2):
            qm = jnp.where(lane < DA_HEAD_DIM if m == 0 else lane >= DA_HEAD_DIM, qh, zero)
            mlane = None
            for j, st, kc, off in chunks:
                s_c = lax.dot_general(qm, kv_refs[2 * j][st:st + kc, cols], dn, preferred_element_type=F32)
                s_scr[:, off:off + kc] = s_c
                for b in range(kc // LANES):
                    blk = s_c[:, b * LANES:(b + 1) * LANES]
                    mlane = blk if mlane is None else jnp.maximum(mlane, blk)
            mx = jnp.max(mlane, axis=-1, keepdims=True)
            acc = None
            for j, st, kc, off in chunks:
                p = jnp.exp2(s_scr[:, off:off + kc] - mx).astype(BF16)
                v_aug = jnp.concatenate([kv_refs[2 * j + 1][st:st + kc, cols], jnp.ones((kc, LANES), BF16)],
                                        axis=1)
                d = jnp.dot(p, v_aug, preferred_element_type=F32)
                acc = d if acc is None else acc + d
            om.append(acc[:, :LANES] * (1.0 / acc[:, LANES:LANES + 1]))
        o = om[0] - lam * om[1]
        ms = jnp.mean(o * o, axis=-1, keepdims=True)
        o = o * lax.rsqrt(ms + LN_EPS) * g_ref[...] * (1.0 - lam_init)
        o_ref[:, cols] = o.astype(o_ref.dtype)


def _attention(q, kvs, lparams, norm_g, lam_init, nb, lq, tq):
    t = q.shape[0]
    qt = lq // tq
    in_specs = [_const_spec((1, DA_HEAD_DIM))] * 4 + [_const_spec((1, DA_V_DIM))]
    in_specs.append(pl.BlockSpec((tq, DA_QK_W), lambda b, i: (b * qt + i, 0)))
    args = list(lparams) + [norm_g, q]
    for k, v, lk in kvs:
        in_specs += [pl.BlockSpec((lk, DA_QK_W), lambda b, i: (b, 0)),
                     pl.BlockSpec((lk, DA_V_W), lambda b, i: (b, 0))]
        args += [k, v]
    src_lens = tuple(lk for _, _, lk in kvs)
    return pl.pallas_call(
        functools.partial(_attn_kernel, src_lens=src_lens, lam_init=lam_init),
        grid=(nb, qt),
        in_specs=in_specs,
        out_specs=pl.BlockSpec((tq, DA_V_W), lambda b, i: (b * qt + i, 0)),
        out_shape=jax.ShapeDtypeStruct((t, DA_V_W), BF16),
        scratch_shapes=[pltpu.VMEM((tq, sum(src_lens)), F32)],
        compiler_params=_cparams("arbitrary", "arbitrary"),
        name="diff_attn",
    )(*args)


DFT_SPLIT = 64


def _dft_tables(L):
    n = jnp.arange(L, dtype=jnp.int32)[None, :]
    k1 = jnp.arange(L // DFT_SPLIT, dtype=jnp.int32)[:, None] * DFT_SPLIT
    k0 = jnp.arange(DFT_SPLIT, dtype=jnp.int32)[:, None]
    ang_a = ((k1 * n) % (2 * L)).astype(F32) * (math.pi / L)
    ang_b = ((k0 * n) % (2 * L)).astype(F32) * (math.pi / L)
    ca, sa = jnp.cos(ang_a)[:, None, :], jnp.sin(ang_a)[:, None, :]
    cb, sb = jnp.cos(ang_b)[None, :, :], jnp.sin(ang_b)[None, :, :]
    return (ca * cb - sa * sb).reshape(L, L), (sa * cb + ca * sb).reshape(L, L)


def _filter_consts(L):
    t = jnp.linspace(0.0, 1.0, L, dtype=F32)[:, None]
    w = 2.0 * math.pi * jnp.arange(L, dtype=F32)[:, None] / L
    f = jnp.linspace(1e-4, HY_BANDS - 1, HY_BANDS, dtype=F32)[None, :]
    emb = jnp.concatenate([t, jnp.cos(f * w), -jnp.sin(f * w)], axis=-1)
    max_decay = math.log(HY_DECAY_TARGET) / HY_DECAY_FAST
    min_decay = math.log(HY_DECAY_TARGET) / HY_DECAY_SLOW
    deltas = jnp.abs(jnp.linspace(min_decay, max_decay, HY_DIM, dtype=F32))
    window = jnp.exp(-t * deltas[None, :]) + HY_DECAY_SHIFT
    return emb, window


def _filter_kernel(emb_ref, win_ref, w1, b1, w2, b2, w3, b3, hs_ref, hd_ref, nyq_ref):
    h = jnp.sin(jnp.dot(emb_ref[...], w1[...], precision=HIGHEST, preferred_element_type=F32) + b1[...])
    h = jnp.sin(jnp.dot(h, w2[...], precision=HIGHEST, preferred_element_type=F32) + b2[...])
    h = jnp.dot(h, w3[...], precision=HIGHEST, preferred_element_type=F32) + b3[...]
    win = win_ref[...]
    hf = h[:, :HY_DIM] * win
    hb = h[:, HY_DIM:] * win
    row = lax.broadcasted_iota(jnp.int32, hf.shape, 0)
    hb = jnp.where(row == 0, 0.0, hb)
    alt = jnp.where(row % 2 == 0, 1.0, -1.0)
    hs_ref[...] = hf + hb
    hd_ref[...] = hf - hb
    nyq_ref[...] = jnp.sum((hf + hb) * alt, axis=0, keepdims=True)


def _spectrum_kernel(c_ref, s_ref, hs_ref, hd_ref, kre_ref, kim_ref, *, n_fft):
    i = pl.program_id(0)
    tk = c_ref.shape[0]
    kidx = i * tk + lax.broadcasted_iota(jnp.int32, (tk, 1), 0)
    scale = jnp.where(kidx == 0, 1.0 / n_fft, 2.0 / n_fft)
    kre = jnp.dot(c_ref[...], hs_ref[...], precision=HIGHEST, preferred_element_type=F32)
    kim = -jnp.dot(s_ref[...], hd_ref[...], precision=HIGHEST, preferred_element_type=F32)
    kre_ref[...] = kre * scale
    kim_ref[...] = kim * scale


def _hyena_filter_spectrum(L, cmat, smat, w1, b1, w2, b2, w3, b3):
    emb, window = _filter_consts(L)
    full = lambda a: _const_spec(a.shape)
    ins = [emb, window, w1, b1.reshape(1, -1), w2, b2.reshape(1, -1), w3, b3.reshape(1, -1)]
    hs, hd, nyq = pl.pallas_call(
        _filter_kernel,
        grid=(1,),
        in_specs=[full(a) for a in ins],
        out_specs=[_const_spec((L, HY_DIM)), _const_spec((L, HY_DIM)), _const_spec((1, HY_DIM))],
        out_shape=[jax.ShapeDtypeStruct((L, HY_DIM), F32), jax.ShapeDtypeStruct((L, HY_DIM), F32),
                   jax.ShapeDtypeStruct((1, HY_DIM), F32)],
        compiler_params=_cparams("arbitrary"),
        name="hyena_filter",
    )(*ins)
    tk = min(256, L)
    kre, kim = pl.pallas_call(
        functools.partial(_spectrum_kernel, n_fft=2 * L),
        grid=(L // tk,),
        in_specs=[pl.BlockSpec((tk, L), lambda i: (i, 0)), pl.BlockSpec((tk, L), lambda i: (i, 0)),
                  _const_spec((L, HY_DIM)), _const_spec((L, HY_DIM))],
        out_specs=[pl.BlockSpec((tk, HY_DIM), lambda i: (i, 0))] * 2,
        out_shape=[jax.ShapeDtypeStruct((L, HY_DIM), F32)] * 2,
        compiler_params=_cparams("arbitrary"),
        name="hyena_spectrum",
    )(cmat, smat, hs, hd)
    return kre, kim, nyq * (1.0 / (2 * L))


HY_ROW_BLOCK = 1024


def _hyena_kernel(z_ref, cw_ref, cb_ref, c_ref, s_ref, kre_ref, kim_ref, nyq_ref, skip_ref, o_ref,
                  u_ref, x0_ref, p_ref, q_ref):
    L = z_ref.shape[0]
    row = lax.broadcasted_iota(jnp.int32, (L, HY_DIM), 0)

    def conv(j):
        cols = slice(j * HY_DIM, (j + 1) * HY_DIM)
        z = z_ref[:, cols].astype(F32)
        zprev = jnp.where(row == 0, 0.0, pltpu.roll(z, 1, 0))
        znext = jnp.where(row == L - 1, 0.0, pltpu.roll(z, L - 1, 0))
        return zprev * cw_ref[0:1, cols] + z * cw_ref[1:2, cols] + znext * cw_ref[2:3, cols] + cb_ref[:, cols]

    u = conv(2) * conv(1)
    ub = u.astype(BF16)
    u_ref[...] = u
    alt = jnp.where(row % 2 == 0, 1.0, -1.0)
    nyq_term = jnp.sum(u * alt, axis=0, keepdims=True) * nyq_ref[...]
    x0_ref[...] = conv(0)
    blk = min(HY_ROW_BLOCK, L)
    for r in range(0, L, blk):
        rows = slice(r, r + blk)
        a = jnp.dot(c_ref[rows, :], ub, preferred_element_type=F32)
        b = jnp.dot(s_ref[rows, :], ub, preferred_element_type=F32)
        kre = kre_ref[rows, :]
        kim = kim_ref[rows, :]
        p_ref[rows, :] = (a * kre + b * kim).astype(BF16)
        q_ref[rows, :] = (b * kre - a * kim).astype(BF16)
    for r in range(0, L, blk):
        rows = slice(r, r + blk)
        y = (jnp.dot(c_ref[rows, :], p_ref[...], preferred_element_type=F32)
             + jnp.dot(s_ref[rows, :], q_ref[...], preferred_element_type=F32))
        ub_rows = u_ref[rows, :]
        row_b = lax.broadcasted_iota(jnp.int32, (blk, HY_DIM), 0)
        y = y + jnp.where(row_b % 2 == 0, nyq_term, -nyq_term) + ub_rows * skip_ref[...]
        o_ref[rows, :] = (y * x0_ref[rows, :]).astype(o_ref.dtype)


def _hyena(zhy, nb, L, conv_w, conv_b, cmat_bf, smat_bf, kre, kim, nyq, skip):
    t = zhy.shape[0]
    return pl.pallas_call(
        _hyena_kernel,
        grid=(nb,),
        in_specs=[pl.BlockSpec((L, 3 * HY_DIM), lambda b: (b, 0)),
                  _const_spec((3, 3 * HY_DIM)), _const_spec((1, 3 * HY_DIM)),
                  pl.BlockSpec((L, L), lambda b: (0, 0), pipeline_mode=pl.Buffered(1)),
                  pl.BlockSpec((L, L), lambda b: (0, 0), pipeline_mode=pl.Buffered(1)),
                  _const_spec((L, HY_DIM)), _const_spec((L, HY_DIM)),
                  _const_spec((1, HY_DIM)), _const_spec((1, HY_DIM))],
        out_specs=pl.BlockSpec((L, HY_DIM), lambda b: (b, 0)),
        out_shape=jax.ShapeDtypeStruct((t, HY_DIM), BF16),
        scratch_shapes=[pltpu.VMEM((L, HY_DIM), F32), pltpu.VMEM((L, HY_DIM), F32),
                        pltpu.VMEM((L, HY_DIM), BF16), pltpu.VMEM((L, HY_DIM), BF16)],
        compiler_params=_cparams("arbitrary"),
        name="hyena_conv",
    )(zhy, conv_w, conv_b.reshape(1, -1), cmat_bf, smat_bf, kre, kim, nyq, skip.reshape(1, -1))


SUBLANES = 8


def _split_bf16(w):
    hi = w.astype(BF16)
    return jnp.stack([hi, (w - hi.astype(F32)).astype(BF16)])


def _store_token_tiles(ref, val):
    n = val.shape[0]
    for j in range(val.shape[1] // LANES):
        ref[pl.ds(j, n, stride=SUBLANES), :] = val[:, j * LANES:(j + 1) * LANES]


def _load_token_tiles(ref, n):
    return jnp.concatenate([ref[pl.ds(j, n, stride=SUBLANES), :] for j in range(SUBLANES)], axis=1)


def _merge_kernel(*refs, alpha, n_alias, n_real):
    h2_ref, rt_ref = refs[-2:]
    i = pl.program_id(0)

    @pl.when(i < n_real)
    def _():
        _merge_tile(*refs, alpha=alpha, n_alias=n_alias)

    @pl.when(i >= n_real)
    def _():
        h2_ref[...] = jnp.zeros_like(h2_ref)
        rt_ref[...] = jnp.zeros_like(rt_ref)


def _merge_tile(*refs, alpha, n_alias):
    (zgm_ref, yb_ref, yc_ref, gate_ref, x_ref, g1_ref, sh2_ref, sc2_ref, lng_ref, lnb_ref, ws_ref, bs_ref,
     pa_ref, pb_ref, pc_ref, wo_ref, l1g_ref, l1b_ref, wr_ref, br_ref) = refs[:20]
    x1_ref, h2_ref, rt_ref = refs[20 + n_alias:]
    tm = x_ref.shape[0]
    d = x_ref.shape[1]
    gm = _gelu_tanh(zgm_ref[...].astype(F32))
    u = gm[:, :GM_DIM]
    v = _layer_norm(gm[:, GM_DIM:], lng_ref[...], lnb_ref[...]).astype(BF16)
    lane_group = lax.broadcasted_iota(jnp.int32, (GM_CHUNK, GM_DIM), 1) // (GM_DIM // GM_GROUPS)
    ya = []
    for cidx in range(tm // GM_CHUNK):
        rows = slice(cidx * GM_CHUNK, (cidx + 1) * GM_CHUNK)
        r = jnp.dot(ws_ref[...], v[rows], preferred_element_type=F32)
        vv = bs_ref[...]
        for g in range(GM_GROUPS):
            vv = vv + jnp.where(lane_group == g, r[g * GM_CHUNK:(g + 1) * GM_CHUNK], 0.0)
        ya.append(u[rows] * vv)
    ya = jnp.concatenate(ya, axis=0) if len(ya) > 1 else ya[0]
    ma = jnp.dot(ya.astype(BF16), pa_ref[...], preferred_element_type=F32)
    mb = jnp.dot(yb_ref[...], pb_ref[...], preferred_element_type=F32)
    mc = jnp.dot(yc_ref[...], pc_ref[...], preferred_element_type=F32)
    merged = (_sigmoid(gate_ref[:, 0:d].astype(F32)) * ma
              + _sigmoid(gate_ref[:, d:2 * d].astype(F32)) * mb
              + _sigmoid(gate_ref[:, 2 * d:3 * d].astype(F32)) * mc)
    out = jnp.dot(merged.astype(BF16), wo_ref[...], preferred_element_type=F32)
    x1 = _layer_norm(alpha * x_ref[...] + g1_ref[...] * out, l1g_ref[...], l1b_ref[...])
    x1_ref[...] = x1
    h2 = x1 * (1.0 + sc2_ref[...]) + sh2_ref[...]
    _store_token_tiles(h2_ref, h2)
    h2_hi = h2.astype(BF16)
    h2_lo = (h2 - h2_hi.astype(F32)).astype(BF16)
    lg = (jnp.dot(h2_hi, wr_ref[0], preferred_element_type=F32)
          + jnp.dot(h2_hi, wr_ref[1], preferred_element_type=F32)
          + jnp.dot(h2_lo, wr_ref[0], preferred_element_type=F32) + br_ref[...])
    rt_ref[...] = _route(lg)


ROUTE_E0, ROUTE_E1, ROUTE_W0, ROUTE_W1 = 0, 1, 2, 3


def _route(lg):
    neg = jnp.float32(-3.0e38)
    lane_i = lax.broadcasted_iota(jnp.int32, lg.shape, 1)
    lane = lane_i.astype(F32)
    big = jnp.float32(LANES)
    is_g = lane_i < MOE_GROUPS
    gl = jnp.where(is_g, lg, neg)
    gmax = jnp.max(gl, axis=-1, keepdims=True)
    g_idx = jnp.min(jnp.where(gl == gmax, lane, big), axis=-1, keepdims=True)
    g_prob = 1.0 / jnp.sum(jnp.where(is_g, jnp.exp(gl - gmax), 0.0), axis=-1, keepdims=True)
    e_lo = MOE_GROUPS + MOE_EXPERTS_PER_GROUP * g_idx
    el = jnp.where(lane >= e_lo, jnp.where(lane < e_lo + MOE_EXPERTS_PER_GROUP, lg, neg), neg)
    v1 = jnp.max(el, axis=-1, keepdims=True)
    i1 = jnp.min(jnp.where(el == v1, lane, big), axis=-1, keepdims=True)
    el2 = jnp.where(lane == i1, neg, el)
    v2 = jnp.max(el2, axis=-1, keepdims=True)
    i2 = jnp.min(jnp.where(el2 == v2, lane, big), axis=-1, keepdims=True)
    e21 = jnp.exp(v2 - v1)
    w1 = g_prob / (1.0 + e21)
    w2 = w1 * e21
    swap = i2 < i1
    rec = jnp.where(lane_i == ROUTE_E0, jnp.minimum(i1, i2) - MOE_GROUPS, 0.0)
    rec = jnp.where(lane_i == ROUTE_E1, jnp.maximum(i1, i2) - MOE_GROUPS, rec)
    rec = jnp.where(lane_i == ROUTE_W0, jnp.where(swap, w2, w1), rec)
    return jnp.where(lane_i == ROUTE_W1, jnp.where(swap, w1, w2), rec)


def _merge(zgm, yb, yc, gate, x2d, mod3, mod_row0, rows_per_mod, lp, alpha, tm, t_all, row0, prev):
    t, d = x2d.shape
    tiles_per_mod = rows_per_mod // tm
    off = row0 // tm
    n_real = t // tm
    n_fill = (t_all - row0 - t) // tm if prev is None else 0
    real = lambda i: jnp.minimum(i, n_real - 1)

    def mod_map(piece):
        return lambda i: (mod_row0 + real(i) // tiles_per_mod, 0, piece)

    row = lambda w: pl.BlockSpec((tm, w), lambda i: (real(i), 0))
    row_off = lambda w: pl.BlockSpec((tm, w), lambda i: (off + i, 0))
    consts = [lp["gm_ln_g"], lp["gm_ln_b"], lp["gm_ws"], lp["gm_bs"], lp["p_a"], lp["p_b"], lp["p_c"],
              lp["w_out"], lp["ln1_g"], lp["ln1_b"], lp["w_router"], lp["b_router"]]
    in_specs = [row(2 * GM_DIM), row(HY_DIM), row(DA_V_W), row(N_BRANCH * d), row(d),
                pl.BlockSpec((None, 1, d), mod_map(2)), pl.BlockSpec((None, 1, d), mod_map(3)),
                pl.BlockSpec((None, 1, d), mod_map(4))] + [_const_spec(a.shape) for a in consts]
    args = [zgm, yb, yc, gate, x2d, mod3, mod3, mod3, *consts]
    aliases = {}
    if prev is not None:
        aliases = {len(args): 1, len(args) + 1: 2}
        in_specs += [pl.BlockSpec(memory_space=pl.ANY)] * 2
        args += list(prev)
    return pl.pallas_call(
        functools.partial(_merge_kernel, alpha=alpha, n_alias=len(aliases), n_real=n_real),
        grid=(n_real + n_fill,),
        in_specs=in_specs,
        out_specs=[row(d), pl.BlockSpec((tm * SUBLANES, LANES), lambda i: (off + i, 0)), row_off(LANES)],
        out_shape=[jax.ShapeDtypeStruct((t, d), F32), jax.ShapeDtypeStruct((t_all * SUBLANES, LANES), F32),
                   jax.ShapeDtypeStruct((t_all, LANES), F32)],
        input_output_aliases=aliases,
        compiler_params=_cparams("arbitrary"),
        name="merge_ln1",
    )(*args)


PAIR_CLASSES = tuple((MOE_EXPERTS_PER_GROUP * g + a, MOE_EXPERTS_PER_GROUP * g + b)
                     for g in range(MOE_GROUPS)
                     for a in range(MOE_EXPERTS_PER_GROUP) for b in range(a + 1, MOE_EXPERTS_PER_GROUP))
FLAG_FIRST, FLAG_LAST, FLAG_FINAL, FLAG_NEW_GROUP, FLAG_HI = 1, 2, 4, 8, 16


def _tile_copy(src, src_row, dst, dst_row, sem):
    s0 = pl.multiple_of(src_row * SUBLANES, SUBLANES)
    d0 = pl.multiple_of(dst_row * SUBLANES, SUBLANES)
    return pltpu.make_async_copy(src.at[pl.ds(s0, SUBLANES)], dst.at[pl.ds(d0, SUBLANES)], sem)


def _expert_kernel(vt_ref, ve_ref, vlo_ref, vhi_ref, vflag_ref, src_ref, nsrc_ref, dst_ref, pdst_ref,
                   wlo_ref, whi_ref, h2_hbm, wg_hbm, wu_hbm, wd_hbm, y_hbm,
                   xbuf, acc, ybuf, wg_grp, wu_grp, wd_grp, wg_stage, wu_stage, wd_stage, gsem, ssem, wsem,
                   *, n_tiles, layer):
    v = pl.program_id(0)
    tile, expert, lo, hi, flag = vt_ref[v], ve_ref[v], vlo_ref[v], vhi_ref[v], vflag_ref[v]
    tm = acc.shape[0]
    slot = tile % 2
    first = (flag & FLAG_FIRST) != 0
    group0 = pl.multiple_of((expert // MOE_EXPERTS_PER_GROUP) * MOE_EXPERTS_PER_GROUP, MOE_EXPERTS_PER_GROUP)
    e_in_group = expert - group0

    @pl.when((flag & FLAG_NEW_GROUP) != 0)
    def _():
        streams = ((wg_hbm, wg_stage, wg_grp), (wu_hbm, wu_stage, wu_grp), (wd_hbm, wd_stage, wd_grp))
        for k in range(MOE_EXPERTS_PER_GROUP):
            copies = [pltpu.make_async_copy(w_hbm.at[layer, group0 + k], stage, wsem.at[i])
                      for i, (w_hbm, stage, _) in enumerate(streams)]
            for cp in copies:
                cp.start()
            for cp, (_, stage, w_grp) in zip(copies, streams):
                cp.wait()
                w_grp[k] = stage[...].astype(BF16)

    def issue_gather(idx_ref, to_slot):
        def body(i, carry):
            for j in range(SUBLANES):
                r = i * SUBLANES + j
                _tile_copy(h2_hbm, idx_ref[0, r], xbuf.at[to_slot], r, gsem.at[to_slot]).start(priority=j % 2)
            return carry

        lax.fori_loop(0, tm // SUBLANES, body, 0)

    def issue_scatter(idx_ref):
        def body(i, carry):
            for j in range(SUBLANES):
                r = i * SUBLANES + j
                _tile_copy(ybuf, r, y_hbm, idx_ref[0, r], ssem).start(priority=j % 2)
            return carry

        lax.fori_loop(0, tm // SUBLANES, body, 0)

    has_next = tile + 1 < n_tiles
    prefetch = first & (tile > 0) & has_next

    @pl.when(first)
    def _():
        @pl.when(tile == 0)
        def _():
            issue_gather(src_ref, slot)

        pltpu.make_async_copy(h2_hbm.at[pl.ds(0, tm * SUBLANES)], xbuf.at[slot], gsem.at[slot]).wait()

        @pl.when(jnp.logical_not(prefetch))
        def _():
            @pl.when(has_next)
            def _():
                issue_gather(nsrc_ref, 1 - slot)

            @pl.when(tile > 0)
            def _():
                issue_scatter(pdst_ref)

    def compute(with_prefetch, r0=0, nrows=None):
        nrows = tm if nrows is None else nrows
        rows = pl.ds(r0, nrows)
        xb = jnp.concatenate([xbuf.at[slot][pl.ds(r0 * SUBLANES + j, nrows, stride=SUBLANES), :]
                              for j in range(SUBLANES)], axis=1).astype(BF16)
        if with_prefetch:
            for r in range(tm):
                _tile_copy(h2_hbm, nsrc_ref[0, r], xbuf.at[1 - slot], r, gsem.at[1 - slot]).start(priority=r % 2)
                _tile_copy(ybuf, r, y_hbm, pdst_ref[0, r], ssem).start(priority=r % 2)
        g = jnp.dot(xb, wg_grp[e_in_group], preferred_element_type=F32)
        u = jnp.dot(xb, wu_grp[e_in_group], preferred_element_type=F32)
        hmid = (g * _sigmoid(g) * u).astype(BF16)
        y = jnp.dot(hmid, wd_grp[e_in_group], preferred_element_type=F32)
        w_b = jnp.where((flag & FLAG_HI) != 0, whi_ref[rows, :], wlo_ref[rows, :])
        row = r0 + lax.broadcasted_iota(jnp.int32, (nrows, 1), 0)
        y = jnp.where((row >= lo) & (row < hi), y * jnp.concatenate([w_b] * SUBLANES, axis=1), 0.0)
        if with_prefetch:
            acc[...] = y
        elif nrows < tm:
            acc[rows, :] += y
        else:
            @pl.when(first)
            def _():
                acc[...] = y

            @pl.when(jnp.logical_not(first))
            def _():
                acc[...] += y

    @pl.when(prefetch)
    def _():
        compute(True)

    half = tm // 2
    plain = jnp.logical_not(prefetch) & (hi > lo)
    in_low = jnp.logical_not(first) & (hi <= half)
    in_high = jnp.logical_not(first) & (lo >= half)

    @pl.when(plain & jnp.logical_not(in_low | in_high))
    def _():
        compute(False)

    @pl.when(plain & in_low)
    def _():
        compute(False, 0, half)

    @pl.when(plain & in_high)
    def _():
        compute(False, half, tm - half)

    @pl.when((flag & FLAG_LAST) != 0)
    def _():
        whole = pltpu.make_async_copy(ybuf, y_hbm.at[pl.ds(0, tm * SUBLANES)], ssem)

        @pl.when(tile > 0)
        def _():
            whole.wait()

        _store_token_tiles(ybuf, acc[...])

        @pl.when((flag & FLAG_FINAL) != 0)
        def _():
            issue_scatter(dst_ref)
            whole.wait()


def _experts(h2, plan, w_gate, w_up, w_down, layer, tm):
    t = h2.shape[0] // SUBLANES
    vt, ve, vlo, vhi, vflag, order, wlo_b, whi_b = plan
    n_vis = vt.shape[0]
    n_tiles = t // tm
    d, hid = w_gate.shape[-2:]
    grp = MOE_EXPERTS_PER_GROUP
    assert d == SUBLANES * LANES
    idx_spec = lambda nxt: pl.BlockSpec(
        (None, 1, tm), lambda v, vt, *_: (jnp.clip(vt[v] + nxt, 0, n_tiles - 1), 0, 0), memory_space=pltpu.SMEM)
    w_tile = pl.BlockSpec((tm, LANES), lambda v, vt, *_: (vt[v], 0))
    any_spec = pl.BlockSpec(memory_space=pl.ANY)
    grid_spec = pltpu.PrefetchScalarGridSpec(
        num_scalar_prefetch=5,
        grid=(n_vis,),
        in_specs=[idx_spec(0), idx_spec(1), idx_spec(0), idx_spec(-1), w_tile, w_tile,
                  any_spec, any_spec, any_spec, any_spec],
        out_specs=any_spec,
        scratch_shapes=[pltpu.VMEM((2, tm * SUBLANES, LANES), F32), pltpu.VMEM((tm, d), F32),
                        pltpu.VMEM((tm * SUBLANES, LANES), F32),
                        pltpu.VMEM((grp, d, hid), BF16), pltpu.VMEM((grp, d, hid), BF16),
                        pltpu.VMEM((grp, hid, d), BF16),
                        pltpu.VMEM((d, hid), F32), pltpu.VMEM((d, hid), F32), pltpu.VMEM((hid, d), F32),
                        pltpu.SemaphoreType.DMA((2,)), pltpu.SemaphoreType.DMA(()), pltpu.SemaphoreType.DMA((3,))],
    )
    idx3 = order.reshape(n_tiles, 1, tm)
    return pl.pallas_call(
        functools.partial(_expert_kernel, n_tiles=n_tiles, layer=layer),
        grid_spec=grid_spec,
        out_shape=jax.ShapeDtypeStruct((t * SUBLANES, LANES), F32),
        compiler_params=_cparams("arbitrary"),
        name="moe_experts",
    )(vt, ve, vlo, vhi, vflag, idx3, idx3, idx3, idx3, wlo_b, whi_b, h2, w_gate, w_up, w_down)


def _visit_plan(route, tm):
    t = route.shape[0]
    i32 = jnp.int32
    cls_lo = jnp.asarray(np.array([p[0] for p in PAIR_CLASSES], np.int32))
    cls_hi = jnp.asarray(np.array([p[1] for p in PAIR_CLASSES], np.int32))
    cls_key = cls_lo * MOE_N_EXPERTS + cls_hi
    key = route[:, ROUTE_E0].astype(i32) * MOE_N_EXPERTS + route[:, ROUTE_E1].astype(i32)
    idx_bits = max(1, (t - 1).bit_length())
    assert (MOE_N_EXPERTS * MOE_N_EXPERTS) << idx_bits <= 2 ** 31
    packed, wlo, whi = lax.sort((key * (1 << idx_bits) + lax.iota(i32, t), route[:, ROUTE_W0], route[:, ROUTE_W1]),
                                num_keys=1)
    order = packed & ((1 << idx_bits) - 1)
    counts = jnp.sum((key[:, None] == cls_key[None, :]).astype(i32), axis=0)
    n_tiles = t // tm

    def segment_visits(seg_counts, seg_expert):
        n_seg = seg_counts.shape[0]
        ends = jnp.cumsum(seg_counts)
        starts = ends - seg_counts
        first_t = starts // tm
        nvis = jnp.where(seg_counts > 0, jnp.maximum(ends - 1, 0) // tm - first_t + 1, 0)
        cv_end = jnp.cumsum(nvis)
        cv_start = cv_end - nvis
        v = jnp.arange(n_tiles + n_seg, dtype=i32)
        active = v < cv_end[-1]
        s = jnp.minimum(jnp.sum((cv_end[None, :] <= v[:, None]).astype(i32), axis=1), n_seg - 1)
        tile = first_t[s] + v - cv_start[s]
        lo = jnp.clip(starts[s] - tile * tm, 0, tm)
        hi = jnp.clip(ends[s] - tile * tm, 0, tm)
        return tile, seg_expert[s], lo, hi, active

    run_id = np.cumsum([0] + [int(a[0] != b[0]) for a, b in zip(PAIR_CLASSES[:-1], PAIR_CLASSES[1:])])
    run_lo = jnp.asarray(np.array([PAIR_CLASSES[list(run_id).index(r)][0] for r in range(run_id[-1] + 1)], np.int32))
    in_run = jnp.asarray(run_id[None, :] == np.arange(run_id[-1] + 1)[:, None])
    run_counts = jnp.sum(jnp.where(in_run, counts[None, :], 0), axis=1)
    parts = [segment_visits(run_counts, run_lo) + (0,), segment_visits(counts, cls_hi) + (1,)]
    tile = jnp.concatenate([p[0] for p in parts])
    e = jnp.concatenate([p[1] for p in parts])
    lo = jnp.concatenate([p[2] for p in parts])
    hi = jnp.concatenate([p[3] for p in parts])
    active = jnp.concatenate([p[4] for p in parts])
    is_hi = jnp.concatenate([jnp.full(p[0].shape, p[5], i32) for p in parts])
    tile_stride = 2 * (tm + 1)
    order_key = jnp.where(active, tile * tile_stride + lo * 2 + is_hi, jnp.iinfo(jnp.int32).max)
    _, tile, e, lo, hi, is_hi, active = lax.sort((order_key, tile, e, lo, hi, is_hi, active.astype(i32)), num_keys=1)
    active = active == 1
    n_active = jnp.sum(active.astype(i32))
    v = jnp.arange(tile.shape[0], dtype=i32)
    e_last = jnp.max(jnp.where(active, v, -1))
    e = jnp.where(active, e, e[e_last])
    tile = jnp.where(active, tile, n_tiles - 1)
    lo = jnp.where(active, lo, 0)
    hi = jnp.where(active, hi, 0)
    prev_t = jnp.concatenate([jnp.full((1,), -1, i32), tile[:-1]])
    next_t = jnp.concatenate([tile[1:], jnp.full((1,), -1, i32)])
    is_final = v == n_active - 1
    is_last = (next_t != tile) | is_final
    group = e // MOE_EXPERTS_PER_GROUP
    prev_g = jnp.concatenate([jnp.full((1,), -1, i32), group[:-1]])
    flag = jnp.where(active, (prev_t != tile) * FLAG_FIRST + is_last * FLAG_LAST + is_final * FLAG_FINAL
                     + (prev_g != group) * FLAG_NEW_GROUP + is_hi * FLAG_HI, 0)
    cast = lambda z: z.astype(i32)
    bcast = lambda w: jnp.broadcast_to(w[:, None], (t, LANES))
    return cast(tile), cast(e), cast(lo), cast(hi), cast(flag), cast(order), bcast(wlo), bcast(whi)


def _combine_kernel(x_ref, y_ref, g2_ref, lg_ref, lb_ref, o_ref, *, alpha):
    y = _load_token_tiles(y_ref, x_ref.shape[0])
    o_ref[...] = _layer_norm(alpha * x_ref[...] + g2_ref[...] * y, lg_ref[...], lb_ref[...])


def _combine(x1, y, row0, mod3, mod_row0, rows_per_mod, ln_g, ln_b, alpha, tm):
    t, d = x1.shape
    tiles_per_mod = rows_per_mod // tm
    t0 = row0 // tm
    return pl.pallas_call(
        functools.partial(_combine_kernel, alpha=alpha),
        grid=(t // tm,),
        in_specs=[pl.BlockSpec((tm, d), lambda i: (i, 0)),
                  pl.BlockSpec((tm * SUBLANES, LANES), lambda i: (t0 + i, 0)),
                  pl.BlockSpec((None, 1, d), lambda i: (mod_row0 + i // tiles_per_mod, 0, 5)),
                  _const_spec((1, d)), _const_spec((1, d))],
        out_specs=pl.BlockSpec((tm, d), lambda i: (i, 0)),
        out_shape=jax.ShapeDtypeStruct((t, d), F32),
        compiler_params=_cparams("arbitrary"),
        name="combine_ln2",
    )(x1, y, mod3, ln_g.reshape(1, d), ln_b.reshape(1, d))


ROW_TILE = 512
SMALL_ROW_TILE = 256


def _pick_tile(n, pref):
    tm = min(pref, n)
    while n % tm:
        tm //= 2
    return tm


def kernel(x, c, ctx, c_ctx, ada_w, ada_b, w_in, gm_ln_g, gm_ln_b, gm_ws, gm_bs, hy_conv_w, hy_conv_b,
           hy_f_w1, hy_f_b1, hy_f_w2, hy_f_b2, hy_f_w3, hy_f_b3, hy_skip, da_lq1, da_lk1, da_lq2, da_lk2,
           da_norm_g, p_a, p_b, p_c, w_out, ln1_g, ln1_b, moe_wg, moe_bg, moe_we, moe_be,
           ex_w_gate, ex_w_up, ex_w_down, ln2_g, ln2_b):
    B, L, D = x.shape
    Lc = ctx.shape[1]
    depth = ada_w.shape[0]
    alpha = (2.0 * depth) ** 0.25
    T, Tc = B * L, B * Lc

    mp = -(-(B + 1) // 8) * 8
    c_all = jnp.zeros((mp, D), F32).at[:B].set(c).at[B].set(c_ctx)
    mod = _modulation(c_all, ada_w, ada_b)

    rope_tabs = _rope_tables(L // GRID_W)
    cm, sm = _dft_tables(L)
    cm_bf, sm_bf = cm.astype(BF16), sm.astype(BF16)
    cmc, smc = _dft_tables(Lc)
    cmc_bf, smc_bf = cmc.astype(BF16), smc.astype(BF16)

    seg_all = ((OFF_GM, OFF_HY, "gm"), (OFF_HY, OFF_Q, "hy"), (OFF_Q, OFF_K, "q"), (OFF_K, OFF_V, "k"),
               (OFF_V, OFF_GATE, "v"), (OFF_GATE, OFF_GATE + N_BRANCH * D, "gate"))
    seg_kv = ((0, DA_QK_W, "k"), (DA_QK_W, DA_QK_W + DA_V_W, "v"))

    tm_l = _pick_tile(L, ROW_TILE)
    tm_c = _pick_tile(Lc, SMALL_ROW_TILE)
    tq_l = _pick_tile(L, ROW_TILE)
    tq_c = _pick_tile(Lc, SMALL_ROW_TILE)
    tm_m = _pick_tile(L, ROW_TILE)
    tm_mc = _pick_tile(Tc, ROW_TILE)

    xs = x.reshape(T, D)
    xc = ctx.reshape(Tc, D)
    for l in range(depth):
        last = l == depth - 1
        lam_init = 0.8 - 0.6 * math.exp(-0.3 * l)
        mod3 = mod[l].reshape(mp, 1, 6 * D)
        w_l = w_in[l].astype(BF16)
        lparams = [a[l].reshape(1, DA_HEAD_DIM) for a in (da_lq1, da_lk1, da_lq2, da_lk2)]
        norm_g = da_norm_g[l].reshape(1, DA_V_DIM)
        lp = {
            "gm_ln_g": gm_ln_g[l].reshape(1, GM_DIM), "gm_ln_b": gm_ln_b[l].reshape(1, GM_DIM),
            "gm_ws": gm_ws[l].reshape(GM_GROUPS * GM_CHUNK, GM_CHUNK).astype(BF16),
            "gm_bs": jnp.repeat(jnp.transpose(gm_bs[l]), GM_DIM // GM_GROUPS, axis=1),
            "p_a": p_a[l].astype(BF16), "p_b": p_b[l].astype(BF16), "p_c": p_c[l].astype(BF16),
            "w_out": w_out[l].astype(BF16),
            "ln1_g": ln1_g[l].reshape(1, D), "ln1_b": ln1_b[l].reshape(1, D),
            "w_router": _split_bf16(jnp.zeros((D, LANES), F32).at[:, :MOE_GROUPS].set(moe_wg[l])
                                    .at[:, MOE_GROUPS:MOE_GROUPS + MOE_N_EXPERTS].set(moe_we[l])),
            "b_router": jnp.zeros((1, LANES), F32).at[0, :MOE_GROUPS].set(moe_bg[l])
                           .at[0, MOE_GROUPS:MOE_GROUPS + MOE_N_EXPERTS].set(moe_be[l]),
        }
        fw = (hy_f_w1[l], hy_f_b1[l], hy_f_w2[l], hy_f_b2[l], hy_f_w3[l], hy_f_b3[l])

        zgm, zhy, q, k, v, gate = _inproj(xs, mod3, 0, L, w_l, seg_all, rope_tabs, L, tm_l)
        if last:
            k_c, v_c = _inproj(xc, mod3, B, Tc, w_l[:, OFF_K:OFF_GATE], seg_kv, None, Lc, tm_c)
        else:
            zgm_c, zhy_c, q_c, k_c, v_c, gate_c = _inproj(xc, mod3, B, Tc, w_l, seg_all, None, Lc, tm_c)
        y_c = _attention(q, [(k, v, L), (k_c, v_c, Lc)], lparams, norm_g, lam_init, B, L, tq_l)
        kre, kim, nyq = _hyena_filter_spectrum(L, cm, sm, *fw)
        y_b = _hyena(zhy, B, L, hy_conv_w[l], hy_conv_b[l], cm_bf, sm_bf, kre, kim, nyq, hy_skip[l])
        t_all = T if last else T + Tc
        x1, h2, route = _merge(zgm, y_b, y_c, gate, xs, mod3, 0, L, lp, alpha, tm_m, t_all, 0, None)

        if not last:
            yc_c = _attention(q_c, [(k_c, v_c, Lc)], lparams, norm_g, lam_init, B, Lc, tq_c)
            kre_c, kim_c, nyq_c = _hyena_filter_spectrum(Lc, cmc, smc, *fw)
            yb_c = _hyena(zhy_c, B, Lc, hy_conv_w[l], hy_conv_b[l], cmc_bf, smc_bf, kre_c, kim_c, nyq_c,
                          hy_skip[l])
            x1c, h2, route = _merge(zgm_c, yb_c, yc_c, gate_c, xc, mod3, B, Tc, lp, alpha, tm_mc, t_all, T,
                                    (h2, route))

        moe_tm = _pick_tile(t_all, ROW_TILE)
        plan = _visit_plan(route, moe_tm)
        y = _experts(h2, plan, ex_w_gate, ex_w_up, ex_w_down, l, moe_tm)
        xs = _combine(x1, y, 0, mod3, 0, L, ln2_g[l], ln2_b[l], alpha, tm_c)
        if not last:
            xc = _combine(x1c, y, T, mod3, B, Tc, ln2_g[l], ln2_b[l], alpha, tm_c)
    return xs.reshape(B, L, D)
```

```python
import functools
import math

import numpy as np
import jax
import jax.numpy as jnp
from jax import lax
from jax.experimental import pallas as pl
from jax.experimental.pallas import tpu as pltpu

F32 = jnp.float32
BF16 = jnp.bfloat16
HIGHEST = lax.Precision.HIGHEST

GRID_W = 64
GM_DIM = 256
GM_GROUPS = 4
GM_CHUNK = 128
HY_DIM = 256
HY_EMB = 33
HY_BANDS = (HY_EMB - 1) // 2
HY_DECAY_FAST = 0.3
HY_DECAY_SLOW = 1.5
HY_DECAY_TARGET = 1e-2
HY_DECAY_SHIFT = 0.05
DA_HEADS = 4
DA_HEAD_DIM = 64
DA_V_DIM = 2 * DA_HEAD_DIM
DA_QK_W = DA_HEADS * 2 * DA_HEAD_DIM
DA_V_W = DA_HEADS * DA_V_DIM
ROPE_BASE = 10000.0
N_BRANCH = 3
OFF_GM = 0
OFF_HY = OFF_GM + 2 * GM_DIM
OFF_Q = OFF_HY + 3 * HY_DIM
OFF_K = OFF_Q + DA_QK_W
OFF_V = OFF_K + DA_QK_W
OFF_GATE = OFF_V + DA_V_W
MOE_GROUPS = 4
MOE_EXPERTS_PER_GROUP = 8
MOE_N_EXPERTS = MOE_GROUPS * MOE_EXPERTS_PER_GROUP
MOE_TOP_K = 2
LN_EPS = 1e-5
LANES = 128
VMEM_LIMIT = 56 * 1024 * 1024


def _cparams(*sem):
    return pltpu.CompilerParams(dimension_semantics=sem, vmem_limit_bytes=VMEM_LIMIT)


def _sigmoid(x):
    return 1.0 / (1.0 + jnp.exp(-x))


def _layer_norm(x, g, b):
    mu = jnp.mean(x, axis=-1, keepdims=True)
    xc = x - mu
    var = jnp.mean(xc * xc, axis=-1, keepdims=True)
    return xc * lax.rsqrt(var + LN_EPS) * g + b


def _gelu_tanh(x):
    return 0.5 * x * (1.0 + jnp.tanh(math.sqrt(2.0 / math.pi) * (x + 0.044715 * (x * x * x))))


def _const_spec(shape):
    nd = len(shape)
    return pl.BlockSpec(shape, lambda *_: (0,) * nd)


def _mod_kernel(c_ref, w_ref, b_ref, o_ref):
    c = c_ref[...]
    s = c * _sigmoid(c)
    o_ref[...] = jnp.dot(s, w_ref[...], precision=HIGHEST, preferred_element_type=F32) + b_ref[...]


def _modulation(c_all, ada_w, ada_b):
    depth, d, n = ada_w.shape
    mp = c_all.shape[0]
    tn = 512
    return pl.pallas_call(
        _mod_kernel,
        grid=(depth, n // tn),
        in_specs=[pl.BlockSpec((mp, d), lambda l, j: (0, 0)),
                  pl.BlockSpec((None, d, tn), lambda l, j: (l, 0, j)),
                  pl.BlockSpec((None, 1, tn), lambda l, j: (l, 0, j))],
        out_specs=pl.BlockSpec((None, mp, tn), lambda l, j: (l, 0, j)),
        out_shape=jax.ShapeDtypeStruct((depth, mp, n), F32),
        compiler_params=_cparams("arbitrary", "arbitrary"),
        name="adaln_mod",
    )(c_all, ada_w, ada_b.reshape(depth, 1, n))


def _rope_tables(rows):
    n_freq = DA_HEAD_DIM // 4
    row = jnp.broadcast_to(jnp.arange(rows)[:, None], (rows, GRID_W)).reshape(-1).astype(F32)
    col = jnp.broadcast_to(jnp.arange(GRID_W)[None, :], (rows, GRID_W)).reshape(-1).astype(F32)
    inv = ROPE_BASE ** (-jnp.arange(n_freq, dtype=F32) / n_freq)
    ang_r = row[:, None] * inv
    ang_c = col[:, None] * inv
    c64 = jnp.concatenate([jnp.cos(ang_r), jnp.cos(ang_r), jnp.cos(ang_c), jnp.cos(ang_c)], axis=-1)
    s64 = jnp.concatenate([-jnp.sin(ang_r), jnp.sin(ang_r), -jnp.sin(ang_c), jnp.sin(ang_c)], axis=-1)
    return jnp.tile(c64, (1, LANES // DA_HEAD_DIM)), jnp.tile(s64, (1, LANES // DA_HEAD_DIM))


def _rope_block(xb, cos, sin):
    lane = lax.broadcasted_iota(jnp.int32, xb.shape, 1)
    n_freq = DA_HEAD_DIM // 4
    first_half = (lane % (2 * n_freq)) < n_freq
    partner = jnp.where(first_half, pltpu.roll(xb, LANES - n_freq, 1), pltpu.roll(xb, n_freq, 1))
    return xb * cos + partner * sin


def _inproj_kernel(*refs, segs, use_rope, n_chunk):
    if use_rope:
        x_ref, sh_ref, sc_ref, w_ref, cos_ref, sin_ref = refs[:6]
        out_refs = refs[6:]
    else:
        x_ref, sh_ref, sc_ref, w_ref = refs[:4]
        out_refs = refs[4:]
    h = (x_ref[...] * (1.0 + sc_ref[...]) + sh_ref[...]).astype(BF16)
    for (a, b, kind), o_ref in zip(segs, out_refs):
        for c0 in range(a, b, n_chunk):
            c1 = min(c0 + n_chunk, b)
            acc = jnp.dot(h, w_ref[:, c0:c1], preferred_element_type=F32)
            if kind == "q":
                acc = acc * (DA_HEAD_DIM ** -0.5 * math.log2(math.e))
            if use_rope and kind in ("q", "k"):
                cos = cos_ref[...]
                sin = sin_ref[...]
                for j in range((c1 - c0) // LANES):
                    blk = _rope_block(acc[:, j * LANES:(j + 1) * LANES], cos, sin)
                    o_ref[:, c0 - a + j * LANES:c0 - a + (j + 1) * LANES] = blk.astype(o_ref.dtype)
            else:
                o_ref[:, c0 - a:c1 - a] = acc.astype(o_ref.dtype)


def _inproj(x2d, mod3, mod_row0, rows_per_mod, w, segs, rope_tabs, seq_len, tm):
    t, d = x2d.shape
    n = w.shape[1]
    use_rope = rope_tabs is not None
    tiles_per_mod = rows_per_mod // tm
    tiles_per_seq = seq_len // tm

    def mod_map(piece):
        return lambda i: (mod_row0 + i // tiles_per_mod, 0, piece)

    in_specs = [pl.BlockSpec((tm, d), lambda i: (i, 0)),
                pl.BlockSpec((None, 1, d), mod_map(0)),
                pl.BlockSpec((None, 1, d), mod_map(1)),
                pl.BlockSpec((d, n), lambda i: (0, 0), pipeline_mode=pl.Buffered(1))]
    args = [x2d, mod3, mod3, w]
    if use_rope:
        in_specs += [pl.BlockSpec((tm, LANES), lambda i: (i % tiles_per_seq, 0))] * 2
        args += list(rope_tabs)
    out_specs = [pl.BlockSpec((tm, b - a), lambda i: (i, 0)) for a, b, _ in segs]
    out_shape = [jax.ShapeDtypeStruct((t, b - a), BF16) for a, b, _ in segs]
    return pl.pallas_call(
        functools.partial(_inproj_kernel, segs=segs, use_rope=use_rope, n_chunk=512),
        grid=(t // tm,),
        in_specs=in_specs, out_specs=out_specs, out_shape=out_shape,
        compiler_params=_cparams("arbitrary"),
        name="inproj",
    )(*args)


ATTN_KEY_CHUNK = 256


def _attn_kernel(*refs, src_lens, lam_init):
    n_src = len(src_lens)
    lq1, lk1, lq2, lk2, g_ref, q_ref = refs[:6]
    kv_refs = refs[6:6 + 2 * n_src]
    o_ref, s_scr = refs[6 + 2 * n_src:]
    lam = (jnp.exp(jnp.sum(lq1[...] * lk1[...], axis=-1, keepdims=True))
           - jnp.exp(jnp.sum(lq2[...] * lk2[...], axis=-1, keepdims=True)) + lam_init)
    tq = q_ref.shape[0]
    lane = lax.broadcasted_iota(jnp.int32, (tq, LANES), 1)
    dn = (((1,), (1,)), ((), ()))
    chunks = []
    off = 0
    for j, n in enumerate(src_lens):
        kc = min(ATTN_KEY_CHUNK, n)
        for st in range(0, n, kc):
            chunks.append((j, st, kc, off))
            off += kc
    for h in range(DA_HEADS):
        cols = slice(h * LANES, (h + 1) * LANES)
        qh = q_ref[:, cols]
        zero = jnp.zeros_like(qh)
        om = []
        for m in range(2):
            qm = jnp.where(lane < DA_HEAD_DIM if m == 0 else lane >= DA_HEAD_DIM, qh, zero)
            mlane = None
            for j, st, kc, off in chunks:
                s_c = lax.dot_general(qm, kv_refs[2 * j][st:st + kc, cols], dn, preferred_element_type=F32)
                s_scr[:, off:off + kc] = s_c
                for b in range(kc // LANES):
                    blk = s_c[:, b * LANES:(b + 1) * LANES]
                    mlane = blk if mlane is None else jnp.maximum(mlane, blk)
            mx = jnp.max(mlane, axis=-1, keepdims=True)
            acc = None
            for j, st, kc, off in chunks:
                p = jnp.exp2(s_scr[:, off:off + kc] - mx).astype(BF16)
                v_aug = jnp.concatenate([kv_refs[2 * j + 1][st:st + kc, cols], jnp.ones((kc, LANES), BF16)],
                                        axis=1)
                d = jnp.dot(p, v_aug, preferred_element_type=F32)
                acc = d if acc is None else acc + d
            om.append(acc[:, :LANES] * (1.0 / acc[:, LANES:LANES + 1]))
        o = om[0] - lam * om[1]
        ms = jnp.mean(o * o, axis=-1, keepdims=True)
        o = o * lax.rsqrt(ms + LN_EPS) * g_ref[...] * (1.0 - lam_init)
        o_ref[:, cols] = o.astype(o_ref.dtype)


def _attention(q, kvs, lparams, norm_g, lam_init, nb, lq, tq):
    t = q.shape[0]
    qt = lq // tq
    in_specs = [_const_spec((1, DA_HEAD_DIM))] * 4 + [_const_spec((1, DA_V_DIM))]
    in_specs.append(pl.BlockSpec((tq, DA_QK_W), lambda b, i: (b * qt + i, 0)))
    args = list(lparams) + [norm_g, q]
    for k, v, lk in kvs:
        in_specs += [pl.BlockSpec((lk, DA_QK_W), lambda b, i: (b, 0)),
                     pl.BlockSpec((lk, DA_V_W), lambda b, i: (b, 0))]
        args += [k, v]
    src_lens = tuple(lk for _, _, lk in kvs)
    return pl.pallas_call(
        functools.partial(_attn_kernel, src_lens=src_lens, lam_init=lam_init),
        grid=(nb, qt),
        in_specs=in_specs,
        out_specs=pl.BlockSpec((tq, DA_V_W), lambda b, i: (b * qt + i, 0)),
        out_shape=jax.ShapeDtypeStruct((t, DA_V_W), BF16),
        scratch_shapes=[pltpu.VMEM((tq, sum(src_lens)), F32)],
        compiler_params=_cparams("arbitrary", "arbitrary"),
        name="diff_attn",
    )(*args)


DFT_SPLIT = 64


def _dft_tables(L):
    n = jnp.arange(L, dtype=jnp.int32)[None, :]
    k1 = jnp.arange(L // DFT_SPLIT, dtype=jnp.int32)[:, None] * DFT_SPLIT
    k0 = jnp.arange(DFT_SPLIT, dtype=jnp.int32)[:, None]
    ang_a = ((k1 * n) % (2 * L)).astype(F32) * (math.pi / L)
    ang_b = ((k0 * n) % (2 * L)).astype(F32) * (math.pi / L)
    ca, sa = jnp.cos(ang_a)[:, None, :], jnp.sin(ang_a)[:, None, :]
    cb, sb = jnp.cos(ang_b)[None, :, :], jnp.sin(ang_b)[None, :, :]
    return (ca * cb - sa * sb).reshape(L, L), (sa * cb + ca * sb).reshape(L, L)


def _filter_consts(L):
    t = jnp.linspace(0.0, 1.0, L, dtype=F32)[:, None]
    w = 2.0 * math.pi * jnp.arange(L, dtype=F32)[:, None] / L
    f = jnp.linspace(1e-4, HY_BANDS - 1, HY_BANDS, dtype=F32)[None, :]
    emb = jnp.concatenate([t, jnp.cos(f * w), -jnp.sin(f * w)], axis=-1)
    max_decay = math.log(HY_DECAY_TARGET) / HY_DECAY_FAST
    min_decay = math.log(HY_DECAY_TARGET) / HY_DECAY_SLOW
    deltas = jnp.abs(jnp.linspace(min_decay, max_decay, HY_DIM, dtype=F32))
    window = jnp.exp(-t * deltas[None, :]) + HY_DECAY_SHIFT
    return emb, window


def _filter_kernel(emb_ref, win_ref, w1, b1, w2, b2, w3, b3, hs_ref, hd_ref, nyq_ref):
    h = jnp.sin(jnp.dot(emb_ref[...], w1[...], precision=HIGHEST, preferred_element_type=F32) + b1[...])
    h = jnp.sin(jnp.dot(h, w2[...], precision=HIGHEST, preferred_element_type=F32) + b2[...])
    h = jnp.dot(h, w3[...], precision=HIGHEST, preferred_element_type=F32) + b3[...]
    win = win_ref[...]
    hf = h[:, :HY_DIM] * win
    hb = h[:, HY_DIM:] * win
    row = lax.broadcasted_iota(jnp.int32, hf.shape, 0)
    hb = jnp.where(row == 0, 0.0, hb)
    alt = jnp.where(row % 2 == 0, 1.0, -1.0)
    hs_ref[...] = hf + hb
    hd_ref[...] = hf - hb
    nyq_ref[...] = jnp.sum((hf + hb) * alt, axis=0, keepdims=True)


def _spectrum_kernel(c_ref, s_ref, hs_ref, hd_ref, kre_ref, kim_ref, *, n_fft):
    i = pl.program_id(0)
    tk = c_ref.shape[0]
    kidx = i * tk + lax.broadcasted_iota(jnp.int32, (tk, 1), 0)
    scale = jnp.where(kidx == 0, 1.0 / n_fft, 2.0 / n_fft)
    kre = jnp.dot(c_ref[...], hs_ref[...], precision=HIGHEST, preferred_element_type=F32)
    kim = -jnp.dot(s_ref[...], hd_ref[...], precision=HIGHEST, preferred_element_type=F32)
    kre_ref[...] = kre * scale
    kim_ref[...] = kim * scale


def _hyena_filter_spectrum(L, cmat, smat, w1, b1, w2, b2, w3, b3):
    emb, window = _filter_consts(L)
    full = lambda a: _const_spec(a.shape)
    ins = [emb, window, w1, b1.reshape(1, -1), w2, b2.reshape(1, -1), w3, b3.reshape(1, -1)]
    hs, hd, nyq = pl.pallas_call(
        _filter_kernel,
        grid=(1,),
        in_specs=[full(a) for a in ins],
        out_specs=[_const_spec((L, HY_DIM)), _const_spec((L, HY_DIM)), _const_spec((1, HY_DIM))],
        out_shape=[jax.ShapeDtypeStruct((L, HY_DIM), F32), jax.ShapeDtypeStruct((L, HY_DIM), F32),
                   jax.ShapeDtypeStruct((1, HY_DIM), F32)],
        compiler_params=_cparams("arbitrary"),
        name="hyena_filter",
    )(*ins)
    tk = min(256, L)
    kre, kim = pl.pallas_call(
        functools.partial(_spectrum_kernel, n_fft=2 * L),
        grid=(L // tk,),
        in_specs=[pl.BlockSpec((tk, L), lambda i: (i, 0)), pl.BlockSpec((tk, L), lambda i: (i, 0)),
                  _const_spec((L, HY_DIM)), _const_spec((L, HY_DIM))],
        out_specs=[pl.BlockSpec((tk, HY_DIM), lambda i: (i, 0))] * 2,
        out_shape=[jax.ShapeDtypeStruct((L, HY_DIM), F32)] * 2,
        compiler_params=_cparams("arbitrary"),
        name="hyena_spectrum",
    )(cmat, smat, hs, hd)
    return kre, kim, nyq * (1.0 / (2 * L))


HY_ROW_BLOCK = 1024


def _hyena_kernel(z_ref, cw_ref, cb_ref, c_ref, s_ref, kre_ref, kim_ref, nyq_ref, skip_ref, o_ref,
                  u_ref, x0_ref, p_ref, q_ref):
    L = z_ref.shape[0]
    row = lax.broadcasted_iota(jnp.int32, (L, HY_DIM), 0)

    def conv(j):
        cols = slice(j * HY_DIM, (j + 1) * HY_DIM)
        z = z_ref[:, cols].astype(F32)
        zprev = jnp.where(row == 0, 0.0, pltpu.roll(z, 1, 0))
        znext = jnp.where(row == L - 1, 0.0, pltpu.roll(z, L - 1, 0))
        return zprev * cw_ref[0:1, cols] + z * cw_ref[1:2, cols] + znext * cw_ref[2:3, cols] + cb_ref[:, cols]

    u = conv(2) * conv(1)
    ub = u.astype(BF16)
    u_ref[...] = u
    alt = jnp.where(row % 2 == 0, 1.0, -1.0)
    nyq_term = jnp.sum(u * alt, axis=0, keepdims=True) * nyq_ref[...]
    x0_ref[...] = conv(0)
    blk = min(HY_ROW_BLOCK, L)
    for r in range(0, L, blk):
        rows = slice(r, r + blk)
        a = jnp.dot(c_ref[rows, :], ub, preferred_element_type=F32)
        b = jnp.dot(s_ref[rows, :], ub, preferred_element_type=F32)
        kre = kre_ref[rows, :]
        kim = kim_ref[rows, :]
        p_ref[rows, :] = (a * kre + b * kim).astype(BF16)
        q_ref[rows, :] = (b * kre - a * kim).astype(BF16)
    for r in range(0, L, blk):
        rows = slice(r, r + blk)
        y = (jnp.dot(c_ref[rows, :], p_ref[...], preferred_element_type=F32)
             + jnp.dot(s_ref[rows, :], q_ref[...], preferred_element_type=F32))
        ub_rows = u_ref[rows, :]
        row_b = lax.broadcasted_iota(jnp.int32, (blk, HY_DIM), 0)
        y = y + jnp.where(row_b % 2 == 0, nyq_term, -nyq_term) + ub_rows * skip_ref[...]
        o_ref[rows, :] = (y * x0_ref[rows, :]).astype(o_ref.dtype)


def _hyena(zhy, nb, L, conv_w, conv_b, cmat_bf, smat_bf, kre, kim, nyq, skip):
    t = zhy.shape[0]
    return pl.pallas_call(
        _hyena_kernel,
        grid=(nb,),
        in_specs=[pl.BlockSpec((L, 3 * HY_DIM), lambda b: (b, 0)),
                  _const_spec((3, 3 * HY_DIM)), _const_spec((1, 3 * HY_DIM)),
                  pl.BlockSpec((L, L), lambda b: (0, 0), pipeline_mode=pl.Buffered(1)),
                  pl.BlockSpec((L, L), lambda b: (0, 0), pipeline_mode=pl.Buffered(1)),
                  _const_spec((L, HY_DIM)), _const_spec((L, HY_DIM)),
                  _const_spec((1, HY_DIM)), _const_spec((1, HY_DIM))],
        out_specs=pl.BlockSpec((L, HY_DIM), lambda b: (b, 0)),
        out_shape=jax.ShapeDtypeStruct((t, HY_DIM), BF16),
        scratch_shapes=[pltpu.VMEM((L, HY_DIM), F32), pltpu.VMEM((L, HY_DIM), F32),
                        pltpu.VMEM((L, HY_DIM), BF16), pltpu.VMEM((L, HY_DIM), BF16)],
        compiler_params=_cparams("arbitrary"),
        name="hyena_conv",
    )(zhy, conv_w, conv_b.reshape(1, -1), cmat_bf, smat_bf, kre, kim, nyq, skip.reshape(1, -1))


SUBLANES = 8


def _split_bf16(w):
    hi = w.astype(BF16)
    return jnp.stack([hi, (w - hi.astype(F32)).astype(BF16)])


def _store_token_tiles(ref, val):
    n = val.shape[0]
    for j in range(val.shape[1] // LANES):
        ref[pl.ds(j, n, stride=SUBLANES), :] = val[:, j * LANES:(j + 1) * LANES]


def _load_token_tiles(ref, n):
    return jnp.concatenate([ref[pl.ds(j, n, stride=SUBLANES), :] for j in range(SUBLANES)], axis=1)


def _merge_kernel(*refs, alpha, n_alias, n_real):
    h2_ref, rt_ref = refs[-2:]
    i = pl.program_id(0)

    @pl.when(i < n_real)
    def _():
        _merge_tile(*refs, alpha=alpha, n_alias=n_alias)

    @pl.when(i >= n_real)
    def _():
        h2_ref[...] = jnp.zeros_like(h2_ref)
        rt_ref[...] = jnp.zeros_like(rt_ref)


def _merge_tile(*refs, alpha, n_alias):
    (zgm_ref, yb_ref, yc_ref, gate_ref, x_ref, g1_ref, sh2_ref, sc2_ref, lng_ref, lnb_ref, ws_ref, bs_ref,
     pa_ref, pb_ref, pc_ref, wo_ref, l1g_ref, l1b_ref, wr_ref, br_ref) = refs[:20]
    x1_ref, h2_ref, rt_ref = refs[20 + n_alias:]
    tm = x_ref.shape[0]
    d = x_ref.shape[1]
    gm = _gelu_tanh(zgm_ref[...].astype(F32))
    u = gm[:, :GM_DIM]
    v = _layer_norm(gm[:, GM_DIM:], lng_ref[...], lnb_ref[...]).astype(BF16)
    lane_group = lax.broadcasted_iota(jnp.int32, (GM_CHUNK, GM_DIM), 1) // (GM_DIM // GM_GROUPS)
    ya = []
    for cidx in range(tm // GM_CHUNK):
        rows = slice(cidx * GM_CHUNK, (cidx + 1) * GM_CHUNK)
        r = jnp.dot(ws_ref[...], v[rows], preferred_element_type=F32)
        vv = bs_ref[...]
        for g in range(GM_GROUPS):
            vv = vv + jnp.where(lane_group == g, r[g * GM_CHUNK:(g + 1) * GM_CHUNK], 0.0)
        ya.append(u[rows] * vv)
    ya = jnp.concatenate(ya, axis=0) if len(ya) > 1 else ya[0]
    ma = jnp.dot(ya.astype(BF16), pa_ref[...], preferred_element_type=F32)
    mb = jnp.dot(yb_ref[...], pb_ref[...], preferred_element_type=F32)
    mc = jnp.dot(yc_ref[...], pc_ref[...], preferred_element_type=F32)
    merged = (_sigmoid(gate_ref[:, 0:d].astype(F32)) * ma
              + _sigmoid(gate_ref[:, d:2 * d].astype(F32)) * mb
              + _sigmoid(gate_ref[:, 2 * d:3 * d].astype(F32)) * mc)
    out = jnp.dot(merged.astype(BF16), wo_ref[...], preferred_element_type=F32)
    x1 = _layer_norm(alpha * x_ref[...] + g1_ref[...] * out, l1g_ref[...], l1b_ref[...])
    x1_ref[...] = x1
    h2 = x1 * (1.0 + sc2_ref[...]) + sh2_ref[...]
    _store_token_tiles(h2_ref, h2)
    h2_hi = h2.astype(BF16)
    h2_lo = (h2 - h2_hi.astype(F32)).astype(BF16)
    lg = (jnp.dot(h2_hi, wr_ref[0], preferred_element_type=F32)
          + jnp.dot(h2_hi, wr_ref[1], preferred_element_type=F32)
          + jnp.dot(h2_lo, wr_ref[0], preferred_element_type=F32) + br_ref[...])
    rt_ref[...] = _route(lg)


ROUTE_E0, ROUTE_E1, ROUTE_W0, ROUTE_W1 = 0, 1, 2, 3


def _route(lg):
    neg = jnp.float32(-3.0e38)
    lane_i = lax.broadcasted_iota(jnp.int32, lg.shape, 1)
    lane = lane_i.astype(F32)
    big = jnp.float32(LANES)
    is_g = lane_i < MOE_GROUPS
    gl = jnp.where(is_g, lg, neg)
    gmax = jnp.max(gl, axis=-1, keepdims=True)
    g_idx = jnp.min(jnp.where(gl == gmax, lane, big), axis=-1, keepdims=True)
    g_prob = 1.0 / jnp.sum(jnp.where(is_g, jnp.exp(gl - gmax), 0.0), axis=-1, keepdims=True)
    e_lo = MOE_GROUPS + MOE_EXPERTS_PER_GROUP * g_idx
    el = jnp.where(lane >= e_lo, jnp.where(lane < e_lo + MOE_EXPERTS_PER_GROUP, lg, neg), neg)
    v1 = jnp.max(el, axis=-1, keepdims=True)
    i1 = jnp.min(jnp.where(el == v1, lane, big), axis=-1, keepdims=True)
    el2 = jnp.where(lane == i1, neg, el)
    v2 = jnp.max(el2, axis=-1, keepdims=True)
    i2 = jnp.min(jnp.where(el2 == v2, lane, big), axis=-1, keepdims=True)
    e21 = jnp.exp(v2 - v1)
    w1 = g_prob / (1.0 + e21)
    w2 = w1 * e21
    swap = i2 < i1
    rec = jnp.where(lane_i == ROUTE_E0, jnp.minimum(i1, i2) - MOE_GROUPS, 0.0)
    rec = jnp.where(lane_i == ROUTE_E1, jnp.maximum(i1, i2) - MOE_GROUPS, rec)
    rec = jnp.where(lane_i == ROUTE_W0, jnp.where(swap, w2, w1), rec)
    return jnp.where(lane_i == ROUTE_W1, jnp.where(swap, w1, w2), rec)


def _merge(zgm, yb, yc, gate, x2d, mod3, mod_row0, rows_per_mod, lp, alpha, tm, t_all, row0, prev):
    t, d = x2d.shape
    tiles_per_mod = rows_per_mod // tm
    off = row0 // tm
    n_real = t // tm
    n_fill = (t_all - row0 - t) // tm if prev is None else 0
    real = lambda i: jnp.minimum(i, n_real - 1)

    def mod_map(piece):
        return lambda i: (mod_row0 + real(i) // tiles_per_mod, 0, piece)

    row = lambda w: pl.BlockSpec((tm, w), lambda i: (real(i), 0))
    row_off = lambda w: pl.BlockSpec((tm, w), lambda i: (off + i, 0))
    consts = [lp["gm_ln_g"], lp["gm_ln_b"], lp["gm_ws"], lp["gm_bs"], lp["p_a"], lp["p_b"], lp["p_c"],
              lp["w_out"], lp["ln1_g"], lp["ln1_b"], lp["w_router"], lp["b_router"]]
    in_specs = [row(2 * GM_DIM), row(HY_DIM), row(DA_V_W), row(N_BRANCH * d), row(d),
                pl.BlockSpec((None, 1, d), mod_map(2)), pl.BlockSpec((None, 1, d), mod_map(3)),
                pl.BlockSpec((None, 1, d), mod_map(4))] + [_const_spec(a.shape) for a in consts]
    args = [zgm, yb, yc, gate, x2d, mod3, mod3, mod3, *consts]
    aliases = {}
    if prev is not None:
        aliases = {len(args): 1, len(args) + 1: 2}
        in_specs += [pl.BlockSpec(memory_space=pl.ANY)] * 2
        args += list(prev)
    return pl.pallas_call(
        functools.partial(_merge_kernel, alpha=alpha, n_alias=len(aliases), n_real=n_real),
        grid=(n_real + n_fill,),
        in_specs=in_specs,
        out_specs=[row(d), pl.BlockSpec((tm * SUBLANES, LANES), lambda i: (off + i, 0)), row_off(LANES)],
        out_shape=[jax.ShapeDtypeStruct((t, d), F32), jax.ShapeDtypeStruct((t_all * SUBLANES, LANES), F32),
                   jax.ShapeDtypeStruct((t_all, LANES), F32)],
        input_output_aliases=aliases,
        compiler_params=_cparams("arbitrary"),
        name="merge_ln1",
    )(*args)


PAIR_CLASSES = tuple((MOE_EXPERTS_PER_GROUP * g + a, MOE_EXPERTS_PER_GROUP * g + b)
                     for g in range(MOE_GROUPS)
                     for a in range(MOE_EXPERTS_PER_GROUP) for b in range(a + 1, MOE_EXPERTS_PER_GROUP))
FLAG_FIRST, FLAG_LAST, FLAG_FINAL, FLAG_NEW_GROUP, FLAG_HI = 1, 2, 4, 8, 16


def _tile_copy(src, src_row, dst, dst_row, sem):
    s0 = pl.multiple_of(src_row * SUBLANES, SUBLANES)
    d0 = pl.multiple_of(dst_row * SUBLANES, SUBLANES)
    return pltpu.make_async_copy(src.at[pl.ds(s0, SUBLANES)], dst.at[pl.ds(d0, SUBLANES)], sem)


def _expert_kernel(vt_ref, ve_ref, vlo_ref, vhi_ref, vflag_ref, src_ref, nsrc_ref, dst_ref, pdst_ref,
                   wlo_ref, whi_ref, h2_hbm, wg_hbm, wu_hbm, wd_hbm, y_hbm,
                   xbuf, acc, ybuf, wg_grp, wu_grp, wd_grp, wg_stage, wu_stage, wd_stage, gsem, ssem, wsem,
                   *, n_tiles, layer):
    v = pl.program_id(0)
    tile, expert, lo, hi, flag = vt_ref[v], ve_ref[v], vlo_ref[v], vhi_ref[v], vflag_ref[v]
    tm = acc.shape[0]
    slot = tile % 2
    first = (flag & FLAG_FIRST) != 0
    group0 = pl.multiple_of((expert // MOE_EXPERTS_PER_GROUP) * MOE_EXPERTS_PER_GROUP, MOE_EXPERTS_PER_GROUP)
    e_in_group = expert - group0

    @pl.when((flag & FLAG_NEW_GROUP) != 0)
    def _():
        streams = ((wg_hbm, wg_stage, wg_grp), (wu_hbm, wu_stage, wu_grp), (wd_hbm, wd_stage, wd_grp))
        for k in range(MOE_EXPERTS_PER_GROUP):
            copies = [pltpu.make_async_copy(w_hbm.at[layer, group0 + k], stage, wsem.at[i])
                      for i, (w_hbm, stage, _) in enumerate(streams)]
            for cp in copies:
                cp.start()
            for cp, (_, stage, w_grp) in zip(copies, streams):
                cp.wait()
                w_grp[k] = stage[...].astype(BF16)

    def issue_gather(idx_ref, to_slot):
        def body(i, carry):
            for j in range(SUBLANES):
                r = i * SUBLANES + j
                _tile_copy(h2_hbm, idx_ref[0, r], xbuf.at[to_slot], r, gsem.at[to_slot]).start(priority=j % 2)
            return carry

        lax.fori_loop(0, tm // SUBLANES, body, 0)

    def issue_scatter(idx_ref):
        def body(i, carry):
            for j in range(SUBLANES):
                r = i * SUBLANES + j
                _tile_copy(ybuf, r, y_hbm, idx_ref[0, r], ssem).start(priority=j % 2)
            return carry

        lax.fori_loop(0, tm // SUBLANES, body, 0)

    has_next = tile + 1 < n_tiles
    prefetch = first & (tile > 0) & has_next

    @pl.when(first)
    def _():
        @pl.when(tile == 0)
        def _():
            issue_gather(src_ref, slot)

        pltpu.make_async_copy(h2_hbm.at[pl.ds(0, tm * SUBLANES)], xbuf.at[slot], gsem.at[slot]).wait()

        @pl.when(jnp.logical_not(prefetch))
        def _():
            @pl.when(has_next)
            def _():
                issue_gather(nsrc_ref, 1 - slot)

            @pl.when(tile > 0)
            def _():
                issue_scatter(pdst_ref)

    def compute(with_prefetch, r0=0, nrows=None):
        nrows = tm if nrows is None else nrows
        rows = pl.ds(r0, nrows)
        xb = jnp.concatenate([xbuf.at[slot][pl.ds(r0 * SUBLANES + j, nrows, stride=SUBLANES), :]
                              for j in range(SUBLANES)], axis=1).astype(BF16)
        if with_prefetch:
            for r in range(tm):
                _tile_copy(h2_hbm, nsrc_ref[0, r], xbuf.at[1 - slot], r, gsem.at[1 - slot]).start(priority=r % 2)
                _tile_copy(ybuf, r, y_hbm, pdst_ref[0, r], ssem).start(priority=r % 2)
        g = jnp.dot(xb, wg_grp[e_in_group], preferred_element_type=F32)
        u = jnp.dot(xb, wu_grp[e_in_group], preferred_element_type=F32)
        hmid = (g * _sigmoid(g) * u).astype(BF16)
        y = jnp.dot(hmid, wd_grp[e_in_group], preferred_element_type=F32)
        w_b = jnp.where((flag & FLAG_HI) != 0, whi_ref[rows, :], wlo_ref[rows, :])
        row = r0 + lax.broadcasted_iota(jnp.int32, (nrows, 1), 0)
        y = jnp.where((row >= lo) & (row < hi), y * jnp.concatenate([w_b] * SUBLANES, axis=1), 0.0)
        if with_prefetch:
            acc[...] = y
        elif nrows < tm:
            acc[rows, :] += y
        else:
            @pl.when(first)
            def _():
                acc[...] = y

            @pl.when(jnp.logical_not(first))
            def _():
                acc[...] += y

    @pl.when(prefetch)
    def _():
        compute(True)

    half = tm // 2
    plain = jnp.logical_not(prefetch) & (hi > lo)
    in_low = jnp.logical_not(first) & (hi <= half)
    in_high = jnp.logical_not(first) & (lo >= half)

    @pl.when(plain & jnp.logical_not(in_low | in_high))
    def _():
        compute(False)

    @pl.when(plain & in_low)
    def _():
        compute(False, 0, half)

    @pl.when(plain & in_high)
    def _():
        compute(False, half, tm - half)

    @pl.when((flag & FLAG_LAST) != 0)
    def _():
        whole = pltpu.make_async_copy(ybuf, y_hbm.at[pl.ds(0, tm * SUBLANES)], ssem)

        @pl.when(tile > 0)
        def _():
            whole.wait()

        _store_token_tiles(ybuf, acc[...])

        @pl.when((flag & FLAG_FINAL) != 0)
        def _():
            issue_scatter(dst_ref)
            whole.wait()


def _experts(h2, plan, w_gate, w_up, w_down, layer, tm):
    t = h2.shape[0] // SUBLANES
    vt, ve, vlo, vhi, vflag, order, wlo_b, whi_b = plan
    n_vis = vt.shape[0]
    n_tiles = t // tm
    d, hid = w_gate.shape[-2:]
    grp = MOE_EXPERTS_PER_GROUP
    assert d == SUBLANES * LANES
    idx_spec = lambda nxt: pl.BlockSpec(
        (None, 1, tm), lambda v, vt, *_: (jnp.clip(vt[v] + nxt, 0, n_tiles - 1), 0, 0), memory_space=pltpu.SMEM)
    w_tile = pl.BlockSpec((tm, LANES), lambda v, vt, *_: (vt[v], 0))
    any_spec = pl.BlockSpec(memory_space=pl.ANY)
    grid_spec = pltpu.PrefetchScalarGridSpec(
        num_scalar_prefetch=5,
        grid=(n_vis,),
        in_specs=[idx_spec(0), idx_spec(1), idx_spec(0), idx_spec(-1), w_tile, w_tile,
                  any_spec, any_spec, any_spec, any_spec],
        out_specs=any_spec,
        scratch_shapes=[pltpu.VMEM((2, tm * SUBLANES, LANES), F32), pltpu.VMEM((tm, d), F32),
                        pltpu.VMEM((tm * SUBLANES, LANES), F32),
                        pltpu.VMEM((grp, d, hid), BF16), pltpu.VMEM((grp, d, hid), BF16),
                        pltpu.VMEM((grp, hid, d), BF16),
                        pltpu.VMEM((d, hid), F32), pltpu.VMEM((d, hid), F32), pltpu.VMEM((hid, d), F32),
                        pltpu.SemaphoreType.DMA((2,)), pltpu.SemaphoreType.DMA(()), pltpu.SemaphoreType.DMA((3,))],
    )
    idx3 = order.reshape(n_tiles, 1, tm)
    return pl.pallas_call(
        functools.partial(_expert_kernel, n_tiles=n_tiles, layer=layer),
        grid_spec=grid_spec,
        out_shape=jax.ShapeDtypeStruct((t * SUBLANES, LANES), F32),
        compiler_params=_cparams("arbitrary"),
        name="moe_experts",
    )(vt, ve, vlo, vhi, vflag, idx3, idx3, idx3, idx3, wlo_b, whi_b, h2, w_gate, w_up, w_down)


def _visit_plan(route, tm):
    t = route.shape[0]
    i32 = jnp.int32
    cls_lo = jnp.asarray(np.array([p[0] for p in PAIR_CLASSES], np.int32))
    cls_hi = jnp.asarray(np.array([p[1] for p in PAIR_CLASSES], np.int32))
    cls_key = cls_lo * MOE_N_EXPERTS + cls_hi
    key = route[:, ROUTE_E0].astype(i32) * MOE_N_EXPERTS + route[:, ROUTE_E1].astype(i32)
    idx_bits = max(1, (t - 1).bit_length())
    assert (MOE_N_EXPERTS * MOE_N_EXPERTS) << idx_bits <= 2 ** 31
    packed, wlo, whi = lax.sort((key * (1 << idx_bits) + lax.iota(i32, t), route[:, ROUTE_W0], route[:, ROUTE_W1]),
                                num_keys=1)
    order = packed & ((1 << idx_bits) - 1)
    counts = jnp.sum((key[:, None] == cls_key[None, :]).astype(i32), axis=0)
    n_tiles = t // tm

    def segment_visits(seg_counts, seg_expert):
        n_seg = seg_counts.shape[0]
        ends = jnp.cumsum(seg_counts)
        starts = ends - seg_counts
        first_t = starts // tm
        nvis = jnp.where(seg_counts > 0, jnp.maximum(ends - 1, 0) // tm - first_t + 1, 0)
        cv_end = jnp.cumsum(nvis)
        cv_start = cv_end - nvis
        v = jnp.arange(n_tiles + n_seg, dtype=i32)
        active = v < cv_end[-1]
        s = jnp.minimum(jnp.sum((cv_end[None, :] <= v[:, None]).astype(i32), axis=1), n_seg - 1)
        tile = first_t[s] + v - cv_start[s]
        lo = jnp.clip(starts[s] - tile * tm, 0, tm)
        hi = jnp.clip(ends[s] - tile * tm, 0, tm)
        return tile, seg_expert[s], lo, hi, active

    run_id = np.cumsum([0] + [int(a[0] != b[0]) for a, b in zip(PAIR_CLASSES[:-1], PAIR_CLASSES[1:])])
    run_lo = jnp.asarray(np.array([PAIR_CLASSES[list(run_id).index(r)][0] for r in range(run_id[-1] + 1)], np.int32))
    in_run = jnp.asarray(run_id[None, :] == np.arange(run_id[-1] + 1)[:, None])
    run_counts = jnp.sum(jnp.where(in_run, counts[None, :], 0), axis=1)
    parts = [segment_visits(run_counts, run_lo) + (0,), segment_visits(counts, cls_hi) + (1,)]
    tile = jnp.concatenate([p[0] for p in parts])
    e = jnp.concatenate([p[1] for p in parts])
    lo = jnp.concatenate([p[2] for p in parts])
    hi = jnp.concatenate([p[3] for p in parts])
    active = jnp.concatenate([p[4] for p in parts])
    is_hi = jnp.concatenate([jnp.full(p[0].shape, p[5], i32) for p in parts])
    tile_stride = 2 * (tm + 1)
    order_key = jnp.where(active, tile * tile_stride + lo * 2 + is_hi, jnp.iinfo(jnp.int32).max)
    _, tile, e, lo, hi, is_hi, active = lax.sort((order_key, tile, e, lo, hi, is_hi, active.astype(i32)), num_keys=1)
    active = active == 1
    n_active = jnp.sum(active.astype(i32))
    v = jnp.arange(tile.shape[0], dtype=i32)
    e_last = jnp.max(jnp.where(active, v, -1))
    e = jnp.where(active, e, e[e_last])
    tile = jnp.where(active, tile, n_tiles - 1)
    lo = jnp.where(active, lo, 0)
    hi = jnp.where(active, hi, 0)
    prev_t = jnp.concatenate([jnp.full((1,), -1, i32), tile[:-1]])
    next_t = jnp.concatenate([tile[1:], jnp.full((1,), -1, i32)])
    is_final = v == n_active - 1
    is_last = (next_t != tile) | is_final
    group = e // MOE_EXPERTS_PER_GROUP
    prev_g = jnp.concatenate([jnp.full((1,), -1, i32), group[:-1]])
    flag = jnp.where(active, (prev_t != tile) * FLAG_FIRST + is_last * FLAG_LAST + is_final * FLAG_FINAL
                     + (prev_g != group) * FLAG_NEW_GROUP + is_hi * FLAG_HI, 0)
    cast = lambda z: z.astype(i32)
    bcast = lambda w: jnp.broadcast_to(w[:, None], (t, LANES))
    return cast(tile), cast(e), cast(lo), cast(hi), cast(flag), cast(order), bcast(wlo), bcast(whi)


def _combine_kernel(x_ref, y_ref, g2_ref, lg_ref, lb_ref, o_ref, *, alpha):
    y = _load_token_tiles(y_ref, x_ref.shape[0])
    o_ref[...] = _layer_norm(alpha * x_ref[...] + g2_ref[...] * y, lg_ref[...], lb_ref[...])


def _combine(x1, y, row0, mod3, mod_row0, rows_per_mod, ln_g, ln_b, alpha, tm):
    t, d = x1.shape
    tiles_per_mod = rows_per_mod // tm
    t0 = row0 // tm
    return pl.pallas_call(
        functools.partial(_combine_kernel, alpha=alpha),
        grid=(t // tm,),
        in_specs=[pl.BlockSpec((tm, d), lambda i: (i, 0)),
                  pl.BlockSpec((tm * SUBLANES, LANES), lambda i: (t0 + i, 0)),
                  pl.BlockSpec((None, 1, d), lambda i: (mod_row0 + i // tiles_per_mod, 0, 5)),
                  _const_spec((1, d)), _const_spec((1, d))],
        out_specs=pl.BlockSpec((tm, d), lambda i: (i, 0)),
        out_shape=jax.ShapeDtypeStruct((t, d), F32),
        compiler_params=_cparams("arbitrary"),
        name="combine_ln2",
    )(x1, y, mod3, ln_g.reshape(1, d), ln_b.reshape(1, d))


ROW_TILE = 512
SMALL_ROW_TILE = 256


def _pick_tile(n, pref):
    tm = min(pref, n)
    while n % tm:
        tm //= 2
    return tm


def kernel(x, c, ctx, c_ctx, ada_w, ada_b, w_in, gm_ln_g, gm_ln_b, gm_ws, gm_bs, hy_conv_w, hy_conv_b,
           hy_f_w1, hy_f_b1, hy_f_w2, hy_f_b2, hy_f_w3, hy_f_b3, hy_skip, da_lq1, da_lk1, da_lq2, da_lk2,
           da_norm_g, p_a, p_b, p_c, w_out, ln1_g, ln1_b, moe_wg, moe_bg, moe_we, moe_be,
           ex_w_gate, ex_w_up, ex_w_down, ln2_g, ln2_b):
    B, L, D = x.shape
    Lc = ctx.shape[1]
    depth = ada_w.shape[0]
    alpha = (2.0 * depth) ** 0.25
    T, Tc = B * L, B * Lc

    mp = -(-(B + 1) // 8) * 8
    c_all = jnp.zeros((mp, D), F32).at[:B].set(c).at[B].set(c_ctx)
    mod = _modulation(c_all, ada_w, ada_b)

    rope_tabs = _rope_tables(L // GRID_W)
    cm, sm = _dft_tables(L)
    cm_bf, sm_bf = cm.astype(BF16), sm.astype(BF16)
    cmc, smc = _dft_tables(Lc)
    cmc_bf, smc_bf = cmc.astype(BF16), smc.astype(BF16)

    seg_all = ((OFF_GM, OFF_HY, "gm"), (OFF_HY, OFF_Q, "hy"), (OFF_Q, OFF_K, "q"), (OFF_K, OFF_V, "k"),
               (OFF_V, OFF_GATE, "v"), (OFF_GATE, OFF_GATE + N_BRANCH * D, "gate"))
    seg_kv = ((0, DA_QK_W, "k"), (DA_QK_W, DA_QK_W + DA_V_W, "v"))

    tm_l = _pick_tile(L, ROW_TILE)
    tm_c = _pick_tile(Lc, SMALL_ROW_TILE)
    tq_l = _pick_tile(L, ROW_TILE)
    tq_c = _pick_tile(Lc, SMALL_ROW_TILE)
    tm_m = _pick_tile(L, ROW_TILE)
    tm_mc = _pick_tile(Tc, ROW_TILE)

    xs = x.reshape(T, D)
    xc = ctx.reshape(Tc, D)
    for l in range(depth):
        last = l == depth - 1
        lam_init = 0.8 - 0.6 * math.exp(-0.3 * l)
        mod3 = mod[l].reshape(mp, 1, 6 * D)
        w_l = w_in[l].astype(BF16)
        lparams = [a[l].reshape(1, DA_HEAD_DIM) for a in (da_lq1, da_lk1, da_lq2, da_lk2)]
        norm_g = da_norm_g[l].reshape(1, DA_V_DIM)
        lp = {
            "gm_ln_g": gm_ln_g[l].reshape(1, GM_DIM), "gm_ln_b": gm_ln_b[l].reshape(1, GM_DIM),
            "gm_ws": gm_ws[l].reshape(GM_GROUPS * GM_CHUNK, GM_CHUNK).astype(BF16),
            "gm_bs": jnp.repeat(jnp.transpose(gm_bs[l]), GM_DIM // GM_GROUPS, axis=1),
            "p_a": p_a[l].astype(BF16), "p_b": p_b[l].astype(BF16), "p_c": p_c[l].astype(BF16),
            "w_out": w_out[l].astype(BF16),
            "ln1_g": ln1_g[l].reshape(1, D), "ln1_b": ln1_b[l].reshape(1, D),
            "w_router": _split_bf16(jnp.zeros((D, LANES), F32).at[:, :MOE_GROUPS].set(moe_wg[l])
                                    .at[:, MOE_GROUPS:MOE_GROUPS + MOE_N_EXPERTS].set(moe_we[l])),
            "b_router": jnp.zeros((1, LANES), F32).at[0, :MOE_GROUPS].set(moe_bg[l])
                           .at[0, MOE_GROUPS:MOE_GROUPS + MOE_N_EXPERTS].set(moe_be[l]),
        }
        fw = (hy_f_w1[l], hy_f_b1[l], hy_f_w2[l], hy_f_b2[l], hy_f_w3[l], hy_f_b3[l])

        zgm, zhy, q, k, v, gate = _inproj(xs, mod3, 0, L, w_l, seg_all, rope_tabs, L, tm_l)
        if last:
            k_c, v_c = _inproj(xc, mod3, B, Tc, w_l[:, OFF_K:OFF_GATE], seg_kv, None, Lc, tm_c)
        else:
            zgm_c, zhy_c, q_c, k_c, v_c, gate_c = _inproj(xc, mod3, B, Tc, w_l, seg_all, None, Lc, tm_c)
        y_c = _attention(q, [(k, v, L), (k_c, v_c, Lc)], lparams, norm_g, lam_init, B, L, tq_l)
        kre, kim, nyq = _hyena_filter_spectrum(L, cm, sm, *fw)
        y_b = _hyena(zhy, B, L, hy_conv_w[l], hy_conv_b[l], cm_bf, sm_bf, kre, kim, nyq, hy_skip[l])
        t_all = T if last else T + Tc
        x1, h2, route = _merge(zgm, y_b, y_c, gate, xs, mod3, 0, L, lp, alpha, tm_m, t_all, 0, None)

        if not last:
            yc_c = _attention(q_c, [(k_c, v_c, Lc)], lparams, norm_g, lam_init, B, Lc, tq_c)
            kre_c, kim_c, nyq_c = _hyena_filter_spectrum(Lc, cmc, smc, *fw)
            yb_c = _hyena(zhy_c, B, Lc, hy_conv_w[l], hy_conv_b[l], cmc_bf, smc_bf, kre_c, kim_c, nyq_c,
                          hy_skip[l])
            x1c, h2, route = _merge(zgm_c, yb_c, yc_c, gate_c, xc, mod3, B, Tc, lp, alpha, tm_mc, t_all, T,
                                    (h2, route))

        moe_tm = _pick_tile(t_all, SMALL_ROW_TILE)
        plan = _visit_plan(route, moe_tm)
        y = _experts(h2, plan, ex_w_gate, ex_w_up, ex_w_down, l, moe_tm)
        xs = _combine(x1, y, 0, mod3, 0, L, ln2_g[l], ln2_b[l], alpha, tm_c)
        if not last:
            xc = _combine(x1c, y, T, mod3, B, Tc, ln2_g[l], ln2_b[l], alpha, tm_c)
    return xs.reshape(B, L, D)
```

```python
import functools
import math

import numpy as np
import jax
import jax.numpy as jnp
from jax import lax
from jax.experimental import pallas as pl
from jax.experimental.pallas import tpu as pltpu

F32 = jnp.float32
BF16 = jnp.bfloat16
HIGHEST = lax.Precision.HIGHEST

GRID_W = 64
GM_DIM = 256
GM_GROUPS = 4
GM_CHUNK = 128
HY_DIM = 256
HY_EMB = 33
HY_BANDS = (HY_EMB - 1) // 2
HY_DECAY_FAST = 0.3
HY_DECAY_SLOW = 1.5
HY_DECAY_TARGET = 1e-2
HY_DECAY_SHIFT = 0.05
DA_HEADS = 4
DA_HEAD_DIM = 64
DA_V_DIM = 2 * DA_HEAD_DIM
DA_QK_W = DA_HEADS * 2 * DA_HEAD_DIM
DA_V_W = DA_HEADS * DA_V_DIM
ROPE_BASE = 10000.0
N_BRANCH = 3
OFF_GM = 0
OFF_HY = OFF_GM + 2 * GM_DIM
OFF_Q = OFF_HY + 3 * HY_DIM
OFF_K = OFF_Q + DA_QK_W
OFF_V = OFF_K + DA_QK_W
OFF_GATE = OFF_V + DA_V_W
MOE_GROUPS = 4
MOE_EXPERTS_PER_GROUP = 8
MOE_N_EXPERTS = MOE_GROUPS * MOE_EXPERTS_PER_GROUP
MOE_TOP_K = 2
LN_EPS = 1e-5
LANES = 128
VMEM_LIMIT = 56 * 1024 * 1024


def _cparams(*sem):
    return pltpu.CompilerParams(dimension_semantics=sem, vmem_limit_bytes=VMEM_LIMIT)


def _sigmoid(x):
    return 1.0 / (1.0 + jnp.exp(-x))


def _layer_norm(x, g, b):
    mu = jnp.mean(x, axis=-1, keepdims=True)
    xc = x - mu
    var = jnp.mean(xc * xc, axis=-1, keepdims=True)
    return xc * lax.rsqrt(var + LN_EPS) * g + b


def _gelu_tanh(x):
    return 0.5 * x * (1.0 + jnp.tanh(math.sqrt(2.0 / math.pi) * (x + 0.044715 * (x * x * x))))


def _const_spec(shape):
    nd = len(shape)
    return pl.BlockSpec(shape, lambda *_: (0,) * nd)


def _mod_kernel(c_ref, w_ref, b_ref, o_ref):
    c = c_ref[...]
    s = c * _sigmoid(c)
    o_ref[...] = jnp.dot(s, w_ref[...], precision=HIGHEST, preferred_element_type=F32) + b_ref[...]


def _modulation(c_all, ada_w, ada_b):
    depth, d, n = ada_w.shape
    mp = c_all.shape[0]
    tn = 512
    return pl.pallas_call(
        _mod_kernel,
        grid=(depth, n // tn),
        in_specs=[pl.BlockSpec((mp, d), lambda l, j: (0, 0)),
                  pl.BlockSpec((None, d, tn), lambda l, j: (l, 0, j)),
                  pl.BlockSpec((None, 1, tn), lambda l, j: (l, 0, j))],
        out_specs=pl.BlockSpec((None, mp, tn), lambda l, j: (l, 0, j)),
        out_shape=jax.ShapeDtypeStruct((depth, mp, n), F32),
        compiler_params=_cparams("arbitrary", "arbitrary"),
        name="adaln_mod",
    )(c_all, ada_w, ada_b.reshape(depth, 1, n))


def _rope_tables(rows):
    n_freq = DA_HEAD_DIM // 4
    row = jnp.broadcast_to(jnp.arange(rows)[:, None], (rows, GRID_W)).reshape(-1).astype(F32)
    col = jnp.broadcast_to(jnp.arange(GRID_W)[None, :], (rows, GRID_W)).reshape(-1).astype(F32)
    inv = ROPE_BASE ** (-jnp.arange(n_freq, dtype=F32) / n_freq)
    ang_r = row[:, None] * inv
    ang_c = col[:, None] * inv
    c64 = jnp.concatenate([jnp.cos(ang_r), jnp.cos(ang_r), jnp.cos(ang_c), jnp.cos(ang_c)], axis=-1)
    s64 = jnp.concatenate([-jnp.sin(ang_r), jnp.sin(ang_r), -jnp.sin(ang_c), jnp.sin(ang_c)], axis=-1)
    return jnp.tile(c64, (1, LANES // DA_HEAD_DIM)), jnp.tile(s64, (1, LANES // DA_HEAD_DIM))


def _rope_block(xb, cos, sin):
    lane = lax.broadcasted_iota(jnp.int32, xb.shape, 1)
    n_freq = DA_HEAD_DIM // 4
    first_half = (lane % (2 * n_freq)) < n_freq
    partner = jnp.where(first_half, pltpu.roll(xb, LANES - n_freq, 1), pltpu.roll(xb, n_freq, 1))
    return xb * cos + partner * sin


def _inproj_kernel(*refs, segs, use_rope, n_chunk):
    if use_rope:
        x_ref, sh_ref, sc_ref, w_ref, cos_ref, sin_ref = refs[:6]
        out_refs = refs[6:]
    else:
        x_ref, sh_ref, sc_ref, w_ref = refs[:4]
        out_refs = refs[4:]
    h = (x_ref[...] * (1.0 + sc_ref[...]) + sh_ref[...]).astype(BF16)
    for (a, b, kind), o_ref in zip(segs, out_refs):
        for c0 in range(a, b, n_chunk):
            c1 = min(c0 + n_chunk, b)
            acc = jnp.dot(h, w_ref[:, c0:c1], preferred_element_type=F32)
            if kind == "q":
                acc = acc * (DA_HEAD_DIM ** -0.5 * math.log2(math.e))
            if use_rope and kind in ("q", "k"):
                cos = cos_ref[...]
                sin = sin_ref[...]
                for j in range((c1 - c0) // LANES):
                    blk = _rope_block(acc[:, j * LANES:(j + 1) * LANES], cos, sin)
                    o_ref[:, c0 - a + j * LANES:c0 - a + (j + 1) * LANES] = blk.astype(o_ref.dtype)
            else:
                o_ref[:, c0 - a:c1 - a] = acc.astype(o_ref.dtype)


def _inproj(x2d, mod3, mod_row0, rows_per_mod, w, segs, rope_tabs, seq_len, tm):
    t, d = x2d.shape
    n = w.shape[1]
    use_rope = rope_tabs is not None
    tiles_per_mod = rows_per_mod // tm
    tiles_per_seq = seq_len // tm

    def mod_map(piece):
        return lambda i: (mod_row0 + i // tiles_per_mod, 0, piece)

    in_specs = [pl.BlockSpec((tm, d), lambda i: (i, 0)),
                pl.BlockSpec((None, 1, d), mod_map(0)),
                pl.BlockSpec((None, 1, d), mod_map(1)),
                pl.BlockSpec((d, n), lambda i: (0, 0), pipeline_mode=pl.Buffered(1))]
    args = [x2d, mod3, mod3, w]
    if use_rope:
        in_specs += [pl.BlockSpec((tm, LANES), lambda i: (i % tiles_per_seq, 0))] * 2
        args += list(rope_tabs)
    out_specs = [pl.BlockSpec((tm, b - a), lambda i: (i, 0)) for a, b, _ in segs]
    out_shape = [jax.ShapeDtypeStruct((t, b - a), BF16) for a, b, _ in segs]
    return pl.pallas_call(
        functools.partial(_inproj_kernel, segs=segs, use_rope=use_rope, n_chunk=512),
        grid=(t // tm,),
        in_specs=in_specs, out_specs=out_specs, out_shape=out_shape,
        compiler_params=_cparams("arbitrary"),
        name="inproj",
    )(*args)


ATTN_KEY_CHUNK = 256


def _attn_kernel(*refs, src_lens, lam_init):
    n_src = len(src_lens)
    lq1, lk1, lq2, lk2, g_ref, q_ref = refs[:6]
    kv_refs = refs[6:6 + 2 * n_src]
    o_ref, s_scr = refs[6 + 2 * n_src:]
    lam = (jnp.exp(jnp.sum(lq1[...] * lk1[...], axis=-1, keepdims=True))
           - jnp.exp(jnp.sum(lq2[...] * lk2[...], axis=-1, keepdims=True)) + lam_init)
    tq = q_ref.shape[0]
    lane = lax.broadcasted_iota(jnp.int32, (tq, LANES), 1)
    dn = (((1,), (1,)), ((), ()))
    chunks = []
    off = 0
    for j, n in enumerate(src_lens):
        kc = min(ATTN_KEY_CHUNK, n)
        for st in range(0, n, kc):
            chunks.append((j, st, kc, off))
            off += kc
    for h in range(DA_HEADS):
        cols = slice(h * LANES, (h + 1) * LANES)
        qh = q_ref[:, cols]
        zero = jnp.zeros_like(qh)
        om = []
        for m in range(2):
            qm = jnp.where(lane < DA_HEAD_DIM if m == 0 else lane >= DA_HEAD_DIM, qh, zero)
            mlane = None
            for j, st, kc, off in chunks:
                s_c = lax.dot_general(qm, kv_refs[2 * j][st:st + kc, cols], dn, preferred_element_type=F32)
                s_scr[:, off:off + kc] = s_c
                for b in range(kc // LANES):
                    blk = s_c[:, b * LANES:(b + 1) * LANES]
                    mlane = blk if mlane is None else jnp.maximum(mlane, blk)
            mx = jnp.max(mlane, axis=-1, keepdims=True)
            acc = None
            for j, st, kc, off in chunks:
                p = jnp.exp2(s_scr[:, off:off + kc] - mx).astype(BF16)
                v_aug = jnp.concatenate([kv_refs[2 * j + 1][st:st + kc, cols], jnp.ones((kc, LANES), BF16)],
                                        axis=1)
                d = jnp.dot(p, v_aug, preferred_element_type=F32)
                acc = d if acc is None else acc + d
            om.append(acc[:, :LANES] * (1.0 / acc[:, LANES:LANES + 1]))
        o = om[0] - lam * om[1]
        ms = jnp.mean(o * o, axis=-1, keepdims=True)
        o = o * lax.rsqrt(ms + LN_EPS) * g_ref[...] * (1.0 - lam_init)
        o_ref[:, cols] = o.astype(o_ref.dtype)


def _attention(q, kvs, lparams, norm_g, lam_init, nb, lq, tq):
    t = q.shape[0]
    qt = lq // tq
    in_specs = [_const_spec((1, DA_HEAD_DIM))] * 4 + [_const_spec((1, DA_V_DIM))]
    in_specs.append(pl.BlockSpec((tq, DA_QK_W), lambda b, i: (b * qt + i, 0)))
    args = list(lparams) + [norm_g, q]
    for k, v, lk in kvs:
        in_specs += [pl.BlockSpec((lk, DA_QK_W), lambda b, i: (b, 0)),
                     pl.BlockSpec((lk, DA_V_W), lambda b, i: (b, 0))]
        args += [k, v]
    src_lens = tuple(lk for _, _, lk in kvs)
    return pl.pallas_call(
        functools.partial(_attn_kernel, src_lens=src_lens, lam_init=lam_init),
        grid=(nb, qt),
        in_specs=in_specs,
        out_specs=pl.BlockSpec((tq, DA_V_W), lambda b, i: (b * qt + i, 0)),
        out_shape=jax.ShapeDtypeStruct((t, DA_V_W), BF16),
        scratch_shapes=[pltpu.VMEM((tq, sum(src_lens)), F32)],
        compiler_params=_cparams("arbitrary", "arbitrary"),
        name="diff_attn",
    )(*args)


DFT_SPLIT = 64


def _dft_tables(L):
    n = jnp.arange(L, dtype=jnp.int32)[None, :]
    k1 = jnp.arange(L // DFT_SPLIT, dtype=jnp.int32)[:, None] * DFT_SPLIT
    k0 = jnp.arange(DFT_SPLIT, dtype=jnp.int32)[:, None]
    ang_a = ((k1 * n) % (2 * L)).astype(F32) * (math.pi / L)
    ang_b = ((k0 * n) % (2 * L)).astype(F32) * (math.pi / L)
    ca, sa = jnp.cos(ang_a)[:, None, :], jnp.sin(ang_a)[:, None, :]
    cb, sb = jnp.cos(ang_b)[None, :, :], jnp.sin(ang_b)[None, :, :]
    return (ca * cb - sa * sb).reshape(L, L), (sa * cb + ca * sb).reshape(L, L)


def _filter_consts(L):
    t = jnp.linspace(0.0, 1.0, L, dtype=F32)[:, None]
    w = 2.0 * math.pi * jnp.arange(L, dtype=F32)[:, None] / L
    f = jnp.linspace(1e-4, HY_BANDS - 1, HY_BANDS, dtype=F32)[None, :]
    emb = jnp.concatenate([t, jnp.cos(f * w), -jnp.sin(f * w)], axis=-1)
    max_decay = math.log(HY_DECAY_TARGET) / HY_DECAY_FAST
    min_decay = math.log(HY_DECAY_TARGET) / HY_DECAY_SLOW
    deltas = jnp.abs(jnp.linspace(min_decay, max_decay, HY_DIM, dtype=F32))
    window = jnp.exp(-t * deltas[None, :]) + HY_DECAY_SHIFT
    return emb, window


def _filter_kernel(emb_ref, win_ref, w1, b1, w2, b2, w3, b3, hs_ref, hd_ref, nyq_ref):
    h = jnp.sin(jnp.dot(emb_ref[...], w1[...], precision=HIGHEST, preferred_element_type=F32) + b1[...])
    h = jnp.sin(jnp.dot(h, w2[...], precision=HIGHEST, preferred_element_type=F32) + b2[...])
    h = jnp.dot(h, w3[...], precision=HIGHEST, preferred_element_type=F32) + b3[...]
    win = win_ref[...]
    hf = h[:, :HY_DIM] * win
    hb = h[:, HY_DIM:] * win
    row = lax.broadcasted_iota(jnp.int32, hf.shape, 0)
    hb = jnp.where(row == 0, 0.0, hb)
    alt = jnp.where(row % 2 == 0, 1.0, -1.0)
    hs_ref[...] = hf + hb
    hd_ref[...] = hf - hb
    nyq_ref[...] = jnp.sum((hf + hb) * alt, axis=0, keepdims=True)


def _spectrum_kernel(c_ref, s_ref, hs_ref, hd_ref, kre_ref, kim_ref, *, n_fft):
    i = pl.program_id(0)
    tk = c_ref.shape[0]
    kidx = i * tk + lax.broadcasted_iota(jnp.int32, (tk, 1), 0)
    scale = jnp.where(kidx == 0, 1.0 / n_fft, 2.0 / n_fft)
    kre = jnp.dot(c_ref[...], hs_ref[...], precision=HIGHEST, preferred_element_type=F32)
    kim = -jnp.dot(s_ref[...], hd_ref[...], precision=HIGHEST, preferred_element_type=F32)
    kre_ref[...] = kre * scale
    kim_ref[...] = kim * scale


def _hyena_filter_spectrum(L, cmat, smat, w1, b1, w2, b2, w3, b3):
    emb, window = _filter_consts(L)
    full = lambda a: _const_spec(a.shape)
    ins = [emb, window, w1, b1.reshape(1, -1), w2, b2.reshape(1, -1), w3, b3.reshape(1, -1)]
    hs, hd, nyq = pl.pallas_call(
        _filter_kernel,
        grid=(1,),
        in_specs=[full(a) for a in ins],
        out_specs=[_const_spec((L, HY_DIM)), _const_spec((L, HY_DIM)), _const_spec((1, HY_DIM))],
        out_shape=[jax.ShapeDtypeStruct((L, HY_DIM), F32), jax.ShapeDtypeStruct((L, HY_DIM), F32),
                   jax.ShapeDtypeStruct((1, HY_DIM), F32)],
        compiler_params=_cparams("arbitrary"),
        name="hyena_filter",
    )(*ins)
    tk = min(256, L)
    kre, kim = pl.pallas_call(
        functools.partial(_spectrum_kernel, n_fft=2 * L),
        grid=(L // tk,),
        in_specs=[pl.BlockSpec((tk, L), lambda i: (i, 0)), pl.BlockSpec((tk, L), lambda i: (i, 0)),
                  _const_spec((L, HY_DIM)), _const_spec((L, HY_DIM))],
        out_specs=[pl.BlockSpec((tk, HY_DIM), lambda i: (i, 0))] * 2,
        out_shape=[jax.ShapeDtypeStruct((L, HY_DIM), F32)] * 2,
        compiler_params=_cparams("arbitrary"),
        name="hyena_spectrum",
    )(cmat, smat, hs, hd)
    return kre, kim, nyq * (1.0 / (2 * L))


HY_ROW_BLOCK = 1024


def _hyena_kernel(z_ref, cw_ref, cb_ref, c_ref, s_ref, kre_ref, kim_ref, nyq_ref, skip_ref, o_ref,
                  u_ref, x0_ref, p_ref, q_ref):
    L = z_ref.shape[0]
    row = lax.broadcasted_iota(jnp.int32, (L, HY_DIM), 0)

    def conv(j):
        cols = slice(j * HY_DIM, (j + 1) * HY_DIM)
        z = z_ref[:, cols].astype(F32)
        zprev = jnp.where(row == 0, 0.0, pltpu.roll(z, 1, 0))
        znext = jnp.where(row == L - 1, 0.0, pltpu.roll(z, L - 1, 0))
        return zprev * cw_ref[0:1, cols] + z * cw_ref[1:2, cols] + znext * cw_ref[2:3, cols] + cb_ref[:, cols]

    u = conv(2) * conv(1)
    ub = u.astype(BF16)
    u_ref[...] = u
    alt = jnp.where(row % 2 == 0, 1.0, -1.0)
    nyq_term = jnp.sum(u * alt, axis=0, keepdims=True) * nyq_ref[...]
    x0_ref[...] = conv(0)
    blk = min(HY_ROW_BLOCK, L)
    for r in range(0, L, blk):
        rows = slice(r, r + blk)
        a = jnp.dot(c_ref[rows, :], ub, preferred_element_type=F32)
        b = jnp.dot(s_ref[rows, :], ub, preferred_element_type=F32)
        kre = kre_ref[rows, :]
        kim = kim_ref[rows, :]
        p_ref[rows, :] = (a * kre + b * kim).astype(BF16)
        q_ref[rows, :] = (b * kre - a * kim).astype(BF16)
    for r in range(0, L, blk):
        rows = slice(r, r + blk)
        y = (jnp.dot(c_ref[rows, :], p_ref[...], preferred_element_type=F32)
             + jnp.dot(s_ref[rows, :], q_ref[...], preferred_element_type=F32))
        ub_rows = u_ref[rows, :]
        row_b = lax.broadcasted_iota(jnp.int32, (blk, HY_DIM), 0)
        y = y + jnp.where(row_b % 2 == 0, nyq_term, -nyq_term) + ub_rows * skip_ref[...]
        o_ref[rows, :] = (y * x0_ref[rows, :]).astype(o_ref.dtype)


def _hyena(zhy, nb, L, conv_w, conv_b, cmat_bf, smat_bf, kre, kim, nyq, skip):
    t = zhy.shape[0]
    return pl.pallas_call(
        _hyena_kernel,
        grid=(nb,),
        in_specs=[pl.BlockSpec((L, 3 * HY_DIM), lambda b: (b, 0)),
                  _const_spec((3, 3 * HY_DIM)), _const_spec((1, 3 * HY_DIM)),
                  pl.BlockSpec((L, L), lambda b: (0, 0), pipeline_mode=pl.Buffered(1)),
                  pl.BlockSpec((L, L), lambda b: (0, 0), pipeline_mode=pl.Buffered(1)),
                  _const_spec((L, HY_DIM)), _const_spec((L, HY_DIM)),
                  _const_spec((1, HY_DIM)), _const_spec((1, HY_DIM))],
        out_specs=pl.BlockSpec((L, HY_DIM), lambda b: (b, 0)),
        out_shape=jax.ShapeDtypeStruct((t, HY_DIM), BF16),
        scratch_shapes=[pltpu.VMEM((L, HY_DIM), F32), pltpu.VMEM((L, HY_DIM), F32),
                        pltpu.VMEM((L, HY_DIM), BF16), pltpu.VMEM((L, HY_DIM), BF16)],
        compiler_params=_cparams("arbitrary"),
        name="hyena_conv",
    )(zhy, conv_w, conv_b.reshape(1, -1), cmat_bf, smat_bf, kre, kim, nyq, skip.reshape(1, -1))


SUBLANES = 8


def _split_bf16(w):
    hi = w.astype(BF16)
    return jnp.stack([hi, (w - hi.astype(F32)).astype(BF16)])


def _store_token_tiles(ref, val):
    n = val.shape[0]
    for j in range(val.shape[1] // LANES):
        ref[pl.ds(j, n, stride=SUBLANES), :] = val[:, j * LANES:(j + 1) * LANES]


def _load_token_tiles(ref, n):
    return jnp.concatenate([ref[pl.ds(j, n, stride=SUBLANES), :] for j in range(SUBLANES)], axis=1)


def _merge_kernel(*refs, alpha, n_alias, n_real):
    h2_ref, rt_ref = refs[-2:]
    i = pl.program_id(0)

    @pl.when(i < n_real)
    def _():
        _merge_tile(*refs, alpha=alpha, n_alias=n_alias)

    @pl.when(i >= n_real)
    def _():
        h2_ref[...] = jnp.zeros_like(h2_ref)
        rt_ref[...] = jnp.zeros_like(rt_ref)


def _merge_tile(*refs, alpha, n_alias):
    (zgm_ref, yb_ref, yc_ref, gate_ref, x_ref, g1_ref, sh2_ref, sc2_ref, lng_ref, lnb_ref, ws_ref, bs_ref,
     pa_ref, pb_ref, pc_ref, wo_ref, l1g_ref, l1b_ref, wr_ref, br_ref) = refs[:20]
    x1_ref, h2_ref, rt_ref = refs[20 + n_alias:]
    tm = x_ref.shape[0]
    d = x_ref.shape[1]
    gm = _gelu_tanh(zgm_ref[...].astype(F32))
    u = gm[:, :GM_DIM]
    v = _layer_norm(gm[:, GM_DIM:], lng_ref[...], lnb_ref[...]).astype(BF16)
    lane_group = lax.broadcasted_iota(jnp.int32, (GM_CHUNK, GM_DIM), 1) // (GM_DIM // GM_GROUPS)
    ya = []
    for cidx in range(tm // GM_CHUNK):
        rows = slice(cidx * GM_CHUNK, (cidx + 1) * GM_CHUNK)
        r = jnp.dot(ws_ref[...], v[rows], preferred_element_type=F32)
        vv = bs_ref[...]
        for g in range(GM_GROUPS):
            vv = vv + jnp.where(lane_group == g, r[g * GM_CHUNK:(g + 1) * GM_CHUNK], 0.0)
        ya.append(u[rows] * vv)
    ya = jnp.concatenate(ya, axis=0) if len(ya) > 1 else ya[0]
    ma = jnp.dot(ya.astype(BF16), pa_ref[...], preferred_element_type=F32)
    mb = jnp.dot(yb_ref[...], pb_ref[...], preferred_element_type=F32)
    mc = jnp.dot(yc_ref[...], pc_ref[...], preferred_element_type=F32)
    merged = (_sigmoid(gate_ref[:, 0:d].astype(F32)) * ma
              + _sigmoid(gate_ref[:, d:2 * d].astype(F32)) * mb
              + _sigmoid(gate_ref[:, 2 * d:3 * d].astype(F32)) * mc)
    out = jnp.dot(merged.astype(BF16), wo_ref[...], preferred_element_type=F32)
    x1 = _layer_norm(alpha * x_ref[...] + g1_ref[...] * out, l1g_ref[...], l1b_ref[...])
    x1_ref[...] = x1
    h2 = x1 * (1.0 + sc2_ref[...]) + sh2_ref[...]
    _store_token_tiles(h2_ref, h2)
    h2_hi = h2.astype(BF16)
    h2_lo = (h2 - h2_hi.astype(F32)).astype(BF16)
    lg = (jnp.dot(h2_hi, wr_ref[0], preferred_element_type=F32)
          + jnp.dot(h2_hi, wr_ref[1], preferred_element_type=F32)
          + jnp.dot(h2_lo, wr_ref[0], preferred_element_type=F32) + br_ref[...])
    rt_ref[...] = _route(lg)


ROUTE_E0, ROUTE_E1, ROUTE_W0, ROUTE_W1 = 0, 1, 2, 3


def _route(lg):
    neg = jnp.float32(-3.0e38)
    lane_i = lax.broadcasted_iota(jnp.int32, lg.shape, 1)
    lane = lane_i.astype(F32)
    big = jnp.float32(LANES)
    is_g = lane_i < MOE_GROUPS
    gl = jnp.where(is_g, lg, neg)
    gmax = jnp.max(gl, axis=-1, keepdims=True)
    g_idx = jnp.min(jnp.where(gl == gmax, lane, big), axis=-1, keepdims=True)
    g_prob = 1.0 / jnp.sum(jnp.where(is_g, jnp.exp(gl - gmax), 0.0), axis=-1, keepdims=True)
    e_lo = MOE_GROUPS + MOE_EXPERTS_PER_GROUP * g_idx
    el = jnp.where(lane >= e_lo, jnp.where(lane < e_lo + MOE_EXPERTS_PER_GROUP, lg, neg), neg)
    v1 = jnp.max(el, axis=-1, keepdims=True)
    i1 = jnp.min(jnp.where(el == v1, lane, big), axis=-1, keepdims=True)
    el2 = jnp.where(lane == i1, neg, el)
    v2 = jnp.max(el2, axis=-1, keepdims=True)
    i2 = jnp.min(jnp.where(el2 == v2, lane, big), axis=-1, keepdims=True)
    e21 = jnp.exp(v2 - v1)
    w1 = g_prob / (1.0 + e21)
    w2 = w1 * e21
    swap = i2 < i1
    rec = jnp.where(lane_i == ROUTE_E0, jnp.minimum(i1, i2) - MOE_GROUPS, 0.0)
    rec = jnp.where(lane_i == ROUTE_E1, jnp.maximum(i1, i2) - MOE_GROUPS, rec)
    rec = jnp.where(lane_i == ROUTE_W0, jnp.where(swap, w2, w1), rec)
    return jnp.where(lane_i == ROUTE_W1, jnp.where(swap, w1, w2), rec)


def _merge(zgm, yb, yc, gate, x2d, mod3, mod_row0, rows_per_mod, lp, alpha, tm, t_all, row0, prev):
    t, d = x2d.shape
    tiles_per_mod = rows_per_mod // tm
    off = row0 // tm
    n_real = t // tm
    n_fill = (t_all - row0 - t) // tm if prev is None else 0
    real = lambda i: jnp.minimum(i, n_real - 1)

    def mod_map(piece):
        return lambda i: (mod_row0 + real(i) // tiles_per_mod, 0, piece)

    row = lambda w: pl.BlockSpec((tm, w), lambda i: (real(i), 0))
    row_off = lambda w: pl.BlockSpec((tm, w), lambda i: (off + i, 0))
    consts = [lp["gm_ln_g"], lp["gm_ln_b"], lp["gm_ws"], lp["gm_bs"], lp["p_a"], lp["p_b"], lp["p_c"],
              lp["w_out"], lp["ln1_g"], lp["ln1_b"], lp["w_router"], lp["b_router"]]
    in_specs = [row(2 * GM_DIM), row(HY_DIM), row(DA_V_W), row(N_BRANCH * d), row(d),
                pl.BlockSpec((None, 1, d), mod_map(2)), pl.BlockSpec((None, 1, d), mod_map(3)),
                pl.BlockSpec((None, 1, d), mod_map(4))] + [_const_spec(a.shape) for a in consts]
    args = [zgm, yb, yc, gate, x2d, mod3, mod3, mod3, *consts]
    aliases = {}
    if prev is not None:
        aliases = {len(args): 1, len(args) + 1: 2}
        in_specs += [pl.BlockSpec(memory_space=pl.ANY)] * 2
        args += list(prev)
    return pl.pallas_call(
        functools.partial(_merge_kernel, alpha=alpha, n_alias=len(aliases), n_real=n_real),
        grid=(n_real + n_fill,),
        in_specs=in_specs,
        out_specs=[row(d), pl.BlockSpec((tm * SUBLANES, LANES), lambda i: (off + i, 0)), row_off(LANES)],
        out_shape=[jax.ShapeDtypeStruct((t, d), F32), jax.ShapeDtypeStruct((t_all * SUBLANES, LANES), F32),
                   jax.ShapeDtypeStruct((t_all, LANES), F32)],
        input_output_aliases=aliases,
        compiler_params=_cparams("arbitrary"),
        name="merge_ln1",
    )(*args)


PAIR_CLASSES = tuple((MOE_EXPERTS_PER_GROUP * g + a, MOE_EXPERTS_PER_GROUP * g + b)
                     for g in range(MOE_GROUPS)
                     for a in range(MOE_EXPERTS_PER_GROUP) for b in range(a + 1, MOE_EXPERTS_PER_GROUP))
FLAG_FIRST, FLAG_LAST, FLAG_FINAL, FLAG_NEW_GROUP, FLAG_HI = 1, 2, 4, 8, 16


def _tile_copy(src, src_row, dst, dst_row, sem):
    s0 = pl.multiple_of(src_row * SUBLANES, SUBLANES)
    d0 = pl.multiple_of(dst_row * SUBLANES, SUBLANES)
    return pltpu.make_async_copy(src.at[pl.ds(s0, SUBLANES)], dst.at[pl.ds(d0, SUBLANES)], sem)


def _expert_kernel(vt_ref, ve_ref, vlo_ref, vhi_ref, vflag_ref, src_ref, nsrc_ref, dst_ref, pdst_ref,
                   wlo_ref, whi_ref, h2_hbm, wg_hbm, wu_hbm, wd_hbm, y_hbm,
                   xbuf, acc, ybuf, wg_grp, wu_grp, wd_grp, wg_stage, wu_stage, wd_stage, gsem, ssem, wsem,
                   *, n_tiles, layer):
    v = pl.program_id(0)
    tile, expert, lo, hi, flag = vt_ref[v], ve_ref[v], vlo_ref[v], vhi_ref[v], vflag_ref[v]
    tm = acc.shape[0]
    slot = tile % 2
    first = (flag & FLAG_FIRST) != 0
    group0 = pl.multiple_of((expert // MOE_EXPERTS_PER_GROUP) * MOE_EXPERTS_PER_GROUP, MOE_EXPERTS_PER_GROUP)
    e_in_group = expert - group0

    @pl.when((flag & FLAG_NEW_GROUP) != 0)
    def _():
        streams = ((wg_hbm, wg_stage, wg_grp), (wu_hbm, wu_stage, wu_grp), (wd_hbm, wd_stage, wd_grp))
        for k in range(MOE_EXPERTS_PER_GROUP):
            copies = [pltpu.make_async_copy(w_hbm.at[layer, group0 + k], stage, wsem.at[i])
                      for i, (w_hbm, stage, _) in enumerate(streams)]
            for cp in copies:
                cp.start()
            for cp, (_, stage, w_grp) in zip(copies, streams):
                cp.wait()
                w_grp[k] = stage[...].astype(BF16)

    def issue_gather(idx_ref, to_slot):
        def body(i, carry):
            for j in range(SUBLANES):
                r = i * SUBLANES + j
                _tile_copy(h2_hbm, idx_ref[0, r], xbuf.at[to_slot], r, gsem.at[to_slot]).start(priority=j % 2)
            return carry

        lax.fori_loop(0, tm // SUBLANES, body, 0)

    def issue_scatter(idx_ref):
        def body(i, carry):
            for j in range(SUBLANES):
                r = i * SUBLANES + j
                _tile_copy(ybuf, r, y_hbm, idx_ref[0, r], ssem).start(priority=j % 2)
            return carry

        lax.fori_loop(0, tm // SUBLANES, body, 0)

    has_next = tile + 1 < n_tiles
    prefetch = first & (tile > 0) & has_next

    @pl.when(first)
    def _():
        @pl.when(tile == 0)
        def _():
            issue_gather(src_ref, slot)

        pltpu.make_async_copy(h2_hbm.at[pl.ds(0, tm * SUBLANES)], xbuf.at[slot], gsem.at[slot]).wait()

        @pl.when(jnp.logical_not(prefetch))
        def _():
            @pl.when(has_next)
            def _():
                issue_gather(nsrc_ref, 1 - slot)

            @pl.when(tile > 0)
            def _():
                issue_scatter(pdst_ref)

    def compute(with_prefetch, r0=0, nrows=None):
        nrows = tm if nrows is None else nrows
        rows = pl.ds(r0, nrows)
        xb = jnp.concatenate([xbuf.at[slot][pl.ds(r0 * SUBLANES + j, nrows, stride=SUBLANES), :]
                              for j in range(SUBLANES)], axis=1).astype(BF16)
        g = jnp.dot(xb, wg_grp[e_in_group], preferred_element_type=F32)
        u = jnp.dot(xb, wu_grp[e_in_group], preferred_element_type=F32)
        hmid = (g * _sigmoid(g) * u).astype(BF16)
        y = jnp.dot(hmid, wd_grp[e_in_group], preferred_element_type=F32)
        w_b = jnp.where((flag & FLAG_HI) != 0, whi_ref[rows, :], wlo_ref[rows, :])
        row = r0 + lax.broadcasted_iota(jnp.int32, (nrows, 1), 0)
        y = jnp.where((row >= lo) & (row < hi), y * jnp.concatenate([w_b] * SUBLANES, axis=1), 0.0)
        if with_prefetch:
            for r in range(tm):
                _tile_copy(h2_hbm, nsrc_ref[0, r], xbuf.at[1 - slot], r, gsem.at[1 - slot]).start(priority=r % 2)
                _tile_copy(ybuf, r, y_hbm, pdst_ref[0, r], ssem).start(priority=r % 2)
            acc[...] = y
        elif nrows < tm:
            acc[rows, :] += y
        else:
            @pl.when(first)
            def _():
                acc[...] = y

            @pl.when(jnp.logical_not(first))
            def _():
                acc[...] += y

    @pl.when(prefetch)
    def _():
        compute(True)

    half = tm // 2
    plain = jnp.logical_not(prefetch) & (hi > lo)
    in_low = jnp.logical_not(first) & (hi <= half)
    in_high = jnp.logical_not(first) & (lo >= half)

    @pl.when(plain & jnp.logical_not(in_low | in_high))
    def _():
        compute(False)

    @pl.when(plain & in_low)
    def _():
        compute(False, 0, half)

    @pl.when(plain & in_high)
    def _():
        compute(False, half, tm - half)

    @pl.when((flag & FLAG_LAST) != 0)
    def _():
        whole = pltpu.make_async_copy(ybuf, y_hbm.at[pl.ds(0, tm * SUBLANES)], ssem)

        @pl.when(tile > 0)
        def _():
            whole.wait()

        _store_token_tiles(ybuf, acc[...])

        @pl.when((flag & FLAG_FINAL) != 0)
        def _():
            issue_scatter(dst_ref)
            whole.wait()


def _experts(h2, plan, w_gate, w_up, w_down, layer, tm):
    t = h2.shape[0] // SUBLANES
    vt, ve, vlo, vhi, vflag, order, wlo_b, whi_b = plan
    n_vis = vt.shape[0]
    n_tiles = t // tm
    d, hid = w_gate.shape[-2:]
    grp = MOE_EXPERTS_PER_GROUP
    assert d == SUBLANES * LANES
    idx_spec = lambda nxt: pl.BlockSpec(
        (None, 1, tm), lambda v, vt, *_: (jnp.clip(vt[v] + nxt, 0, n_tiles - 1), 0, 0), memory_space=pltpu.SMEM)
    w_tile = pl.BlockSpec((tm, LANES), lambda v, vt, *_: (vt[v], 0))
    any_spec = pl.BlockSpec(memory_space=pl.ANY)
    grid_spec = pltpu.PrefetchScalarGridSpec(
        num_scalar_prefetch=5,
        grid=(n_vis,),
        in_specs=[idx_spec(0), idx_spec(1), idx_spec(0), idx_spec(-1), w_tile, w_tile,
                  any_spec, any_spec, any_spec, any_spec],
        out_specs=any_spec,
        scratch_shapes=[pltpu.VMEM((2, tm * SUBLANES, LANES), F32), pltpu.VMEM((tm, d), F32),
                        pltpu.VMEM((tm * SUBLANES, LANES), F32),
                        pltpu.VMEM((grp, d, hid), BF16), pltpu.VMEM((grp, d, hid), BF16),
                        pltpu.VMEM((grp, hid, d), BF16),
                        pltpu.VMEM((d, hid), F32), pltpu.VMEM((d, hid), F32), pltpu.VMEM((hid, d), F32),
                        pltpu.SemaphoreType.DMA((2,)), pltpu.SemaphoreType.DMA(()), pltpu.SemaphoreType.DMA((3,))],
    )
    idx3 = order.reshape(n_tiles, 1, tm)
    return pl.pallas_call(
        functools.partial(_expert_kernel, n_tiles=n_tiles, layer=layer),
        grid_spec=grid_spec,
        out_shape=jax.ShapeDtypeStruct((t * SUBLANES, LANES), F32),
        compiler_params=_cparams("arbitrary"),
        name="moe_experts",
    )(vt, ve, vlo, vhi, vflag, idx3, idx3, idx3, idx3, wlo_b, whi_b, h2, w_gate, w_up, w_down)


def _visit_plan(route, tm):
    t = route.shape[0]
    i32 = jnp.int32
    cls_lo = jnp.asarray(np.array([p[0] for p in PAIR_CLASSES], np.int32))
    cls_hi = jnp.asarray(np.array([p[1] for p in PAIR_CLASSES], np.int32))
    cls_key = cls_lo * MOE_N_EXPERTS + cls_hi
    key = route[:, ROUTE_E0].astype(i32) * MOE_N_EXPERTS + route[:, ROUTE_E1].astype(i32)
    idx_bits = max(1, (t - 1).bit_length())
    assert (MOE_N_EXPERTS * MOE_N_EXPERTS) << idx_bits <= 2 ** 31
    packed, wlo, whi = lax.sort((key * (1 << idx_bits) + lax.iota(i32, t), route[:, ROUTE_W0], route[:, ROUTE_W1]),
                                num_keys=1)
    order = packed & ((1 << idx_bits) - 1)
    counts = jnp.sum((key[:, None] == cls_key[None, :]).astype(i32), axis=0)
    n_tiles = t // tm

    def segment_visits(seg_counts, seg_expert):
        n_seg = seg_counts.shape[0]
        ends = jnp.cumsum(seg_counts)
        starts = ends - seg_counts
        first_t = starts // tm
        nvis = jnp.where(seg_counts > 0, jnp.maximum(ends - 1, 0) // tm - first_t + 1, 0)
        cv_end = jnp.cumsum(nvis)
        cv_start = cv_end - nvis
        v = jnp.arange(n_tiles + n_seg, dtype=i32)
        active = v < cv_end[-1]
        s = jnp.minimum(jnp.sum((cv_end[None, :] <= v[:, None]).astype(i32), axis=1), n_seg - 1)
        tile = first_t[s] + v - cv_start[s]
        lo = jnp.clip(starts[s] - tile * tm, 0, tm)
        hi = jnp.clip(ends[s] - tile * tm, 0, tm)
        return tile, seg_expert[s], lo, hi, active

    run_id = np.cumsum([0] + [int(a[0] != b[0]) for a, b in zip(PAIR_CLASSES[:-1], PAIR_CLASSES[1:])])
    run_lo = jnp.asarray(np.array([PAIR_CLASSES[list(run_id).index(r)][0] for r in range(run_id[-1] + 1)], np.int32))
    in_run = jnp.asarray(run_id[None, :] == np.arange(run_id[-1] + 1)[:, None])
    run_counts = jnp.sum(jnp.where(in_run, counts[None, :], 0), axis=1)
    parts = [segment_visits(run_counts, run_lo) + (0,), segment_visits(counts, cls_hi) + (1,)]
    tile = jnp.concatenate([p[0] for p in parts])
    e = jnp.concatenate([p[1] for p in parts])
    lo = jnp.concatenate([p[2] for p in parts])
    hi = jnp.concatenate([p[3] for p in parts])
    active = jnp.concatenate([p[4] for p in parts])
    is_hi = jnp.concatenate([jnp.full(p[0].shape, p[5], i32) for p in parts])
    tile_stride = 2 * (tm + 1)
    order_key = jnp.where(active, tile * tile_stride + lo * 2 + is_hi, jnp.iinfo(jnp.int32).max)
    _, tile, e, lo, hi, is_hi, active = lax.sort((order_key, tile, e, lo, hi, is_hi, active.astype(i32)), num_keys=1)
    active = active == 1
    n_active = jnp.sum(active.astype(i32))
    v = jnp.arange(tile.shape[0], dtype=i32)
    e_last = jnp.max(jnp.where(active, v, -1))
    e = jnp.where(active, e, e[e_last])
    tile = jnp.where(active, tile, n_tiles - 1)
    lo = jnp.where(active, lo, 0)
    hi = jnp.where(active, hi, 0)
    prev_t = jnp.concatenate([jnp.full((1,), -1, i32), tile[:-1]])
    next_t = jnp.concatenate([tile[1:], jnp.full((1,), -1, i32)])
    is_final = v == n_active - 1
    is_last = (next_t != tile) | is_final
    group = e // MOE_EXPERTS_PER_GROUP
    prev_g = jnp.concatenate([jnp.full((1,), -1, i32), group[:-1]])
    flag = jnp.where(active, (prev_t != tile) * FLAG_FIRST + is_last * FLAG_LAST + is_final * FLAG_FINAL
                     + (prev_g != group) * FLAG_NEW_GROUP + is_hi * FLAG_HI, 0)
    cast = lambda z: z.astype(i32)
    bcast = lambda w: jnp.broadcast_to(w[:, None], (t, LANES))
    return cast(tile), cast(e), cast(lo), cast(hi), cast(flag), cast(order), bcast(wlo), bcast(whi)


def _combine_kernel(x_ref, y_ref, g2_ref, lg_ref, lb_ref, o_ref, *, alpha):
    y = _load_token_tiles(y_ref, x_ref.shape[0])
    o_ref[...] = _layer_norm(alpha * x_ref[...] + g2_ref[...] * y, lg_ref[...], lb_ref[...])


def _combine(x1, y, row0, mod3, mod_row0, rows_per_mod, ln_g, ln_b, alpha, tm):
    t, d = x1.shape
    tiles_per_mod = rows_per_mod // tm
    t0 = row0 // tm
    return pl.pallas_call(
        functools.partial(_combine_kernel, alpha=alpha),
        grid=(t // tm,),
        in_specs=[pl.BlockSpec((tm, d), lambda i: (i, 0)),
                  pl.BlockSpec((tm * SUBLANES, LANES), lambda i: (t0 + i, 0)),
                  pl.BlockSpec((None, 1, d), lambda i: (mod_row0 + i // tiles_per_mod, 0, 5)),
                  _const_spec((1, d)), _const_spec((1, d))],
        out_specs=pl.BlockSpec((tm, d), lambda i: (i, 0)),
        out_shape=jax.ShapeDtypeStruct((t, d), F32),
        compiler_params=_cparams("arbitrary"),
        name="combine_ln2",
    )(x1, y, mod3, ln_g.reshape(1, d), ln_b.reshape(1, d))


ROW_TILE = 512
SMALL_ROW_TILE = 256


def _pick_tile(n, pref):
    tm = min(pref, n)
    while n % tm:
        tm //= 2
    return tm


def kernel(x, c, ctx, c_ctx, ada_w, ada_b, w_in, gm_ln_g, gm_ln_b, gm_ws, gm_bs, hy_conv_w, hy_conv_b,
           hy_f_w1, hy_f_b1, hy_f_w2, hy_f_b2, hy_f_w3, hy_f_b3, hy_skip, da_lq1, da_lk1, da_lq2, da_lk2,
           da_norm_g, p_a, p_b, p_c, w_out, ln1_g, ln1_b, moe_wg, moe_bg, moe_we, moe_be,
           ex_w_gate, ex_w_up, ex_w_down, ln2_g, ln2_b):
    B, L, D = x.shape
    Lc = ctx.shape[1]
    depth = ada_w.shape[0]
    alpha = (2.0 * depth) ** 0.25
    T, Tc = B * L, B * Lc

    mp = -(-(B + 1) // 8) * 8
    c_all = jnp.zeros((mp, D), F32).at[:B].set(c).at[B].set(c_ctx)
    mod = _modulation(c_all, ada_w, ada_b)

    rope_tabs = _rope_tables(L // GRID_W)
    cm, sm = _dft_tables(L)
    cm_bf, sm_bf = cm.astype(BF16), sm.astype(BF16)
    cmc, smc = _dft_tables(Lc)
    cmc_bf, smc_bf = cmc.astype(BF16), smc.astype(BF16)

    seg_all = ((OFF_GM, OFF_HY, "gm"), (OFF_HY, OFF_Q, "hy"), (OFF_Q, OFF_K, "q"), (OFF_K, OFF_V, "k"),
               (OFF_V, OFF_GATE, "v"), (OFF_GATE, OFF_GATE + N_BRANCH * D, "gate"))
    seg_kv = ((0, DA_QK_W, "k"), (DA_QK_W, DA_QK_W + DA_V_W, "v"))

    tm_l = _pick_tile(L, ROW_TILE)
    tm_c = _pick_tile(Lc, SMALL_ROW_TILE)
    tq_l = _pick_tile(L, ROW_TILE)
    tq_c = _pick_tile(Lc, SMALL_ROW_TILE)
    tm_m = _pick_tile(L, ROW_TILE)
    tm_mc = _pick_tile(Tc, ROW_TILE)

    xs = x.reshape(T, D)
    xc = ctx.reshape(Tc, D)
    for l in range(depth):
        last = l == depth - 1
        lam_init = 0.8 - 0.6 * math.exp(-0.3 * l)
        mod3 = mod[l].reshape(mp, 1, 6 * D)
        w_l = w_in[l].astype(BF16)
        lparams = [a[l].reshape(1, DA_HEAD_DIM) for a in (da_lq1, da_lk1, da_lq2, da_lk2)]
        norm_g = da_norm_g[l].reshape(1, DA_V_DIM)
        lp = {
            "gm_ln_g": gm_ln_g[l].reshape(1, GM_DIM), "gm_ln_b": gm_ln_b[l].reshape(1, GM_DIM),
            "gm_ws": gm_ws[l].reshape(GM_GROUPS * GM_CHUNK, GM_CHUNK).astype(BF16),
            "gm_bs": jnp.repeat(jnp.transpose(gm_bs[l]), GM_DIM // GM_GROUPS, axis=1),
            "p_a": p_a[l].astype(BF16), "p_b": p_b[l].astype(BF16), "p_c": p_c[l].astype(BF16),
            "w_out": w_out[l].astype(BF16),
            "ln1_g": ln1_g[l].reshape(1, D), "ln1_b": ln1_b[l].reshape(1, D),
            "w_router": _split_bf16(jnp.zeros((D, LANES), F32).at[:, :MOE_GROUPS].set(moe_wg[l])
                                    .at[:, MOE_GROUPS:MOE_GROUPS + MOE_N_EXPERTS].set(moe_we[l])),
            "b_router": jnp.zeros((1, LANES), F32).at[0, :MOE_GROUPS].set(moe_bg[l])
                           .at[0, MOE_GROUPS:MOE_GROUPS + MOE_N_EXPERTS].set(moe_be[l]),
        }
        fw = (hy_f_w1[l], hy_f_b1[l], hy_f_w2[l], hy_f_b2[l], hy_f_w3[l], hy_f_b3[l])

        zgm, zhy, q, k, v, gate = _inproj(xs, mod3, 0, L, w_l, seg_all, rope_tabs, L, tm_l)
        if last:
            k_c, v_c = _inproj(xc, mod3, B, Tc, w_l[:, OFF_K:OFF_GATE], seg_kv, None, Lc, tm_c)
        else:
            zgm_c, zhy_c, q_c, k_c, v_c, gate_c = _inproj(xc, mod3, B, Tc, w_l, seg_all, None, Lc, tm_c)
        y_c = _attention(q, [(k, v, L), (k_c, v_c, Lc)], lparams, norm_g, lam_init, B, L, tq_l)
        kre, kim, nyq = _hyena_filter_spectrum(L, cm, sm, *fw)
        y_b = _hyena(zhy, B, L, hy_conv_w[l], hy_conv_b[l], cm_bf, sm_bf, kre, kim, nyq, hy_skip[l])
        t_all = T if last else T + Tc
        x1, h2, route = _merge(zgm, y_b, y_c, gate, xs, mod3, 0, L, lp, alpha, tm_m, t_all, 0, None)

        if not last:
            yc_c = _attention(q_c, [(k_c, v_c, Lc)], lparams, norm_g, lam_init, B, Lc, tq_c)
            kre_c, kim_c, nyq_c = _hyena_filter_spectrum(Lc, cmc, smc, *fw)
            yb_c = _hyena(zhy_c, B, Lc, hy_conv_w[l], hy_conv_b[l], cmc_bf, smc_bf, kre_c, kim_c, nyq_c,
                          hy_skip[l])
            x1c, h2, route = _merge(zgm_c, yb_c, yc_c, gate_c, xc, mod3, B, Tc, lp, alpha, tm_mc, t_all, T,
                                    (h2, route))

        moe_tm = _pick_tile(t_all, ROW_TILE)
        plan = _visit_plan(route, moe_tm)
        y = _experts(h2, plan, ex_w_gate, ex_w_up, ex_w_down, l, moe_tm)
        xs = _combine(x1, y, 0, mod3, 0, L, ln2_g[l], ln2_b[l], alpha, tm_c)
        if not last:
            xc = _combine(x1c, y, T, mod3, B, Tc, ln2_g[l], ln2_b[l], alpha, tm_c)
    return xs.reshape(B, L, D)
```
